```python
import math
import jax, jax.numpy as jnp
from jax import lax
import numpy as np

D_MODEL = 1024
BATCH = 8
SEQ = 2048
DEPTH = 2
DEC_BATCH = 128
DEC_SEQ = 1
PAST_LEN = 16384
PAGE_SIZE = 128

N_EVEN = (DEPTH + 1) // 2
N_ODD = DEPTH // 2
GLA_HEADS = 4
GLA_DK = D_MODEL // 16
GLA_DV = D_MODEL // 8
GLA_K_WIDTH = GLA_HEADS * GLA_DK
GLA_V_WIDTH = GLA_HEADS * GLA_DV
GATE_RANK = 16
GATE_NORMALIZER = 16.0
POOL_WINDOWS = (2, 4, 8, 16)
POOL_GROUPS = len(POOL_WINDOWS)
POOL_WIDTH = D_MODEL // 2
POOL_GROUP_WIDTH = POOL_WIDTH // POOL_GROUPS
POOL_BUF = max(POOL_WINDOWS) - 1
EVEN_IN = 2 * GLA_K_WIDTH + 2 * GLA_V_WIDTH + GATE_RANK + POOL_WIDTH
RET_HEADS = 4
RET_DK = D_MODEL // RET_HEADS
RET_DV = 2 * RET_DK
RET_QK_WIDTH = RET_HEADS * RET_DK
RET_V_WIDTH = RET_HEADS * RET_DV
ODD_IN = 2 * RET_QK_WIDTH + 2 * RET_V_WIDTH
ROPE_BASE = 10000.0
D_FF = -(-8 * D_MODEL // (3 * 256)) * 256
CHUNK = 64
EPS = 1e-6

kernel_name = "hybrid_gla_pool_retnet_step"


def rms_norm(x, gain=None):
    x32 = x.astype(jnp.float32)
    y = x32 * lax.rsqrt(jnp.mean(x32 * x32, axis=-1, keepdims=True) + EPS)
    if gain is not None:
        y = y * gain.astype(jnp.float32)
    return y.astype(x.dtype)


def gated_linear_recurrence(q, k, v, log_a, s0):
    B, H, L, dk = q.shape
    dv = v.shape[-1]
    c = CHUNK if L >= CHUNK else L
    n = -(-L // c)
    pad = n * c - L
    f32 = jnp.float32
    q, k, v, log_a = (t.astype(f32) for t in (q, k, v, log_a))
    if pad:
        pw = ((0, 0), (0, 0), (0, pad), (0, 0))
        q, k, v, log_a = (jnp.pad(t, pw) for t in (q, k, v, log_a))

    def to_chunks(t):
        return t.reshape(B, H, n, c, t.shape[-1]).transpose(2, 0, 1, 3, 4)

    mask = jnp.tril(jnp.ones((c, c), dtype=bool))

    def step(S, inp):
        qc, kc, vc, gc = inp
        b = jnp.cumsum(gc, axis=2)
        qe = qc * jnp.exp(b)
        ke = kc * jnp.exp(-b)
        att = jnp.where(mask, jnp.einsum('bhtd,bhsd->bhts', qe, ke), 0.0)
        o = jnp.einsum('bhtd,bhdv->bhtv', qe, S) + jnp.einsum('bhts,bhsv->bhtv', att, vc)
        b_last = b[:, :, -1:, :]
        S = jnp.exp(b_last[:, :, 0, :])[..., None] * S + jnp.einsum(
            'bhsd,bhsv->bhdv', kc * jnp.exp(b_last - b), vc)
        return S, o

    S, o = lax.scan(step, s0.astype(f32),
                    (to_chunks(q), to_chunks(k), to_chunks(v), to_chunks(log_a)))
    o = o.transpose(1, 2, 0, 3, 4).reshape(B, H, n * c, dv)[:, :, :L]
    return o, S


def multiscale_pool(u, buf, n_past, pool_w, pool_scale):
    B, L, P = u.shape
    f32 = jnp.float32
    ext = jnp.concatenate([buf.astype(f32), u.astype(f32)], axis=1)
    cs = jnp.concatenate([jnp.zeros((B, 1, P), f32), jnp.cumsum(ext, axis=1)], axis=1)
    end = cs[:, POOL_BUF + 1:]
    idx = jnp.arange(L, dtype=jnp.int32)
    means = []
    for g, w in enumerate(POOL_WINDOWS):
        sl = slice(g * POOL_GROUP_WIDTH, (g + 1) * POOL_GROUP_WIDTH)
        start = cs[:, POOL_BUF + 1 - w: POOL_BUF + 1 - w + L, sl]
        count = jnp.minimum(w, n_past + idx + 1).astype(f32)[None, :, None]
        means.append((end[..., sl] - start) / count)
    pooled = jnp.concatenate(means, axis=-1) - u.astype(f32)
    pooled = jnp.einsum('blgc,gcd->blgd',
                        pooled.reshape(B, L, POOL_GROUPS, POOL_GROUP_WIDTH),
                        pool_w.astype(f32)).reshape(B, L, P)
    pooled = pooled * pool_scale.astype(f32)
    return pooled.astype(u.dtype), ext[:, -POOL_BUF:]


def gla_pool_mixer(h, s_gla, s_pool, pos0, w_in, w_gate_b, b_gate, gla_gain,
                   pool_w, pool_scale, w_out):
    B, L, _ = h.shape
    proj = h @ w_in
    cuts = [GLA_K_WIDTH, 2 * GLA_K_WIDTH, 2 * GLA_K_WIDTH + GLA_V_WIDTH,
            2 * GLA_K_WIDTH + 2 * GLA_V_WIDTH, 2 * GLA_K_WIDTH + 2 * GLA_V_WIDTH + GATE_RANK]
    q, k, v, g, a, u = jnp.split(proj, cuts, axis=-1)

    def heads(t, d):
        return t.reshape(B, L, GLA_HEADS, d).transpose(0, 2, 1, 3)

    log_a = jax.nn.log_sigmoid((a @ w_gate_b + b_gate).astype(jnp.float32)) / GATE_NORMALIZER
    o, s_new = gated_linear_recurrence(heads(q, GLA_DK) * (GLA_DK ** -0.5), heads(k, GLA_DK),
                                       heads(v, GLA_DV), heads(log_a, GLA_DK), s_gla)
    o = rms_norm(o, gla_gain).astype(h.dtype)
    o = o.transpose(0, 2, 1, 3).reshape(B, L, GLA_V_WIDTH) * jax.nn.silu(g)
    p, buf_new = multiscale_pool(u, s_pool, pos0, pool_w, pool_scale)
    y = jnp.concatenate([o, p], axis=-1) @ w_out
    return y, s_new, buf_new


def theta_shift(x, sin, cos):
    x1 = x[..., 0::2]
    x2 = x[..., 1::2]
    rot = jnp.stack([-x2, x1], axis=-1).reshape(x.shape)
    return x * cos + rot * sin


def retention_mixer(h, s_ret, pos0, w_in, w_out):
    B, L, _ = h.shape
    f32 = jnp.float32
    proj = h @ w_in
    q, k, v, g = jnp.split(proj, [RET_QK_WIDTH, 2 * RET_QK_WIDTH,
                                  2 * RET_QK_WIDTH + RET_V_WIDTH], axis=-1)

    def heads(t, d):
        return t.reshape(B, L, RET_HEADS, d).transpose(0, 2, 1, 3).astype(f32)

    pos = pos0 + jnp.arange(L, dtype=f32)
    angle = jnp.repeat(1.0 / (ROPE_BASE ** jnp.linspace(0.0, 1.0, RET_DK // 2, dtype=f32)), 2)
    ang = pos[:, None] * angle[None, :]
    sin, cos = jnp.sin(ang), jnp.cos(ang)
    qh = theta_shift(heads(q, RET_DK), sin, cos)
    kh = theta_shift(heads(k, RET_DK) * (RET_DK ** -0.5), sin, cos)
    log_gamma = jnp.log(1.0 - 2.0 ** (-5.0 - jnp.arange(RET_HEADS, dtype=f32)))
    log_a = jnp.broadcast_to(log_gamma[None, :, None, None], (B, RET_HEADS, L, RET_DK))
    o, s_new = gated_linear_recurrence(qh, kh, heads(v, RET_DV), log_a, s_ret)
    o = rms_norm(o).astype(h.dtype)
    o = o.transpose(0, 2, 1, 3).reshape(B, L, RET_V_WIDTH) * jax.nn.silu(g)
    return o @ w_out, s_new


def swiglu(h, w_gate, w_up, w_down):
    return (jax.nn.silu(h @ w_gate) * (h @ w_up)) @ w_down


def run_trunk(x, s_gla, s_pool, s_ret, pos0, norm_mix, norm_ffn, norm_final,
              w_in_even, w_gate_b, b_gate, gla_gain, pool_w, pool_scale, w_out_even,
              w_in_odd, w_out_odd, w_ffn_gate, w_ffn_up, w_ffn_down):
    new_gla, new_pool, new_ret = [], [], []
    for i in range(DEPTH):
        h = rms_norm(x, norm_mix[i])
        if i % 2 == 0:
            e = i // 2
            y, sg, sp = gla_pool_mixer(h, s_gla[e], s_pool[e], pos0, w_in_even[e], w_gate_b[e],
                                       b_gate[e], gla_gain[e], pool_w[e], pool_scale[e],
                                       w_out_even[e])
            new_gla.append(sg)
            new_pool.append(sp)
        else:
            o = i // 2
            y, sr = retention_mixer(h, s_ret[o], pos0, w_in_odd[o], w_out_odd[o])
            new_ret.append(sr)
        x = x + y
        x = x + swiglu(rms_norm(x, norm_ffn[i]), w_ffn_gate[i], w_ffn_up[i], w_ffn_down[i])
    return rms_norm(x, norm_final), jnp.stack(new_gla), jnp.stack(new_pool), jnp.stack(new_ret)


def setup_inputs(seed: int = 0) -> dict:
    key = jax.random.key(seed)
    ks = jax.random.split(key, 24)
    f32 = jnp.float32

    def nrm(k, shape, scale):
        return jax.random.normal(k, shape, f32) * scale

    return {
        "x_prompt": nrm(ks[0], (BATCH, SEQ, D_MODEL), 1.0),
        "x_sample": nrm(ks[1], (DEC_BATCH, DEC_SEQ, D_MODEL), 1.0),
        "state_gla": nrm(ks[2], (N_EVEN, DEC_BATCH, GLA_HEADS, GLA_DK, GLA_DV), 0.5),
        "state_pool": nrm(ks[3], (N_EVEN, DEC_BATCH, POOL_BUF, POOL_WIDTH), 1.0),
        "state_ret": nrm(ks[4], (N_ODD, DEC_BATCH, RET_HEADS, RET_DK, RET_DV), 1.0),
        "norm_mix": 1.0 + nrm(ks[5], (DEPTH, D_MODEL), 0.02),
        "norm_ffn": 1.0 + nrm(ks[6], (DEPTH, D_MODEL), 0.02),
        "norm_final": 1.0 + nrm(ks[7], (D_MODEL,), 0.02),
        "w_in_even": nrm(ks[8], (N_EVEN, D_MODEL, EVEN_IN), D_MODEL ** -0.5),
        "w_gate_b": nrm(ks[9], (N_EVEN, GATE_RANK, GLA_K_WIDTH), GATE_RANK ** -0.5),
        "b_gate": nrm(ks[10], (N_EVEN, GLA_K_WIDTH), 0.1),
        "gla_gain": 1.0 + nrm(ks[11], (N_EVEN, GLA_DV), 0.02),
        "pool_w": nrm(ks[12], (N_EVEN, POOL_GROUPS, POOL_GROUP_WIDTH, POOL_GROUP_WIDTH),
                       POOL_GROUP_WIDTH ** -0.5),
        "pool_scale": 1.0 + nrm(ks[13], (N_EVEN, POOL_WIDTH), 0.02),
        "w_out_even": nrm(ks[14], (N_EVEN, GLA_V_WIDTH + POOL_WIDTH, D_MODEL),
                          (GLA_V_WIDTH + POOL_WIDTH) ** -0.5),
        "w_in_odd": nrm(ks[15], (N_ODD, D_MODEL, ODD_IN), D_MODEL ** -0.5),
        "w_out_odd": nrm(ks[16], (N_ODD, RET_V_WIDTH, D_MODEL), RET_V_WIDTH ** -0.5),
        "w_ffn_gate": nrm(ks[17], (DEPTH, D_MODEL, D_FF), D_MODEL ** -0.5),
        "w_ffn_up": nrm(ks[18], (DEPTH, D_MODEL, D_FF), D_MODEL ** -0.5),
        "w_ffn_down": nrm(ks[19], (DEPTH, D_FF, D_MODEL), D_FF ** -0.5),
    }


def reference(x_prompt, x_sample, state_gla, state_pool, state_ret, norm_mix, norm_ffn,
              norm_final, w_in_even, w_gate_b, b_gate, gla_gain, pool_w, pool_scale,
              w_out_even, w_in_odd, w_out_odd, w_ffn_gate, w_ffn_up, w_ffn_down):
    weights = (norm_mix, norm_ffn, norm_final, w_in_even, w_gate_b, b_gate, gla_gain, pool_w,
               pool_scale, w_out_even, w_in_odd, w_out_odd, w_ffn_gate, w_ffn_up, w_ffn_down)
    B = x_prompt.shape[0]
    z_gla = jnp.zeros((N_EVEN, B, GLA_HEADS, GLA_DK, GLA_DV), jnp.float32)
    z_pool = jnp.zeros((N_EVEN, B, POOL_BUF, POOL_WIDTH), x_prompt.dtype)
    z_ret = jnp.zeros((N_ODD, B, RET_HEADS, RET_DK, RET_DV), jnp.float32)
    y_prompt, gla_p, pool_p, ret_p = run_trunk(x_prompt, z_gla, z_pool, z_ret, 0, *weights)
    y_sample, gla_s, pool_s, ret_s = run_trunk(x_sample, state_gla, state_pool, state_ret,
                                               PAST_LEN, *weights)
    return (y_prompt, y_sample,
            gla_p.astype(state_gla.dtype), gla_s.astype(state_gla.dtype),
            pool_p.astype(state_pool.dtype), pool_s.astype(state_pool.dtype),
            ret_p.astype(state_ret.dtype), ret_s.astype(state_ret.dtype))
```

```python
import functools

import numpy as np
import jax
import jax.numpy as jnp
from jax import lax
from jax.experimental import pallas as pl
from jax.experimental.pallas import tpu as pltpu

f32 = jnp.float32
bf16 = jnp.bfloat16

EPS = 1e-6
PAST_LEN = 16384
GLA_HEADS, GLA_DK, GLA_DV = 4, 64, 128
GLA_CHUNK = 64
GATE_RANK = 16
GATE_NORMALIZER = 16.0
POOL_WINDOWS = (2, 4, 8, 16)
POOL_GW = 128
POOL_BUF = max(POOL_WINDOWS) - 1
POOL_HIST = 16
RET_HEADS, RET_DK, RET_DV = 4, 256, 512
RET_CHUNK = 256
ROPE_BASE = 10000.0
LANES = 128
MIB = 1024 * 1024

NT_DIMS = (((1,), (1,)), ((), ()))
TN_DIMS = (((0,), (0,)), ((), ()))


def _params(semantics, vmem_mib):
    return pltpu.CompilerParams(dimension_semantics=semantics, vmem_limit_bytes=vmem_mib * MIB)


def _rms(x, gain=None):
    y = x * lax.rsqrt(jnp.mean(x * x, axis=-1, keepdims=True) + EPS)
    return y if gain is None else y * gain


def _silu(g):
    return g * jax.nn.sigmoid(g)


def _in_even_body(x_ref, gain_ref, w_ref, wgb_ref, bg_ref, qkvg_ref, loga_ref, u_ref):
    nq = qkvg_ref.shape[1]
    nu = u_ref.shape[1]
    h = _rms(x_ref[...], gain_ref[...]).astype(bf16)
    proj = jnp.dot(h, w_ref[...], preferred_element_type=f32)
    qkvg_ref[...] = proj[:, :nq].astype(bf16)
    u_ref[...] = proj[:, nq:nq + nu]
    a = proj[:, nq + nu:].astype(bf16)
    z = jnp.dot(a, wgb_ref[...], preferred_element_type=f32) + bg_ref[...]
    loga_ref[...] = (jnp.minimum(z, 0.0) - jnp.log1p(jnp.exp(-jnp.abs(z)))) * (1.0 / GATE_NORMALIZER)


def _in_even(x, gain, w, wgb, bg, *, tm):
    m, d = x.shape
    nq = 2 * GLA_HEADS * GLA_DK + 2 * GLA_HEADS * GLA_DV
    nu = POOL_GW * len(POOL_WINDOWS)
    nk = GLA_HEADS * GLA_DK
    const = lambda i: (0, 0)
    return pl.pallas_call(
        _in_even_body,
        grid=(m // tm,),
        in_specs=[
            pl.BlockSpec((tm, d), lambda i: (i, 0)),
            pl.BlockSpec((1, d), const),
            pl.BlockSpec(w.shape, const),
            pl.BlockSpec(wgb.shape, const),
            pl.BlockSpec((1, nk), const),
        ],
        out_specs=[
            pl.BlockSpec((tm, nq), lambda i: (i, 0)),
            pl.BlockSpec((tm, nk), lambda i: (i, 0)),
            pl.BlockSpec((tm, nu), lambda i: (i, 0)),
        ],
        out_shape=[
            jax.ShapeDtypeStruct((m, nq), bf16),
            jax.ShapeDtypeStruct((m, nk), f32),
            jax.ShapeDtypeStruct((m, nu), f32),
        ],
        compiler_params=_params(("arbitrary",), 48),
        name="in_even",
    )(x, gain, w, wgb, bg)


def _gla_pool_prompt_body(qkvg_ref, loga_ref, u_ref, x_ref, tril_ref, gain_ref, pw_ref, ps_ref, wout_ref,
                          xo_ref, so_ref, st_ref, uext_ref, op_ref):
    t = x_ref.shape[0]
    kw = GLA_HEADS * GLA_DK
    vw = GLA_HEADS * GLA_DV
    i = pl.program_id(1)

    @pl.when(i == 0)
    def _():
        st_ref[...] = jnp.zeros_like(st_ref)
        uext_ref[0:POOL_HIST, :] = jnp.zeros((POOL_HIST, uext_ref.shape[1]), f32)

    tril = tril_ref[...]
    causal = lax.broadcasted_iota(jnp.int32, (GLA_CHUNK, GLA_CHUNK), 0) >= lax.broadcasted_iota(
        jnp.int32, (GLA_CHUNK, GLA_CHUNK), 1)
    gain = gain_ref[...]
    for c in range(t // GLA_CHUNK):
        rows = slice(c * GLA_CHUNK, (c + 1) * GLA_CHUNK)
        bc = jnp.dot(tril, loga_ref[rows, :], precision=lax.Precision.HIGHEST, preferred_element_type=f32)
        blast = bc[GLA_CHUNK - 1:GLA_CHUNK, :]
        q = qkvg_ref[rows, 0:kw].astype(f32) * (GLA_DK ** -0.5)
        k = qkvg_ref[rows, kw:2 * kw].astype(f32)
        qe = (q * jnp.exp(bc)).astype(bf16)
        ke = (k * jnp.exp(-bc)).astype(bf16)
        kd = (k * jnp.exp(blast - bc)).astype(bf16)
        elast = jnp.exp(blast)
        for h in range(GLA_HEADS):
            ks = slice(h * GLA_DK, (h + 1) * GLA_DK)
            v = qkvg_ref[rows, 2 * kw + h * GLA_DV:2 * kw + (h + 1) * GLA_DV]
            att = lax.dot_general(qe[:, ks], ke[:, ks], NT_DIMS, preferred_element_type=f32)
            att = jnp.where(causal, att, 0.0).astype(bf16)
            st = st_ref[h]
            o = lax.dot_general(qe[:, ks], st.astype(bf16), NT_DIMS, preferred_element_type=f32)
            o = o + jnp.dot(att, v, preferred_element_type=f32)
            st_ref[h] = st * elast[:, ks] + lax.dot_general(v, kd[:, ks], TN_DIMS, preferred_element_type=f32)
            g = qkvg_ref[rows, 2 * kw + vw + h * GLA_DV:2 * kw + vw + (h + 1) * GLA_DV].astype(f32)
            op_ref[rows, h * GLA_DV:(h + 1) * GLA_DV] = (_rms(o, gain) * _silu(g)).astype(bf16)

    u = u_ref[...]
    uext_ref[POOL_HIST:POOL_HIST + t, :] = u
    pos = i * t + lax.broadcasted_iota(jnp.int32, (t, 1), 0)
    for gi, w in enumerate(POOL_WINDOWS):
        ls = slice(gi * POOL_GW, (gi + 1) * POOL_GW)
        s = u[:, ls]
        for j in range(1, w):
            s = s + uext_ref[POOL_HIST - j:POOL_HIST - j + t, ls]
        cnt = jnp.minimum(w, pos + 1).astype(f32)
        pooled = (s / cnt - u[:, ls]).astype(bf16)
        pg = jnp.dot(pooled, pw_ref[gi], preferred_element_type=f32) * ps_ref[:, ls]
        op_ref[:, vw + gi * POOL_GW:vw + (gi + 1) * POOL_GW] = pg.astype(bf16)
    uext_ref[0:POOL_HIST, :] = uext_ref[t:t + POOL_HIST, :]

    xo_ref[...] = x_ref[...] + jnp.dot(op_ref[...], wout_ref[...], preferred_element_type=f32)

    @pl.when(i == pl.num_programs(1) - 1)
    def _():
        for h in range(GLA_HEADS):
            so_ref[0, h] = st_ref[h].T


def _gla_pool_prompt(qkvg, loga, u, x, tril, gain, pw, ps, wout, *, batch, t):
    m, d = x.shape
    nt = m // batch // t
    row = lambda b, i: (b * nt + i, 0)
    const2 = lambda b, i: (0, 0)
    pw_cols = GLA_HEADS * GLA_DV + POOL_GW * len(POOL_WINDOWS)
    return pl.pallas_call(
        _gla_pool_prompt_body,
        grid=(batch, nt),
        in_specs=[
            pl.BlockSpec((t, qkvg.shape[1]), row),
            pl.BlockSpec((t, loga.shape[1]), row),
            pl.BlockSpec((t, u.shape[1]), row),
            pl.BlockSpec((t, d), row),
            pl.BlockSpec(tril.shape, const2),
            pl.BlockSpec(gain.shape, const2),
            pl.BlockSpec(pw.shape, lambda b, i: (0, 0, 0)),
            pl.BlockSpec(ps.shape, const2),
            pl.BlockSpec(wout.shape, const2),
        ],
        out_specs=[
            pl.BlockSpec((t, d), row),
            pl.BlockSpec((1, GLA_HEADS, GLA_DK, GLA_DV), lambda b, i: (b, 0, 0, 0)),
        ],
        out_shape=[
            jax.ShapeDtypeStruct((m, d), f32),
            jax.ShapeDtypeStruct((batch, GLA_HEADS, GLA_DK, GLA_DV), f32),
        ],
        scratch_shapes=[
            pltpu.VMEM((GLA_HEADS, GLA_DV, GLA_DK), f32),
            pltpu.VMEM((POOL_HIST + t, u.shape[1]), f32),
            pltpu.VMEM((t, pw_cols), bf16),
        ],
        compiler_params=_params(("arbitrary", "arbitrary"), 40),
        name="gla_pool_prompt",
    )(qkvg, loga, u, x, tril, gain, pw, ps, wout)


def _gla_pool_sample_body(qkvg_ref, loga_ref, u_ref, s_ref, buf_ref, gain_ref, pw_ref, ps_ref,
                          op_ref, so_ref):
    bb = u_ref.shape[0]
    kw = GLA_HEADS * GLA_DK
    vw = GLA_HEADS * GLA_DV
    gain = gain_ref[...]
    qkvg = qkvg_ref[...].astype(f32)
    alpha = jnp.exp(loga_ref[...])
    qs = qkvg[:, 0:kw] * (GLA_DK ** -0.5)
    k = qkvg[:, kw:2 * kw]

    def column(row):
        return jnp.broadcast_to(row, (LANES, kw)).T

    o_rows = []
    for b in range(bb):
        acol = column(alpha[b:b + 1, :])
        qcol = column(qs[b:b + 1, :])
        kcol = column(k[b:b + 1, :])
        o_heads = []
        for h in range(GLA_HEADS):
            ks = slice(h * GLA_DK, (h + 1) * GLA_DK)
            v = qkvg[b:b + 1, 2 * kw + h * GLA_DV:2 * kw + (h + 1) * GLA_DV]
            s_new = acol[ks, :] * s_ref[b, h] + kcol[ks, :] * v
            so_ref[b, h] = s_new
            o = jnp.sum(qcol[ks, :] * s_new, axis=0, keepdims=True)
            g = qkvg[b:b + 1, 2 * kw + vw + h * GLA_DV:2 * kw + vw + (h + 1) * GLA_DV]
            o_heads.append(_rms(o, gain) * _silu(g))
        o_rows.append(jnp.concatenate(o_heads, axis=1))
    op_ref[:, 0:vw] = jnp.concatenate(o_rows, axis=0).astype(bf16)

    u = u_ref[...]
    for gi, w in enumerate(POOL_WINDOWS):
        ls = slice(gi * POOL_GW, (gi + 1) * POOL_GW)
        s = u[:, ls] + jnp.sum(buf_ref[:, POOL_BUF - (w - 1):POOL_BUF, ls], axis=1)
        cnt = float(min(w, PAST_LEN + 1))
        pooled = (s / cnt - u[:, ls]).astype(bf16)
        pg = jnp.dot(pooled, pw_ref[gi], preferred_element_type=f32) * ps_ref[:, ls]
        op_ref[:, vw + gi * POOL_GW:vw + (gi + 1) * POOL_GW] = pg.astype(bf16)


def _gla_pool_sample(qkvg, loga, u, s, buf, gain, pw, ps, *, bb):
    n = u.shape[0]
    row = lambda i: (i, 0)
    const2 = lambda i: (0, 0)
    ow = GLA_HEADS * GLA_DV + POOL_GW * len(POOL_WINDOWS)
    return pl.pallas_call(
        _gla_pool_sample_body,
        grid=(n // bb,),
        in_specs=[
            pl.BlockSpec((bb, qkvg.shape[1]), row),
            pl.BlockSpec((bb, loga.shape[1]), row),
            pl.BlockSpec((bb, u.shape[1]), row),
            pl.BlockSpec((bb,) + s.shape[1:], lambda i: (i, 0, 0, 0)),
            pl.BlockSpec((bb,) + buf.shape[1:], lambda i: (i, 0, 0)),
            pl.BlockSpec(gain.shape, const2),
            pl.BlockSpec(pw.shape, lambda i: (0, 0, 0)),
            pl.BlockSpec(ps.shape, const2),
        ],
        out_specs=[
            pl.BlockSpec((bb, ow), row),
            pl.BlockSpec((bb,) + s.shape[1:], lambda i: (i, 0, 0, 0)),
        ],
        out_shape=[
            jax.ShapeDtypeStruct((n, ow), bf16),
            jax.ShapeDtypeStruct(s.shape, f32),
        ],
        compiler_params=_params(("arbitrary",), 32),
        name="gla_pool_sample",
    )(qkvg, loga, u, s, buf, gain, pw, ps)


def _proj_res_body(x_ref, a_ref, w_ref, o_ref):
    o_ref[...] = x_ref[...] + jnp.dot(a_ref[...], w_ref[...], preferred_element_type=f32)


def _proj_res(x, a, w, *, tm):
    m, d = x.shape
    return pl.pallas_call(
        _proj_res_body,
        grid=(m // tm,),
        in_specs=[
            pl.BlockSpec((tm, d), lambda i: (i, 0)),
            pl.BlockSpec((tm, a.shape[1]), lambda i: (i, 0)),
            pl.BlockSpec(w.shape, lambda i: (0, 0)),
        ],
        out_specs=pl.BlockSpec((tm, d), lambda i: (i, 0)),
        out_shape=jax.ShapeDtypeStruct((m, d), f32),
        compiler_params=_params(("arbitrary",), 32),
        name="proj_res",
    )(x, a, w)


def _ffn_body(x_ref, gain_ref, wg_ref, wu_ref, wd_ref, fgain_ref, o_ref, h_ref, acc_ref, *, final_norm):
    j = pl.program_id(1)

    @pl.when(j == 0)
    def _():
        h_ref[...] = _rms(x_ref[...], gain_ref[...]).astype(bf16)
        acc_ref[...] = jnp.zeros_like(acc_ref)

    h = h_ref[...]
    g = jnp.dot(h, wg_ref[...], preferred_element_type=f32)
    u = jnp.dot(h, wu_ref[...], preferred_element_type=f32)
    a = (_silu(g) * u).astype(bf16)
    acc_ref[...] += jnp.dot(a, wd_ref[...], preferred_element_type=f32)

    @pl.when(j == pl.num_programs(1) - 1)
    def _():
        y = x_ref[...] + acc_ref[...]
        if final_norm:
            y = _rms(y, fgain_ref[...])
        o_ref[...] = y


def _ffn(x, gain, wg, wu, wd, fgain, *, tm, tf, final_norm):
    m, d = x.shape
    ff = wg.shape[1]
    return pl.pallas_call(
        functools.partial(_ffn_body, final_norm=final_norm),
        grid=(m // tm, ff // tf),
        in_specs=[
            pl.BlockSpec((tm, d), lambda i, j: (i, 0)),
            pl.BlockSpec((1, d), lambda i, j: (0, 0)),
            pl.BlockSpec((d, tf), lambda i, j: (0, j)),
            pl.BlockSpec((d, tf), lambda i, j: (0, j)),
            pl.BlockSpec((tf, d), lambda i, j: (j, 0)),
            pl.BlockSpec((1, d), lambda i, j: (0, 0)),
        ],
        out_specs=pl.BlockSpec((tm, d), lambda i, j: (i, 0)),
        out_shape=jax.ShapeDtypeStruct((m, d), f32),
        scratch_shapes=[pltpu.VMEM((tm, d), bf16), pltpu.VMEM((tm, d), f32)],
        compiler_params=_params(("arbitrary", "arbitrary"), 48),
        name="ffn_final" if final_norm else "ffn",
    )(x, gain, wg, wu, wd, fgain)


def _in_odd_body(x_ref, gain_ref, w_ref, cos_ref, sin_ref, o_ref, h_ref):
    j = pl.program_id(1)
    tn = o_ref.shape[1]
    qk_blocks = 2 * RET_HEADS * RET_DK // tn

    @pl.when(j == 0)
    def _():
        h_ref[...] = _rms(x_ref[...], gain_ref[...]).astype(bf16)

    p = jnp.dot(h_ref[...], w_ref[...], preferred_element_type=f32)

    @pl.when(j < qk_blocks)
    def _():
        is_k = j >= qk_blocks // 2
        xs = p * jnp.where(is_k, RET_DK ** -0.5, 1.0).astype(f32)
        even = lax.broadcasted_iota(jnp.int32, xs.shape, 1) % 2 == 0
        partner = jnp.where(even, pltpu.roll(xs, tn - 1, 1), pltpu.roll(xs, 1, 1))
        cos = cos_ref[...]
        sin = sin_ref[...]
        for h in range(tn // RET_DK):
            hs = slice(h * RET_DK, (h + 1) * RET_DK)
            o_ref[:, hs] = (xs[:, hs] * cos + partner[:, hs] * sin).astype(bf16)

    @pl.when(j >= qk_blocks)
    def _():
        o_ref[...] = p.astype(bf16)


def _in_odd(x, gain, w, cos, sin, *, tm, tn):
    m, d = x.shape
    n = w.shape[1]
    ntab = cos.shape[0] // tm
    return pl.pallas_call(
        _in_odd_body,
        grid=(m // tm, n // tn),
        in_specs=[
            pl.BlockSpec((tm, d), lambda i, j: (i, 0)),
            pl.BlockSpec((1, d), lambda i, j: (0, 0)),
            pl.BlockSpec((d, tn), lambda i, j: (0, j)),
            pl.BlockSpec((tm, RET_DK), lambda i, j: (i % ntab, 0)),
            pl.BlockSpec((tm, RET_DK), lambda i, j: (i % ntab, 0)),
        ],
        out_specs=pl.BlockSpec((tm, tn), lambda i, j: (i, j)),
        out_shape=jax.ShapeDtypeStruct((m, n), bf16),
        scratch_shapes=[pltpu.VMEM((tm, d), bf16)],
        compiler_params=_params(("arbitrary", "arbitrary"), 40),
        name="in_odd",
    )(x, gain, w, cos, sin)


def _ret_prompt_body(q_ref, k_ref, v_ref, g_ref, x_ref, dmask_ref, dq_ref, dk_ref, wout_ref,
                     xo_ref, so_ref, s_ref, og_ref, *, gamma_c):
    c = pl.program_id(1)

    @pl.when(c == 0)
    def _():
        s_ref[...] = jnp.zeros_like(s_ref)

    for h in range(RET_HEADS):
        ks = slice(h * RET_DK, (h + 1) * RET_DK)
        vs = slice(h * RET_DV, (h + 1) * RET_DV)
        q = q_ref[:, ks]
        k = k_ref[:, ks]
        v = v_ref[:, vs]
        att = lax.dot_general(q, k, NT_DIMS, preferred_element_type=f32) * dmask_ref[h]
        s = s_ref[h]
        o = jnp.dot(q, s.astype(bf16), preferred_element_type=f32) * dq_ref[h]
        o = o + jnp.dot(att.astype(bf16), v, preferred_element_type=f32)
        kd = (k.astype(f32) * dk_ref[h]).astype(bf16)
        s_ref[h] = gamma_c[h] * s + lax.dot_general(kd, v, TN_DIMS, preferred_element_type=f32)
        g = g_ref[:, vs].astype(f32)
        og_ref[:, vs] = (_rms(o) * _silu(g)).astype(bf16)

    xo_ref[...] = x_ref[...] + jnp.dot(og_ref[...], wout_ref[...], preferred_element_type=f32)

    @pl.when(c == pl.num_programs(1) - 1)
    def _():
        so_ref[0] = s_ref[...]


def _ret_prompt(qkvg, x, dmask, dq, dk, wout, gamma_c, *, batch, c):
    m, d = x.shape
    nc = m // batch // c
    qw = RET_HEADS * RET_DK
    vw = RET_HEADS * RET_DV
    const3 = lambda b, i: (0, 0, 0)
    return pl.pallas_call(
        functools.partial(_ret_prompt_body, gamma_c=gamma_c),
        grid=(batch, nc),
        in_specs=[
            pl.BlockSpec((c, qw), lambda b, i: (b * nc + i, 0)),
            pl.BlockSpec((c, qw), lambda b, i: (b * nc + i, 1)),
            pl.BlockSpec((c, vw), lambda b, i: (b * nc + i, 1)),
            pl.BlockSpec((c, vw), lambda b, i: (b * nc + i, 2)),
            pl.BlockSpec((c, d), lambda b, i: (b * nc + i, 0)),
            pl.BlockSpec(dmask.shape, const3),
            pl.BlockSpec(dq.shape, const3),
            pl.BlockSpec(dk.shape, const3),
            pl.BlockSpec(wout.shape, lambda b, i: (0, 0)),
        ],
        out_specs=[
            pl.BlockSpec((c, d), lambda b, i: (b * nc + i, 0)),
            pl.BlockSpec((1, RET_HEADS, RET_DK, RET_DV), lambda b, i: (b, 0, 0, 0)),
        ],
        out_shape=[
            jax.ShapeDtypeStruct((m, d), f32),
            jax.ShapeDtypeStruct((batch, RET_HEADS, RET_DK, RET_DV), f32),
        ],
        scratch_shapes=[
            pltpu.VMEM((RET_HEADS, RET_DK, RET_DV), f32),
            pltpu.VMEM((c, vw), bf16),
        ],
        compiler_params=_params(("arbitrary", "arbitrary"), 48),
        name="ret_prompt",
    )(qkvg, qkvg, qkvg, qkvg, x, dmask, dq, dk, wout)


def _ret_sample_body(q_ref, k_ref, v_ref, g_ref, s_ref, og_ref, so_ref, *, gamma):
    q = q_ref[0].astype(f32)
    k = k_ref[0].astype(f32)
    v = v_ref[0].astype(f32)
    g = g_ref[0].astype(f32)
    qw = RET_HEADS * RET_DK
    qcol = jnp.broadcast_to(q, (LANES, qw)).T
    kcol = jnp.broadcast_to(k, (LANES, qw)).T
    for h in range(RET_HEADS):
        ks = slice(h * RET_DK, (h + 1) * RET_DK)
        o_tiles = []
        for t in range(RET_DV // LANES):
            ls = slice(h * RET_DV + t * LANES, h * RET_DV + (t + 1) * LANES)
            cs = slice(t * LANES, (t + 1) * LANES)
            s_new = gamma[h] * s_ref[0, h, :, cs] + kcol[ks, :] * v[:, ls]
            so_ref[0, h, :, cs] = s_new
            o_tiles.append(jnp.sum(qcol[ks, :] * s_new, axis=0, keepdims=True))
        o = jnp.concatenate(o_tiles, axis=1)
        vs = slice(h * RET_DV, (h + 1) * RET_DV)
        og_ref[0, :, vs] = (_rms(o) * _silu(g[:, vs])).astype(bf16)


def _ret_sample(qkvg, s, gamma):
    n = qkvg.shape[0]
    qw = RET_HEADS * RET_DK
    vw = RET_HEADS * RET_DV
    qkvg3 = qkvg.reshape(n, 1, qkvg.shape[1])
    og = pl.pallas_call(
        functools.partial(_ret_sample_body, gamma=gamma),
        grid=(n,),
        in_specs=[
            pl.BlockSpec((1, 1, qw), lambda b: (b, 0, 0)),
            pl.BlockSpec((1, 1, qw), lambda b: (b, 0, 1)),
            pl.BlockSpec((1, 1, vw), lambda b: (b, 0, 1)),
            pl.BlockSpec((1, 1, vw), lambda b: (b, 0, 2)),
            pl.BlockSpec((1,) + s.shape[1:], lambda b: (b, 0, 0, 0)),
        ],
        out_specs=[
            pl.BlockSpec((1, 1, vw), lambda b: (b, 0, 0)),
            pl.BlockSpec((1,) + s.shape[1:], lambda b: (b, 0, 0, 0)),
        ],
        out_shape=[
            jax.ShapeDtypeStruct((n, 1, vw), bf16),
            jax.ShapeDtypeStruct(s.shape, f32),
        ],
        compiler_params=_params(("arbitrary",), 32),
        name="ret_sample",
    )(qkvg3, qkvg3, qkvg3, qkvg3, s)
    return og[0].reshape(n, vw), og[1]


def _rope_tables(pos):
    angle = jnp.repeat(1.0 / (ROPE_BASE ** jnp.linspace(0.0, 1.0, RET_DK // 2, dtype=f32)), 2)
    ang = pos[:, None] * angle[None, :]
    sign = jnp.where(jnp.arange(RET_DK) % 2 == 0, -1.0, 1.0).astype(f32)
    return jnp.cos(ang), jnp.sin(ang) * sign


def _ret_decay_tables(c):
    gam = 1.0 - 2.0 ** (-5.0 - np.arange(RET_HEADS, dtype=np.float64))
    lg = np.log(gam)
    t = np.arange(c, dtype=np.float64)
    diff = t[:, None] - t[None, :]
    dmask = np.where(diff >= 0, np.exp(lg[:, None, None] * diff[None]), 0.0)
    dq = np.exp(lg[:, None] * (t[None, :] + 1.0))[..., None]
    dk = np.exp(lg[:, None] * (c - 1.0 - t[None, :]))[..., None]
    gamma_c = tuple(float(x) for x in np.exp(lg * c))
    gamma = tuple(float(x) for x in gam)
    as32 = lambda a: jnp.asarray(a, dtype=f32)
    return as32(dmask), as32(dq), as32(dk), gamma_c, gamma


def kernel(x_prompt, x_sample, state_gla, state_pool, state_ret, norm_mix, norm_ffn, norm_final, w_in_even,
           w_gate_b, b_gate, gla_gain, pool_w, pool_scale, w_out_even, w_in_odd, w_out_odd, w_ffn_gate,
           w_ffn_up, w_ffn_down):
    batch, seq, d = x_prompt.shape
    n_s = x_sample.shape[0]
    assert norm_mix.shape[0] == 2 and x_sample.shape[1] == 1

    nq = 2 * GLA_HEADS * GLA_DK + 2 * GLA_HEADS * GLA_DV
    we = w_in_even[0]
    we = jnp.concatenate(
        [we[:, :nq], we[:, nq + GATE_RANK:], we[:, nq:nq + GATE_RANK], jnp.zeros((d, LANES - GATE_RANK), f32)],
        axis=1).astype(bf16)
    wgb = jnp.concatenate([w_gate_b[0], jnp.zeros((LANES - GATE_RANK, w_gate_b.shape[2]), f32)], axis=0).astype(bf16)
    bg = b_gate[0][None, :]
    gg = gla_gain[0][None, :]
    pw = pool_w[0].astype(bf16)
    ps = pool_scale[0][None, :]
    woe = w_out_even[0].astype(bf16)
    wio = w_in_odd[0].astype(bf16)
    woo = w_out_odd[0].astype(bf16)
    wg = w_ffn_gate.astype(bf16)
    wu = w_ffn_up.astype(bf16)
    wd = w_ffn_down.astype(bf16)
    nm = norm_mix[:, None, :]
    nf = norm_ffn[:, None, :]
    nfin = norm_final[None, :]
    tril = jnp.asarray(np.tril(np.ones((GLA_CHUNK, GLA_CHUNK), np.float32)))
    dmask, dq, dk, gamma_c, gamma = _ret_decay_tables(RET_CHUNK)
    cos_p, sin_p = _rope_tables(jnp.arange(seq, dtype=f32))
    cos_s, sin_s = _rope_tables(jnp.full((n_s,), float(PAST_LEN), f32))

    tf = w_ffn_gate.shape[2] // 2
    tm_p = 512

    xp = x_prompt.reshape(batch * seq, d)
    qkvg, loga, u_p = _in_even(xp, nm[0], we, wgb, bg, tm=tm_p)
    xp, gla_p = _gla_pool_prompt(qkvg, loga, u_p, xp, tril, gg, pw, ps, woe, batch=batch, t=256)
    xp = _ffn(xp, nf[0], wg[0], wu[0], wd[0], nfin, tm=tm_p, tf=tf, final_norm=False)
    qkvg2 = _in_odd(xp, nm[1], wio, cos_p, sin_p, tm=tm_p, tn=1024)
    xp, ret_p = _ret_prompt(qkvg2, xp, dmask, dq, dk, woo, gamma_c, batch=batch, c=RET_CHUNK)
    y_prompt = _ffn(xp, nf[1], wg[1], wu[1], wd[1], nfin, tm=tm_p, tf=tf, final_norm=True)
    pool_p = u_p.reshape(batch, seq, -1)[:, seq - POOL_BUF:, :]

    xs = x_sample.reshape(n_s, d)
    qkvg_s, loga_s, u_s = _in_even(xs, nm[0], we, wgb, bg, tm=n_s)
    op_s, gla_s = _gla_pool_sample(qkvg_s, loga_s, u_s, state_gla[0], state_pool[0], gg, pw, ps, bb=8)
    xs = _proj_res(xs, op_s, woe, tm=n_s)
    xs = _ffn(xs, nf[0], wg[0], wu[0], wd[0], nfin, tm=n_s, tf=tf, final_norm=False)
    qkvg2_s = _in_odd(xs, nm[1], wio, cos_s, sin_s, tm=n_s, tn=1024)
    og_s, ret_s = _ret_sample(qkvg2_s, state_ret[0], gamma)
    xs = _proj_res(xs, og_s, woo, tm=n_s)
    y_sample = _ffn(xs, nf[1], wg[1], wu[1], wd[1], nfin, tm=n_s, tf=tf, final_norm=True)
    pool_s = jnp.concatenate([state_pool[0][:, 1:, :], u_s[:, None, :]], axis=1)

    return (y_prompt.reshape(batch, seq, d), y_sample.reshape(n_s, 1, d),
            gla_p[None], gla_s[None], pool_p[None], pool_s[None], ret_p[None], ret_s[None])
```

```python
import functools

import numpy as np
import jax
import jax.numpy as jnp
from jax import lax
from jax.experimental import pallas as pl
from jax.experimental.pallas import tpu as pltpu

f32 = jnp.float32
bf16 = jnp.bfloat16

EPS = 1e-6
PAST_LEN = 16384
GLA_HEADS, GLA_DK, GLA_DV = 4, 64, 128
GLA_CHUNK = 64
GATE_RANK = 16
GATE_NORMALIZER = 16.0
POOL_WINDOWS = (2, 4, 8, 16)
POOL_GW = 128
POOL_BUF = max(POOL_WINDOWS) - 1
POOL_HIST = 32
RET_HEADS, RET_DK, RET_DV = 4, 256, 512
RET_CHUNK = 256
ROPE_BASE = 10000.0
LANES = 128
MIB = 1024 * 1024

NT_DIMS = (((1,), (1,)), ((), ()))
TN_DIMS = (((0,), (0,)), ((), ()))


def _params(semantics, vmem_mib):
    return pltpu.CompilerParams(dimension_semantics=semantics, vmem_limit_bytes=vmem_mib * MIB)


def _rms(x, gain=None):
    y = x * lax.rsqrt(jnp.mean(x * x, axis=-1, keepdims=True) + EPS)
    return y if gain is None else y * gain


def _silu(g):
    return g * jax.nn.sigmoid(g)


def _in_even_body(x_ref, gain_ref, w_ref, wgb_ref, bg_ref, qkvg_ref, loga_ref, u_ref):
    nq = qkvg_ref.shape[1]
    nu = u_ref.shape[1]
    h = _rms(x_ref[...], gain_ref[...]).astype(bf16)
    proj = jnp.dot(h, w_ref[...], preferred_element_type=f32)
    qkvg_ref[...] = proj[:, :nq].astype(bf16)
    u_ref[...] = proj[:, nq:nq + nu]
    a = proj[:, nq + nu:].astype(bf16)
    z = jnp.dot(a, wgb_ref[...], preferred_element_type=f32) + bg_ref[...]
    loga_ref[...] = (jnp.minimum(z, 0.0) - jnp.log1p(jnp.exp(-jnp.abs(z)))) * (1.0 / GATE_NORMALIZER)


def _in_even(x, gain, w, wgb, bg, *, tm):
    m, d = x.shape
    nq = 2 * GLA_HEADS * GLA_DK + 2 * GLA_HEADS * GLA_DV
    nu = POOL_GW * len(POOL_WINDOWS)
    nk = GLA_HEADS * GLA_DK
    const = lambda i: (0, 0)
    return pl.pallas_call(
        _in_even_body,
        grid=(m // tm,),
        in_specs=[
            pl.BlockSpec((tm, d), lambda i: (i, 0)),
            pl.BlockSpec((1, d), const),
            pl.BlockSpec(w.shape, const),
            pl.BlockSpec(wgb.shape, const),
            pl.BlockSpec((1, nk), const),
        ],
        out_specs=[
            pl.BlockSpec((tm, nq), lambda i: (i, 0)),
            pl.BlockSpec((tm, nk), lambda i: (i, 0)),
            pl.BlockSpec((tm, nu), lambda i: (i, 0)),
        ],
        out_shape=[
            jax.ShapeDtypeStruct((m, nq), bf16),
            jax.ShapeDtypeStruct((m, nk), f32),
            jax.ShapeDtypeStruct((m, nu), f32),
        ],
        compiler_params=_params(("arbitrary",), 48),
        name="in_even",
    )(x, gain, w, wgb, bg)


def _gla_pool_prompt_body(qkvg_ref, loga_ref, u_ref, x_ref, tril_ref, gain_ref, pw_ref, ps_ref, wout_ref,
                          xo_ref, so_ref, st_ref, o_ref, e_ref, p_ref, q_ref, op_ref):
    t = x_ref.shape[0]
    ck = GLA_CHUNK
    kw = GLA_HEADS * GLA_DK
    vw = GLA_HEADS * GLA_DV
    pair_w = 2 * GLA_DK
    i = pl.program_id(1)

    @pl.when(i == 0)
    def _():
        st_ref[...] = jnp.zeros_like(st_ref)
        e_ref[0:POOL_HIST, :] = jnp.zeros((POOL_HIST, e_ref.shape[1]), f32)

    tril = tril_ref[...]
    row = lax.broadcasted_iota(jnp.int32, (2 * ck, pair_w), 0)
    lane = lax.broadcasted_iota(jnp.int32, (2 * ck, pair_w), 1)
    first_lanes = lane < GLA_DK
    first_lanes_ck = lax.broadcasted_iota(jnp.int32, (ck, pair_w), 1) < GLA_DK
    same_head = (row < ck) == first_lanes
    causal = same_head & ((row % ck) >= (lane % GLA_DK))
    st = [st_ref[p] for p in range(GLA_HEADS // 2)]
    for c in range(t // ck):
        rows = slice(c * ck, (c + 1) * ck)
        la = loga_ref[rows, :]
        la_hi = la.astype(bf16)
        la_lo = (la - la_hi.astype(f32)).astype(bf16)
        bc = jnp.dot(tril, la_hi, preferred_element_type=f32) + jnp.dot(tril, la_lo, preferred_element_type=f32)
        blast = bc[ck - 1:ck, :]
        q = qkvg_ref[rows, 0:kw].astype(f32) * (GLA_DK ** -0.5)
        k = qkvg_ref[rows, kw:2 * kw].astype(f32)
        qe = q * jnp.exp(bc)
        ke = (k * jnp.exp(-bc)).astype(bf16)
        kd = (k * jnp.exp(blast - bc)).astype(bf16)
        elast = jnp.exp(blast)
        for p in range(GLA_HEADS // 2):
            pl_ = slice(p * pair_w, (p + 1) * pair_w)
            qe_p = qe[:, pl_]
            lhs_q = jnp.concatenate([jnp.where(first_lanes_ck, qe_p, 0.0),
                                     jnp.where(first_lanes_ck, 0.0, qe_p)], axis=0).astype(bf16)
            ke_p = ke[:, pl_]
            att = lax.dot_general(lhs_q, jnp.concatenate([ke_p, ke_p], axis=0), NT_DIMS,
                                  preferred_element_type=f32)
            att = jnp.where(causal, att, 0.0).astype(bf16)
            va = qkvg_ref[rows, 2 * kw + (2 * p) * GLA_DV:2 * kw + (2 * p + 1) * GLA_DV]
            vb = qkvg_ref[rows, 2 * kw + (2 * p + 1) * GLA_DV:2 * kw + (2 * p + 2) * GLA_DV]
            o = lax.dot_general(lhs_q, st[p].astype(bf16), NT_DIMS, preferred_element_type=f32)
            o = o + jnp.dot(att, jnp.concatenate([va, vb], axis=0), preferred_element_type=f32)
            o_ref[rows, (2 * p) * GLA_DV:(2 * p + 1) * GLA_DV] = o[:ck]
            o_ref[rows, (2 * p + 1) * GLA_DV:(2 * p + 2) * GLA_DV] = o[ck:]
            r = lax.dot_general(jnp.concatenate([va, vb], axis=1), kd[:, pl_], TN_DIMS, preferred_element_type=f32)
            st[p] = st[p] * elast[:, pl_] + jnp.where(first_lanes, r[:GLA_DV], r[GLA_DV:])
    for p in range(GLA_HEADS // 2):
        st_ref[p] = st[p]

    gain = gain_ref[...]
    for h in range(GLA_HEADS):
        hs = slice(h * GLA_DV, (h + 1) * GLA_DV)
        g = qkvg_ref[:, 2 * kw + vw + h * GLA_DV:2 * kw + vw + (h + 1) * GLA_DV].astype(f32)
        op_ref[:, hs] = (_rms(o_ref[:, hs], gain) * _silu(g)).astype(bf16)

    hist = POOL_HIST
    n = t + hist
    u = u_ref[...]
    e_ref[hist:n, :] = u
    gw = POOL_GW
    p_ref[8:n, :] = e_ref[8:n, :] + e_ref[7:n - 1, :]
    q_ref[16:n, gw:] = p_ref[16:n, gw:] + p_ref[14:n - 2, gw:]
    p_ref[24:n, 2 * gw:] = q_ref[24:n, 2 * gw:] + q_ref[20:n - 4, 2 * gw:]
    q_ref[32:n, 3 * gw:] = p_ref[32:n, 3 * gw:] + p_ref[24:n - 8, 3 * gw:]
    sums = (p_ref, q_ref, p_ref, q_ref)
    pos = i * t + lax.broadcasted_iota(jnp.int32, (t, 1), 0)
    for gi, w in enumerate(POOL_WINDOWS):
        ls = slice(gi * gw, (gi + 1) * gw)
        cnt = jnp.minimum(w, pos + 1).astype(f32)
        pooled = (sums[gi][hist:n, ls] / cnt - u[:, ls]).astype(bf16)
        pg = jnp.dot(pooled, pw_ref[gi], preferred_element_type=f32) * ps_ref[:, ls]
        op_ref[:, vw + gi * gw:vw + (gi + 1) * gw] = pg.astype(bf16)
    e_ref[hist - 16:hist, :] = e_ref[n - 16:n, :]

    xo_ref[...] = x_ref[...] + jnp.dot(op_ref[...], wout_ref[...], preferred_element_type=f32)

    @pl.when(i == pl.num_programs(1) - 1)
    def _():
        for p in range(GLA_HEADS // 2):
            s_pair = st_ref[p].T
            so_ref[0, 2 * p] = s_pair[:GLA_DK]
            so_ref[0, 2 * p + 1] = s_pair[GLA_DK:]


def _gla_pool_prompt(qkvg, loga, u, x, tril, gain, pw, ps, wout, *, batch, t):
    m, d = x.shape
    nt = m // batch // t
    row = lambda b, i: (b * nt + i, 0)
    const2 = lambda b, i: (0, 0)
    vw = GLA_HEADS * GLA_DV
    uw = u.shape[1]
    return pl.pallas_call(
        _gla_pool_prompt_body,
        grid=(batch, nt),
        in_specs=[
            pl.BlockSpec((t, qkvg.shape[1]), row),
            pl.BlockSpec((t, loga.shape[1]), row),
            pl.BlockSpec((t, uw), row),
            pl.BlockSpec((t, d), row),
            pl.BlockSpec(tril.shape, const2),
            pl.BlockSpec(gain.shape, const2),
            pl.BlockSpec(pw.shape, lambda b, i: (0, 0, 0)),
            pl.BlockSpec(ps.shape, const2),
            pl.BlockSpec(wout.shape, const2),
        ],
        out_specs=[
            pl.BlockSpec((t, d), row),
            pl.BlockSpec((1, GLA_HEADS, GLA_DK, GLA_DV), lambda b, i: (b, 0, 0, 0)),
        ],
        out_shape=[
            jax.ShapeDtypeStruct((m, d), f32),
            jax.ShapeDtypeStruct((batch, GLA_HEADS, GLA_DK, GLA_DV), f32),
        ],
        scratch_shapes=[
            pltpu.VMEM((GLA_HEADS // 2, GLA_DV, 2 * GLA_DK), f32),
            pltpu.VMEM((t, vw), f32),
            pltpu.VMEM((POOL_HIST + t, uw), f32),
            pltpu.VMEM((POOL_HIST + t, uw), f32),
            pltpu.VMEM((POOL_HIST + t, uw), f32),
            pltpu.VMEM((t, vw + uw), bf16),
        ],
        compiler_params=_params(("arbitrary", "arbitrary"), 40),
        name="gla_pool_prompt",
    )(qkvg, loga, u, x, tril, gain, pw, ps, wout)


def _gla_pool_sample_body(qkvg_ref, loga_ref, u_ref, s_ref, buf_ref, gain_ref, pw_ref, ps_ref,
                          op_ref, so_ref):
    bb = u_ref.shape[0]
    kw = GLA_HEADS * GLA_DK
    vw = GLA_HEADS * GLA_DV
    gain = gain_ref[...]
    qkvg = qkvg_ref[...].astype(f32)
    alpha = jnp.exp(loga_ref[...])
    qs = qkvg[:, 0:kw] * (GLA_DK ** -0.5)
    k = qkvg[:, kw:2 * kw]

    def column(row):
        return jnp.broadcast_to(row, (LANES, kw)).T

    o_rows = []
    for b in range(bb):
        acol = column(alpha[b:b + 1, :])
        qcol = column(qs[b:b + 1, :])
        kcol = column(k[b:b + 1, :])
        o_heads = []
        for h in range(GLA_HEADS):
            ks = slice(h * GLA_DK, (h + 1) * GLA_DK)
            v = qkvg[b:b + 1, 2 * kw + h * GLA_DV:2 * kw + (h + 1) * GLA_DV]
            s_new = acol[ks, :] * s_ref[b, h] + kcol[ks, :] * v
            so_ref[b, h] = s_new
            o = jnp.sum(qcol[ks, :] * s_new, axis=0, keepdims=True)
            g = qkvg[b:b + 1, 2 * kw + vw + h * GLA_DV:2 * kw + vw + (h + 1) * GLA_DV]
            o_heads.append(_rms(o, gain) * _silu(g))
        o_rows.append(jnp.concatenate(o_heads, axis=1))
    op_ref[:, 0:vw] = jnp.concatenate(o_rows, axis=0).astype(bf16)

    u = u_ref[...]
    for gi, w in enumerate(POOL_WINDOWS):
        ls = slice(gi * POOL_GW, (gi + 1) * POOL_GW)
        s = u[:, ls] + jnp.sum(buf_ref[:, POOL_BUF - (w - 1):POOL_BUF, ls], axis=1)
        cnt = float(min(w, PAST_LEN + 1))
        pooled = (s / cnt - u[:, ls]).astype(bf16)
        pg = jnp.dot(pooled, pw_ref[gi], preferred_element_type=f32) * ps_ref[:, ls]
        op_ref[:, vw + gi * POOL_GW:vw + (gi + 1) * POOL_GW] = pg.astype(bf16)


def _gla_pool_sample(qkvg, loga, u, s, buf, gain, pw, ps, *, bb):
    n = u.shape[0]
    row = lambda i: (i, 0)
    const2 = lambda i: (0, 0)
    ow = GLA_HEADS * GLA_DV + POOL_GW * len(POOL_WINDOWS)
    return pl.pallas_call(
        _gla_pool_sample_body,
        grid=(n // bb,),
        in_specs=[
            pl.BlockSpec((bb, qkvg.shape[1]), row),
            pl.BlockSpec((bb, loga.shape[1]), row),
            pl.BlockSpec((bb, u.shape[1]), row),
            pl.BlockSpec((bb,) + s.shape[1:], lambda i: (i, 0, 0, 0)),
            pl.BlockSpec((bb,) + buf.shape[1:], lambda i: (i, 0, 0)),
            pl.BlockSpec(gain.shape, const2),
            pl.BlockSpec(pw.shape, lambda i: (0, 0, 0)),
            pl.BlockSpec(ps.shape, const2),
        ],
        out_specs=[
            pl.BlockSpec((bb, ow), row),
            pl.BlockSpec((bb,) + s.shape[1:], lambda i: (i, 0, 0, 0)),
        ],
        out_shape=[
            jax.ShapeDtypeStruct((n, ow), bf16),
            jax.ShapeDtypeStruct(s.shape, f32),
        ],
        compiler_params=_params(("arbitrary",), 32),
        name="gla_pool_sample",
    )(qkvg, loga, u, s, buf, gain, pw, ps)


def _proj_res_body(x_ref, a_ref, w_ref, o_ref):
    o_ref[...] = x_ref[...] + jnp.dot(a_ref[...], w_ref[...], preferred_element_type=f32)


def _proj_res(x, a, w, *, tm):
    m, d = x.shape
    return pl.pallas_call(
        _proj_res_body,
        grid=(m // tm,),
        in_specs=[
            pl.BlockSpec((tm, d), lambda i: (i, 0)),
            pl.BlockSpec((tm, a.shape[1]), lambda i: (i, 0)),
            pl.BlockSpec(w.shape, lambda i: (0, 0)),
        ],
        out_specs=pl.BlockSpec((tm, d), lambda i: (i, 0)),
        out_shape=jax.ShapeDtypeStruct((m, d), f32),
        compiler_params=_params(("arbitrary",), 32),
        name="proj_res",
    )(x, a, w)


def _ffn_body(x_ref, gain_ref, wg_ref, wu_ref, wd_ref, fgain_ref, o_ref, h_ref, acc_ref, *, final_norm):
    j = pl.program_id(1)

    @pl.when(j == 0)
    def _():
        h_ref[...] = _rms(x_ref[...], gain_ref[...]).astype(bf16)
        acc_ref[...] = jnp.zeros_like(acc_ref)

    h = h_ref[...]
    g = jnp.dot(h, wg_ref[...], preferred_element_type=f32)
    u = jnp.dot(h, wu_ref[...], preferred_element_type=f32)
    a = (_silu(g) * u).astype(bf16)
    acc_ref[...] += jnp.dot(a, wd_ref[...], preferred_element_type=f32)

    @pl.when(j == pl.num_programs(1) - 1)
    def _():
        y = x_ref[...] + acc_ref[...]
        if final_norm:
            y = _rms(y, fgain_ref[...])
        o_ref[...] = y


def _ffn(x, gain, wg, wu, wd, fgain, *, tm, tf, final_norm):
    m, d = x.shape
    ff = wg.shape[1]
    return pl.pallas_call(
        functools.partial(_ffn_body, final_norm=final_norm),
        grid=(m // tm, ff // tf),
        in_specs=[
            pl.BlockSpec((tm, d), lambda i, j: (i, 0)),
            pl.BlockSpec((1, d), lambda i, j: (0, 0)),
            pl.BlockSpec((d, tf), lambda i, j: (0, j)),
            pl.BlockSpec((d, tf), lambda i, j: (0, j)),
            pl.BlockSpec((tf, d), lambda i, j: (j, 0)),
            pl.BlockSpec((1, d), lambda i, j: (0, 0)),
        ],
        out_specs=pl.BlockSpec((tm, d), lambda i, j: (i, 0)),
        out_shape=jax.ShapeDtypeStruct((m, d), f32),
        scratch_shapes=[pltpu.VMEM((tm, d), bf16), pltpu.VMEM((tm, d), f32)],
        compiler_params=_params(("arbitrary", "arbitrary"), 48),
        name="ffn_final" if final_norm else "ffn",
    )(x, gain, wg, wu, wd, fgain)


def _in_odd_body(x_ref, gain_ref, wqk_ref, wvg_ref, cos_ref, sin_ref, o_ref, h_ref, *, tn, split_halves):
    h_ref[...] = _rms(x_ref[...], gain_ref[...]).astype(bf16)
    qw = RET_HEADS * RET_DK
    half = RET_DK // 2
    cos = cos_ref[...]
    sin = sin_ref[...]
    for c in range(2 * qw // tn):
        c0 = c * tn
        p = jnp.dot(h_ref[...], wqk_ref[:, c0:c0 + tn], preferred_element_type=f32)
        if c0 >= qw:
            p = p * (RET_DK ** -0.5)
        for hh in range(tn // RET_DK):
            h0 = hh * RET_DK
            if split_halves:
                ev = p[:, h0:h0 + half]
                od = p[:, h0 + half:h0 + RET_DK]
                o_ref[:, c0 + h0:c0 + h0 + half] = (ev * cos - od * sin).astype(bf16)
                o_ref[:, c0 + h0 + half:c0 + h0 + RET_DK] = (od * cos + ev * sin).astype(bf16)
            else:
                xh = p[:, h0:h0 + RET_DK]
                even = lax.broadcasted_iota(jnp.int32, xh.shape, 1) % 2 == 0
                partner = jnp.where(even, pltpu.roll(xh, RET_DK - 1, 1), pltpu.roll(xh, 1, 1))
                o_ref[:, c0 + h0:c0 + h0 + RET_DK] = (xh * cos + partner * sin).astype(bf16)
    for c in range(wvg_ref.shape[1] // tn):
        c0 = c * tn
        p = jnp.dot(h_ref[...], wvg_ref[:, c0:c0 + tn], preferred_element_type=f32)
        o_ref[:, 2 * qw + c0:2 * qw + c0 + tn] = p.astype(bf16)


def _in_odd(x, gain, wqk, wvg, cos, sin, *, tm, tn, split_halves):
    m, d = x.shape
    n = wqk.shape[1] + wvg.shape[1]
    ntab = cos.shape[0] // tm
    tw = cos.shape[1]
    resident = dict(pipeline_mode=pl.Buffered(1))
    return pl.pallas_call(
        functools.partial(_in_odd_body, tn=tn, split_halves=split_halves),
        grid=(m // tm,),
        in_specs=[
            pl.BlockSpec((tm, d), lambda i: (i, 0)),
            pl.BlockSpec((1, d), lambda i: (0, 0)),
            pl.BlockSpec(wqk.shape, lambda i: (0, 0), **resident),
            pl.BlockSpec(wvg.shape, lambda i: (0, 0), **resident),
            pl.BlockSpec((tm, tw), lambda i: (i % ntab, 0)),
            pl.BlockSpec((tm, tw), lambda i: (i % ntab, 0)),
        ],
        out_specs=pl.BlockSpec((tm, n), lambda i: (i, 0)),
        out_shape=jax.ShapeDtypeStruct((m, n), bf16),
        scratch_shapes=[pltpu.VMEM((tm, d), bf16)],
        compiler_params=_params(("arbitrary",), 48),
        name="in_odd",
    )(x, gain, wqk, wvg, cos, sin)


def _ret_prompt_body(q_ref, k_ref, v_ref, g_ref, x_ref, dmask_ref, dq_ref, dk_ref, wout_ref,
                     xo_ref, so_ref, s_ref, og_ref, slab_ref, *, gamma_c):
    c = pl.program_id(1)

    @pl.when(c == 0)
    def _():
        s_ref[...] = jnp.zeros_like(s_ref)

    for h in range(RET_HEADS):
        ks = slice(h * RET_DK, (h + 1) * RET_DK)
        vs = slice(h * RET_DV, (h + 1) * RET_DV)
        q = q_ref[:, ks]
        k = k_ref[:, ks]
        v = v_ref[:, vs]
        att = lax.dot_general(q, k, NT_DIMS, preferred_element_type=f32) * dmask_ref[h]
        s = s_ref[h]
        o = jnp.dot(q, s.astype(bf16), preferred_element_type=f32) * dq_ref[h]
        o = o + jnp.dot(att.astype(bf16), v, preferred_element_type=f32)
        kd = (k.astype(f32) * dk_ref[h]).astype(bf16)
        s_ref[h] = gamma_c[h] * s + lax.dot_general(kd, v, TN_DIMS, preferred_element_type=f32)
        g = g_ref[:, vs].astype(f32)
        og_ref[:, vs] = (_rms(o) * _silu(g)).astype(bf16)

    xo_ref[...] = x_ref[...] + jnp.dot(og_ref[...], wout_ref[...], preferred_element_type=f32)

    @pl.when(c == pl.num_programs(1) - 1)
    def _():
        half = RET_DK // 2
        for h in range(RET_HEADS):
            for t in range(RET_DV // LANES):
                ls = slice(t * LANES, (t + 1) * LANES)
                slab_ref[pl.ds(0, half, stride=2), :] = s_ref[h, 0:half, ls]
                slab_ref[pl.ds(1, half, stride=2), :] = s_ref[h, half:RET_DK, ls]
                so_ref[0, h, :, ls] = slab_ref[...]


def _ret_prompt(qkvg, x, dmask, dq, dk, wout, gamma_c, *, batch, c):
    m, d = x.shape
    nc = m // batch // c
    qw = RET_HEADS * RET_DK
    vw = RET_HEADS * RET_DV
    const3 = lambda b, i: (0, 0, 0)
    return pl.pallas_call(
        functools.partial(_ret_prompt_body, gamma_c=gamma_c),
        grid=(batch, nc),
        in_specs=[
            pl.BlockSpec((c, qw), lambda b, i: (b * nc + i, 0)),
            pl.BlockSpec((c, qw), lambda b, i: (b * nc + i, 1)),
            pl.BlockSpec((c, vw), lambda b, i: (b * nc + i, 1)),
            pl.BlockSpec((c, vw), lambda b, i: (b * nc + i, 2)),
            pl.BlockSpec((c, d), lambda b, i: (b * nc + i, 0)),
            pl.BlockSpec(dmask.shape, const3),
            pl.BlockSpec(dq.shape, const3),
            pl.BlockSpec(dk.shape, const3),
            pl.BlockSpec(wout.shape, lambda b, i: (0, 0)),
        ],
        out_specs=[
            pl.BlockSpec((c, d), lambda b, i: (b * nc + i, 0)),
            pl.BlockSpec((1, RET_HEADS, RET_DK, RET_DV), lambda b, i: (b, 0, 0, 0)),
        ],
        out_shape=[
            jax.ShapeDtypeStruct((m, d), f32),
            jax.ShapeDtypeStruct((batch, RET_HEADS, RET_DK, RET_DV), f32),
        ],
        scratch_shapes=[
            pltpu.VMEM((RET_HEADS, RET_DK, RET_DV), f32),
            pltpu.VMEM((c, vw), bf16),
            pltpu.VMEM((RET_DK, LANES), f32),
        ],
        compiler_params=_params(("arbitrary", "arbitrary"), 48),
        name="ret_prompt",
    )(qkvg, qkvg, qkvg, qkvg, x, dmask, dq, dk, wout)


def _ret_sample_body(q_ref, k_ref, v_ref, g_ref, s_ref, og_ref, so_ref, *, gamma):
    q = q_ref[0].astype(f32)
    k = k_ref[0].astype(f32)
    v = v_ref[0].astype(f32)
    g = g_ref[0].astype(f32)
    qw = RET_HEADS * RET_DK
    qcol = jnp.broadcast_to(q, (LANES, qw)).T
    kcol = jnp.broadcast_to(k, (LANES, qw)).T
    for h in range(RET_HEADS):
        ks = slice(h * RET_DK, (h + 1) * RET_DK)
        o_tiles = []
        for t in range(RET_DV // LANES):
            ls = slice(h * RET_DV + t * LANES, h * RET_DV + (t + 1) * LANES)
            cs = slice(t * LANES, (t + 1) * LANES)
            s_new = gamma[h] * s_ref[0, h, :, cs] + kcol[ks, :] * v[:, ls]
            so_ref[0, h, :, cs] = s_new
            o_tiles.append(jnp.sum(qcol[ks, :] * s_new, axis=0, keepdims=True))
        o = jnp.concatenate(o_tiles, axis=1)
        vs = slice(h * RET_DV, (h + 1) * RET_DV)
        og_ref[0, :, vs] = (_rms(o) * _silu(g[:, vs])).astype(bf16)


def _ret_sample(qkvg, s, gamma):
    n = qkvg.shape[0]
    qw = RET_HEADS * RET_DK
    vw = RET_HEADS * RET_DV
    qkvg3 = qkvg.reshape(n, 1, qkvg.shape[1])
    og = pl.pallas_call(
        functools.partial(_ret_sample_body, gamma=gamma),
        grid=(n,),
        in_specs=[
            pl.BlockSpec((1, 1, qw), lambda b: (b, 0, 0)),
            pl.BlockSpec((1, 1, qw), lambda b: (b, 0, 1)),
            pl.BlockSpec((1, 1, vw), lambda b: (b, 0, 1)),
            pl.BlockSpec((1, 1, vw), lambda b: (b, 0, 2)),
            pl.BlockSpec((1,) + s.shape[1:], lambda b: (b, 0, 0, 0)),
        ],
        out_specs=[
            pl.BlockSpec((1, 1, vw), lambda b: (b, 0, 0)),
            pl.BlockSpec((1,) + s.shape[1:], lambda b: (b, 0, 0, 0)),
        ],
        out_shape=[
            jax.ShapeDtypeStruct((n, 1, vw), bf16),
            jax.ShapeDtypeStruct(s.shape, f32),
        ],
        compiler_params=_params(("arbitrary",), 32),
        name="ret_sample",
    )(qkvg3, qkvg3, qkvg3, qkvg3, s)
    return og[0].reshape(n, vw), og[1]


def _rope_tables(pos, per_pair):
    pair_angle = 1.0 / (ROPE_BASE ** jnp.linspace(0.0, 1.0, RET_DK // 2, dtype=f32))
    if per_pair:
        ang = pos[:, None] * pair_angle[None, :]
        return jnp.cos(ang), jnp.sin(ang)
    ang = pos[:, None] * jnp.repeat(pair_angle, 2)[None, :]
    sign = jnp.where(jnp.arange(RET_DK) % 2 == 0, -1.0, 1.0).astype(f32)
    return jnp.cos(ang), jnp.sin(ang) * sign


def _regroup_even_odd(w):
    d, n = w.shape
    return w.reshape(d, n // RET_DK, RET_DK // 2, 2).transpose(0, 1, 3, 2).reshape(d, n)


def _ret_decay_tables(c):
    gam = 1.0 - 2.0 ** (-5.0 - np.arange(RET_HEADS, dtype=np.float64))
    lg = np.log(gam)
    t = np.arange(c, dtype=np.float64)
    diff = t[:, None] - t[None, :]
    dmask = np.where(diff >= 0, np.exp(lg[:, None, None] * diff[None]), 0.0)
    dq = np.exp(lg[:, None] * (t[None, :] + 1.0))[..., None]
    dk = np.exp(lg[:, None] * (c - 1.0 - t[None, :]))[..., None]
    gamma_c = tuple(float(x) for x in np.exp(lg * c))
    gamma = tuple(float(x) for x in gam)
    as32 = lambda a: jnp.asarray(a, dtype=f32)
    return as32(dmask), as32(dq), as32(dk), gamma_c, gamma


def kernel(x_prompt, x_sample, state_gla, state_pool, state_ret, norm_mix, norm_ffn, norm_final, w_in_even,
           w_gate_b, b_gate, gla_gain, pool_w, pool_scale, w_out_even, w_in_odd, w_out_odd, w_ffn_gate,
           w_ffn_up, w_ffn_down):
    batch, seq, d = x_prompt.shape
    n_s = x_sample.shape[0]
    assert norm_mix.shape[0] == 2 and x_sample.shape[1] == 1

    nq = 2 * GLA_HEADS * GLA_DK + 2 * GLA_HEADS * GLA_DV
    we = w_in_even[0]
    we = jnp.concatenate(
        [we[:, :nq], we[:, nq + GATE_RANK:], we[:, nq:nq + GATE_RANK], jnp.zeros((d, LANES - GATE_RANK), f32)],
        axis=1).astype(bf16)
    wgb = jnp.concatenate([w_gate_b[0], jnp.zeros((LANES - GATE_RANK, w_gate_b.shape[2]), f32)], axis=0).astype(bf16)
    bg = b_gate[0][None, :]
    gg = gla_gain[0][None, :]
    pw = pool_w[0].astype(bf16)
    ps = pool_scale[0][None, :]
    woe = w_out_even[0].astype(bf16)
    qkw = 2 * RET_HEADS * RET_DK
    wqk = w_in_odd[0][:, :qkw].astype(bf16)
    wqk_split = _regroup_even_odd(w_in_odd[0][:, :qkw]).astype(bf16)
    wvg = w_in_odd[0][:, qkw:].astype(bf16)
    woo = w_out_odd[0].astype(bf16)
    wg = w_ffn_gate.astype(bf16)
    wu = w_ffn_up.astype(bf16)
    wd = w_ffn_down.astype(bf16)
    nm = norm_mix[:, None, :]
    nf = norm_ffn[:, None, :]
    nfin = norm_final[None, :]
    tril = jnp.asarray(np.tril(np.ones((GLA_CHUNK, GLA_CHUNK), np.float32)), dtype=bf16)
    dmask, dq, dk, gamma_c, gamma = _ret_decay_tables(RET_CHUNK)
    cos_p, sin_p = _rope_tables(jnp.arange(seq, dtype=f32), per_pair=True)
    cos_s, sin_s = _rope_tables(jnp.full((n_s,), float(PAST_LEN), f32), per_pair=False)

    tf = w_ffn_gate.shape[2] // 2
    tm_p = 512

    xp = x_prompt.reshape(batch * seq, d)
    qkvg, loga, u_p = _in_even(xp, nm[0], we, wgb, bg, tm=tm_p)
    xp, gla_p = _gla_pool_prompt(qkvg, loga, u_p, xp, tril, gg, pw, ps, woe, batch=batch, t=512)
    xp = _ffn(xp, nf[0], wg[0], wu[0], wd[0], nfin, tm=tm_p, tf=tf, final_norm=False)
    qkvg2 = _in_odd(xp, nm[1], wqk_split, wvg, cos_p, sin_p, tm=tm_p, tn=512, split_halves=True)
    xp, ret_p = _ret_prompt(qkvg2, xp, dmask, dq, dk, woo, gamma_c, batch=batch, c=RET_CHUNK)
    y_prompt = _ffn(xp, nf[1], wg[1], wu[1], wd[1], nfin, tm=tm_p, tf=tf, final_norm=True)
    pool_p = u_p.reshape(batch, seq, -1)[:, seq - POOL_BUF:, :]

    xs = x_sample.reshape(n_s, d)
    qkvg_s, loga_s, u_s = _in_even(xs, nm[0], we, wgb, bg, tm=n_s)
    op_s, gla_s = _gla_pool_sample(qkvg_s, loga_s, u_s, state_gla[0], state_pool[0], gg, pw, ps, bb=8)
    xs = _proj_res(xs, op_s, woe, tm=n_s)
    xs = _ffn(xs, nf[0], wg[0], wu[0], wd[0], nfin, tm=n_s, tf=tf, final_norm=False)
    qkvg2_s = _in_odd(xs, nm[1], wqk, wvg, cos_s, sin_s, tm=n_s, tn=512, split_halves=False)
    og_s, ret_s = _ret_sample(qkvg2_s, state_ret[0], gamma)
    xs = _proj_res(xs, og_s, woo, tm=n_s)
    y_sample = _ffn(xs, nf[1], wg[1], wu[1], wd[1], nfin, tm=n_s, tf=tf, final_norm=True)
    pool_s = jnp.concatenate([state_pool[0][:, 1:, :], u_s[:, None, :]], axis=1)

    return (y_prompt.reshape(batch, seq, d), y_sample.reshape(n_s, 1, d),
            gla_p[None], gla_s[None], pool_p[None], pool_s[None], ret_p[None], ret_s[None])
```

```python
import functools

import numpy as np
import jax
import jax.numpy as jnp
from jax import lax
from jax.experimental import pallas as pl
from jax.experimental.pallas import tpu as pltpu

f32 = jnp.float32
bf16 = jnp.bfloat16

EPS = 1e-6
PAST_LEN = 16384
GLA_HEADS, GLA_DK, GLA_DV = 4, 64, 128
GLA_CHUNK = 64
GATE_RANK = 16
GATE_NORMALIZER = 16.0
POOL_WINDOWS = (2, 4, 8, 16)
POOL_GW = 128
POOL_BUF = max(POOL_WINDOWS) - 1
POOL_HIST = 32
RET_HEADS, RET_DK, RET_DV = 4, 256, 512
RET_CHUNK = 256
ROPE_BASE = 10000.0
LANES = 128
MIB = 1024 * 1024

NT_DIMS = (((1,), (1,)), ((), ()))
TN_DIMS = (((0,), (0,)), ((), ()))


def _params(semantics, vmem_mib):
    return pltpu.CompilerParams(dimension_semantics=semantics, vmem_limit_bytes=vmem_mib * MIB)


def _rms(x, gain=None):
    y = x * lax.rsqrt(jnp.mean(x * x, axis=-1, keepdims=True) + EPS)
    return y if gain is None else y * gain


def _silu(g):
    return g * jax.nn.sigmoid(g)


def _in_even_body(x_ref, gain_ref, w_ref, wgb_ref, bg_ref, qkvg_ref, loga_ref, u_ref):
    nq = qkvg_ref.shape[1]
    nu = u_ref.shape[1]
    h = _rms(x_ref[...], gain_ref[...]).astype(bf16)
    proj = jnp.dot(h, w_ref[...], preferred_element_type=f32)
    qkvg_ref[...] = proj[:, :nq].astype(bf16)
    u_ref[...] = proj[:, nq:nq + nu]
    a = proj[:, nq + nu:].astype(bf16)
    z = jnp.dot(a, wgb_ref[...], preferred_element_type=f32) + bg_ref[...]
    loga_ref[...] = (jnp.minimum(z, 0.0) - jnp.log1p(jnp.exp(-jnp.abs(z)))) * (1.0 / GATE_NORMALIZER)


def _in_even(x, gain, w, wgb, bg, *, tm):
    m, d = x.shape
    nq = 2 * GLA_HEADS * GLA_DK + 2 * GLA_HEADS * GLA_DV
    nu = POOL_GW * len(POOL_WINDOWS)
    nk = GLA_HEADS * GLA_DK
    const = lambda i: (0, 0)
    return pl.pallas_call(
        _in_even_body,
        grid=(m // tm,),
        in_specs=[
            pl.BlockSpec((tm, d), lambda i: (i, 0)),
            pl.BlockSpec((1, d), const),
            pl.BlockSpec(w.shape, const),
            pl.BlockSpec(wgb.shape, const),
            pl.BlockSpec((1, nk), const),
        ],
        out_specs=[
            pl.BlockSpec((tm, nq), lambda i: (i, 0)),
            pl.BlockSpec((tm, nk), lambda i: (i, 0)),
            pl.BlockSpec((tm, nu), lambda i: (i, 0)),
        ],
        out_shape=[
            jax.ShapeDtypeStruct((m, nq), bf16),
            jax.ShapeDtypeStruct((m, nk), f32),
            jax.ShapeDtypeStruct((m, nu), f32),
        ],
        compiler_params=_params(("arbitrary",), 48),
        name="in_even",
    )(x, gain, w, wgb, bg)


def _gla_pool_prompt_body(qkvg_ref, loga_ref, u_ref, x_ref, tril_ref, gain_ref, pw_ref, ps_ref, wout_ref,
                          xo_ref, so_ref, st_ref, o_ref, e_ref, p_ref, q_ref, op_ref):
    t = x_ref.shape[0]
    ck = GLA_CHUNK
    kw = GLA_HEADS * GLA_DK
    vw = GLA_HEADS * GLA_DV
    pair_w = 2 * GLA_DK
    i = pl.program_id(1)

    @pl.when(i == 0)
    def _():
        st_ref[...] = jnp.zeros_like(st_ref)
        e_ref[0:POOL_HIST, :] = jnp.zeros((POOL_HIST, e_ref.shape[1]), f32)

    tril = tril_ref[...]
    row = lax.broadcasted_iota(jnp.int32, (2 * ck, pair_w), 0)
    lane = lax.broadcasted_iota(jnp.int32, (2 * ck, pair_w), 1)
    first_lanes = lane < GLA_DK
    first_lanes_ck = lax.broadcasted_iota(jnp.int32, (ck, pair_w), 1) < GLA_DK
    same_head = (row < ck) == first_lanes
    causal = same_head & ((row % ck) >= (lane % GLA_DK))
    st = [st_ref[p] for p in range(GLA_HEADS // 2)]
    for c in range(t // ck):
        rows = slice(c * ck, (c + 1) * ck)
        la = loga_ref[rows, :]
        la_hi = la.astype(bf16)
        la_lo = (la - la_hi.astype(f32)).astype(bf16)
        bc = jnp.dot(tril, la_hi, preferred_element_type=f32) + jnp.dot(tril, la_lo, preferred_element_type=f32)
        blast = bc[ck - 1:ck, :]
        q = qkvg_ref[rows, 0:kw].astype(f32) * (GLA_DK ** -0.5)
        k = qkvg_ref[rows, kw:2 * kw].astype(f32)
        qe = q * jnp.exp(bc)
        ke = (k * jnp.exp(-bc)).astype(bf16)
        kd = (k * jnp.exp(blast - bc)).astype(bf16)
        elast = jnp.exp(blast)
        for p in range(GLA_HEADS // 2):
            pl_ = slice(p * pair_w, (p + 1) * pair_w)
            qe_p = qe[:, pl_]
            lhs_q = jnp.concatenate([jnp.where(first_lanes_ck, qe_p, 0.0),
                                     jnp.where(first_lanes_ck, 0.0, qe_p)], axis=0).astype(bf16)
            ke_p = ke[:, pl_]
            att = lax.dot_general(lhs_q, jnp.concatenate([ke_p, ke_p], axis=0), NT_DIMS,
                                  preferred_element_type=f32)
            att = jnp.where(causal, att, 0.0).astype(bf16)
            va = qkvg_ref[rows, 2 * kw + (2 * p) * GLA_DV:2 * kw + (2 * p + 1) * GLA_DV]
            vb = qkvg_ref[rows, 2 * kw + (2 * p + 1) * GLA_DV:2 * kw + (2 * p + 2) * GLA_DV]
            o = lax.dot_general(lhs_q, st[p].astype(bf16), NT_DIMS, preferred_element_type=f32)
            o = o + jnp.dot(att, jnp.concatenate([va, vb], axis=0), preferred_element_type=f32)
            o_ref[rows, (2 * p) * GLA_DV:(2 * p + 1) * GLA_DV] = o[:ck]
            o_ref[rows, (2 * p + 1) * GLA_DV:(2 * p + 2) * GLA_DV] = o[ck:]
            r = lax.dot_general(jnp.concatenate([va, vb], axis=1), kd[:, pl_], TN_DIMS, preferred_element_type=f32)
            st[p] = st[p] * elast[:, pl_] + jnp.where(first_lanes, r[:GLA_DV], r[GLA_DV:])
    for p in range(GLA_HEADS // 2):
        st_ref[p] = st[p]

    gain = gain_ref[...]
    for h in range(GLA_HEADS):
        hs = slice(h * GLA_DV, (h + 1) * GLA_DV)
        g = qkvg_ref[:, 2 * kw + vw + h * GLA_DV:2 * kw + vw + (h + 1) * GLA_DV].astype(f32)
        op_ref[:, hs] = (_rms(o_ref[:, hs], gain) * _silu(g)).astype(bf16)

    hist = POOL_HIST
    n = t + hist
    u = u_ref[...]
    e_ref[hist:n, :] = u
    gw = POOL_GW
    p_ref[8:n, :] = e_ref[8:n, :] + e_ref[7:n - 1, :]
    q_ref[16:n, gw:] = p_ref[16:n, gw:] + p_ref[14:n - 2, gw:]
    p_ref[24:n, 2 * gw:] = q_ref[24:n, 2 * gw:] + q_ref[20:n - 4, 2 * gw:]
    q_ref[32:n, 3 * gw:] = p_ref[32:n, 3 * gw:] + p_ref[24:n - 8, 3 * gw:]
    sums = (p_ref, q_ref, p_ref, q_ref)
    pos = i * t + lax.broadcasted_iota(jnp.int32, (t, 1), 0)
    for gi, w in enumerate(POOL_WINDOWS):
        ls = slice(gi * gw, (gi + 1) * gw)
        cnt = jnp.minimum(w, pos + 1).astype(f32)
        pooled = (sums[gi][hist:n, ls] / cnt - u[:, ls]).astype(bf16)
        pg = jnp.dot(pooled, pw_ref[gi], preferred_element_type=f32) * ps_ref[:, ls]
        op_ref[:, vw + gi * gw:vw + (gi + 1) * gw] = pg.astype(bf16)
    e_ref[hist - 16:hist, :] = e_ref[n - 16:n, :]

    xo_ref[...] = x_ref[...] + jnp.dot(op_ref[...], wout_ref[...], preferred_element_type=f32)

    @pl.when(i == pl.num_programs(1) - 1)
    def _():
        for p in range(GLA_HEADS // 2):
            s_pair = st_ref[p].T
            so_ref[0, 2 * p] = s_pair[:GLA_DK]
            so_ref[0, 2 * p + 1] = s_pair[GLA_DK:]


def _gla_pool_prompt(qkvg, loga, u, x, tril, gain, pw, ps, wout, *, batch, t):
    m, d = x.shape
    nt = m // batch // t
    row = lambda b, i: (b * nt + i, 0)
    const2 = lambda b, i: (0, 0)
    vw = GLA_HEADS * GLA_DV
    uw = u.shape[1]
    return pl.pallas_call(
        _gla_pool_prompt_body,
        grid=(batch, nt),
        in_specs=[
            pl.BlockSpec((t, qkvg.shape[1]), row),
            pl.BlockSpec((t, loga.shape[1]), row),
            pl.BlockSpec((t, uw), row),
            pl.BlockSpec((t, d), row),
            pl.BlockSpec(tril.shape, const2),
            pl.BlockSpec(gain.shape, const2),
            pl.BlockSpec(pw.shape, lambda b, i: (0, 0, 0)),
            pl.BlockSpec(ps.shape, const2),
            pl.BlockSpec(wout.shape, const2),
        ],
        out_specs=[
            pl.BlockSpec((t, d), row),
            pl.BlockSpec((1, GLA_HEADS, GLA_DK, GLA_DV), lambda b, i: (b, 0, 0, 0)),
        ],
        out_shape=[
            jax.ShapeDtypeStruct((m, d), f32),
            jax.ShapeDtypeStruct((batch, GLA_HEADS, GLA_DK, GLA_DV), f32),
        ],
        scratch_shapes=[
            pltpu.VMEM((GLA_HEADS // 2, GLA_DV, 2 * GLA_DK), f32),
            pltpu.VMEM((t, vw), f32),
            pltpu.VMEM((POOL_HIST + t, uw), f32),
            pltpu.VMEM((POOL_HIST + t, uw), f32),
            pltpu.VMEM((POOL_HIST + t, uw), f32),
            pltpu.VMEM((t, vw + uw), bf16),
        ],
        compiler_params=_params(("arbitrary", "arbitrary"), 40),
        name="gla_pool_prompt",
    )(qkvg, loga, u, x, tril, gain, pw, ps, wout)


def _gla_pool_sample_body(qkvg_ref, loga_ref, u_ref, s_ref, buf_ref, gain_ref, pw_ref, ps_ref,
                          op_ref, so_ref):
    bb = u_ref.shape[0]
    kw = GLA_HEADS * GLA_DK
    vw = GLA_HEADS * GLA_DV
    gain = gain_ref[...]
    qkvg = qkvg_ref[...].astype(f32)
    alpha = jnp.exp(loga_ref[...])
    qs = qkvg[:, 0:kw] * (GLA_DK ** -0.5)
    k = qkvg[:, kw:2 * kw]

    def column(row):
        return jnp.broadcast_to(row, (LANES, kw)).T

    o_rows = []
    for b in range(bb):
        acol = column(alpha[b:b + 1, :])
        qcol = column(qs[b:b + 1, :])
        kcol = column(k[b:b + 1, :])
        o_heads = []
        for h in range(GLA_HEADS):
            ks = slice(h * GLA_DK, (h + 1) * GLA_DK)
            v = qkvg[b:b + 1, 2 * kw + h * GLA_DV:2 * kw + (h + 1) * GLA_DV]
            s_new = acol[ks, :] * s_ref[b, h] + kcol[ks, :] * v
            so_ref[b, h] = s_new
            o = jnp.sum(qcol[ks, :] * s_new, axis=0, keepdims=True)
            g = qkvg[b:b + 1, 2 * kw + vw + h * GLA_DV:2 * kw + vw + (h + 1) * GLA_DV]
            o_heads.append(_rms(o, gain) * _silu(g))
        o_rows.append(jnp.concatenate(o_heads, axis=1))
    op_ref[:, 0:vw] = jnp.concatenate(o_rows, axis=0).astype(bf16)

    u = u_ref[...]
    for gi, w in enumerate(POOL_WINDOWS):
        ls = slice(gi * POOL_GW, (gi + 1) * POOL_GW)
        s = u[:, ls] + jnp.sum(buf_ref[:, POOL_BUF - (w - 1):POOL_BUF, ls], axis=1)
        cnt = float(min(w, PAST_LEN + 1))
        pooled = (s / cnt - u[:, ls]).astype(bf16)
        pg = jnp.dot(pooled, pw_ref[gi], preferred_element_type=f32) * ps_ref[:, ls]
        op_ref[:, vw + gi * POOL_GW:vw + (gi + 1) * POOL_GW] = pg.astype(bf16)


def _gla_pool_sample(qkvg, loga, u, s, buf, gain, pw, ps, *, bb):
    n = u.shape[0]
    row = lambda i: (i, 0)
    const2 = lambda i: (0, 0)
    ow = GLA_HEADS * GLA_DV + POOL_GW * len(POOL_WINDOWS)
    return pl.pallas_call(
        _gla_pool_sample_body,
        grid=(n // bb,),
        in_specs=[
            pl.BlockSpec((bb, qkvg.shape[1]), row),
            pl.BlockSpec((bb, loga.shape[1]), row),
            pl.BlockSpec((bb, u.shape[1]), row),
            pl.BlockSpec((bb,) + s.shape[1:], lambda i: (i, 0, 0, 0)),
            pl.BlockSpec((bb,) + buf.shape[1:], lambda i: (i, 0, 0)),
            pl.BlockSpec(gain.shape, const2),
            pl.BlockSpec(pw.shape, lambda i: (0, 0, 0)),
            pl.BlockSpec(ps.shape, const2),
        ],
        out_specs=[
            pl.BlockSpec((bb, ow), row),
            pl.BlockSpec((bb,) + s.shape[1:], lambda i: (i, 0, 0, 0)),
        ],
        out_shape=[
            jax.ShapeDtypeStruct((n, ow), bf16),
            jax.ShapeDtypeStruct(s.shape, f32),
        ],
        compiler_params=_params(("arbitrary",), 32),
        name="gla_pool_sample",
    )(qkvg, loga, u, s, buf, gain, pw, ps)


def _proj_res_body(x_ref, a_ref, w_ref, o_ref):
    o_ref[...] = x_ref[...] + jnp.dot(a_ref[...], w_ref[...], preferred_element_type=f32)


def _proj_res(x, a, w, *, tm):
    m, d = x.shape
    return pl.pallas_call(
        _proj_res_body,
        grid=(m // tm,),
        in_specs=[
            pl.BlockSpec((tm, d), lambda i: (i, 0)),
            pl.BlockSpec((tm, a.shape[1]), lambda i: (i, 0)),
            pl.BlockSpec(w.shape, lambda i: (0, 0)),
        ],
        out_specs=pl.BlockSpec((tm, d), lambda i: (i, 0)),
        out_shape=jax.ShapeDtypeStruct((m, d), f32),
        compiler_params=_params(("arbitrary",), 32),
        name="proj_res",
    )(x, a, w)


def _ffn_body(x_ref, gain_ref, wg_ref, wu_ref, wd_ref, fgain_ref, o_ref, h_ref, acc_ref, *, tf, final_norm):
    h_ref[...] = _rms(x_ref[...], gain_ref[...]).astype(bf16)
    for c in range(wg_ref.shape[1] // tf):
        cs = slice(c * tf, (c + 1) * tf)
        g = jnp.dot(h_ref[...], wg_ref[:, cs], preferred_element_type=f32)
        u = jnp.dot(h_ref[...], wu_ref[:, cs], preferred_element_type=f32)
        a = (_silu(g) * u).astype(bf16)
        part = jnp.dot(a, wd_ref[cs, :], preferred_element_type=f32)
        if c == 0:
            acc_ref[...] = part
        else:
            acc_ref[...] += part
    y = x_ref[...] + acc_ref[...]
    if final_norm:
        y = _rms(y, fgain_ref[...])
    o_ref[...] = y


def _ffn(x, gain, wg, wu, wd, fgain, *, layer, tm, tf, final_norm):
    m, d = x.shape
    ff = wg.shape[2]
    resident = dict(pipeline_mode=pl.Buffered(1))
    return pl.pallas_call(
        functools.partial(_ffn_body, tf=tf, final_norm=final_norm),
        grid=(m // tm,),
        in_specs=[
            pl.BlockSpec((tm, d), lambda i: (i, 0)),
            pl.BlockSpec((None, 1, d), lambda i: (layer, 0, 0)),
            pl.BlockSpec((None, d, ff), lambda i: (layer, 0, 0), **resident),
            pl.BlockSpec((None, d, ff), lambda i: (layer, 0, 0), **resident),
            pl.BlockSpec((None, ff, d), lambda i: (layer, 0, 0), **resident),
            pl.BlockSpec((1, d), lambda i: (0, 0)),
        ],
        out_specs=pl.BlockSpec((tm, d), lambda i: (i, 0)),
        out_shape=jax.ShapeDtypeStruct((m, d), f32),
        scratch_shapes=[pltpu.VMEM((tm, d), bf16), pltpu.VMEM((tm, d), f32)],
        compiler_params=_params(("arbitrary",), 48),
        name="ffn_final" if final_norm else "ffn",
    )(x, gain, wg, wu, wd, fgain)


def _in_odd_body(x_ref, gain_ref, wqk_ref, wvg_ref, qcos_ref, qsin_ref, kcos_ref, ksin_ref, o_ref, h_ref, *,
                 tn, split_halves):
    h_ref[...] = _rms(x_ref[...], gain_ref[...]).astype(bf16)
    qw = RET_HEADS * RET_DK
    half = RET_DK // 2
    tw = qcos_ref.shape[1] // RET_HEADS
    for c in range(2 * qw // tn):
        c0 = c * tn
        p = jnp.dot(h_ref[...], wqk_ref[:, c0:c0 + tn], preferred_element_type=f32)
        cos_ref, sin_ref = (qcos_ref, qsin_ref) if c0 < qw else (kcos_ref, ksin_ref)
        for hh in range(tn // RET_DK):
            h0 = hh * RET_DK
            head = (c0 % qw + h0) // RET_DK
            cos = cos_ref[:, head * tw:(head + 1) * tw]
            sin = sin_ref[:, head * tw:(head + 1) * tw]
            if split_halves:
                ev = p[:, h0:h0 + half]
                od = p[:, h0 + half:h0 + RET_DK]
                o_ref[:, c0 + h0:c0 + h0 + half] = (ev * cos - od * sin).astype(bf16)
                o_ref[:, c0 + h0 + half:c0 + h0 + RET_DK] = (od * cos + ev * sin).astype(bf16)
            else:
                xh = p[:, h0:h0 + RET_DK]
                even = lax.broadcasted_iota(jnp.int32, xh.shape, 1) % 2 == 0
                partner = jnp.where(even, pltpu.roll(xh, RET_DK - 1, 1), pltpu.roll(xh, 1, 1))
                o_ref[:, c0 + h0:c0 + h0 + RET_DK] = (xh * cos + partner * sin).astype(bf16)
    for c in range(wvg_ref.shape[1] // tn):
        c0 = c * tn
        p = jnp.dot(h_ref[...], wvg_ref[:, c0:c0 + tn], preferred_element_type=f32)
        o_ref[:, 2 * qw + c0:2 * qw + c0 + tn] = p.astype(bf16)


def _in_odd(x, gain, wqk, wvg, tables, *, layer, tm, tn, split_halves):
    m, d = x.shape
    n = wqk.shape[1] + wvg.shape[1]
    ntab = tables[0].shape[0] // tm
    tw = tables[0].shape[1]
    resident = dict(pipeline_mode=pl.Buffered(1))
    table_spec = pl.BlockSpec((tm, tw), lambda i: (i % ntab, 0))
    return pl.pallas_call(
        functools.partial(_in_odd_body, tn=tn, split_halves=split_halves),
        grid=(m // tm,),
        in_specs=[
            pl.BlockSpec((tm, d), lambda i: (i, 0)),
            pl.BlockSpec((None, 1, d), lambda i: (layer, 0, 0)),
            pl.BlockSpec(wqk.shape, lambda i: (0, 0), **resident),
            pl.BlockSpec(wvg.shape, lambda i: (0, 0), **resident),
            table_spec, table_spec, table_spec, table_spec,
        ],
        out_specs=pl.BlockSpec((tm, n), lambda i: (i, 0)),
        out_shape=jax.ShapeDtypeStruct((m, n), bf16),
        scratch_shapes=[pltpu.VMEM((tm, d), bf16)],
        compiler_params=_params(("arbitrary",), 56),
        name="in_odd",
    )(x, gain, wqk, wvg, *tables)


def _ret_prompt_body(q_ref, k_ref, v_ref, g_ref, x_ref, wout_ref, xo_ref, so_ref, s_ref, sb_ref, slab_ref, *,
                     gamma_c):
    c = pl.program_id(1)
    n = q_ref.shape[0]

    @pl.when(c == 0)
    def _():
        s_ref[...] = jnp.zeros_like(s_ref)
        sb_ref[...] = jnp.zeros_like(sb_ref)

    causal = lax.broadcasted_iota(jnp.int32, (n, n), 0) >= lax.broadcasted_iota(jnp.int32, (n, n), 1)
    heads = range(RET_HEADS)
    ks = [slice(h * RET_DK, (h + 1) * RET_DK) for h in heads]
    vs = [slice(h * RET_DV, (h + 1) * RET_DV) for h in heads]
    att = [jnp.where(causal, lax.dot_general(q_ref[:, ks[h]], k_ref[:, ks[h]], NT_DIMS,
                                             preferred_element_type=f32), 0.0).astype(bf16) for h in heads]
    o = [jnp.dot(q_ref[:, ks[h]], sb_ref[h], preferred_element_type=f32)
         + jnp.dot(att[h], v_ref[:, vs[h]], preferred_element_type=f32) for h in heads]
    for h in heads:
        kv = lax.dot_general(k_ref[:, ks[h]], v_ref[:, vs[h]], TN_DIMS, preferred_element_type=f32)
        s_new = gamma_c[h] * (s_ref[h] + kv)
        s_ref[h] = s_new
        sb_ref[h] = s_new.astype(bf16)
    y = x_ref[...]
    for h in heads:
        og = (_rms(o[h]) * _silu(g_ref[:, vs[h]].astype(f32))).astype(bf16)
        y = y + jnp.dot(og, wout_ref[vs[h], :], preferred_element_type=f32)
    xo_ref[...] = y

    @pl.when(c == pl.num_programs(1) - 1)
    def _():
        half = RET_DK // 2
        for h in range(RET_HEADS):
            for t in range(RET_DV // LANES):
                ls = slice(t * LANES, (t + 1) * LANES)
                slab_ref[pl.ds(0, half, stride=2), :] = s_ref[h, 0:half, ls]
                slab_ref[pl.ds(1, half, stride=2), :] = s_ref[h, half:RET_DK, ls]
                so_ref[0, h, :, ls] = slab_ref[...]


def _ret_prompt(qkvg, x, wout, gamma_c, *, batch, c):
    m, d = x.shape
    nc = m // batch // c
    qw = RET_HEADS * RET_DK
    vw = RET_HEADS * RET_DV
    return pl.pallas_call(
        functools.partial(_ret_prompt_body, gamma_c=gamma_c),
        grid=(batch, nc),
        in_specs=[
            pl.BlockSpec((c, qw), lambda b, i: (b * nc + i, 0)),
            pl.BlockSpec((c, qw), lambda b, i: (b * nc + i, 1)),
            pl.BlockSpec((c, vw), lambda b, i: (b * nc + i, 1)),
            pl.BlockSpec((c, vw), lambda b, i: (b * nc + i, 2)),
            pl.BlockSpec((c, d), lambda b, i: (b * nc + i, 0)),
            pl.BlockSpec(wout.shape, lambda b, i: (0, 0), pipeline_mode=pl.Buffered(1)),
        ],
        out_specs=[
            pl.BlockSpec((c, d), lambda b, i: (b * nc + i, 0)),
            pl.BlockSpec((1, RET_HEADS, RET_DK, RET_DV), lambda b, i: (b, 0, 0, 0)),
        ],
        out_shape=[
            jax.ShapeDtypeStruct((m, d), f32),
            jax.ShapeDtypeStruct((batch, RET_HEADS, RET_DK, RET_DV), f32),
        ],
        scratch_shapes=[
            pltpu.VMEM((RET_HEADS, RET_DK, RET_DV), f32),
            pltpu.VMEM((RET_HEADS, RET_DK, RET_DV), bf16),
            pltpu.VMEM((RET_DK, LANES), f32),
        ],
        compiler_params=_params(("arbitrary", "arbitrary"), 48),
        name="ret_prompt",
    )(qkvg, qkvg, qkvg, qkvg, x, wout)


def _ret_sample_body(q_ref, k_ref, v_ref, g_ref, s_ref, og_ref, so_ref, *, gamma):
    q = q_ref[0].astype(f32)
    k = k_ref[0].astype(f32)
    v = v_ref[0].astype(f32)
    g = g_ref[0].astype(f32)
    qw = RET_HEADS * RET_DK
    qcol = jnp.broadcast_to(q, (LANES, qw)).T
    kcol = jnp.broadcast_to(k, (LANES, qw)).T
    for h in range(RET_HEADS):
        ks = slice(h * RET_DK, (h + 1) * RET_DK)
        o_tiles = []
        for t in range(RET_DV // LANES):
            ls = slice(h * RET_DV + t * LANES, h * RET_DV + (t + 1) * LANES)
            cs = slice(t * LANES, (t + 1) * LANES)
            s_new = gamma[h] * s_ref[0, h, :, cs] + kcol[ks, :] * v[:, ls]
            so_ref[0, h, :, cs] = s_new
            o_tiles.append(jnp.sum(qcol[ks, :] * s_new, axis=0, keepdims=True))
        o = jnp.concatenate(o_tiles, axis=1)
        vs = slice(h * RET_DV, (h + 1) * RET_DV)
        og_ref[0, :, vs] = (_rms(o) * _silu(g[:, vs])).astype(bf16)


def _ret_sample(qkvg, s, gamma):
    n = qkvg.shape[0]
    qw = RET_HEADS * RET_DK
    vw = RET_HEADS * RET_DV
    qkvg3 = qkvg.reshape(n, 1, qkvg.shape[1])
    og = pl.pallas_call(
        functools.partial(_ret_sample_body, gamma=gamma),
        grid=(n,),
        in_specs=[
            pl.BlockSpec((1, 1, qw), lambda b: (b, 0, 0)),
            pl.BlockSpec((1, 1, qw), lambda b: (b, 0, 1)),
            pl.BlockSpec((1, 1, vw), lambda b: (b, 0, 1)),
            pl.BlockSpec((1, 1, vw), lambda b: (b, 0, 2)),
            pl.BlockSpec((1,) + s.shape[1:], lambda b: (b, 0, 0, 0)),
        ],
        out_specs=[
            pl.BlockSpec((1, 1, vw), lambda b: (b, 0, 0)),
            pl.BlockSpec((1,) + s.shape[1:], lambda b: (b, 0, 0, 0)),
        ],
        out_shape=[
            jax.ShapeDtypeStruct((n, 1, vw), bf16),
            jax.ShapeDtypeStruct(s.shape, f32),
        ],
        compiler_params=_params(("arbitrary",), 32),
        name="ret_sample",
    )(qkvg3, qkvg3, qkvg3, qkvg3, s)
    return og[0].reshape(n, vw), og[1]


def _rope_tables(pos, per_pair, q_scale, k_scale):
    pair_angle = 1.0 / (ROPE_BASE ** jnp.linspace(0.0, 1.0, RET_DK // 2, dtype=f32))
    if per_pair:
        ang = pos[:, None] * pair_angle[None, :]
        cos, sin = jnp.cos(ang), jnp.sin(ang)
    else:
        ang = pos[:, None] * jnp.repeat(pair_angle, 2)[None, :]
        sign = jnp.where(jnp.arange(RET_DK) % 2 == 0, -1.0, 1.0).astype(f32)
        cos, sin = jnp.cos(ang), jnp.sin(ang) * sign

    def per_head(table, scale):
        return (table[:, None, :] * scale[:, :, None]).reshape(table.shape[0], -1)

    return per_head(cos, q_scale), per_head(sin, q_scale), per_head(cos, k_scale), per_head(sin, k_scale)


def _regroup_even_odd(w):
    d, n = w.shape
    return w.reshape(d, n // RET_DK, RET_DK // 2, 2).transpose(0, 1, 3, 2).reshape(d, n)


def _ret_decay(seq, c):
    gam = 1.0 - 2.0 ** (-5.0 - np.arange(RET_HEADS, dtype=np.float64))
    lg = np.log(gam)
    steps = (np.arange(seq) % c + 1.0)[:, None]
    q_scale = jnp.asarray(np.exp(lg[None, :] * steps), dtype=f32)
    k_scale = jnp.asarray(np.exp(-lg[None, :] * steps) * RET_DK ** -0.5, dtype=f32)
    gamma_c = tuple(float(x) for x in np.exp(lg * c))
    gamma = tuple(float(x) for x in gam)
    return q_scale, k_scale, gamma_c, gamma


def kernel(x_prompt, x_sample, state_gla, state_pool, state_ret, norm_mix, norm_ffn, norm_final, w_in_even,
           w_gate_b, b_gate, gla_gain, pool_w, pool_scale, w_out_even, w_in_odd, w_out_odd, w_ffn_gate,
           w_ffn_up, w_ffn_down):
    batch, seq, d = x_prompt.shape
    n_s = x_sample.shape[0]
    assert norm_mix.shape[0] == 2 and x_sample.shape[1] == 1

    nq = 2 * GLA_HEADS * GLA_DK + 2 * GLA_HEADS * GLA_DV
    we = w_in_even[0]
    we = jnp.concatenate(
        [we[:, :nq], we[:, nq + GATE_RANK:], we[:, nq:nq + GATE_RANK], jnp.zeros((d, LANES - GATE_RANK), f32)],
        axis=1).astype(bf16)
    wgb = jnp.concatenate([w_gate_b[0], jnp.zeros((LANES - GATE_RANK, w_gate_b.shape[2]), f32)], axis=0).astype(bf16)
    bg = b_gate[0][None, :]
    gg = gla_gain[0][None, :]
    pw = pool_w[0].astype(bf16)
    ps = pool_scale[0][None, :]
    woe = w_out_even[0].astype(bf16)
    qkw = 2 * RET_HEADS * RET_DK
    wqk = w_in_odd[0][:, :qkw].astype(bf16)
    wqk_split = _regroup_even_odd(w_in_odd[0][:, :qkw]).astype(bf16)
    wvg = w_in_odd[0][:, qkw:].astype(bf16)
    woo = w_out_odd[0].astype(bf16)
    wg = w_ffn_gate.astype(bf16)
    wu = w_ffn_up.astype(bf16)
    wd = w_ffn_down.astype(bf16)
    nm = norm_mix[:, None, :]
    nf = norm_ffn[:, None, :]
    nfin = norm_final[None, :]
    tril = jnp.asarray(np.tril(np.ones((GLA_CHUNK, GLA_CHUNK), np.float32)), dtype=bf16)
    q_scale, k_scale, gamma_c, gamma = _ret_decay(seq, RET_CHUNK)
    tables_p = _rope_tables(jnp.arange(seq, dtype=f32), True, q_scale, k_scale)
    tables_s = _rope_tables(jnp.full((n_s,), float(PAST_LEN), f32), False,
                            jnp.ones((n_s, RET_HEADS), f32), jnp.full((n_s, RET_HEADS), RET_DK ** -0.5, f32))

    tf = 256
    tm_p = 512

    xp = x_prompt.reshape(batch * seq, d)
    qkvg, loga, u_p = _in_even(xp, nm[0], we, wgb, bg, tm=tm_p)
    xp, gla_p = _gla_pool_prompt(qkvg, loga, u_p, xp, tril, gg, pw, ps, woe, batch=batch, t=512)
    xp = _ffn(xp, nf, wg, wu, wd, nfin, layer=0, tm=tm_p, tf=tf, final_norm=False)
    qkvg2 = _in_odd(xp, nm, wqk_split, wvg, tables_p, layer=1, tm=tm_p, tn=512, split_halves=True)
    xp, ret_p = _ret_prompt(qkvg2, xp, woo, gamma_c, batch=batch, c=RET_CHUNK)
    y_prompt = _ffn(xp, nf, wg, wu, wd, nfin, layer=1, tm=tm_p, tf=tf, final_norm=True)
    pool_p = u_p.reshape(batch, seq, -1)[:, seq - POOL_BUF:, :]

    xs = x_sample.reshape(n_s, d)
    qkvg_s, loga_s, u_s = _in_even(xs, nm[0], we, wgb, bg, tm=n_s)
    op_s, gla_s = _gla_pool_sample(qkvg_s, loga_s, u_s, state_gla[0], state_pool[0], gg, pw, ps, bb=8)
    xs = _proj_res(xs, op_s, woe, tm=n_s)
    xs = _ffn(xs, nf, wg, wu, wd, nfin, layer=0, tm=n_s, tf=tf, final_norm=False)
    qkvg2_s = _in_odd(xs, nm, wqk, wvg, tables_s, layer=1, tm=n_s, tn=512, split_halves=False)
    og_s, ret_s = _ret_sample(qkvg2_s, state_ret[0], gamma)
    xs = _proj_res(xs, og_s, woo, tm=n_s)
    y_sample = _ffn(xs, nf, wg, wu, wd, nfin, layer=1, tm=n_s, tf=tf, final_norm=True)
    pool_s = jnp.concatenate([state_pool[0][:, 1:, :], u_s[:, None, :]], axis=1)

    return (y_prompt.reshape(batch, seq, d), y_sample.reshape(n_s, 1, d),
            gla_p[None], gla_s[None], pool_p[None], pool_s[None], ret_p[None], ret_s[None])
```

```python
import functools

import numpy as np
import jax
import jax.numpy as jnp
from jax import lax
from jax.experimental import pallas as pl
from jax.experimental.pallas import tpu as pltpu

f32 = jnp.float32
bf16 = jnp.bfloat16

EPS = 1e-6
PAST_LEN = 16384
GLA_HEADS, GLA_DK, GLA_DV = 4, 64, 128
GLA_CHUNK = 64
GATE_RANK = 16
GATE_NORMALIZER = 16.0
POOL_WINDOWS = (2, 4, 8, 16)
POOL_GW = 128
POOL_BUF = max(POOL_WINDOWS) - 1
POOL_HIST = 32
RET_HEADS, RET_DK, RET_DV = 4, 256, 512
RET_CHUNK = 256
ROPE_BASE = 10000.0
LANES = 128
MIB = 1024 * 1024

NT_DIMS = (((1,), (1,)), ((), ()))
TN_DIMS = (((0,), (0,)), ((), ()))


def _params(semantics, vmem_mib):
    return pltpu.CompilerParams(dimension_semantics=semantics, vmem_limit_bytes=vmem_mib * MIB)


def _rms(x, gain=None):
    y = x * lax.rsqrt(jnp.mean(x * x, axis=-1, keepdims=True) + EPS)
    return y if gain is None else y * gain


def _silu(g):
    return g * jax.nn.sigmoid(g)


def _in_even_body(x_ref, gain_ref, w_ref, wgb_ref, bg_ref, qkvg_ref, loga_ref, u_ref):
    nq = qkvg_ref.shape[1]
    nu = u_ref.shape[1]
    h = _rms(x_ref[...], gain_ref[...]).astype(bf16)
    proj = jnp.dot(h, w_ref[...], preferred_element_type=f32)
    qkvg_ref[...] = proj[:, :nq].astype(bf16)
    u_ref[...] = proj[:, nq:nq + nu]
    a = proj[:, nq + nu:].astype(bf16)
    z = jnp.dot(a, wgb_ref[...], preferred_element_type=f32) + bg_ref[...]
    loga_ref[...] = (jnp.minimum(z, 0.0) - jnp.log1p(jnp.exp(-jnp.abs(z)))) * (1.0 / GATE_NORMALIZER)


def _in_even(x, gain, w, wgb, bg, *, tm):
    m, d = x.shape
    nq = 2 * GLA_HEADS * GLA_DK + 2 * GLA_HEADS * GLA_DV
    nu = POOL_GW * len(POOL_WINDOWS)
    nk = GLA_HEADS * GLA_DK
    const = lambda i: (0, 0)
    return pl.pallas_call(
        _in_even_body,
        grid=(m // tm,),
        in_specs=[
            pl.BlockSpec((tm, d), lambda i: (i, 0)),
            pl.BlockSpec((1, d), const),
            pl.BlockSpec(w.shape, const),
            pl.BlockSpec(wgb.shape, const),
            pl.BlockSpec((1, nk), const),
        ],
        out_specs=[
            pl.BlockSpec((tm, nq), lambda i: (i, 0)),
            pl.BlockSpec((tm, nk), lambda i: (i, 0)),
            pl.BlockSpec((tm, nu), lambda i: (i, 0)),
        ],
        out_shape=[
            jax.ShapeDtypeStruct((m, nq), bf16),
            jax.ShapeDtypeStruct((m, nk), f32),
            jax.ShapeDtypeStruct((m, nu), f32),
        ],
        compiler_params=_params(("arbitrary",), 48),
        name="in_even",
    )(x, gain, w, wgb, bg)


def _gla_pool_prompt_body(qkvg_ref, loga_ref, u_ref, x_ref, tril_ref, gain_ref, pw_ref, ps_ref, wout_ref,
                          xo_ref, so_ref, st_ref, o_ref, e_ref, p_ref, q_ref, op_ref):
    t = x_ref.shape[0]
    ck = GLA_CHUNK
    kw = GLA_HEADS * GLA_DK
    vw = GLA_HEADS * GLA_DV
    pair_w = 2 * GLA_DK
    i = pl.program_id(1)

    @pl.when(i == 0)
    def _():
        st_ref[...] = jnp.zeros_like(st_ref)
        e_ref[0:POOL_HIST, :] = jnp.zeros((POOL_HIST, e_ref.shape[1]), f32)

    tril = tril_ref[...]
    row = lax.broadcasted_iota(jnp.int32, (2 * ck, pair_w), 0)
    lane = lax.broadcasted_iota(jnp.int32, (2 * ck, pair_w), 1)
    first_lanes = lane < GLA_DK
    first_lanes_ck = lax.broadcasted_iota(jnp.int32, (ck, pair_w), 1) < GLA_DK
    same_head = (row < ck) == first_lanes
    causal = same_head & ((row % ck) >= (lane % GLA_DK))
    st = [st_ref[p] for p in range(GLA_HEADS // 2)]
    for c in range(t // ck):
        rows = slice(c * ck, (c + 1) * ck)
        la = loga_ref[rows, :]
        la_hi = la.astype(bf16)
        la_lo = (la - la_hi.astype(f32)).astype(bf16)
        bc = jnp.dot(tril, la_hi, preferred_element_type=f32) + jnp.dot(tril, la_lo, preferred_element_type=f32)
        blast = bc[ck - 1:ck, :]
        q = qkvg_ref[rows, 0:kw].astype(f32) * (GLA_DK ** -0.5)
        k = qkvg_ref[rows, kw:2 * kw].astype(f32)
        qe = q * jnp.exp(bc)
        ke = (k * jnp.exp(-bc)).astype(bf16)
        kd = (k * jnp.exp(blast - bc)).astype(bf16)
        elast = jnp.exp(blast)
        for p in range(GLA_HEADS // 2):
            pl_ = slice(p * pair_w, (p + 1) * pair_w)
            qe_p = qe[:, pl_]
            lhs_q = jnp.concatenate([jnp.where(first_lanes_ck, qe_p, 0.0),
                                     jnp.where(first_lanes_ck, 0.0, qe_p)], axis=0).astype(bf16)
            ke_p = ke[:, pl_]
            att = lax.dot_general(lhs_q, jnp.concatenate([ke_p, ke_p], axis=0), NT_DIMS,
                                  preferred_element_type=f32)
            att = jnp.where(causal, att, 0.0).astype(bf16)
            va = qkvg_ref[rows, 2 * kw + (2 * p) * GLA_DV:2 * kw + (2 * p + 1) * GLA_DV]
            vb = qkvg_ref[rows, 2 * kw + (2 * p + 1) * GLA_DV:2 * kw + (2 * p + 2) * GLA_DV]
            o = lax.dot_general(lhs_q, st[p].astype(bf16), NT_DIMS, preferred_element_type=f32)
            o = o + jnp.dot(att, jnp.concatenate([va, vb], axis=0), preferred_element_type=f32)
            o_ref[rows, (2 * p) * GLA_DV:(2 * p + 1) * GLA_DV] = o[:ck]
            o_ref[rows, (2 * p + 1) * GLA_DV:(2 * p + 2) * GLA_DV] = o[ck:]
            r = lax.dot_general(jnp.concatenate([va, vb], axis=1), kd[:, pl_], TN_DIMS, preferred_element_type=f32)
            st[p] = st[p] * elast[:, pl_] + jnp.where(first_lanes, r[:GLA_DV], r[GLA_DV:])
    for p in range(GLA_HEADS // 2):
        st_ref[p] = st[p]

    gain = gain_ref[...]
    for h in range(GLA_HEADS):
        hs = slice(h * GLA_DV, (h + 1) * GLA_DV)
        g = qkvg_ref[:, 2 * kw + vw + h * GLA_DV:2 * kw + vw + (h + 1) * GLA_DV].astype(f32)
        op_ref[:, hs] = (_rms(o_ref[:, hs], gain) * _silu(g)).astype(bf16)

    hist = POOL_HIST
    n = t + hist
    u = u_ref[...]
    e_ref[hist:n, :] = u
    gw = POOL_GW
    p_ref[8:n, :] = e_ref[8:n, :] + e_ref[7:n - 1, :]
    q_ref[16:n, gw:] = p_ref[16:n, gw:] + p_ref[14:n - 2, gw:]
    p_ref[24:n, 2 * gw:] = q_ref[24:n, 2 * gw:] + q_ref[20:n - 4, 2 * gw:]
    q_ref[32:n, 3 * gw:] = p_ref[32:n, 3 * gw:] + p_ref[24:n - 8, 3 * gw:]
    sums = (p_ref, q_ref, p_ref, q_ref)
    pos = i * t + lax.broadcasted_iota(jnp.int32, (t, 1), 0)
    for gi, w in enumerate(POOL_WINDOWS):
        ls = slice(gi * gw, (gi + 1) * gw)
        cnt = jnp.minimum(w, pos + 1).astype(f32)
        pooled = (sums[gi][hist:n, ls] / cnt - u[:, ls]).astype(bf16)
        pg = jnp.dot(pooled, pw_ref[gi], preferred_element_type=f32) * ps_ref[:, ls]
        op_ref[:, vw + gi * gw:vw + (gi + 1) * gw] = pg.astype(bf16)
    e_ref[hist - 16:hist, :] = e_ref[n - 16:n, :]

    xo_ref[...] = x_ref[...] + jnp.dot(op_ref[...], wout_ref[...], preferred_element_type=f32)

    @pl.when(i == pl.num_programs(1) - 1)
    def _():
        for p in range(GLA_HEADS // 2):
            s_pair = st_ref[p].T
            so_ref[0, 2 * p] = s_pair[:GLA_DK]
            so_ref[0, 2 * p + 1] = s_pair[GLA_DK:]


def _gla_pool_prompt(qkvg, loga, u, x, tril, gain, pw, ps, wout, *, batch, t):
    m, d = x.shape
    nt = m // batch // t
    row = lambda b, i: (b * nt + i, 0)
    const2 = lambda b, i: (0, 0)
    vw = GLA_HEADS * GLA_DV
    uw = u.shape[1]
    return pl.pallas_call(
        _gla_pool_prompt_body,
        grid=(batch, nt),
        in_specs=[
            pl.BlockSpec((t, qkvg.shape[1]), row),
            pl.BlockSpec((t, loga.shape[1]), row),
            pl.BlockSpec((t, uw), row),
            pl.BlockSpec((t, d), row),
            pl.BlockSpec(tril.shape, const2),
            pl.BlockSpec(gain.shape, const2),
            pl.BlockSpec(pw.shape, lambda b, i: (0, 0, 0)),
            pl.BlockSpec(ps.shape, const2),
            pl.BlockSpec(wout.shape, const2),
        ],
        out_specs=[
            pl.BlockSpec((t, d), row),
            pl.BlockSpec((1, GLA_HEADS, GLA_DK, GLA_DV), lambda b, i: (b, 0, 0, 0)),
        ],
        out_shape=[
            jax.ShapeDtypeStruct((m, d), f32),
            jax.ShapeDtypeStruct((batch, GLA_HEADS, GLA_DK, GLA_DV), f32),
        ],
        scratch_shapes=[
            pltpu.VMEM((GLA_HEADS // 2, GLA_DV, 2 * GLA_DK), f32),
            pltpu.VMEM((t, vw), f32),
            pltpu.VMEM((POOL_HIST + t, uw), f32),
            pltpu.VMEM((POOL_HIST + t, uw), f32),
            pltpu.VMEM((POOL_HIST + t, uw), f32),
            pltpu.VMEM((t, vw + uw), bf16),
        ],
        compiler_params=_params(("arbitrary", "arbitrary"), 40),
        name="gla_pool_prompt",
    )(qkvg, loga, u, x, tril, gain, pw, ps, wout)


def _gla_pool_sample_body(qkvg_ref, loga_ref, u_ref, s_ref, buf_ref, gain_ref, pw_ref, ps_ref,
                          op_ref, so_ref):
    bb = u_ref.shape[0]
    kw = GLA_HEADS * GLA_DK
    vw = GLA_HEADS * GLA_DV
    gain = gain_ref[...]
    qkvg = qkvg_ref[...].astype(f32)
    alpha = jnp.exp(loga_ref[...])
    qs = qkvg[:, 0:kw] * (GLA_DK ** -0.5)
    k = qkvg[:, kw:2 * kw]

    def column(row):
        return jnp.broadcast_to(row, (LANES, kw)).T

    o_rows = []
    for b in range(bb):
        acol = column(alpha[b:b + 1, :])
        qcol = column(qs[b:b + 1, :])
        kcol = column(k[b:b + 1, :])
        o_heads = []
        for h in range(GLA_HEADS):
            ks = slice(h * GLA_DK, (h + 1) * GLA_DK)
            v = qkvg[b:b + 1, 2 * kw + h * GLA_DV:2 * kw + (h + 1) * GLA_DV]
            s_new = acol[ks, :] * s_ref[b, h] + kcol[ks, :] * v
            so_ref[b, h] = s_new
            o = jnp.sum(qcol[ks, :] * s_new, axis=0, keepdims=True)
            g = qkvg[b:b + 1, 2 * kw + vw + h * GLA_DV:2 * kw + vw + (h + 1) * GLA_DV]
            o_heads.append(_rms(o, gain) * _silu(g))
        o_rows.append(jnp.concatenate(o_heads, axis=1))
    op_ref[:, 0:vw] = jnp.concatenate(o_rows, axis=0).astype(bf16)

    u = u_ref[...]
    for gi, w in enumerate(POOL_WINDOWS):
        ls = slice(gi * POOL_GW, (gi + 1) * POOL_GW)
        s = u[:, ls] + jnp.sum(buf_ref[:, POOL_BUF - (w - 1):POOL_BUF, ls], axis=1)
        cnt = float(min(w, PAST_LEN + 1))
        pooled = (s / cnt - u[:, ls]).astype(bf16)
        pg = jnp.dot(pooled, pw_ref[gi], preferred_element_type=f32) * ps_ref[:, ls]
        op_ref[:, vw + gi * POOL_GW:vw + (gi + 1) * POOL_GW] = pg.astype(bf16)


def _gla_pool_sample(qkvg, loga, u, s, buf, gain, pw, ps, *, bb):
    n = u.shape[0]
    row = lambda i: (i, 0)
    const2 = lambda i: (0, 0)
    ow = GLA_HEADS * GLA_DV + POOL_GW * len(POOL_WINDOWS)
    return pl.pallas_call(
        _gla_pool_sample_body,
        grid=(n // bb,),
        in_specs=[
            pl.BlockSpec((bb, qkvg.shape[1]), row),
            pl.BlockSpec((bb, loga.shape[1]), row),
            pl.BlockSpec((bb, u.shape[1]), row),
            pl.BlockSpec((bb,) + s.shape[1:], lambda i: (i, 0, 0, 0)),
            pl.BlockSpec((bb,) + buf.shape[1:], lambda i: (i, 0, 0)),
            pl.BlockSpec(gain.shape, const2),
            pl.BlockSpec(pw.shape, lambda i: (0, 0, 0)),
            pl.BlockSpec(ps.shape, const2),
        ],
        out_specs=[
            pl.BlockSpec((bb, ow), row),
            pl.BlockSpec((bb,) + s.shape[1:], lambda i: (i, 0, 0, 0)),
        ],
        out_shape=[
            jax.ShapeDtypeStruct((n, ow), bf16),
            jax.ShapeDtypeStruct(s.shape, f32),
        ],
        compiler_params=_params(("arbitrary",), 32),
        name="gla_pool_sample",
    )(qkvg, loga, u, s, buf, gain, pw, ps)


def _proj_res_body(x_ref, a_ref, w_ref, o_ref):
    o_ref[...] = x_ref[...] + jnp.dot(a_ref[...], w_ref[...], preferred_element_type=f32)


def _proj_res(x, a, w, *, tm):
    m, d = x.shape
    return pl.pallas_call(
        _proj_res_body,
        grid=(m // tm,),
        in_specs=[
            pl.BlockSpec((tm, d), lambda i: (i, 0)),
            pl.BlockSpec((tm, a.shape[1]), lambda i: (i, 0)),
            pl.BlockSpec(w.shape, lambda i: (0, 0)),
        ],
        out_specs=pl.BlockSpec((tm, d), lambda i: (i, 0)),
        out_shape=jax.ShapeDtypeStruct((m, d), f32),
        compiler_params=_params(("arbitrary",), 32),
        name="proj_res",
    )(x, a, w)


def _ret_token_pieces(q_ref, k_ref, v_ref, g_ref, s_ref, og_ref, so_ref, gamma):
    def piece(j, h):
        def run():
            ks = slice(h * RET_DK, (h + 1) * RET_DK)
            vs = slice(h * RET_DV, (h + 1) * RET_DV)
            qcol = jnp.broadcast_to(q_ref[j, :, ks].astype(f32), (LANES, RET_DK)).T
            kcol = jnp.broadcast_to(k_ref[j, :, ks].astype(f32), (LANES, RET_DK)).T
            v = v_ref[j, :, vs].astype(f32)
            g = g_ref[j, :, vs].astype(f32)
            o_tiles = []
            for t in range(RET_DV // LANES):
                cs = slice(t * LANES, (t + 1) * LANES)
                s_new = gamma[h] * s_ref[j, h, :, cs] + kcol * v[:, cs]
                so_ref[j, h, :, cs] = s_new
                o_tiles.append(jnp.sum(qcol * s_new, axis=0, keepdims=True))
            o = jnp.concatenate(o_tiles, axis=1)
            og_ref[j, :, vs] = (_rms(o) * _silu(g)).astype(bf16)
        return run

    return [piece(j, h) for j in range(s_ref.shape[0]) for h in range(RET_HEADS)]


def _ffn_body(*refs, tf, n_sub, final_norm, rider_gamma):
    x_ref, gain_ref, wg_ref, wu_ref, wd_ref, fgain_ref = refs[:6]
    pieces = []
    if rider_gamma is None:
        o_ref, h_ref, acc_ref = refs[6:]
    else:
        rq_ref, rk_ref, rv_ref, rg_ref, rs_ref, o_ref, rog_ref, rso_ref, h_ref, acc_ref = refs[6:]
        pieces = _ret_token_pieces(rq_ref, rk_ref, rv_ref, rg_ref, rs_ref, rog_ref, rso_ref, rider_gamma)
    n_chunks = wg_ref.shape[1] // tf
    bounds = [n_chunks * s // n_sub for s in range(n_sub + 1)]

    def run_chunks(chunks):
        for n, c in enumerate(chunks):
            cs = slice(c * tf, (c + 1) * tf)
            g = jnp.dot(h_ref[...], wg_ref[:, cs], preferred_element_type=f32)
            u = jnp.dot(h_ref[...], wu_ref[:, cs], preferred_element_type=f32)
            a = (_silu(g) * u).astype(bf16)
            part = jnp.dot(a, wd_ref[cs, :], preferred_element_type=f32)
            if c == 0:
                acc_ref[...] = part
            else:
                acc_ref[...] += part
            for p in range(len(pieces)):
                if p * len(chunks) // len(pieces) == n:
                    pieces[p]()

    def sub_step(s):
        if s == 0:
            h_ref[...] = _rms(x_ref[...], gain_ref[...]).astype(bf16)
        run_chunks(range(bounds[s], bounds[s + 1]))
        if s == n_sub - 1:
            y = x_ref[...] + acc_ref[...]
            if final_norm:
                y = _rms(y, fgain_ref[...])
            o_ref[...] = y

    if n_sub == 1:
        sub_step(0)
    else:
        for s in range(n_sub):
            pl.when(pl.program_id(1) == s)(functools.partial(sub_step, s))


def _ffn(x, gain, wg, wu, wd, fgain, *, layer, tm, tf, final_norm, rider=None):
    m, d = x.shape
    ff = wg.shape[2]
    steps = m // tm
    n_sub = 1 if rider is None else 2
    resident = dict(pipeline_mode=pl.Buffered(1))
    in_specs = [
        pl.BlockSpec((tm, d), lambda i, s: (i, 0)),
        pl.BlockSpec((None, 1, d), lambda i, s: (layer, 0, 0)),
        pl.BlockSpec((None, d, ff), lambda i, s: (layer, 0, 0), **resident),
        pl.BlockSpec((None, d, ff), lambda i, s: (layer, 0, 0), **resident),
        pl.BlockSpec((None, ff, d), lambda i, s: (layer, 0, 0), **resident),
        pl.BlockSpec((1, d), lambda i, s: (0, 0)),
    ]
    args = [x, gain, wg, wu, wd, fgain]
    out_specs = [pl.BlockSpec((tm, d), lambda i, s: (i, 0))]
    out_shape = [jax.ShapeDtypeStruct((m, d), f32)]
    gamma = None
    vmem = 48
    if rider is not None:
        qkvg3, state, rows, gamma = rider
        assert 2 * steps * rows == state.shape[0]
        qw = RET_HEADS * RET_DK
        vw = RET_HEADS * RET_DV
        blk = lambda col: (lambda i, s: (2 * i + s, 0, col))
        state_spec = pl.BlockSpec((rows,) + state.shape[1:], lambda i, s: (2 * i + s, 0, 0, 0))
        in_specs += [
            pl.BlockSpec((rows, 1, qw), blk(0)),
            pl.BlockSpec((rows, 1, qw), blk(1)),
            pl.BlockSpec((rows, 1, vw), blk(1)),
            pl.BlockSpec((rows, 1, vw), blk(2)),
            state_spec,
        ]
        args += [qkvg3, qkvg3, qkvg3, qkvg3, state]
        out_specs += [pl.BlockSpec((rows, 1, vw), blk(0)), state_spec]
        out_shape += [
            jax.ShapeDtypeStruct((state.shape[0], 1, vw), bf16),
            jax.ShapeDtypeStruct(state.shape, f32),
        ]
        vmem = 56
    out = pl.pallas_call(
        functools.partial(_ffn_body, tf=tf, n_sub=n_sub, final_norm=final_norm, rider_gamma=gamma),
        grid=(steps, n_sub),
        in_specs=in_specs,
        out_specs=out_specs,
        out_shape=out_shape,
        scratch_shapes=[pltpu.VMEM((tm, d), bf16), pltpu.VMEM((tm, d), f32)],
        compiler_params=_params(("arbitrary", "arbitrary"), vmem),
        name="ffn_final" if final_norm else "ffn",
    )(*args)
    return out[0] if rider is None else out


def _in_odd_body(x_ref, gain_ref, wqk_ref, wvg_ref, qcos_ref, qsin_ref, kcos_ref, ksin_ref, o_ref, h_ref, *,
                 tn, split_halves):
    h_ref[...] = _rms(x_ref[...], gain_ref[...]).astype(bf16)
    qw = RET_HEADS * RET_DK
    half = RET_DK // 2
    tw = qcos_ref.shape[1] // RET_HEADS
    for c in range(2 * qw // tn):
        c0 = c * tn
        p = jnp.dot(h_ref[...], wqk_ref[:, c0:c0 + tn], preferred_element_type=f32)
        cos_ref, sin_ref = (qcos_ref, qsin_ref) if c0 < qw else (kcos_ref, ksin_ref)
        for hh in range(tn // RET_DK):
            h0 = hh * RET_DK
            head = (c0 % qw + h0) // RET_DK
            cos = cos_ref[:, head * tw:(head + 1) * tw]
            sin = sin_ref[:, head * tw:(head + 1) * tw]
            if split_halves:
                ev = p[:, h0:h0 + half]
                od = p[:, h0 + half:h0 + RET_DK]
                o_ref[:, c0 + h0:c0 + h0 + half] = (ev * cos - od * sin).astype(bf16)
                o_ref[:, c0 + h0 + half:c0 + h0 + RET_DK] = (od * cos + ev * sin).astype(bf16)
            else:
                xh = p[:, h0:h0 + RET_DK]
                even = lax.broadcasted_iota(jnp.int32, xh.shape, 1) % 2 == 0
                partner = jnp.where(even, pltpu.roll(xh, RET_DK - 1, 1), pltpu.roll(xh, 1, 1))
                o_ref[:, c0 + h0:c0 + h0 + RET_DK] = (xh * cos + partner * sin).astype(bf16)
    for c in range(wvg_ref.shape[1] // tn):
        c0 = c * tn
        p = jnp.dot(h_ref[...], wvg_ref[:, c0:c0 + tn], preferred_element_type=f32)
        o_ref[:, 2 * qw + c0:2 * qw + c0 + tn] = p.astype(bf16)


def _in_odd(x, gain, wqk, wvg, tables, *, layer, tm, tn, split_halves):
    m, d = x.shape
    n = wqk.shape[1] + wvg.shape[1]
    ntab = tables[0].shape[0] // tm
    tw = tables[0].shape[1]
    resident = dict(pipeline_mode=pl.Buffered(1))
    table_spec = pl.BlockSpec((tm, tw), lambda i: (i % ntab, 0))
    return pl.pallas_call(
        functools.partial(_in_odd_body, tn=tn, split_halves=split_halves),
        grid=(m // tm,),
        in_specs=[
            pl.BlockSpec((tm, d), lambda i: (i, 0)),
            pl.BlockSpec((None, 1, d), lambda i: (layer, 0, 0)),
            pl.BlockSpec(wqk.shape, lambda i: (0, 0), **resident),
            pl.BlockSpec(wvg.shape, lambda i: (0, 0), **resident),
            table_spec, table_spec, table_spec, table_spec,
        ],
        out_specs=pl.BlockSpec((tm, n), lambda i: (i, 0)),
        out_shape=jax.ShapeDtypeStruct((m, n), bf16),
        scratch_shapes=[pltpu.VMEM((tm, d), bf16)],
        compiler_params=_params(("arbitrary",), 56),
        name="in_odd",
    )(x, gain, wqk, wvg, *tables)


def _ret_prompt_body(q_ref, k_ref, v_ref, g_ref, x_ref, wout_ref, xo_ref, so_ref, s_ref, sb_ref, slab_ref, *,
                     gamma_c):
    c = pl.program_id(1)
    n = q_ref.shape[0]

    @pl.when(c == 0)
    def _():
        s_ref[...] = jnp.zeros_like(s_ref)
        sb_ref[...] = jnp.zeros_like(sb_ref)

    causal = lax.broadcasted_iota(jnp.int32, (n, n), 0) >= lax.broadcasted_iota(jnp.int32, (n, n), 1)
    heads = range(RET_HEADS)
    ks = [slice(h * RET_DK, (h + 1) * RET_DK) for h in heads]
    vs = [slice(h * RET_DV, (h + 1) * RET_DV) for h in heads]
    att = [jnp.where(causal, lax.dot_general(q_ref[:, ks[h]], k_ref[:, ks[h]], NT_DIMS,
                                             preferred_element_type=f32), 0.0).astype(bf16) for h in heads]
    o = [jnp.dot(q_ref[:, ks[h]], sb_ref[h], preferred_element_type=f32)
         + jnp.dot(att[h], v_ref[:, vs[h]], preferred_element_type=f32) for h in heads]
    for h in heads:
        kv = lax.dot_general(k_ref[:, ks[h]], v_ref[:, vs[h]], TN_DIMS, preferred_element_type=f32)
        s_new = gamma_c[h] * (s_ref[h] + kv)
        s_ref[h] = s_new
        sb_ref[h] = s_new.astype(bf16)
    y = x_ref[...]
    for h in heads:
        og = (_rms(o[h]) * _silu(g_ref[:, vs[h]].astype(f32))).astype(bf16)
        y = y + jnp.dot(og, wout_ref[vs[h], :], preferred_element_type=f32)
    xo_ref[...] = y

    @pl.when(c == pl.num_programs(1) - 1)
    def _():
        half = RET_DK // 2
        for h in range(RET_HEADS):
            for t in range(RET_DV // LANES):
                ls = slice(t * LANES, (t + 1) * LANES)
                slab_ref[pl.ds(0, half, stride=2), :] = s_ref[h, 0:half, ls]
                slab_ref[pl.ds(1, half, stride=2), :] = s_ref[h, half:RET_DK, ls]
                so_ref[0, h, :, ls] = slab_ref[...]


def _ret_prompt(qkvg, x, wout, gamma_c, *, batch, c):
    m, d = x.shape
    nc = m // batch // c
    qw = RET_HEADS * RET_DK
    vw = RET_HEADS * RET_DV
    return pl.pallas_call(
        functools.partial(_ret_prompt_body, gamma_c=gamma_c),
        grid=(batch, nc),
        in_specs=[
            pl.BlockSpec((c, qw), lambda b, i: (b * nc + i, 0)),
            pl.BlockSpec((c, qw), lambda b, i: (b * nc + i, 1)),
            pl.BlockSpec((c, vw), lambda b, i: (b * nc + i, 1)),
            pl.BlockSpec((c, vw), lambda b, i: (b * nc + i, 2)),
            pl.BlockSpec((c, d), lambda b, i: (b * nc + i, 0)),
            pl.BlockSpec(wout.shape, lambda b, i: (0, 0), pipeline_mode=pl.Buffered(1)),
        ],
        out_specs=[
            pl.BlockSpec((c, d), lambda b, i: (b * nc + i, 0)),
            pl.BlockSpec((1, RET_HEADS, RET_DK, RET_DV), lambda b, i: (b, 0, 0, 0)),
        ],
        out_shape=[
            jax.ShapeDtypeStruct((m, d), f32),
            jax.ShapeDtypeStruct((batch, RET_HEADS, RET_DK, RET_DV), f32),
        ],
        scratch_shapes=[
            pltpu.VMEM((RET_HEADS, RET_DK, RET_DV), f32),
            pltpu.VMEM((RET_HEADS, RET_DK, RET_DV), bf16),
            pltpu.VMEM((RET_DK, LANES), f32),
        ],
        compiler_params=_params(("arbitrary", "arbitrary"), 48),
        name="ret_prompt",
    )(qkvg, qkvg, qkvg, qkvg, x, wout)


def _rope_tables(pos, per_pair, q_scale, k_scale):
    pair_angle = 1.0 / (ROPE_BASE ** jnp.linspace(0.0, 1.0, RET_DK // 2, dtype=f32))
    if per_pair:
        ang = pos[:, None] * pair_angle[None, :]
        cos, sin = jnp.cos(ang), jnp.sin(ang)
    else:
        ang = pos[:, None] * jnp.repeat(pair_angle, 2)[None, :]
        sign = jnp.where(jnp.arange(RET_DK) % 2 == 0, -1.0, 1.0).astype(f32)
        cos, sin = jnp.cos(ang), jnp.sin(ang) * sign

    def per_head(table, scale):
        return (table[:, None, :] * scale[:, :, None]).reshape(table.shape[0], -1)

    return per_head(cos, q_scale), per_head(sin, q_scale), per_head(cos, k_scale), per_head(sin, k_scale)


def _regroup_even_odd(w):
    d, n = w.shape
    return w.reshape(d, n // RET_DK, RET_DK // 2, 2).transpose(0, 1, 3, 2).reshape(d, n)


def _ret_decay(seq, c):
    gam = 1.0 - 2.0 ** (-5.0 - np.arange(RET_HEADS, dtype=np.float64))
    lg = np.log(gam)
    steps = (np.arange(seq) % c + 1.0)[:, None]
    q_scale = jnp.asarray(np.exp(lg[None, :] * steps), dtype=f32)
    k_scale = jnp.asarray(np.exp(-lg[None, :] * steps) * RET_DK ** -0.5, dtype=f32)
    gamma_c = tuple(float(x) for x in np.exp(lg * c))
    gamma = tuple(float(x) for x in gam)
    return q_scale, k_scale, gamma_c, gamma


def kernel(x_prompt, x_sample, state_gla, state_pool, state_ret, norm_mix, norm_ffn, norm_final, w_in_even,
           w_gate_b, b_gate, gla_gain, pool_w, pool_scale, w_out_even, w_in_odd, w_out_odd, w_ffn_gate,
           w_ffn_up, w_ffn_down):
    batch, seq, d = x_prompt.shape
    n_s = x_sample.shape[0]
    assert norm_mix.shape[0] == 2 and x_sample.shape[1] == 1

    nq = 2 * GLA_HEADS * GLA_DK + 2 * GLA_HEADS * GLA_DV
    we = w_in_even[0]
    we = jnp.concatenate(
        [we[:, :nq], we[:, nq + GATE_RANK:], we[:, nq:nq + GATE_RANK], jnp.zeros((d, LANES - GATE_RANK), f32)],
        axis=1).astype(bf16)
    wgb = jnp.concatenate([w_gate_b[0], jnp.zeros((LANES - GATE_RANK, w_gate_b.shape[2]), f32)], axis=0).astype(bf16)
    bg = b_gate[0][None, :]
    gg = gla_gain[0][None, :]
    pw = pool_w[0].astype(bf16)
    ps = pool_scale[0][None, :]
    woe = w_out_even[0].astype(bf16)
    qkw = 2 * RET_HEADS * RET_DK
    wqk = w_in_odd[0][:, :qkw].astype(bf16)
    wqk_split = _regroup_even_odd(w_in_odd[0][:, :qkw]).astype(bf16)
    wvg = w_in_odd[0][:, qkw:].astype(bf16)
    woo = w_out_odd[0].astype(bf16)
    wg = w_ffn_gate.astype(bf16)
    wu = w_ffn_up.astype(bf16)
    wd = w_ffn_down.astype(bf16)
    nm = norm_mix[:, None, :]
    nf = norm_ffn[:, None, :]
    nfin = norm_final[None, :]
    tril = jnp.asarray(np.tril(np.ones((GLA_CHUNK, GLA_CHUNK), np.float32)), dtype=bf16)
    q_scale, k_scale, gamma_c, gamma = _ret_decay(seq, RET_CHUNK)
    tables_p = _rope_tables(jnp.arange(seq, dtype=f32), True, q_scale, k_scale)
    tables_s = _rope_tables(jnp.full((n_s,), float(PAST_LEN), f32), False,
                            jnp.ones((n_s, RET_HEADS), f32), jnp.full((n_s, RET_HEADS), RET_DK ** -0.5, f32))

    tf = 256
    tm_p = 512

    xs = x_sample.reshape(n_s, d)
    qkvg_s, loga_s, u_s = _in_even(xs, nm[0], we, wgb, bg, tm=n_s)
    op_s, gla_s = _gla_pool_sample(qkvg_s, loga_s, u_s, state_gla[0], state_pool[0], gg, pw, ps, bb=8)
    xs = _proj_res(xs, op_s, woe, tm=n_s)
    xs = _ffn(xs, nf, wg, wu, wd, nfin, layer=0, tm=n_s, tf=tf, final_norm=False)
    qkvg2_s = _in_odd(xs, nm, wqk, wvg, tables_s, layer=1, tm=n_s, tn=512, split_halves=False)
    qkvg2_s = qkvg2_s.reshape(n_s, 1, -1)

    rows = n_s // (2 * (batch * seq // tm_p))
    xp = x_prompt.reshape(batch * seq, d)
    qkvg, loga, u_p = _in_even(xp, nm[0], we, wgb, bg, tm=tm_p)
    xp, gla_p = _gla_pool_prompt(qkvg, loga, u_p, xp, tril, gg, pw, ps, woe, batch=batch, t=512)
    xp, og_s, ret_s = _ffn(xp, nf, wg, wu, wd, nfin, layer=0, tm=tm_p, tf=tf, final_norm=False,
                           rider=(qkvg2_s, state_ret[0], rows, gamma))
    qkvg2 = _in_odd(xp, nm, wqk_split, wvg, tables_p, layer=1, tm=tm_p, tn=512, split_halves=True)
    xp, ret_p = _ret_prompt(qkvg2, xp, woo, gamma_c, batch=batch, c=RET_CHUNK)
    y_prompt = _ffn(xp, nf, wg, wu, wd, nfin, layer=1, tm=tm_p, tf=tf, final_norm=True)
    pool_p = u_p.reshape(batch, seq, -1)[:, seq - POOL_BUF:, :]

    xs = _proj_res(xs, og_s.reshape(n_s, -1), woo, tm=n_s)
    y_sample = _ffn(xs, nf, wg, wu, wd, nfin, layer=1, tm=n_s, tf=tf, final_norm=True)
    pool_s = jnp.concatenate([state_pool[0][:, 1:, :], u_s[:, None, :]], axis=1)

    return (y_prompt.reshape(batch, seq, d), y_sample.reshape(n_s, 1, d),
            gla_p[None], gla_s[None], pool_p[None], pool_s[None], ret_p[None], ret_s[None])
```

```python
import functools

import numpy as np
import jax
import jax.numpy as jnp
from jax import lax
from jax.experimental import pallas as pl
from jax.experimental.pallas import tpu as pltpu

f32 = jnp.float32
bf16 = jnp.bfloat16

EPS = 1e-6
PAST_LEN = 16384
GLA_HEADS, GLA_DK, GLA_DV = 4, 64, 128
GLA_CHUNK = 64
GATE_RANK = 16
GATE_NORMALIZER = 16.0
POOL_WINDOWS = (2, 4, 8, 16)
POOL_GW = 128
POOL_BUF = max(POOL_WINDOWS) - 1
POOL_HIST = 32
RET_HEADS, RET_DK, RET_DV = 4, 256, 512
RET_CHUNK = 256
ROPE_BASE = 10000.0
LANES = 128
MIB = 1024 * 1024

NT_DIMS = (((1,), (1,)), ((), ()))
TN_DIMS = (((0,), (0,)), ((), ()))


def _params(semantics, vmem_mib):
    return pltpu.CompilerParams(dimension_semantics=semantics, vmem_limit_bytes=vmem_mib * MIB)


def _rms(x, gain=None):
    y = x * lax.rsqrt(jnp.mean(x * x, axis=-1, keepdims=True) + EPS)
    return y if gain is None else y * gain


def _silu(g):
    return g * jax.nn.sigmoid(g)


def _in_even_body(x_ref, gain_ref, w_ref, wgb_ref, bg_ref, qkvg_ref, loga_ref, u_ref, h_ref, *, tn):
    nq = qkvg_ref.shape[1]
    nu = u_ref.shape[1]
    h_ref[...] = _rms(x_ref[...], gain_ref[...]).astype(bf16)
    a = jnp.dot(h_ref[...], w_ref[:, nq + nu:], preferred_element_type=f32).astype(bf16)
    z = jnp.dot(a, wgb_ref[...], preferred_element_type=f32) + bg_ref[...]
    loga_ref[...] = (jnp.minimum(z, 0.0) - jnp.log1p(jnp.exp(-jnp.abs(z)))) * (1.0 / GATE_NORMALIZER)
    for c0 in range(0, nq, tn):
        qkvg_ref[:, c0:c0 + tn] = jnp.dot(h_ref[...], w_ref[:, c0:c0 + tn], preferred_element_type=f32).astype(bf16)
    for c0 in range(0, nu, tn):
        u_ref[:, c0:c0 + tn] = jnp.dot(h_ref[...], w_ref[:, nq + c0:nq + c0 + tn], preferred_element_type=f32)


def _in_even(x, gain, w, wgb, bg, *, tm):
    m, d = x.shape
    nq = 2 * GLA_HEADS * GLA_DK + 2 * GLA_HEADS * GLA_DV
    nu = POOL_GW * len(POOL_WINDOWS)
    nk = GLA_HEADS * GLA_DK
    const = lambda i: (0, 0)
    return pl.pallas_call(
        functools.partial(_in_even_body, tn=512),
        grid=(m // tm,),
        in_specs=[
            pl.BlockSpec((tm, d), lambda i: (i, 0)),
            pl.BlockSpec((1, d), const),
            pl.BlockSpec(w.shape, const, pipeline_mode=pl.Buffered(1)),
            pl.BlockSpec(wgb.shape, const),
            pl.BlockSpec((1, nk), const),
        ],
        out_specs=[
            pl.BlockSpec((tm, nq), lambda i: (i, 0)),
            pl.BlockSpec((tm, nk), lambda i: (i, 0)),
            pl.BlockSpec((tm, nu), lambda i: (i, 0)),
        ],
        out_shape=[
            jax.ShapeDtypeStruct((m, nq), bf16),
            jax.ShapeDtypeStruct((m, nk), f32),
            jax.ShapeDtypeStruct((m, nu), f32),
        ],
        scratch_shapes=[pltpu.VMEM((tm, d), bf16)],
        compiler_params=_params(("arbitrary",), 48),
        name="in_even",
    )(x, gain, w, wgb, bg)


def _gla_pool_prompt_body(qkvg_ref, loga_ref, u_ref, x_ref, tril_ref, gain_ref, pw_ref, ps_ref, wout_ref,
                          xo_ref, so_ref, st_ref, o_ref, e_ref, p_ref, q_ref, op_ref):
    t = x_ref.shape[0]
    ck = GLA_CHUNK
    kw = GLA_HEADS * GLA_DK
    vw = GLA_HEADS * GLA_DV
    pair_w = 2 * GLA_DK
    i = pl.program_id(1)

    @pl.when(i == 0)
    def _():
        st_ref[...] = jnp.zeros_like(st_ref)
        e_ref[0:POOL_HIST, :] = jnp.zeros((POOL_HIST, e_ref.shape[1]), f32)

    tril = tril_ref[...]
    row = lax.broadcasted_iota(jnp.int32, (2 * ck, pair_w), 0)
    lane = lax.broadcasted_iota(jnp.int32, (2 * ck, pair_w), 1)
    first_lanes = lane < GLA_DK
    first_lanes_ck = lax.broadcasted_iota(jnp.int32, (ck, pair_w), 1) < GLA_DK
    same_head = (row < ck) == first_lanes
    causal = same_head & ((row % ck) >= (lane % GLA_DK))
    st = [st_ref[p] for p in range(GLA_HEADS // 2)]
    for c in range(t // ck):
        rows = slice(c * ck, (c + 1) * ck)
        la = loga_ref[rows, :]
        la_hi = la.astype(bf16)
        la_lo = (la - la_hi.astype(f32)).astype(bf16)
        bc = jnp.dot(tril, la_hi, preferred_element_type=f32) + jnp.dot(tril, la_lo, preferred_element_type=f32)
        blast = bc[ck - 1:ck, :]
        q = qkvg_ref[rows, 0:kw].astype(f32) * (GLA_DK ** -0.5)
        k = qkvg_ref[rows, kw:2 * kw].astype(f32)
        qe = q * jnp.exp(bc)
        ke = (k * jnp.exp(-bc)).astype(bf16)
        kd = (k * jnp.exp(blast - bc)).astype(bf16)
        elast = jnp.exp(blast)
        for p in range(GLA_HEADS // 2):
            pl_ = slice(p * pair_w, (p + 1) * pair_w)
            qe_p = qe[:, pl_]
            lhs_q = jnp.concatenate([jnp.where(first_lanes_ck, qe_p, 0.0),
                                     jnp.where(first_lanes_ck, 0.0, qe_p)], axis=0).astype(bf16)
            ke_p = ke[:, pl_]
            att = lax.dot_general(lhs_q, jnp.concatenate([ke_p, ke_p], axis=0), NT_DIMS,
                                  preferred_element_type=f32)
            att = jnp.where(causal, att, 0.0).astype(bf16)
            va = qkvg_ref[rows, 2 * kw + (2 * p) * GLA_DV:2 * kw + (2 * p + 1) * GLA_DV]
            vb = qkvg_ref[rows, 2 * kw + (2 * p + 1) * GLA_DV:2 * kw + (2 * p + 2) * GLA_DV]
            o = lax.dot_general(lhs_q, st[p].astype(bf16), NT_DIMS, preferred_element_type=f32)
            o = o + jnp.dot(att, jnp.concatenate([va, vb], axis=0), preferred_element_type=f32)
            o_ref[rows, (2 * p) * GLA_DV:(2 * p + 1) * GLA_DV] = o[:ck]
            o_ref[rows, (2 * p + 1) * GLA_DV:(2 * p + 2) * GLA_DV] = o[ck:]
            r = lax.dot_general(jnp.concatenate([va, vb], axis=1), kd[:, pl_], TN_DIMS, preferred_element_type=f32)
            st[p] = st[p] * elast[:, pl_] + jnp.where(first_lanes, r[:GLA_DV], r[GLA_DV:])
    for p in range(GLA_HEADS // 2):
        st_ref[p] = st[p]

    gain = gain_ref[...]
    for h in range(GLA_HEADS):
        hs = slice(h * GLA_DV, (h + 1) * GLA_DV)
        g = qkvg_ref[:, 2 * kw + vw + h * GLA_DV:2 * kw + vw + (h + 1) * GLA_DV].astype(f32)
        op_ref[:, hs] = (_rms(o_ref[:, hs], gain) * _silu(g)).astype(bf16)

    hist = POOL_HIST
    n = t + hist
    u = u_ref[...]
    e_ref[hist:n, :] = u
    gw = POOL_GW
    p_ref[8:n, :] = e_ref[8:n, :] + e_ref[7:n - 1, :]
    q_ref[16:n, gw:] = p_ref[16:n, gw:] + p_ref[14:n - 2, gw:]
    p_ref[24:n, 2 * gw:] = q_ref[24:n, 2 * gw:] + q_ref[20:n - 4, 2 * gw:]
    q_ref[32:n, 3 * gw:] = p_ref[32:n, 3 * gw:] + p_ref[24:n - 8, 3 * gw:]
    sums = (p_ref, q_ref, p_ref, q_ref)
    pos = i * t + lax.broadcasted_iota(jnp.int32, (t, 1), 0)
    for gi, w in enumerate(POOL_WINDOWS):
        ls = slice(gi * gw, (gi + 1) * gw)
        cnt = jnp.minimum(w, pos + 1).astype(f32)
        pooled = (sums[gi][hist:n, ls] / cnt - u[:, ls]).astype(bf16)
        pg = jnp.dot(pooled, pw_ref[gi], preferred_element_type=f32) * ps_ref[:, ls]
        op_ref[:, vw + gi * gw:vw + (gi + 1) * gw] = pg.astype(bf16)
    e_ref[hist - 16:hist, :] = e_ref[n - 16:n, :]

    xo_ref[...] = x_ref[...] + jnp.dot(op_ref[...], wout_ref[...], preferred_element_type=f32)

    @pl.when(i == pl.num_programs(1) - 1)
    def _():
        for p in range(GLA_HEADS // 2):
            s_pair = st_ref[p].T
            so_ref[0, 2 * p] = s_pair[:GLA_DK]
            so_ref[0, 2 * p + 1] = s_pair[GLA_DK:]


def _gla_pool_prompt(qkvg, loga, u, x, tril, gain, pw, ps, wout, *, batch, t):
    m, d = x.shape
    nt = m // batch // t
    row = lambda b, i: (b * nt + i, 0)
    const2 = lambda b, i: (0, 0)
    vw = GLA_HEADS * GLA_DV
    uw = u.shape[1]
    return pl.pallas_call(
        _gla_pool_prompt_body,
        grid=(batch, nt),
        in_specs=[
            pl.BlockSpec((t, qkvg.shape[1]), row),
            pl.BlockSpec((t, loga.shape[1]), row),
            pl.BlockSpec((t, uw), row),
            pl.BlockSpec((t, d), row),
            pl.BlockSpec(tril.shape, const2),
            pl.BlockSpec(gain.shape, const2),
            pl.BlockSpec(pw.shape, lambda b, i: (0, 0, 0)),
            pl.BlockSpec(ps.shape, const2),
            pl.BlockSpec(wout.shape, const2),
        ],
        out_specs=[
            pl.BlockSpec((t, d), row),
            pl.BlockSpec((1, GLA_HEADS, GLA_DK, GLA_DV), lambda b, i: (b, 0, 0, 0)),
        ],
        out_shape=[
            jax.ShapeDtypeStruct((m, d), f32),
            jax.ShapeDtypeStruct((batch, GLA_HEADS, GLA_DK, GLA_DV), f32),
        ],
        scratch_shapes=[
            pltpu.VMEM((GLA_HEADS // 2, GLA_DV, 2 * GLA_DK), f32),
            pltpu.VMEM((t, vw), f32),
            pltpu.VMEM((POOL_HIST + t, uw), f32),
            pltpu.VMEM((POOL_HIST + t, uw), f32),
            pltpu.VMEM((POOL_HIST + t, uw), f32),
            pltpu.VMEM((t, vw + uw), bf16),
        ],
        compiler_params=_params(("arbitrary", "arbitrary"), 40),
        name="gla_pool_prompt",
    )(qkvg, loga, u, x, tril, gain, pw, ps, wout)


def _gla_pool_sample_body(qkvg_ref, loga_ref, u_ref, s_ref, buf_ref, gain_ref, pw_ref, ps_ref,
                          op_ref, so_ref):
    bb = u_ref.shape[0]
    kw = GLA_HEADS * GLA_DK
    vw = GLA_HEADS * GLA_DV
    gain = gain_ref[...]
    qkvg = qkvg_ref[...].astype(f32)
    alpha = jnp.exp(loga_ref[...])
    qs = qkvg[:, 0:kw] * (GLA_DK ** -0.5)
    k = qkvg[:, kw:2 * kw]

    def column(row):
        return jnp.broadcast_to(row, (LANES, kw)).T

    o_rows = []
    for b in range(bb):
        acol = column(alpha[b:b + 1, :])
        qcol = column(qs[b:b + 1, :])
        kcol = column(k[b:b + 1, :])
        o_heads = []
        for h in range(GLA_HEADS):
            ks = slice(h * GLA_DK, (h + 1) * GLA_DK)
            v = qkvg[b:b + 1, 2 * kw + h * GLA_DV:2 * kw + (h + 1) * GLA_DV]
            s_new = acol[ks, :] * s_ref[b, h] + kcol[ks, :] * v
            so_ref[b, h] = s_new
            o = jnp.sum(qcol[ks, :] * s_new, axis=0, keepdims=True)
            g = qkvg[b:b + 1, 2 * kw + vw + h * GLA_DV:2 * kw + vw + (h + 1) * GLA_DV]
            o_heads.append(_rms(o, gain) * _silu(g))
        o_rows.append(jnp.concatenate(o_heads, axis=1))
    op_ref[:, 0:vw] = jnp.concatenate(o_rows, axis=0).astype(bf16)

    u = u_ref[...]
    for gi, w in enumerate(POOL_WINDOWS):
        ls = slice(gi * POOL_GW, (gi + 1) * POOL_GW)
        s = u[:, ls] + jnp.sum(buf_ref[:, POOL_BUF - (w - 1):POOL_BUF, ls], axis=1)
        cnt = float(min(w, PAST_LEN + 1))
        pooled = (s / cnt - u[:, ls]).astype(bf16)
        pg = jnp.dot(pooled, pw_ref[gi], preferred_element_type=f32) * ps_ref[:, ls]
        op_ref[:, vw + gi * POOL_GW:vw + (gi + 1) * POOL_GW] = pg.astype(bf16)


def _gla_pool_sample(qkvg, loga, u, s, buf, gain, pw, ps, *, bb):
    n = u.shape[0]
    row = lambda i: (i, 0)
    const2 = lambda i: (0, 0)
    ow = GLA_HEADS * GLA_DV + POOL_GW * len(POOL_WINDOWS)
    return pl.pallas_call(
        _gla_pool_sample_body,
        grid=(n // bb,),
        in_specs=[
            pl.BlockSpec((bb, qkvg.shape[1]), row),
            pl.BlockSpec((bb, loga.shape[1]), row),
            pl.BlockSpec((bb, u.shape[1]), row),
            pl.BlockSpec((bb,) + s.shape[1:], lambda i: (i, 0, 0, 0)),
            pl.BlockSpec((bb,) + buf.shape[1:], lambda i: (i, 0, 0)),
            pl.BlockSpec(gain.shape, const2),
            pl.BlockSpec(pw.shape, lambda i: (0, 0, 0)),
            pl.BlockSpec(ps.shape, const2),
        ],
        out_specs=[
            pl.BlockSpec((bb, ow), row),
            pl.BlockSpec((bb,) + s.shape[1:], lambda i: (i, 0, 0, 0)),
        ],
        out_shape=[
            jax.ShapeDtypeStruct((n, ow), bf16),
            jax.ShapeDtypeStruct(s.shape, f32),
        ],
        compiler_params=_params(("arbitrary",), 32),
        name="gla_pool_sample",
    )(qkvg, loga, u, s, buf, gain, pw, ps)


def _proj_res_body(x_ref, a_ref, w_ref, o_ref):
    o_ref[...] = x_ref[...] + jnp.dot(a_ref[...], w_ref[...], preferred_element_type=f32)


def _proj_res(x, a, w, *, tm):
    m, d = x.shape
    return pl.pallas_call(
        _proj_res_body,
        grid=(m // tm,),
        in_specs=[
            pl.BlockSpec((tm, d), lambda i: (i, 0)),
            pl.BlockSpec((tm, a.shape[1]), lambda i: (i, 0)),
            pl.BlockSpec(w.shape, lambda i: (0, 0)),
        ],
        out_specs=pl.BlockSpec((tm, d), lambda i: (i, 0)),
        out_shape=jax.ShapeDtypeStruct((m, d), f32),
        compiler_params=_params(("arbitrary",), 32),
        name="proj_res",
    )(x, a, w)


def _ret_token_pieces(q_ref, k_ref, v_ref, g_ref, s_ref, og_ref, so_ref, gamma):
    def piece(j, h):
        def run():
            ks = slice(h * RET_DK, (h + 1) * RET_DK)
            vs = slice(h * RET_DV, (h + 1) * RET_DV)
            qcol = jnp.broadcast_to(q_ref[j, :, ks].astype(f32), (LANES, RET_DK)).T
            kcol = jnp.broadcast_to(k_ref[j, :, ks].astype(f32), (LANES, RET_DK)).T
            v = v_ref[j, :, vs].astype(f32)
            g = g_ref[j, :, vs].astype(f32)
            o_tiles = []
            for t in range(RET_DV // LANES):
                cs = slice(t * LANES, (t + 1) * LANES)
                s_new = gamma[h] * s_ref[j, h, :, cs] + kcol * v[:, cs]
                so_ref[j, h, :, cs] = s_new
                o_tiles.append(jnp.sum(qcol * s_new, axis=0, keepdims=True))
            o = jnp.concatenate(o_tiles, axis=1)
            og_ref[j, :, vs] = (_rms(o) * _silu(g)).astype(bf16)
        return run

    return [piece(j, h) for j in range(s_ref.shape[0]) for h in range(RET_HEADS)]


def _ffn_body(*refs, tf, n_sub, final_norm, rider_gamma):
    x_ref, gain_ref, wg_ref, wu_ref, wd_ref, fgain_ref = refs[:6]
    pieces = []
    if rider_gamma is None:
        o_ref, h_ref, acc_ref = refs[6:]
    else:
        rq_ref, rk_ref, rv_ref, rg_ref, rs_ref, o_ref, rog_ref, rso_ref, h_ref, acc_ref = refs[6:]
        pieces = _ret_token_pieces(rq_ref, rk_ref, rv_ref, rg_ref, rs_ref, rog_ref, rso_ref, rider_gamma)
    n_chunks = wg_ref.shape[1] // tf
    bounds = [n_chunks * s // n_sub for s in range(n_sub + 1)]

    def run_chunks(chunks):
        for n, c in enumerate(chunks):
            cs = slice(c * tf, (c + 1) * tf)
            g = jnp.dot(h_ref[...], wg_ref[:, cs], preferred_element_type=f32)
            u = jnp.dot(h_ref[...], wu_ref[:, cs], preferred_element_type=f32)
            a = (_silu(g) * u).astype(bf16)
            part = jnp.dot(a, wd_ref[cs, :], preferred_element_type=f32)
            if c == 0:
                acc_ref[...] = part
            else:
                acc_ref[...] += part
            for p in range(len(pieces)):
                if p * len(chunks) // len(pieces) == n:
                    pieces[p]()

    def sub_step(s):
        if s == 0:
            h_ref[...] = _rms(x_ref[...], gain_ref[...]).astype(bf16)
        run_chunks(range(bounds[s], bounds[s + 1]))
        if s == n_sub - 1:
            y = x_ref[...] + acc_ref[...]
            if final_norm:
                y = _rms(y, fgain_ref[...])
            o_ref[...] = y

    if n_sub == 1:
        sub_step(0)
    else:
        for s in range(n_sub):
            pl.when(pl.program_id(1) == s)(functools.partial(sub_step, s))


def _ffn(x, gain, wg, wu, wd, fgain, *, layer, tm, tf, final_norm, rider=None):
    m, d = x.shape
    ff = wg.shape[2]
    steps = m // tm
    n_sub = 1 if rider is None else 2
    resident = dict(pipeline_mode=pl.Buffered(1))
    in_specs = [
        pl.BlockSpec((tm, d), lambda i, s: (i, 0)),
        pl.BlockSpec((None, 1, d), lambda i, s: (layer, 0, 0)),
        pl.BlockSpec((None, d, ff), lambda i, s: (layer, 0, 0), **resident),
        pl.BlockSpec((None, d, ff), lambda i, s: (layer, 0, 0), **resident),
        pl.BlockSpec((None, ff, d), lambda i, s: (layer, 0, 0), **resident),
        pl.BlockSpec((1, d), lambda i, s: (0, 0)),
    ]
    args = [x, gain, wg, wu, wd, fgain]
    out_specs = [pl.BlockSpec((tm, d), lambda i, s: (i, 0))]
    out_shape = [jax.ShapeDtypeStruct((m, d), f32)]
    gamma = None
    vmem = 48
    if rider is not None:
        qkvg3, state, rows, gamma = rider
        assert 2 * steps * rows == state.shape[0]
        qw = RET_HEADS * RET_DK
        vw = RET_HEADS * RET_DV
        blk = lambda col: (lambda i, s: (2 * i + s, 0, col))
        state_spec = pl.BlockSpec((rows,) + state.shape[1:], lambda i, s: (2 * i + s, 0, 0, 0))
        in_specs += [
            pl.BlockSpec((rows, 1, qw), blk(0)),
            pl.BlockSpec((rows, 1, qw), blk(1)),
            pl.BlockSpec((rows, 1, vw), blk(1)),
            pl.BlockSpec((rows, 1, vw), blk(2)),
            state_spec,
        ]
        args += [qkvg3, qkvg3, qkvg3, qkvg3, state]
        out_specs += [pl.BlockSpec((rows, 1, vw), blk(0)), state_spec]
        out_shape += [
            jax.ShapeDtypeStruct((state.shape[0], 1, vw), bf16),
            jax.ShapeDtypeStruct(state.shape, f32),
        ]
        vmem = 56
    out = pl.pallas_call(
        functools.partial(_ffn_body, tf=tf, n_sub=n_sub, final_norm=final_norm, rider_gamma=gamma),
        grid=(steps, n_sub),
        in_specs=in_specs,
        out_specs=out_specs,
        out_shape=out_shape,
        scratch_shapes=[pltpu.VMEM((tm, d), bf16), pltpu.VMEM((tm, d), f32)],
        compiler_params=_params(("arbitrary", "arbitrary"), vmem),
        name="ffn_final" if final_norm else "ffn",
    )(*args)
    return out[0] if rider is None else out


def _in_odd_body(x_ref, gain_ref, w_ref, perm_ref, cos_ref, sin_ref, qsc_ref, ksc_ref, o_ref, h_ref, wqk_ref, *,
                 tn, split_halves):
    qw = RET_HEADS * RET_DK
    half = RET_DK // 2
    if split_halves:
        @pl.when(pl.program_id(0) == 0)
        def _():
            for hh in range(2 * RET_HEADS):
                hs = slice(hh * RET_DK, (hh + 1) * RET_DK)
                wqk_ref[:, hs] = jnp.dot(w_ref[:, hs], perm_ref[...], preferred_element_type=f32).astype(bf16)

    h_ref[...] = _rms(x_ref[...], gain_ref[...]).astype(bf16)
    cos = cos_ref[...]
    sin = sin_ref[...]
    for c in range(2 * qw // tn):
        c0 = c * tn
        w_chunk = wqk_ref[:, c0:c0 + tn] if split_halves else w_ref[:, c0:c0 + tn]
        p = jnp.dot(h_ref[...], w_chunk, preferred_element_type=f32)
        sc_ref = qsc_ref if c0 < qw else ksc_ref
        for hh in range(tn // RET_DK):
            h0 = hh * RET_DK
            head = (c0 % qw + h0) // RET_DK
            sc = sc_ref[:, head * LANES:(head + 1) * LANES]
            if split_halves:
                ev = p[:, h0:h0 + half]
                od = p[:, h0 + half:h0 + RET_DK]
                o_ref[:, c0 + h0:c0 + h0 + half] = ((ev * cos - od * sin) * sc).astype(bf16)
                o_ref[:, c0 + h0 + half:c0 + h0 + RET_DK] = ((od * cos + ev * sin) * sc).astype(bf16)
            else:
                xh = p[:, h0:h0 + RET_DK]
                even = lax.broadcasted_iota(jnp.int32, xh.shape, 1) % 2 == 0
                partner = jnp.where(even, pltpu.roll(xh, RET_DK - 1, 1), pltpu.roll(xh, 1, 1))
                r = xh * cos + partner * sin
                o_ref[:, c0 + h0:c0 + h0 + half] = (r[:, :half] * sc).astype(bf16)
                o_ref[:, c0 + h0 + half:c0 + h0 + RET_DK] = (r[:, half:] * sc).astype(bf16)
    for c0 in range(2 * qw, w_ref.shape[1], tn):
        p = jnp.dot(h_ref[...], w_ref[:, c0:c0 + tn], preferred_element_type=f32)
        o_ref[:, c0:c0 + tn] = p.astype(bf16)


def _in_odd(x, gain, w, perm, tables, *, layer, tm, tn, split_halves):
    m, d = x.shape
    n = w.shape[1]
    cos, sin, qsc, ksc = tables
    ntab = cos.shape[0] // tm
    qkw = 2 * RET_HEADS * RET_DK
    resident = dict(pipeline_mode=pl.Buffered(1))
    const = lambda i: (0, 0)
    rope_spec = pl.BlockSpec((tm, cos.shape[1]), lambda i: (i % ntab, 0))
    return pl.pallas_call(
        functools.partial(_in_odd_body, tn=tn, split_halves=split_halves),
        grid=(m // tm,),
        in_specs=[
            pl.BlockSpec((tm, d), lambda i: (i, 0)),
            pl.BlockSpec((None, 1, d), lambda i: (layer, 0, 0)),
            pl.BlockSpec(w.shape, const, **resident),
            pl.BlockSpec(perm.shape, const),
            rope_spec, rope_spec,
            pl.BlockSpec(qsc.shape, const),
            pl.BlockSpec(ksc.shape, const),
        ],
        out_specs=pl.BlockSpec((tm, n), lambda i: (i, 0)),
        out_shape=jax.ShapeDtypeStruct((m, n), bf16),
        scratch_shapes=[pltpu.VMEM((tm, d), bf16),
                        pltpu.VMEM((d, qkw) if split_halves else (8, LANES), bf16)],
        compiler_params=_params(("arbitrary",), 56),
        name="in_odd",
    )(x, gain, w, perm, cos, sin, qsc, ksc)


def _ret_prompt_body(q_ref, k_ref, v_ref, g_ref, x_ref, wout_ref, xo_ref, so_ref, s_ref, sb_ref, slab_ref, *,
                     gamma_c):
    c = pl.program_id(1)
    n = q_ref.shape[0]

    @pl.when(c == 0)
    def _():
        s_ref[...] = jnp.zeros_like(s_ref)
        sb_ref[...] = jnp.zeros_like(sb_ref)

    causal = lax.broadcasted_iota(jnp.int32, (n, n), 0) >= lax.broadcasted_iota(jnp.int32, (n, n), 1)
    heads = range(RET_HEADS)
    ks = [slice(h * RET_DK, (h + 1) * RET_DK) for h in heads]
    vs = [slice(h * RET_DV, (h + 1) * RET_DV) for h in heads]
    att = [jnp.where(causal, lax.dot_general(q_ref[:, ks[h]], k_ref[:, ks[h]], NT_DIMS,
                                             preferred_element_type=f32), 0.0).astype(bf16) for h in heads]
    o = [jnp.dot(q_ref[:, ks[h]], sb_ref[h], preferred_element_type=f32)
         + jnp.dot(att[h], v_ref[:, vs[h]], preferred_element_type=f32) for h in heads]
    for h in heads:
        kv = lax.dot_general(k_ref[:, ks[h]], v_ref[:, vs[h]], TN_DIMS, preferred_element_type=f32)
        s_new = gamma_c[h] * (s_ref[h] + kv)
        s_ref[h] = s_new
        sb_ref[h] = s_new.astype(bf16)
    y = x_ref[...]
    for h in heads:
        og = (_rms(o[h]) * _silu(g_ref[:, vs[h]].astype(f32))).astype(bf16)
        y = y + jnp.dot(og, wout_ref[vs[h], :], preferred_element_type=f32)
    xo_ref[...] = y

    @pl.when(c == pl.num_programs(1) - 1)
    def _():
        half = RET_DK // 2
        for h in range(RET_HEADS):
            for t in range(RET_DV // LANES):
                ls = slice(t * LANES, (t + 1) * LANES)
                slab_ref[pl.ds(0, half, stride=2), :] = s_ref[h, 0:half, ls]
                slab_ref[pl.ds(1, half, stride=2), :] = s_ref[h, half:RET_DK, ls]
                so_ref[0, h, :, ls] = slab_ref[...]


def _ret_prompt(qkvg, x, wout, gamma_c, *, batch, c):
    m, d = x.shape
    nc = m // batch // c
    qw = RET_HEADS * RET_DK
    vw = RET_HEADS * RET_DV
    return pl.pallas_call(
        functools.partial(_ret_prompt_body, gamma_c=gamma_c),
        grid=(batch, nc),
        in_specs=[
            pl.BlockSpec((c, qw), lambda b, i: (b * nc + i, 0)),
            pl.BlockSpec((c, qw), lambda b, i: (b * nc + i, 1)),
            pl.BlockSpec((c, vw), lambda b, i: (b * nc + i, 1)),
            pl.BlockSpec((c, vw), lambda b, i: (b * nc + i, 2)),
            pl.BlockSpec((c, d), lambda b, i: (b * nc + i, 0)),
            pl.BlockSpec(wout.shape, lambda b, i: (0, 0), pipeline_mode=pl.Buffered(1)),
        ],
        out_specs=[
            pl.BlockSpec((c, d), lambda b, i: (b * nc + i, 0)),
            pl.BlockSpec((1, RET_HEADS, RET_DK, RET_DV), lambda b, i: (b, 0, 0, 0)),
        ],
        out_shape=[
            jax.ShapeDtypeStruct((m, d), f32),
            jax.ShapeDtypeStruct((batch, RET_HEADS, RET_DK, RET_DV), f32),
        ],
        scratch_shapes=[
            pltpu.VMEM((RET_HEADS, RET_DK, RET_DV), f32),
            pltpu.VMEM((RET_HEADS, RET_DK, RET_DV), bf16),
            pltpu.VMEM((RET_DK, LANES), f32),
        ],
        compiler_params=_params(("arbitrary", "arbitrary"), 48),
        name="ret_prompt",
    )(qkvg, qkvg, qkvg, qkvg, x, wout)


def _rope_tables(pos, per_pair):
    pair_angle = 1.0 / (ROPE_BASE ** jnp.linspace(0.0, 1.0, RET_DK // 2, dtype=f32))
    if per_pair:
        ang = pos[:, None] * pair_angle[None, :]
        return jnp.cos(ang), jnp.sin(ang)
    ang = pos[:, None] * jnp.repeat(pair_angle, 2)[None, :]
    sign = jnp.where(jnp.arange(RET_DK) % 2 == 0, -1.0, 1.0).astype(f32)
    return jnp.cos(ang), jnp.sin(ang) * sign


def _even_odd_perm():
    half = RET_DK // 2
    src = np.concatenate([2 * np.arange(half), 2 * np.arange(half) + 1])
    perm = np.zeros((RET_DK, RET_DK), np.float32)
    perm[src, np.arange(RET_DK)] = 1.0
    return jnp.asarray(perm, dtype=bf16)


def _lane_replicated(scale):
    return jnp.asarray(np.repeat(scale, LANES, axis=1), dtype=f32)


def _ret_decay(rows, c):
    gam = 1.0 - 2.0 ** (-5.0 - np.arange(RET_HEADS, dtype=np.float64))
    lg = np.log(gam)
    steps = (np.arange(rows) % c + 1.0)[:, None]
    q_scale = _lane_replicated(np.exp(lg[None, :] * steps))
    k_scale = _lane_replicated(np.exp(-lg[None, :] * steps) * RET_DK ** -0.5)
    gamma_c = tuple(float(x) for x in np.exp(lg * c))
    gamma = tuple(float(x) for x in gam)
    return q_scale, k_scale, gamma_c, gamma


def kernel(x_prompt, x_sample, state_gla, state_pool, state_ret, norm_mix, norm_ffn, norm_final, w_in_even,
           w_gate_b, b_gate, gla_gain, pool_w, pool_scale, w_out_even, w_in_odd, w_out_odd, w_ffn_gate,
           w_ffn_up, w_ffn_down):
    batch, seq, d = x_prompt.shape
    n_s = x_sample.shape[0]
    assert norm_mix.shape[0] == 2 and x_sample.shape[1] == 1

    nq = 2 * GLA_HEADS * GLA_DK + 2 * GLA_HEADS * GLA_DV
    we = w_in_even[0]
    we = jnp.concatenate(
        [we[:, :nq], we[:, nq + GATE_RANK:], we[:, nq:nq + GATE_RANK], jnp.zeros((d, LANES - GATE_RANK), f32)],
        axis=1).astype(bf16)
    wgb = jnp.concatenate([w_gate_b[0], jnp.zeros((LANES - GATE_RANK, w_gate_b.shape[2]), f32)], axis=0).astype(bf16)
    bg = b_gate[0][None, :]
    gg = gla_gain[0][None, :]
    pw = pool_w[0].astype(bf16)
    ps = pool_scale[0][None, :]
    woe = w_out_even[0].astype(bf16)
    wio = w_in_odd[0].astype(bf16)
    woo = w_out_odd[0].astype(bf16)
    wg = w_ffn_gate.astype(bf16)
    wu = w_ffn_up.astype(bf16)
    wd = w_ffn_down.astype(bf16)
    nm = norm_mix[:, None, :]
    nf = norm_ffn[:, None, :]
    nfin = norm_final[None, :]
    tril = jnp.asarray(np.tril(np.ones((GLA_CHUNK, GLA_CHUNK), np.float32)), dtype=bf16)
    tf = 256
    tm_p = 512
    q_scale, k_scale, gamma_c, gamma = _ret_decay(tm_p, RET_CHUNK)
    tables_p = _rope_tables(jnp.arange(seq, dtype=f32), True) + (q_scale, k_scale)
    tables_s = _rope_tables(jnp.full((n_s,), float(PAST_LEN), f32), False) + (
        _lane_replicated(np.ones((n_s, RET_HEADS))), _lane_replicated(np.full((n_s, RET_HEADS), RET_DK ** -0.5)))
    perm = _even_odd_perm()

    xs = x_sample.reshape(n_s, d)
    qkvg_s, loga_s, u_s = _in_even(xs, nm[0], we, wgb, bg, tm=n_s)
    op_s, gla_s = _gla_pool_sample(qkvg_s, loga_s, u_s, state_gla[0], state_pool[0], gg, pw, ps, bb=8)
    xs = _proj_res(xs, op_s, woe, tm=n_s)
    xs = _ffn(xs, nf, wg, wu, wd, nfin, layer=0, tm=n_s, tf=tf, final_norm=False)
    qkvg2_s = _in_odd(xs, nm, wio, perm, tables_s, layer=1, tm=n_s, tn=512, split_halves=False)
    qkvg2_s = qkvg2_s.reshape(n_s, 1, -1)

    rows = n_s // (2 * (batch * seq // tm_p))
    xp = x_prompt.reshape(batch * seq, d)
    qkvg, loga, u_p = _in_even(xp, nm[0], we, wgb, bg, tm=tm_p)
    xp, gla_p = _gla_pool_prompt(qkvg, loga, u_p, xp, tril, gg, pw, ps, woe, batch=batch, t=512)
    xp, og_s, ret_s = _ffn(xp, nf, wg, wu, wd, nfin, layer=0, tm=tm_p, tf=tf, final_norm=False,
                           rider=(qkvg2_s, state_ret[0], rows, gamma))
    qkvg2 = _in_odd(xp, nm, wio, perm, tables_p, layer=1, tm=tm_p, tn=512, split_halves=True)
    xp, ret_p = _ret_prompt(qkvg2, xp, woo, gamma_c, batch=batch, c=RET_CHUNK)
    y_prompt = _ffn(xp, nf, wg, wu, wd, nfin, layer=1, tm=tm_p, tf=tf, final_norm=True)
    pool_p = u_p.reshape(batch, seq, -1)[:, seq - POOL_BUF:, :]

    xs = _proj_res(xs, og_s.reshape(n_s, -1), woo, tm=n_s)
    y_sample = _ffn(xs, nf, wg, wu, wd, nfin, layer=1, tm=n_s, tf=tf, final_norm=True)
    pool_s = jnp.concatenate([state_pool[0][:, 1:, :], u_s[:, None, :]], axis=1)

    return (y_prompt.reshape(batch, seq, d), y_sample.reshape(n_s, 1, d),
            gla_p[None], gla_s[None], pool_p[None], pool_s[None], ret_p[None], ret_s[None])
```

```python
import functools

import numpy as np
import jax
import jax.numpy as jnp
from jax import lax
from jax.experimental import pallas as pl
from jax.experimental.pallas import tpu as pltpu

f32 = jnp.float32
bf16 = jnp.bfloat16

EPS = 1e-6
PAST_LEN = 16384
GLA_HEADS, GLA_DK, GLA_DV = 4, 64, 128
GLA_CHUNK = 64
GATE_RANK = 16
GATE_NORMALIZER = 16.0
POOL_WINDOWS = (2, 4, 8, 16)
POOL_GW = 128
POOL_BUF = max(POOL_WINDOWS) - 1
POOL_HIST = 32
RET_HEADS, RET_DK, RET_DV = 4, 256, 512
RET_CHUNK = 256
ROPE_BASE = 10000.0
LANES = 128
MIB = 1024 * 1024

NT_DIMS = (((1,), (1,)), ((), ()))
TN_DIMS = (((0,), (0,)), ((), ()))


def _params(semantics, vmem_mib):
    return pltpu.CompilerParams(dimension_semantics=semantics, vmem_limit_bytes=vmem_mib * MIB)


def _rms(x, gain=None):
    y = x * lax.rsqrt(jnp.mean(x * x, axis=-1, keepdims=True) + EPS)
    return y if gain is None else y * gain


def _silu(g):
    return g * jax.nn.sigmoid(g)


def _in_even_body(x_ref, gain_ref, w_ref, wgb_ref, bg_ref, qkvg_ref, loga_ref, u_ref, h_ref, *, tn):
    nq = qkvg_ref.shape[1]
    nu = u_ref.shape[1]
    h_ref[...] = _rms(x_ref[...], gain_ref[...]).astype(bf16)
    a = jnp.dot(h_ref[...], w_ref[:, nq + nu:], preferred_element_type=f32).astype(bf16)
    z = jnp.dot(a, wgb_ref[...], preferred_element_type=f32) + bg_ref[...]
    loga_ref[...] = (jnp.minimum(z, 0.0) - jnp.log1p(jnp.exp(-jnp.abs(z)))) * (1.0 / GATE_NORMALIZER)
    for c0 in range(0, nq, tn):
        qkvg_ref[:, c0:c0 + tn] = jnp.dot(h_ref[...], w_ref[:, c0:c0 + tn], preferred_element_type=f32).astype(bf16)
    for c0 in range(0, nu, tn):
        u_ref[:, c0:c0 + tn] = jnp.dot(h_ref[...], w_ref[:, nq + c0:nq + c0 + tn], preferred_element_type=f32)


def _in_even(x, gain, w, wgb, bg, *, tm):
    m, d = x.shape
    nq = 2 * GLA_HEADS * GLA_DK + 2 * GLA_HEADS * GLA_DV
    nu = POOL_GW * len(POOL_WINDOWS)
    nk = GLA_HEADS * GLA_DK
    const = lambda i: (0, 0)
    return pl.pallas_call(
        functools.partial(_in_even_body, tn=512),
        grid=(m // tm,),
        in_specs=[
            pl.BlockSpec((tm, d), lambda i: (i, 0)),
            pl.BlockSpec((1, d), const),
            pl.BlockSpec(w.shape, const, pipeline_mode=pl.Buffered(1)),
            pl.BlockSpec(wgb.shape, const),
            pl.BlockSpec((1, nk), const),
        ],
        out_specs=[
            pl.BlockSpec((tm, nq), lambda i: (i, 0)),
            pl.BlockSpec((tm, nk), lambda i: (i, 0)),
            pl.BlockSpec((tm, nu), lambda i: (i, 0)),
        ],
        out_shape=[
            jax.ShapeDtypeStruct((m, nq), bf16),
            jax.ShapeDtypeStruct((m, nk), f32),
            jax.ShapeDtypeStruct((m, nu), f32),
        ],
        scratch_shapes=[pltpu.VMEM((tm, d), bf16)],
        compiler_params=_params(("arbitrary",), 48),
        name="in_even",
    )(x, gain, w, wgb, bg)


def _gla_pool_prompt_body(qkvg_ref, loga_ref, u_ref, x_ref, tril_ref, gain_ref, pw_ref, ps_ref, wout_ref,
                          xo_ref, so_ref, st_ref, o_ref, e_ref, p_ref, q_ref, op_ref, after_chunk=None):
    t = x_ref.shape[0]
    ck = GLA_CHUNK
    kw = GLA_HEADS * GLA_DK
    vw = GLA_HEADS * GLA_DV
    pair_w = 2 * GLA_DK
    i = pl.program_id(1)

    @pl.when(i == 0)
    def _():
        st_ref[...] = jnp.zeros_like(st_ref)
        e_ref[0:POOL_HIST, :] = jnp.zeros((POOL_HIST, e_ref.shape[1]), f32)

    tril = tril_ref[...]
    row = lax.broadcasted_iota(jnp.int32, (2 * ck, pair_w), 0)
    lane = lax.broadcasted_iota(jnp.int32, (2 * ck, pair_w), 1)
    first_lanes = lane < GLA_DK
    first_lanes_ck = lax.broadcasted_iota(jnp.int32, (ck, pair_w), 1) < GLA_DK
    same_head = (row < ck) == first_lanes
    causal = same_head & ((row % ck) >= (lane % GLA_DK))
    pairs = range(GLA_HEADS // 2)
    chunks = range(t // ck)

    hist = POOL_HIST
    n = t + hist
    gw = POOL_GW
    u = u_ref[...]
    e_ref[hist:n, :] = u
    p_ref[8:n, :] = e_ref[8:n, :] + e_ref[7:n - 1, :]
    q_ref[16:n, gw:] = p_ref[16:n, gw:] + p_ref[14:n - 2, gw:]
    p_ref[24:n, 2 * gw:] = q_ref[24:n, 2 * gw:] + q_ref[20:n - 4, 2 * gw:]
    q_ref[32:n, 3 * gw:] = p_ref[32:n, 3 * gw:] + p_ref[24:n - 8, 3 * gw:]

    def rows_of(c):
        return slice(c * ck, (c + 1) * ck)

    def v_pair(c, p):
        va = qkvg_ref[rows_of(c), 2 * kw + (2 * p) * GLA_DV:2 * kw + (2 * p + 1) * GLA_DV]
        vb = qkvg_ref[rows_of(c), 2 * kw + (2 * p + 1) * GLA_DV:2 * kw + (2 * p + 2) * GLA_DV]
        return va, vb

    bcs = []
    for c in chunks:
        la = loga_ref[rows_of(c), :]
        la_hi = la.astype(bf16)
        la_lo = (la - la_hi.astype(f32)).astype(bf16)
        bcs.append(jnp.dot(tril, la_hi, preferred_element_type=f32) + jnp.dot(tril, la_lo, preferred_element_type=f32))
    lhs_q, ke2, kds, elast = [], [], [], []
    for c in chunks:
        bc = bcs[c]
        blast = bc[ck - 1:ck, :]
        q = qkvg_ref[rows_of(c), 0:kw].astype(f32) * (GLA_DK ** -0.5)
        k = qkvg_ref[rows_of(c), kw:2 * kw].astype(f32)
        qe = q * jnp.exp(bc)
        ke = (k * jnp.exp(-bc)).astype(bf16)
        kds.append((k * jnp.exp(blast - bc)).astype(bf16))
        elast.append(jnp.exp(blast))
        for p in pairs:
            pl_ = slice(p * pair_w, (p + 1) * pair_w)
            qe_p = qe[:, pl_]
            lhs_q.append(jnp.concatenate([jnp.where(first_lanes_ck, qe_p, 0.0),
                                          jnp.where(first_lanes_ck, 0.0, qe_p)], axis=0).astype(bf16))
            ke2.append(jnp.concatenate([ke[:, pl_], ke[:, pl_]], axis=0))
    att, upd = [], []
    for c in chunks:
        for p in pairs:
            idx = c * len(pairs) + p
            a = lax.dot_general(lhs_q[idx], ke2[idx], NT_DIMS, preferred_element_type=f32)
            att.append(jnp.where(causal, a, 0.0).astype(bf16))
            va, vb = v_pair(c, p)
            r = lax.dot_general(jnp.concatenate([va, vb], axis=1), kds[c][:, p * pair_w:(p + 1) * pair_w], TN_DIMS,
                                preferred_element_type=f32)
            upd.append(jnp.where(first_lanes, r[:GLA_DV], r[GLA_DV:]))
    st = [st_ref[p] for p in pairs]
    for c in chunks:
        for p in pairs:
            idx = c * len(pairs) + p
            va, vb = v_pair(c, p)
            o = lax.dot_general(lhs_q[idx], st[p].astype(bf16), NT_DIMS, preferred_element_type=f32)
            o = o + jnp.dot(att[idx], jnp.concatenate([va, vb], axis=0), preferred_element_type=f32)
            o_ref[rows_of(c), (2 * p) * GLA_DV:(2 * p + 1) * GLA_DV] = o[:ck]
            o_ref[rows_of(c), (2 * p + 1) * GLA_DV:(2 * p + 2) * GLA_DV] = o[ck:]
            st[p] = st[p] * elast[c][:, p * pair_w:(p + 1) * pair_w] + upd[idx]
    for p in pairs:
        st_ref[p] = st[p]

    sums = (p_ref, q_ref, p_ref, q_ref)
    pos = i * t + lax.broadcasted_iota(jnp.int32, (t, 1), 0)
    for gi, w in enumerate(POOL_WINDOWS):
        ls = slice(gi * gw, (gi + 1) * gw)
        cnt = jnp.minimum(w, pos + 1).astype(f32)
        pooled = (sums[gi][hist:n, ls] / cnt - u[:, ls]).astype(bf16)
        pg = jnp.dot(pooled, pw_ref[gi], preferred_element_type=f32) * ps_ref[:, ls]
        op_ref[:, vw + gi * gw:vw + (gi + 1) * gw] = pg.astype(bf16)
    e_ref[hist - 16:hist, :] = e_ref[n - 16:n, :]

    piece = 2 * GLA_DV
    y = x_ref[...]
    for c0 in (vw, vw + piece):
        y = y + jnp.dot(op_ref[:, c0:c0 + piece], wout_ref[c0:c0 + piece, :], preferred_element_type=f32)
    gain = gain_ref[...]
    for p in pairs:
        for h in (2 * p, 2 * p + 1):
            hs = slice(h * GLA_DV, (h + 1) * GLA_DV)
            g = qkvg_ref[:, 2 * kw + vw + h * GLA_DV:2 * kw + vw + (h + 1) * GLA_DV].astype(f32)
            op_ref[:, hs] = (_rms(o_ref[:, hs], gain) * _silu(g)).astype(bf16)
        c0 = p * piece
        y = y + jnp.dot(op_ref[:, c0:c0 + piece], wout_ref[c0:c0 + piece, :], preferred_element_type=f32)
    xo_ref[...] = y

    @pl.when(i == pl.num_programs(1) - 1)
    def _():
        for p in range(GLA_HEADS // 2):
            s_pair = st_ref[p].T
            so_ref[0, 2 * p] = s_pair[:GLA_DK]
            so_ref[0, 2 * p + 1] = s_pair[GLA_DK:]


def _gla_pool_prompt(qkvg, loga, u, x, tril, gain, pw, ps, wout, *, batch, t):
    m, d = x.shape
    nt = m // batch // t
    row = lambda b, i: (b * nt + i, 0)
    const2 = lambda b, i: (0, 0)
    vw = GLA_HEADS * GLA_DV
    uw = u.shape[1]
    return pl.pallas_call(
        _gla_pool_prompt_body,
        grid=(batch, nt),
        in_specs=[
            pl.BlockSpec((t, qkvg.shape[1]), row),
            pl.BlockSpec((t, loga.shape[1]), row),
            pl.BlockSpec((t, uw), row),
            pl.BlockSpec((t, d), row),
            pl.BlockSpec(tril.shape, const2),
            pl.BlockSpec(gain.shape, const2),
            pl.BlockSpec(pw.shape, lambda b, i: (0, 0, 0)),
            pl.BlockSpec(ps.shape, const2),
            pl.BlockSpec(wout.shape, const2),
        ],
        out_specs=[
            pl.BlockSpec((t, d), row),
            pl.BlockSpec((1, GLA_HEADS, GLA_DK, GLA_DV), lambda b, i: (b, 0, 0, 0)),
        ],
        out_shape=[
            jax.ShapeDtypeStruct((m, d), f32),
            jax.ShapeDtypeStruct((batch, GLA_HEADS, GLA_DK, GLA_DV), f32),
        ],
        scratch_shapes=[
            pltpu.VMEM((GLA_HEADS // 2, GLA_DV, 2 * GLA_DK), f32),
            pltpu.VMEM((t, vw), f32),
            pltpu.VMEM((POOL_HIST + t, uw), f32),
            pltpu.VMEM((POOL_HIST + t, uw), f32),
            pltpu.VMEM((POOL_HIST + t, uw), f32),
            pltpu.VMEM((t, vw + uw), bf16),
        ],
        compiler_params=_params(("arbitrary", "arbitrary"), 40),
        name="gla_pool_prompt",
    )(qkvg, loga, u, x, tril, gain, pw, ps, wout)


def _gla_pool_sample_body(qkvg_ref, loga_ref, u_ref, s_ref, buf_ref, gain_ref, pw_ref, ps_ref,
                          op_ref, so_ref):
    bb = u_ref.shape[0]
    kw = GLA_HEADS * GLA_DK
    vw = GLA_HEADS * GLA_DV
    gain = gain_ref[...]
    qkvg = qkvg_ref[...].astype(f32)
    alpha = jnp.exp(loga_ref[...])
    qs = qkvg[:, 0:kw] * (GLA_DK ** -0.5)
    k = qkvg[:, kw:2 * kw]

    def column(row):
        return jnp.broadcast_to(row, (LANES, kw)).T

    o_rows = []
    for b in range(bb):
        acol = column(alpha[b:b + 1, :])
        qcol = column(qs[b:b + 1, :])
        kcol = column(k[b:b + 1, :])
        o_heads = []
        for h in range(GLA_HEADS):
            ks = slice(h * GLA_DK, (h + 1) * GLA_DK)
            v = qkvg[b:b + 1, 2 * kw + h * GLA_DV:2 * kw + (h + 1) * GLA_DV]
            s_new = acol[ks, :] * s_ref[b, h] + kcol[ks, :] * v
            so_ref[b, h] = s_new
            o = jnp.sum(qcol[ks, :] * s_new, axis=0, keepdims=True)
            g = qkvg[b:b + 1, 2 * kw + vw + h * GLA_DV:2 * kw + vw + (h + 1) * GLA_DV]
            o_heads.append(_rms(o, gain) * _silu(g))
        o_rows.append(jnp.concatenate(o_heads, axis=1))
    op_ref[:, 0:vw] = jnp.concatenate(o_rows, axis=0).astype(bf16)

    u = u_ref[...]
    for gi, w in enumerate(POOL_WINDOWS):
        ls = slice(gi * POOL_GW, (gi + 1) * POOL_GW)
        s = u[:, ls] + jnp.sum(buf_ref[:, POOL_BUF - (w - 1):POOL_BUF, ls], axis=1)
        cnt = float(min(w, PAST_LEN + 1))
        pooled = (s / cnt - u[:, ls]).astype(bf16)
        pg = jnp.dot(pooled, pw_ref[gi], preferred_element_type=f32) * ps_ref[:, ls]
        op_ref[:, vw + gi * POOL_GW:vw + (gi + 1) * POOL_GW] = pg.astype(bf16)


def _gla_pool_sample(qkvg, loga, u, s, buf, gain, pw, ps, *, bb):
    n = u.shape[0]
    row = lambda i: (i, 0)
    const2 = lambda i: (0, 0)
    ow = GLA_HEADS * GLA_DV + POOL_GW * len(POOL_WINDOWS)
    return pl.pallas_call(
        _gla_pool_sample_body,
        grid=(n // bb,),
        in_specs=[
            pl.BlockSpec((bb, qkvg.shape[1]), row),
            pl.BlockSpec((bb, loga.shape[1]), row),
            pl.BlockSpec((bb, u.shape[1]), row),
            pl.BlockSpec((bb,) + s.shape[1:], lambda i: (i, 0, 0, 0)),
            pl.BlockSpec((bb,) + buf.shape[1:], lambda i: (i, 0, 0)),
            pl.BlockSpec(gain.shape, const2),
            pl.BlockSpec(pw.shape, lambda i: (0, 0, 0)),
            pl.BlockSpec(ps.shape, const2),
        ],
        out_specs=[
            pl.BlockSpec((bb, ow), row),
            pl.BlockSpec((bb,) + s.shape[1:], lambda i: (i, 0, 0, 0)),
        ],
        out_shape=[
            jax.ShapeDtypeStruct((n, ow), bf16),
            jax.ShapeDtypeStruct(s.shape, f32),
        ],
        compiler_params=_params(("arbitrary",), 32),
        name="gla_pool_sample",
    )(qkvg, loga, u, s, buf, gain, pw, ps)


def _proj_res_body(x_ref, a_ref, w_ref, o_ref):
    o_ref[...] = x_ref[...] + jnp.dot(a_ref[...], w_ref[...], preferred_element_type=f32)


def _proj_res(x, a, w, *, tm):
    m, d = x.shape
    return pl.pallas_call(
        _proj_res_body,
        grid=(m // tm,),
        in_specs=[
            pl.BlockSpec((tm, d), lambda i: (i, 0)),
            pl.BlockSpec((tm, a.shape[1]), lambda i: (i, 0)),
            pl.BlockSpec(w.shape, lambda i: (0, 0)),
        ],
        out_specs=pl.BlockSpec((tm, d), lambda i: (i, 0)),
        out_shape=jax.ShapeDtypeStruct((m, d), f32),
        compiler_params=_params(("arbitrary",), 32),
        name="proj_res",
    )(x, a, w)


def _ret_token_pieces(q_ref, k_ref, v_ref, g_ref, s_ref, og_ref, so_ref, gamma):
    def piece(j, h):
        def run():
            ks = slice(h * RET_DK, (h + 1) * RET_DK)
            vs = slice(h * RET_DV, (h + 1) * RET_DV)
            qcol = jnp.broadcast_to(q_ref[j, :, ks].astype(f32), (LANES, RET_DK)).T
            kcol = jnp.broadcast_to(k_ref[j, :, ks].astype(f32), (LANES, RET_DK)).T
            v = v_ref[j, :, vs].astype(f32)
            g = g_ref[j, :, vs].astype(f32)
            o_tiles = []
            for t in range(RET_DV // LANES):
                cs = slice(t * LANES, (t + 1) * LANES)
                s_new = gamma[h] * s_ref[j, h, :, cs] + kcol * v[:, cs]
                so_ref[j, h, :, cs] = s_new
                o_tiles.append(jnp.sum(qcol * s_new, axis=0, keepdims=True))
            o = jnp.concatenate(o_tiles, axis=1)
            og_ref[j, :, vs] = (_rms(o) * _silu(g)).astype(bf16)
        return run

    return [piece(j, h) for j in range(s_ref.shape[0]) for h in range(RET_HEADS)]


def _ffn_body(*refs, tf, n_sub, final_norm, rider_gamma):
    x_ref, gain_ref, wg_ref, wu_ref, wd_ref, fgain_ref = refs[:6]
    pieces = []
    if rider_gamma is None:
        o_ref, h_ref, acc_ref = refs[6:]
    else:
        rq_ref, rk_ref, rv_ref, rg_ref, rs_ref, o_ref, rog_ref, rso_ref, h_ref, acc_ref = refs[6:]
        pieces = _ret_token_pieces(rq_ref, rk_ref, rv_ref, rg_ref, rs_ref, rog_ref, rso_ref, rider_gamma)
    n_chunks = wg_ref.shape[1] // tf
    bounds = [n_chunks * s // n_sub for s in range(n_sub + 1)]

    def run_chunks(chunks):
        for n, c in enumerate(chunks):
            cs = slice(c * tf, (c + 1) * tf)
            g = jnp.dot(h_ref[...], wg_ref[:, cs], preferred_element_type=f32)
            u = jnp.dot(h_ref[...], wu_ref[:, cs], preferred_element_type=f32)
            a = (_silu(g) * u).astype(bf16)
            part = jnp.dot(a, wd_ref[cs, :], preferred_element_type=f32)
            if c == 0:
                acc_ref[...] = part
            else:
                acc_ref[...] += part
            for p in range(len(pieces)):
                if p * len(chunks) // len(pieces) == n:
                    pieces[p]()

    def sub_step(s):
        if s == 0:
            h_ref[...] = _rms(x_ref[...], gain_ref[...]).astype(bf16)
        run_chunks(range(bounds[s], bounds[s + 1]))
        if s == n_sub - 1:
            y = x_ref[...] + acc_ref[...]
            if final_norm:
                y = _rms(y, fgain_ref[...])
            o_ref[...] = y

    if n_sub == 1:
        sub_step(0)
    else:
        for s in range(n_sub):
            pl.when(pl.program_id(1) == s)(functools.partial(sub_step, s))


def _ffn(x, gain, wg, wu, wd, fgain, *, layer, tm, tf, final_norm, rider=None):
    m, d = x.shape
    ff = wg.shape[2]
    steps = m // tm
    n_sub = 1 if rider is None else 2
    resident = dict(pipeline_mode=pl.Buffered(1))
    in_specs = [
        pl.BlockSpec((tm, d), lambda i, s: (i, 0)),
        pl.BlockSpec((None, 1, d), lambda i, s: (layer, 0, 0)),
        pl.BlockSpec((None, d, ff), lambda i, s: (layer, 0, 0), **resident),
        pl.BlockSpec((None, d, ff), lambda i, s: (layer, 0, 0), **resident),
        pl.BlockSpec((None, ff, d), lambda i, s: (layer, 0, 0), **resident),
        pl.BlockSpec((1, d), lambda i, s: (0, 0)),
    ]
    args = [x, gain, wg, wu, wd, fgain]
    out_specs = [pl.BlockSpec((tm, d), lambda i, s: (i, 0))]
    out_shape = [jax.ShapeDtypeStruct((m, d), f32)]
    gamma = None
    vmem = 48
    if rider is not None:
        qkvg3, state, rows, gamma = rider
        assert 2 * steps * rows == state.shape[0]
        qw = RET_HEADS * RET_DK
        vw = RET_HEADS * RET_DV
        blk = lambda col: (lambda i, s: (2 * i + s, 0, col))
        state_spec = pl.BlockSpec((rows,) + state.shape[1:], lambda i, s: (2 * i + s, 0, 0, 0))
        in_specs += [
            pl.BlockSpec((rows, 1, qw), blk(0)),
            pl.BlockSpec((rows, 1, qw), blk(1)),
            pl.BlockSpec((rows, 1, vw), blk(1)),
            pl.BlockSpec((rows, 1, vw), blk(2)),
            state_spec,
        ]
        args += [qkvg3, qkvg3, qkvg3, qkvg3, state]
        out_specs += [pl.BlockSpec((rows, 1, vw), blk(0)), state_spec]
        out_shape += [
            jax.ShapeDtypeStruct((state.shape[0], 1, vw), bf16),
            jax.ShapeDtypeStruct(state.shape, f32),
        ]
        vmem = 56
    out = pl.pallas_call(
        functools.partial(_ffn_body, tf=tf, n_sub=n_sub, final_norm=final_norm, rider_gamma=gamma),
        grid=(steps, n_sub),
        in_specs=in_specs,
        out_specs=out_specs,
        out_shape=out_shape,
        scratch_shapes=[pltpu.VMEM((tm, d), bf16), pltpu.VMEM((tm, d), f32)],
        compiler_params=_params(("arbitrary", "arbitrary"), vmem),
        name="ffn_final" if final_norm else "ffn",
    )(*args)
    return out[0] if rider is None else out


def _in_odd_body(x_ref, gain_ref, w_ref, perm_ref, cos_ref, sin_ref, qsc_ref, ksc_ref, o_ref, h_ref, wqk_ref, *,
                 tn, split_halves):
    qw = RET_HEADS * RET_DK
    half = RET_DK // 2
    if split_halves:
        @pl.when(pl.program_id(0) == 0)
        def _():
            for hh in range(2 * RET_HEADS):
                hs = slice(hh * RET_DK, (hh + 1) * RET_DK)
                wqk_ref[:, hs] = jnp.dot(w_ref[:, hs], perm_ref[...], preferred_element_type=f32).astype(bf16)

    h_ref[...] = _rms(x_ref[...], gain_ref[...]).astype(bf16)
    cos = cos_ref[...]
    sin = sin_ref[...]
    for c in range(2 * qw // tn):
        c0 = c * tn
        w_chunk = wqk_ref[:, c0:c0 + tn] if split_halves else w_ref[:, c0:c0 + tn]
        p = jnp.dot(h_ref[...], w_chunk, preferred_element_type=f32)
        sc_ref = qsc_ref if c0 < qw else ksc_ref
        for hh in range(tn // RET_DK):
            h0 = hh * RET_DK
            head = (c0 % qw + h0) // RET_DK
            sc = sc_ref[:, head * LANES:(head + 1) * LANES]
            if split_halves:
                ev = p[:, h0:h0 + half]
                od = p[:, h0 + half:h0 + RET_DK]
                o_ref[:, c0 + h0:c0 + h0 + half] = ((ev * cos - od * sin) * sc).astype(bf16)
                o_ref[:, c0 + h0 + half:c0 + h0 + RET_DK] = ((od * cos + ev * sin) * sc).astype(bf16)
            else:
                xh = p[:, h0:h0 + RET_DK]
                even = lax.broadcasted_iota(jnp.int32, xh.shape, 1) % 2 == 0
                partner = jnp.where(even, pltpu.roll(xh, RET_DK - 1, 1), pltpu.roll(xh, 1, 1))
                r = xh * cos + partner * sin
                o_ref[:, c0 + h0:c0 + h0 + half] = (r[:, :half] * sc).astype(bf16)
                o_ref[:, c0 + h0 + half:c0 + h0 + RET_DK] = (r[:, half:] * sc).astype(bf16)
    for c0 in range(2 * qw, w_ref.shape[1], tn):
        p = jnp.dot(h_ref[...], w_ref[:, c0:c0 + tn], preferred_element_type=f32)
        o_ref[:, c0:c0 + tn] = p.astype(bf16)


def _in_odd(x, gain, w, perm, tables, *, layer, tm, tn, split_halves):
    m, d = x.shape
    n = w.shape[1]
    cos, sin, qsc, ksc = tables
    ntab = cos.shape[0] // tm
    qkw = 2 * RET_HEADS * RET_DK
    resident = dict(pipeline_mode=pl.Buffered(1))
    const = lambda i: (0, 0)
    rope_spec = pl.BlockSpec((tm, cos.shape[1]), lambda i: (i % ntab, 0))
    return pl.pallas_call(
        functools.partial(_in_odd_body, tn=tn, split_halves=split_halves),
        grid=(m // tm,),
        in_specs=[
            pl.BlockSpec((tm, d), lambda i: (i, 0)),
            pl.BlockSpec((None, 1, d), lambda i: (layer, 0, 0)),
            pl.BlockSpec(w.shape, const, **resident),
            pl.BlockSpec(perm.shape, const),
            rope_spec, rope_spec,
            pl.BlockSpec(qsc.shape, const),
            pl.BlockSpec(ksc.shape, const),
        ],
        out_specs=pl.BlockSpec((tm, n), lambda i: (i, 0)),
        out_shape=jax.ShapeDtypeStruct((m, n), bf16),
        scratch_shapes=[pltpu.VMEM((tm, d), bf16),
                        pltpu.VMEM((d, qkw) if split_halves else (8, LANES), bf16)],
        compiler_params=_params(("arbitrary",), 56),
        name="in_odd",
    )(x, gain, w, perm, cos, sin, qsc, ksc)


def _ret_prompt_body(q_ref, k_ref, v_ref, g_ref, x_ref, wout_ref, xo_ref, so_ref, s_ref, sb_ref, slab_ref, *,
                     gamma_c):
    c = pl.program_id(1)
    n = q_ref.shape[0]

    @pl.when(c == 0)
    def _():
        s_ref[...] = jnp.zeros_like(s_ref)
        sb_ref[...] = jnp.zeros_like(sb_ref)

    causal = lax.broadcasted_iota(jnp.int32, (n, n), 0) >= lax.broadcasted_iota(jnp.int32, (n, n), 1)
    heads = range(RET_HEADS)
    ks = [slice(h * RET_DK, (h + 1) * RET_DK) for h in heads]
    vs = [slice(h * RET_DV, (h + 1) * RET_DV) for h in heads]
    att = [jnp.where(causal, lax.dot_general(q_ref[:, ks[h]], k_ref[:, ks[h]], NT_DIMS,
                                             preferred_element_type=f32), 0.0).astype(bf16) for h in heads]
    o = [jnp.dot(q_ref[:, ks[h]], sb_ref[h], preferred_element_type=f32)
         + jnp.dot(att[h], v_ref[:, vs[h]], preferred_element_type=f32) for h in heads]
    for h in heads:
        kv = lax.dot_general(k_ref[:, ks[h]], v_ref[:, vs[h]], TN_DIMS, preferred_element_type=f32)
        s_new = gamma_c[h] * (s_ref[h] + kv)
        s_ref[h] = s_new
        sb_ref[h] = s_new.astype(bf16)
    y = x_ref[...]
    for h in heads:
        og = (_rms(o[h]) * _silu(g_ref[:, vs[h]].astype(f32))).astype(bf16)
        y = y + jnp.dot(og, wout_ref[vs[h], :], preferred_element_type=f32)
    xo_ref[...] = y

    @pl.when(c == pl.num_programs(1) - 1)
    def _():
        half = RET_DK // 2
        for h in range(RET_HEADS):
            for t in range(RET_DV // LANES):
                ls = slice(t * LANES, (t + 1) * LANES)
                slab_ref[pl.ds(0, half, stride=2), :] = s_ref[h, 0:half, ls]
                slab_ref[pl.ds(1, half, stride=2), :] = s_ref[h, half:RET_DK, ls]
                so_ref[0, h, :, ls] = slab_ref[...]


def _ret_prompt(qkvg, x, wout, gamma_c, *, batch, c):
    m, d = x.shape
    nc = m // batch // c
    qw = RET_HEADS * RET_DK
    vw = RET_HEADS * RET_DV
    return pl.pallas_call(
        functools.partial(_ret_prompt_body, gamma_c=gamma_c),
        grid=(batch, nc),
        in_specs=[
            pl.BlockSpec((c, qw), lambda b, i: (b * nc + i, 0)),
            pl.BlockSpec((c, qw), lambda b, i: (b * nc + i, 1)),
            pl.BlockSpec((c, vw), lambda b, i: (b * nc + i, 1)),
            pl.BlockSpec((c, vw), lambda b, i: (b * nc + i, 2)),
            pl.BlockSpec((c, d), lambda b, i: (b * nc + i, 0)),
            pl.BlockSpec(wout.shape, lambda b, i: (0, 0), pipeline_mode=pl.Buffered(1)),
        ],
        out_specs=[
            pl.BlockSpec((c, d), lambda b, i: (b * nc + i, 0)),
            pl.BlockSpec((1, RET_HEADS, RET_DK, RET_DV), lambda b, i: (b, 0, 0, 0)),
        ],
        out_shape=[
            jax.ShapeDtypeStruct((m, d), f32),
            jax.ShapeDtypeStruct((batch, RET_HEADS, RET_DK, RET_DV), f32),
        ],
        scratch_shapes=[
            pltpu.VMEM((RET_HEADS, RET_DK, RET_DV), f32),
            pltpu.VMEM((RET_HEADS, RET_DK, RET_DV), bf16),
            pltpu.VMEM((RET_DK, LANES), f32),
        ],
        compiler_params=_params(("arbitrary", "arbitrary"), 48),
        name="ret_prompt",
    )(qkvg, qkvg, qkvg, qkvg, x, wout)


def _rope_tables(pos, per_pair):
    pair_angle = 1.0 / (ROPE_BASE ** jnp.linspace(0.0, 1.0, RET_DK // 2, dtype=f32))
    if per_pair:
        ang = pos[:, None] * pair_angle[None, :]
        return jnp.cos(ang), jnp.sin(ang)
    ang = pos[:, None] * jnp.repeat(pair_angle, 2)[None, :]
    sign = jnp.where(jnp.arange(RET_DK) % 2 == 0, -1.0, 1.0).astype(f32)
    return jnp.cos(ang), jnp.sin(ang) * sign


def _even_odd_perm():
    half = RET_DK // 2
    src = np.concatenate([2 * np.arange(half), 2 * np.arange(half) + 1])
    perm = np.zeros((RET_DK, RET_DK), np.float32)
    perm[src, np.arange(RET_DK)] = 1.0
    return jnp.asarray(perm, dtype=bf16)


def _lane_replicated(scale):
    return jnp.asarray(np.repeat(scale, LANES, axis=1), dtype=f32)


def _ret_decay(rows, c):
    gam = 1.0 - 2.0 ** (-5.0 - np.arange(RET_HEADS, dtype=np.float64))
    lg = np.log(gam)
    steps = (np.arange(rows) % c + 1.0)[:, None]
    q_scale = _lane_replicated(np.exp(lg[None, :] * steps))
    k_scale = _lane_replicated(np.exp(-lg[None, :] * steps) * RET_DK ** -0.5)
    gamma_c = tuple(float(x) for x in np.exp(lg * c))
    gamma = tuple(float(x) for x in gam)
    return q_scale, k_scale, gamma_c, gamma


def kernel(x_prompt, x_sample, state_gla, state_pool, state_ret, norm_mix, norm_ffn, norm_final, w_in_even,
           w_gate_b, b_gate, gla_gain, pool_w, pool_scale, w_out_even, w_in_odd, w_out_odd, w_ffn_gate,
           w_ffn_up, w_ffn_down):
    batch, seq, d = x_prompt.shape
    n_s = x_sample.shape[0]
    assert norm_mix.shape[0] == 2 and x_sample.shape[1] == 1

    nq = 2 * GLA_HEADS * GLA_DK + 2 * GLA_HEADS * GLA_DV
    we = w_in_even[0]
    we = jnp.concatenate(
        [we[:, :nq], we[:, nq + GATE_RANK:], we[:, nq:nq + GATE_RANK], jnp.zeros((d, LANES - GATE_RANK), f32)],
        axis=1).astype(bf16)
    wgb = jnp.concatenate([w_gate_b[0], jnp.zeros((LANES - GATE_RANK, w_gate_b.shape[2]), f32)], axis=0).astype(bf16)
    bg = b_gate[0][None, :]
    gg = gla_gain[0][None, :]
    pw = pool_w[0].astype(bf16)
    ps = pool_scale[0][None, :]
    woe = w_out_even[0].astype(bf16)
    wio = w_in_odd[0].astype(bf16)
    woo = w_out_odd[0].astype(bf16)
    wg = w_ffn_gate.astype(bf16)
    wu = w_ffn_up.astype(bf16)
    wd = w_ffn_down.astype(bf16)
    nm = norm_mix[:, None, :]
    nf = norm_ffn[:, None, :]
    nfin = norm_final[None, :]
    tril = jnp.asarray(np.tril(np.ones((GLA_CHUNK, GLA_CHUNK), np.float32)), dtype=bf16)
    tf = 256
    tm_p = 512
    q_scale, k_scale, gamma_c, gamma = _ret_decay(tm_p, RET_CHUNK)
    tables_p = _rope_tables(jnp.arange(seq, dtype=f32), True) + (q_scale, k_scale)
    tables_s = _rope_tables(jnp.full((n_s,), float(PAST_LEN), f32), False) + (
        _lane_replicated(np.ones((n_s, RET_HEADS))), _lane_replicated(np.full((n_s, RET_HEADS), RET_DK ** -0.5)))
    perm = _even_odd_perm()

    xs = x_sample.reshape(n_s, d)
    qkvg_s, loga_s, u_s = _in_even(xs, nm[0], we, wgb, bg, tm=n_s)
    op_s, gla_s = _gla_pool_sample(qkvg_s, loga_s, u_s, state_gla[0], state_pool[0], gg, pw, ps, bb=8)
    xs = _proj_res(xs, op_s, woe, tm=n_s)
    xs = _ffn(xs, nf, wg, wu, wd, nfin, layer=0, tm=n_s, tf=tf, final_norm=False)
    qkvg2_s = _in_odd(xs, nm, wio, perm, tables_s, layer=1, tm=n_s, tn=512, split_halves=False)
    qkvg2_s = qkvg2_s.reshape(n_s, 1, -1)

    rows = n_s // (2 * (batch * seq // tm_p))
    xp = x_prompt.reshape(batch * seq, d)
    qkvg, loga, u_p = _in_even(xp, nm[0], we, wgb, bg, tm=tm_p)
    xp, gla_p = _gla_pool_prompt(qkvg, loga, u_p, xp, tril, gg, pw, ps, woe, batch=batch, t=512)
    xp, og_s, ret_s = _ffn(xp, nf, wg, wu, wd, nfin, layer=0, tm=tm_p, tf=tf, final_norm=False,
                           rider=(qkvg2_s, state_ret[0], rows, gamma))
    qkvg2 = _in_odd(xp, nm, wio, perm, tables_p, layer=1, tm=tm_p, tn=512, split_halves=True)
    xp, ret_p = _ret_prompt(qkvg2, xp, woo, gamma_c, batch=batch, c=RET_CHUNK)
    y_prompt = _ffn(xp, nf, wg, wu, wd, nfin, layer=1, tm=tm_p, tf=tf, final_norm=True)
    pool_p = u_p.reshape(batch, seq, -1)[:, seq - POOL_BUF:, :]

    xs = _proj_res(xs, og_s.reshape(n_s, -1), woo, tm=n_s)
    y_sample = _ffn(xs, nf, wg, wu, wd, nfin, layer=1, tm=n_s, tf=tf, final_norm=True)
    pool_s = jnp.concatenate([state_pool[0][:, 1:, :], u_s[:, None, :]], axis=1)

    return (y_prompt.reshape(batch, seq, d), y_sample.reshape(n_s, 1, d),
            gla_p[None], gla_s[None], pool_p[None], pool_s[None], ret_p[None], ret_s[None])
```

```python
import functools

import numpy as np
import jax
import jax.numpy as jnp
from jax import lax
from jax.experimental import pallas as pl
from jax.experimental.pallas import tpu as pltpu

f32 = jnp.float32
bf16 = jnp.bfloat16

EPS = 1e-6
PAST_LEN = 16384
GLA_HEADS, GLA_DK, GLA_DV = 4, 64, 128
GLA_CHUNK = 64
GATE_RANK = 16
GATE_NORMALIZER = 16.0
POOL_WINDOWS = (2, 4, 8, 16)
POOL_GW = 128
POOL_BUF = max(POOL_WINDOWS) - 1
POOL_HIST = 32
RET_HEADS, RET_DK, RET_DV = 4, 256, 512
RET_CHUNK = 256
ROPE_BASE = 10000.0
LANES = 128
MIB = 1024 * 1024

NT_DIMS = (((1,), (1,)), ((), ()))
TN_DIMS = (((0,), (0,)), ((), ()))


def _params(semantics, vmem_mib):
    return pltpu.CompilerParams(dimension_semantics=semantics, vmem_limit_bytes=vmem_mib * MIB)


def _rms(x, gain=None):
    y = x * lax.rsqrt(jnp.mean(x * x, axis=-1, keepdims=True) + EPS)
    return y if gain is None else y * gain


def _silu(g):
    return g * jax.nn.sigmoid(g)


def _in_even_body(x_ref, gain_ref, w_ref, shift_ref, wgb_ref, bg_ref, qkvg_ref, loga_ref, u_ref, h_ref, wu_ref, *,
                  tn):
    nq = qkvg_ref.shape[1]
    nu = u_ref.shape[1]

    @pl.when(pl.program_id(0) == 0)
    def _():
        wu_ref[...] = jnp.dot(w_ref[:, nq:], shift_ref[...], preferred_element_type=f32).astype(bf16)

    h_ref[...] = _rms(x_ref[...], gain_ref[...]).astype(bf16)
    a = jnp.dot(h_ref[...], w_ref[:, nq:nq + LANES], preferred_element_type=f32)
    a = jnp.where(lax.broadcasted_iota(jnp.int32, a.shape, 1) < GATE_RANK, a, 0.0).astype(bf16)
    z = jnp.dot(a, wgb_ref[...], preferred_element_type=f32) + bg_ref[...]
    loga_ref[...] = (jnp.minimum(z, 0.0) - jnp.log1p(jnp.exp(-jnp.abs(z)))) * (1.0 / GATE_NORMALIZER)
    for c0 in range(0, nq, tn):
        qkvg_ref[:, c0:c0 + tn] = jnp.dot(h_ref[...], w_ref[:, c0:c0 + tn], preferred_element_type=f32).astype(bf16)
    for c0 in range(0, nu, tn):
        u_ref[:, c0:c0 + tn] = jnp.dot(h_ref[...], wu_ref[:, c0:c0 + tn], preferred_element_type=f32)


def _in_even(x, gain, w, shift, wgb, bg, *, tm):
    m, d = x.shape
    nq = 2 * GLA_HEADS * GLA_DK + 2 * GLA_HEADS * GLA_DV
    nu = POOL_GW * len(POOL_WINDOWS)
    nk = GLA_HEADS * GLA_DK
    const = lambda i: (0, 0)
    return pl.pallas_call(
        functools.partial(_in_even_body, tn=512),
        grid=(m // tm,),
        in_specs=[
            pl.BlockSpec((tm, d), lambda i: (i, 0)),
            pl.BlockSpec((1, d), const),
            pl.BlockSpec(w.shape, const, pipeline_mode=pl.Buffered(1)),
            pl.BlockSpec(shift.shape, const),
            pl.BlockSpec(wgb.shape, const),
            pl.BlockSpec((1, nk), const),
        ],
        out_specs=[
            pl.BlockSpec((tm, nq), lambda i: (i, 0)),
            pl.BlockSpec((tm, nk), lambda i: (i, 0)),
            pl.BlockSpec((tm, nu), lambda i: (i, 0)),
        ],
        out_shape=[
            jax.ShapeDtypeStruct((m, nq), bf16),
            jax.ShapeDtypeStruct((m, nk), f32),
            jax.ShapeDtypeStruct((m, nu), f32),
        ],
        scratch_shapes=[pltpu.VMEM((tm, d), bf16), pltpu.VMEM((d, nu), bf16)],
        compiler_params=_params(("arbitrary",), 48),
        name="in_even",
    )(x, gain, w, shift, wgb, bg)


def _gla_pool_prompt_body(qkvg_ref, loga_ref, u_ref, x_ref, tril_ref, gain_ref, pw_ref, ps_ref, wout_ref,
                          xo_ref, so_ref, st_ref, o_ref, e_ref, p_ref, q_ref, op_ref, after_chunk=None):
    t = x_ref.shape[0]
    ck = GLA_CHUNK
    kw = GLA_HEADS * GLA_DK
    vw = GLA_HEADS * GLA_DV
    pair_w = 2 * GLA_DK
    i = pl.program_id(1)

    @pl.when(i == 0)
    def _():
        st_ref[...] = jnp.zeros_like(st_ref)
        e_ref[0:POOL_HIST, :] = jnp.zeros((POOL_HIST, e_ref.shape[1]), f32)

    tril = tril_ref[...]
    row = lax.broadcasted_iota(jnp.int32, (2 * ck, pair_w), 0)
    lane = lax.broadcasted_iota(jnp.int32, (2 * ck, pair_w), 1)
    first_lanes = lane < GLA_DK
    first_lanes_ck = lax.broadcasted_iota(jnp.int32, (ck, pair_w), 1) < GLA_DK
    same_head = (row < ck) == first_lanes
    causal = same_head & ((row % ck) >= (lane % GLA_DK))
    pairs = range(GLA_HEADS // 2)
    chunks = range(t // ck)

    hist = POOL_HIST
    n = t + hist
    gw = POOL_GW
    u = u_ref[...]
    e_ref[hist:n, :] = u
    p_ref[8:n, :] = e_ref[8:n, :] + e_ref[7:n - 1, :]
    q_ref[16:n, gw:] = p_ref[16:n, gw:] + p_ref[14:n - 2, gw:]
    p_ref[24:n, 2 * gw:] = q_ref[24:n, 2 * gw:] + q_ref[20:n - 4, 2 * gw:]
    q_ref[32:n, 3 * gw:] = p_ref[32:n, 3 * gw:] + p_ref[24:n - 8, 3 * gw:]

    def rows_of(c):
        return slice(c * ck, (c + 1) * ck)

    def v_pair(c, p):
        va = qkvg_ref[rows_of(c), 2 * kw + (2 * p) * GLA_DV:2 * kw + (2 * p + 1) * GLA_DV]
        vb = qkvg_ref[rows_of(c), 2 * kw + (2 * p + 1) * GLA_DV:2 * kw + (2 * p + 2) * GLA_DV]
        return va, vb

    bcs = []
    for c in chunks:
        la = loga_ref[rows_of(c), :]
        la_hi = la.astype(bf16)
        la_lo = (la - la_hi.astype(f32)).astype(bf16)
        bcs.append(jnp.dot(tril, la_hi, preferred_element_type=f32) + jnp.dot(tril, la_lo, preferred_element_type=f32))
    lhs_q, ke2, kds, elast = [], [], [], []
    for c in chunks:
        bc = bcs[c]
        blast = bc[ck - 1:ck, :]
        q = qkvg_ref[rows_of(c), 0:kw].astype(f32) * (GLA_DK ** -0.5)
        k = qkvg_ref[rows_of(c), kw:2 * kw].astype(f32)
        qe = q * jnp.exp(bc)
        ke = (k * jnp.exp(-bc)).astype(bf16)
        kds.append((k * jnp.exp(blast - bc)).astype(bf16))
        elast.append(jnp.exp(blast))
        for p in pairs:
            pl_ = slice(p * pair_w, (p + 1) * pair_w)
            qe_p = qe[:, pl_]
            lhs_q.append(jnp.concatenate([jnp.where(first_lanes_ck, qe_p, 0.0),
                                          jnp.where(first_lanes_ck, 0.0, qe_p)], axis=0).astype(bf16))
            ke2.append(jnp.concatenate([ke[:, pl_], ke[:, pl_]], axis=0))
    att, upd = [], []
    for c in chunks:
        for p in pairs:
            idx = c * len(pairs) + p
            a = lax.dot_general(lhs_q[idx], ke2[idx], NT_DIMS, preferred_element_type=f32)
            att.append(jnp.where(causal, a, 0.0).astype(bf16))
            va, vb = v_pair(c, p)
            r = lax.dot_general(jnp.concatenate([va, vb], axis=1), kds[c][:, p * pair_w:(p + 1) * pair_w], TN_DIMS,
                                preferred_element_type=f32)
            upd.append(jnp.where(first_lanes, r[:GLA_DV], r[GLA_DV:]))
    st = [st_ref[p] for p in pairs]
    for c in chunks:
        for p in pairs:
            idx = c * len(pairs) + p
            va, vb = v_pair(c, p)
            o = lax.dot_general(lhs_q[idx], st[p].astype(bf16), NT_DIMS, preferred_element_type=f32)
            o = o + jnp.dot(att[idx], jnp.concatenate([va, vb], axis=0), preferred_element_type=f32)
            o_ref[rows_of(c), (2 * p) * GLA_DV:(2 * p + 1) * GLA_DV] = o[:ck]
            o_ref[rows_of(c), (2 * p + 1) * GLA_DV:(2 * p + 2) * GLA_DV] = o[ck:]
            st[p] = st[p] * elast[c][:, p * pair_w:(p + 1) * pair_w] + upd[idx]
    for p in pairs:
        st_ref[p] = st[p]

    sums = (p_ref, q_ref, p_ref, q_ref)
    pos = i * t + lax.broadcasted_iota(jnp.int32, (t, 1), 0)
    for gi, w in enumerate(POOL_WINDOWS):
        ls = slice(gi * gw, (gi + 1) * gw)
        cnt = jnp.minimum(w, pos + 1).astype(f32)
        pooled = (sums[gi][hist:n, ls] / cnt - u[:, ls]).astype(bf16)
        pg = jnp.dot(pooled, pw_ref[gi], preferred_element_type=f32) * ps_ref[:, ls]
        op_ref[:, vw + gi * gw:vw + (gi + 1) * gw] = pg.astype(bf16)
    e_ref[hist - 16:hist, :] = e_ref[n - 16:n, :]

    piece = 2 * GLA_DV
    y = x_ref[...]
    for c0 in (vw, vw + piece):
        y = y + jnp.dot(op_ref[:, c0:c0 + piece], wout_ref[c0:c0 + piece, :], preferred_element_type=f32)
    gain = gain_ref[...]
    for p in pairs:
        for h in (2 * p, 2 * p + 1):
            hs = slice(h * GLA_DV, (h + 1) * GLA_DV)
            g = qkvg_ref[:, 2 * kw + vw + h * GLA_DV:2 * kw + vw + (h + 1) * GLA_DV].astype(f32)
            op_ref[:, hs] = (_rms(o_ref[:, hs], gain) * _silu(g)).astype(bf16)
        c0 = p * piece
        y = y + jnp.dot(op_ref[:, c0:c0 + piece], wout_ref[c0:c0 + piece, :], preferred_element_type=f32)
    xo_ref[...] = y

    @pl.when(i == pl.num_programs(1) - 1)
    def _():
        for p in range(GLA_HEADS // 2):
            s_pair = st_ref[p].T
            so_ref[0, 2 * p] = s_pair[:GLA_DK]
            so_ref[0, 2 * p + 1] = s_pair[GLA_DK:]


def _gla_pool_prompt(qkvg, loga, u, x, tril, gain, pw, ps, wout, *, batch, t):
    m, d = x.shape
    nt = m // batch // t
    row = lambda b, i: (b * nt + i, 0)
    const2 = lambda b, i: (0, 0)
    vw = GLA_HEADS * GLA_DV
    uw = u.shape[1]
    return pl.pallas_call(
        _gla_pool_prompt_body,
        grid=(batch, nt),
        in_specs=[
            pl.BlockSpec((t, qkvg.shape[1]), row),
            pl.BlockSpec((t, loga.shape[1]), row),
            pl.BlockSpec((t, uw), row),
            pl.BlockSpec((t, d), row),
            pl.BlockSpec(tril.shape, const2),
            pl.BlockSpec(gain.shape, const2),
            pl.BlockSpec(pw.shape, lambda b, i: (0, 0, 0)),
            pl.BlockSpec(ps.shape, const2),
            pl.BlockSpec(wout.shape, const2),
        ],
        out_specs=[
            pl.BlockSpec((t, d), row),
            pl.BlockSpec((1, GLA_HEADS, GLA_DK, GLA_DV), lambda b, i: (b, 0, 0, 0)),
        ],
        out_shape=[
            jax.ShapeDtypeStruct((m, d), f32),
            jax.ShapeDtypeStruct((batch, GLA_HEADS, GLA_DK, GLA_DV), f32),
        ],
        scratch_shapes=[
            pltpu.VMEM((GLA_HEADS // 2, GLA_DV, 2 * GLA_DK), f32),
            pltpu.VMEM((t, vw), f32),
            pltpu.VMEM((POOL_HIST + t, uw), f32),
            pltpu.VMEM((POOL_HIST + t, uw), f32),
            pltpu.VMEM((POOL_HIST + t, uw), f32),
            pltpu.VMEM((t, vw + uw), bf16),
        ],
        compiler_params=_params(("arbitrary", "arbitrary"), 40),
        name="gla_pool_prompt",
    )(qkvg, loga, u, x, tril, gain, pw, ps, wout)


def _gla_pool_sample_body(qkvg_ref, loga_ref, u_ref, s_ref, buf_ref, gain_ref, pw_ref, ps_ref,
                          op_ref, so_ref):
    bb = u_ref.shape[0]
    kw = GLA_HEADS * GLA_DK
    vw = GLA_HEADS * GLA_DV
    gain = gain_ref[...]
    qkvg = qkvg_ref[...].astype(f32)
    alpha = jnp.exp(loga_ref[...])
    qs = qkvg[:, 0:kw] * (GLA_DK ** -0.5)
    k = qkvg[:, kw:2 * kw]

    def column(row):
        return jnp.broadcast_to(row, (LANES, kw)).T

    o_rows = []
    for b in range(bb):
        acol = column(alpha[b:b + 1, :])
        qcol = column(qs[b:b + 1, :])
        kcol = column(k[b:b + 1, :])
        o_heads = []
        for h in range(GLA_HEADS):
            ks = slice(h * GLA_DK, (h + 1) * GLA_DK)
            v = qkvg[b:b + 1, 2 * kw + h * GLA_DV:2 * kw + (h + 1) * GLA_DV]
            s_new = acol[ks, :] * s_ref[b, h] + kcol[ks, :] * v
            so_ref[b, h] = s_new
            o = jnp.sum(qcol[ks, :] * s_new, axis=0, keepdims=True)
            g = qkvg[b:b + 1, 2 * kw + vw + h * GLA_DV:2 * kw + vw + (h + 1) * GLA_DV]
            o_heads.append(_rms(o, gain) * _silu(g))
        o_rows.append(jnp.concatenate(o_heads, axis=1))
    op_ref[:, 0:vw] = jnp.concatenate(o_rows, axis=0).astype(bf16)

    u = u_ref[...]
    for gi, w in enumerate(POOL_WINDOWS):
        ls = slice(gi * POOL_GW, (gi + 1) * POOL_GW)
        s = u[:, ls] + jnp.sum(buf_ref[:, POOL_BUF - (w - 1):POOL_BUF, ls], axis=1)
        cnt = float(min(w, PAST_LEN + 1))
        pooled = (s / cnt - u[:, ls]).astype(bf16)
        pg = jnp.dot(pooled, pw_ref[gi], preferred_element_type=f32) * ps_ref[:, ls]
        op_ref[:, vw + gi * POOL_GW:vw + (gi + 1) * POOL_GW] = pg.astype(bf16)


def _gla_pool_sample(qkvg, loga, u, s, buf, gain, pw, ps, *, bb):
    n = u.shape[0]
    row = lambda i: (i, 0)
    const2 = lambda i: (0, 0)
    ow = GLA_HEADS * GLA_DV + POOL_GW * len(POOL_WINDOWS)
    return pl.pallas_call(
        _gla_pool_sample_body,
        grid=(n // bb,),
        in_specs=[
            pl.BlockSpec((bb, qkvg.shape[1]), row),
            pl.BlockSpec((bb, loga.shape[1]), row),
            pl.BlockSpec((bb, u.shape[1]), row),
            pl.BlockSpec((bb,) + s.shape[1:], lambda i: (i, 0, 0, 0)),
            pl.BlockSpec((bb,) + buf.shape[1:], lambda i: (i, 0, 0)),
            pl.BlockSpec(gain.shape, const2),
            pl.BlockSpec(pw.shape, lambda i: (0, 0, 0)),
            pl.BlockSpec(ps.shape, const2),
        ],
        out_specs=[
            pl.BlockSpec((bb, ow), row),
            pl.BlockSpec((bb,) + s.shape[1:], lambda i: (i, 0, 0, 0)),
        ],
        out_shape=[
            jax.ShapeDtypeStruct((n, ow), bf16),
            jax.ShapeDtypeStruct(s.shape, f32),
        ],
        compiler_params=_params(("arbitrary",), 32),
        name="gla_pool_sample",
    )(qkvg, loga, u, s, buf, gain, pw, ps)


def _proj_res_body(x_ref, a_ref, w_ref, o_ref):
    o_ref[...] = x_ref[...] + jnp.dot(a_ref[...], w_ref[...], preferred_element_type=f32)


def _proj_res(x, a, w, *, tm):
    m, d = x.shape
    return pl.pallas_call(
        _proj_res_body,
        grid=(m // tm,),
        in_specs=[
            pl.BlockSpec((tm, d), lambda i: (i, 0)),
            pl.BlockSpec((tm, a.shape[1]), lambda i: (i, 0)),
            pl.BlockSpec(w.shape, lambda i: (0, 0)),
        ],
        out_specs=pl.BlockSpec((tm, d), lambda i: (i, 0)),
        out_shape=jax.ShapeDtypeStruct((m, d), f32),
        compiler_params=_params(("arbitrary",), 32),
        name="proj_res",
    )(x, a, w)


def _ret_token_pieces(q_ref, k_ref, v_ref, g_ref, s_ref, og_ref, so_ref, gamma):
    def piece(j, h):
        def run():
            ks = slice(h * RET_DK, (h + 1) * RET_DK)
            vs = slice(h * RET_DV, (h + 1) * RET_DV)
            qcol = jnp.broadcast_to(q_ref[j, :, ks].astype(f32), (LANES, RET_DK)).T
            kcol = jnp.broadcast_to(k_ref[j, :, ks].astype(f32), (LANES, RET_DK)).T
            v = v_ref[j, :, vs].astype(f32)
            g = g_ref[j, :, vs].astype(f32)
            o_tiles = []
            for t in range(RET_DV // LANES):
                cs = slice(t * LANES, (t + 1) * LANES)
                s_new = gamma[h] * s_ref[j, h, :, cs] + kcol * v[:, cs]
                so_ref[j, h, :, cs] = s_new
                o_tiles.append(jnp.sum(qcol * s_new, axis=0, keepdims=True))
            o = jnp.concatenate(o_tiles, axis=1)
            og_ref[j, :, vs] = (_rms(o) * _silu(g)).astype(bf16)
        return run

    return [piece(j, h) for j in range(s_ref.shape[0]) for h in range(RET_HEADS)]


def _ffn_body(*refs, tf, n_sub, final_norm, rider_gamma):
    x_ref, gain_ref, wg_ref, wu_ref, wd_ref, fgain_ref = refs[:6]
    pieces = []
    if rider_gamma is None:
        o_ref, h_ref, acc_ref = refs[6:]
    else:
        rq_ref, rk_ref, rv_ref, rg_ref, rs_ref, o_ref, rog_ref, rso_ref, h_ref, acc_ref = refs[6:]
        pieces = _ret_token_pieces(rq_ref, rk_ref, rv_ref, rg_ref, rs_ref, rog_ref, rso_ref, rider_gamma)
    n_chunks = wg_ref.shape[1] // tf
    bounds = [n_chunks * s // n_sub for s in range(n_sub + 1)]

    def run_chunks(chunks):
        for n, c in enumerate(chunks):
            cs = slice(c * tf, (c + 1) * tf)
            g = jnp.dot(h_ref[...], wg_ref[:, cs], preferred_element_type=f32)
            u = jnp.dot(h_ref[...], wu_ref[:, cs], preferred_element_type=f32)
            a = (_silu(g) * u).astype(bf16)
            part = jnp.dot(a, wd_ref[cs, :], preferred_element_type=f32)
            if c == 0:
                acc_ref[...] = part
            else:
                acc_ref[...] += part
            for p in range(len(pieces)):
                if p * len(chunks) // len(pieces) == n:
                    pieces[p]()

    def sub_step(s):
        if s == 0:
            h_ref[...] = _rms(x_ref[...], gain_ref[...]).astype(bf16)
        run_chunks(range(bounds[s], bounds[s + 1]))
        if s == n_sub - 1:
            y = x_ref[...] + acc_ref[...]
            if final_norm:
                y = _rms(y, fgain_ref[...])
            o_ref[...] = y

    if n_sub == 1:
        sub_step(0)
    else:
        for s in range(n_sub):
            pl.when(pl.program_id(1) == s)(functools.partial(sub_step, s))


def _ffn(x, gain, wg, wu, wd, fgain, *, layer, tm, tf, final_norm, rider=None):
    m, d = x.shape
    ff = wg.shape[2]
    steps = m // tm
    n_sub = 1 if rider is None else 2
    resident = dict(pipeline_mode=pl.Buffered(1))
    in_specs = [
        pl.BlockSpec((tm, d), lambda i, s: (i, 0)),
        pl.BlockSpec((None, 1, d), lambda i, s: (layer, 0, 0)),
        pl.BlockSpec((None, d, ff), lambda i, s: (layer, 0, 0), **resident),
        pl.BlockSpec((None, d, ff), lambda i, s: (layer, 0, 0), **resident),
        pl.BlockSpec((None, ff, d), lambda i, s: (layer, 0, 0), **resident),
        pl.BlockSpec((1, d), lambda i, s: (0, 0)),
    ]
    args = [x, gain, wg, wu, wd, fgain]
    out_specs = [pl.BlockSpec((tm, d), lambda i, s: (i, 0))]
    out_shape = [jax.ShapeDtypeStruct((m, d), f32)]
    gamma = None
    vmem = 48
    if rider is not None:
        qkvg3, state, rows, gamma = rider
        assert 2 * steps * rows == state.shape[0]
        qw = RET_HEADS * RET_DK
        vw = RET_HEADS * RET_DV
        blk = lambda col: (lambda i, s: (2 * i + s, 0, col))
        state_spec = pl.BlockSpec((rows,) + state.shape[1:], lambda i, s: (2 * i + s, 0, 0, 0))
        in_specs += [
            pl.BlockSpec((rows, 1, qw), blk(0)),
            pl.BlockSpec((rows, 1, qw), blk(1)),
            pl.BlockSpec((rows, 1, vw), blk(1)),
            pl.BlockSpec((rows, 1, vw), blk(2)),
            state_spec,
        ]
        args += [qkvg3, qkvg3, qkvg3, qkvg3, state]
        out_specs += [pl.BlockSpec((rows, 1, vw), blk(0)), state_spec]
        out_shape += [
            jax.ShapeDtypeStruct((state.shape[0], 1, vw), bf16),
            jax.ShapeDtypeStruct(state.shape, f32),
        ]
        vmem = 56
    out = pl.pallas_call(
        functools.partial(_ffn_body, tf=tf, n_sub=n_sub, final_norm=final_norm, rider_gamma=gamma),
        grid=(steps, n_sub),
        in_specs=in_specs,
        out_specs=out_specs,
        out_shape=out_shape,
        scratch_shapes=[pltpu.VMEM((tm, d), bf16), pltpu.VMEM((tm, d), f32)],
        compiler_params=_params(("arbitrary", "arbitrary"), vmem),
        name="ffn_final" if final_norm else "ffn",
    )(*args)
    return out[0] if rider is None else out


def _in_odd_body(x_ref, gain_ref, w_ref, perm_ref, cos_ref, sin_ref, qsc_ref, ksc_ref, o_ref, h_ref, wqk_ref, *,
                 tn, split_halves):
    qw = RET_HEADS * RET_DK
    half = RET_DK // 2
    if split_halves:
        @pl.when(pl.program_id(0) == 0)
        def _():
            for hh in range(2 * RET_HEADS):
                hs = slice(hh * RET_DK, (hh + 1) * RET_DK)
                wqk_ref[:, hs] = jnp.dot(w_ref[:, hs], perm_ref[...], preferred_element_type=f32).astype(bf16)

    h_ref[...] = _rms(x_ref[...], gain_ref[...]).astype(bf16)
    cos = cos_ref[...]
    sin = sin_ref[...]
    for c in range(2 * qw // tn):
        c0 = c * tn
        w_chunk = wqk_ref[:, c0:c0 + tn] if split_halves else w_ref[:, c0:c0 + tn]
        p = jnp.dot(h_ref[...], w_chunk, preferred_element_type=f32)
        sc_ref = qsc_ref if c0 < qw else ksc_ref
        for hh in range(tn // RET_DK):
            h0 = hh * RET_DK
            head = (c0 % qw + h0) // RET_DK
            sc = sc_ref[:, head * LANES:(head + 1) * LANES]
            if split_halves:
                ev = p[:, h0:h0 + half]
                od = p[:, h0 + half:h0 + RET_DK]
                o_ref[:, c0 + h0:c0 + h0 + half] = ((ev * cos - od * sin) * sc).astype(bf16)
                o_ref[:, c0 + h0 + half:c0 + h0 + RET_DK] = ((od * cos + ev * sin) * sc).astype(bf16)
            else:
                xh = p[:, h0:h0 + RET_DK]
                even = lax.broadcasted_iota(jnp.int32, xh.shape, 1) % 2 == 0
                partner = jnp.where(even, pltpu.roll(xh, RET_DK - 1, 1), pltpu.roll(xh, 1, 1))
                r = xh * cos + partner * sin
                o_ref[:, c0 + h0:c0 + h0 + half] = (r[:, :half] * sc).astype(bf16)
                o_ref[:, c0 + h0 + half:c0 + h0 + RET_DK] = (r[:, half:] * sc).astype(bf16)
    for c0 in range(2 * qw, w_ref.shape[1], tn):
        p = jnp.dot(h_ref[...], w_ref[:, c0:c0 + tn], preferred_element_type=f32)
        o_ref[:, c0:c0 + tn] = p.astype(bf16)


def _in_odd(x, gain, w, perm, tables, *, layer, tm, tn, split_halves):
    m, d = x.shape
    n = w.shape[1]
    cos, sin, qsc, ksc = tables
    ntab = cos.shape[0] // tm
    qkw = 2 * RET_HEADS * RET_DK
    resident = dict(pipeline_mode=pl.Buffered(1))
    const = lambda i: (0, 0)
    rope_spec = pl.BlockSpec((tm, cos.shape[1]), lambda i: (i % ntab, 0))
    return pl.pallas_call(
        functools.partial(_in_odd_body, tn=tn, split_halves=split_halves),
        grid=(m // tm,),
        in_specs=[
            pl.BlockSpec((tm, d), lambda i: (i, 0)),
            pl.BlockSpec((None, 1, d), lambda i: (layer, 0, 0)),
            pl.BlockSpec(w.shape, const, **resident),
            pl.BlockSpec(perm.shape, const),
            rope_spec, rope_spec,
            pl.BlockSpec(qsc.shape, const),
            pl.BlockSpec(ksc.shape, const),
        ],
        out_specs=pl.BlockSpec((tm, n), lambda i: (i, 0)),
        out_shape=jax.ShapeDtypeStruct((m, n), bf16),
        scratch_shapes=[pltpu.VMEM((tm, d), bf16),
                        pltpu.VMEM((d, qkw) if split_halves else (8, LANES), bf16)],
        compiler_params=_params(("arbitrary",), 56),
        name="in_odd",
    )(x, gain, w, perm, cos, sin, qsc, ksc)


def _ret_prompt_body(q_ref, k_ref, v_ref, g_ref, x_ref, wout_ref, xo_ref, so_ref, s_ref, sb_ref, slab_ref, *,
                     gamma_c, n):
    c = pl.program_id(1)
    subs = [slice(j * n, (j + 1) * n) for j in range(q_ref.shape[0] // n)]

    @pl.when(c == 0)
    def _():
        s_ref[...] = jnp.zeros_like(s_ref)
        sb_ref[...] = jnp.zeros_like(sb_ref)

    causal = lax.broadcasted_iota(jnp.int32, (n, n), 0) >= lax.broadcasted_iota(jnp.int32, (n, n), 1)
    heads = range(RET_HEADS)
    ks = [slice(h * RET_DK, (h + 1) * RET_DK) for h in heads]
    vs = [slice(h * RET_DV, (h + 1) * RET_DV) for h in heads]
    att = [[jnp.where(causal, lax.dot_general(q_ref[r, ks[h]], k_ref[r, ks[h]], NT_DIMS,
                                              preferred_element_type=f32), 0.0).astype(bf16) for h in heads]
           for r in subs]
    o = []
    for j, r in enumerate(subs):
        o.append([jnp.dot(q_ref[r, ks[h]], sb_ref[h], preferred_element_type=f32)
                  + jnp.dot(att[j][h], v_ref[r, vs[h]], preferred_element_type=f32) for h in heads])
        for h in heads:
            kv = lax.dot_general(k_ref[r, ks[h]], v_ref[r, vs[h]], TN_DIMS, preferred_element_type=f32)
            s_new = gamma_c[h] * (s_ref[h] + kv)
            s_ref[h] = s_new
            sb_ref[h] = s_new.astype(bf16)
    for j, r in enumerate(subs):
        y = x_ref[r, :]
        for h in heads:
            og = (_rms(o[j][h]) * _silu(g_ref[r, vs[h]].astype(f32))).astype(bf16)
            y = y + jnp.dot(og, wout_ref[vs[h], :], preferred_element_type=f32)
        xo_ref[r, :] = y

    @pl.when(c == pl.num_programs(1) - 1)
    def _():
        half = RET_DK // 2
        for h in range(RET_HEADS):
            for t in range(RET_DV // LANES):
                ls = slice(t * LANES, (t + 1) * LANES)
                slab_ref[pl.ds(0, half, stride=2), :] = s_ref[h, 0:half, ls]
                slab_ref[pl.ds(1, half, stride=2), :] = s_ref[h, half:RET_DK, ls]
                so_ref[0, h, :, ls] = slab_ref[...]


def _ret_prompt(qkvg, x, wout, gamma_c, *, batch, c, chunk):
    m, d = x.shape
    nc = m // batch // c
    qw = RET_HEADS * RET_DK
    vw = RET_HEADS * RET_DV
    assert c % chunk == 0
    return pl.pallas_call(
        functools.partial(_ret_prompt_body, gamma_c=gamma_c, n=chunk),
        grid=(batch, nc),
        in_specs=[
            pl.BlockSpec((c, qw), lambda b, i: (b * nc + i, 0)),
            pl.BlockSpec((c, qw), lambda b, i: (b * nc + i, 1)),
            pl.BlockSpec((c, vw), lambda b, i: (b * nc + i, 1)),
            pl.BlockSpec((c, vw), lambda b, i: (b * nc + i, 2)),
            pl.BlockSpec((c, d), lambda b, i: (b * nc + i, 0)),
            pl.BlockSpec(wout.shape, lambda b, i: (0, 0), pipeline_mode=pl.Buffered(1)),
        ],
        out_specs=[
            pl.BlockSpec((c, d), lambda b, i: (b * nc + i, 0)),
            pl.BlockSpec((1, RET_HEADS, RET_DK, RET_DV), lambda b, i: (b, 0, 0, 0)),
        ],
        out_shape=[
            jax.ShapeDtypeStruct((m, d), f32),
            jax.ShapeDtypeStruct((batch, RET_HEADS, RET_DK, RET_DV), f32),
        ],
        scratch_shapes=[
            pltpu.VMEM((RET_HEADS, RET_DK, RET_DV), f32),
            pltpu.VMEM((RET_HEADS, RET_DK, RET_DV), bf16),
            pltpu.VMEM((RET_DK, LANES), f32),
        ],
        compiler_params=_params(("arbitrary", "arbitrary"), 48),
        name="ret_prompt",
    )(qkvg, qkvg, qkvg, qkvg, x, wout)


def _rope_tables(pos, per_pair):
    pair_angle = 1.0 / (ROPE_BASE ** jnp.linspace(0.0, 1.0, RET_DK // 2, dtype=f32))
    if per_pair:
        ang = pos[:, None] * pair_angle[None, :]
        return jnp.cos(ang), jnp.sin(ang)
    ang = pos[:, None] * jnp.repeat(pair_angle, 2)[None, :]
    sign = jnp.where(jnp.arange(RET_DK) % 2 == 0, -1.0, 1.0).astype(f32)
    return jnp.cos(ang), jnp.sin(ang) * sign


def _even_odd_perm():
    half = RET_DK // 2
    src = np.concatenate([2 * np.arange(half), 2 * np.arange(half) + 1])
    perm = np.zeros((RET_DK, RET_DK), np.float32)
    perm[src, np.arange(RET_DK)] = 1.0
    return jnp.asarray(perm, dtype=bf16)


def _lane_replicated(scale):
    return jnp.asarray(np.repeat(scale, LANES, axis=1), dtype=f32)


def _ret_decay(rows, c):
    gam = 1.0 - 2.0 ** (-5.0 - np.arange(RET_HEADS, dtype=np.float64))
    lg = np.log(gam)
    steps = (np.arange(rows) % c + 1.0)[:, None]
    q_scale = _lane_replicated(np.exp(lg[None, :] * steps))
    k_scale = _lane_replicated(np.exp(-lg[None, :] * steps) * RET_DK ** -0.5)
    gamma_c = tuple(float(x) for x in np.exp(lg * c))
    gamma = tuple(float(x) for x in gam)
    return q_scale, k_scale, gamma_c, gamma


def kernel(x_prompt, x_sample, state_gla, state_pool, state_ret, norm_mix, norm_ffn, norm_final, w_in_even,
           w_gate_b, b_gate, gla_gain, pool_w, pool_scale, w_out_even, w_in_odd, w_out_odd, w_ffn_gate,
           w_ffn_up, w_ffn_down):
    batch, seq, d = x_prompt.shape
    n_s = x_sample.shape[0]
    assert norm_mix.shape[0] == 2 and x_sample.shape[1] == 1

    we = jnp.pad(w_in_even[0], ((0, 0), (0, -w_in_even.shape[2] % LANES))).astype(bf16)
    nu = POOL_GW * len(POOL_WINDOWS)
    shift = np.zeros((nu + LANES, nu), np.float32)
    shift[GATE_RANK + np.arange(nu), np.arange(nu)] = 1.0
    shift = jnp.asarray(shift, dtype=bf16)
    wgb = jnp.concatenate([w_gate_b[0], jnp.zeros((LANES - GATE_RANK, w_gate_b.shape[2]), f32)], axis=0).astype(bf16)
    bg = b_gate[0][None, :]
    gg = gla_gain[0][None, :]
    pw = pool_w[0].astype(bf16)
    ps = pool_scale[0][None, :]
    woe = w_out_even[0].astype(bf16)
    wio = w_in_odd[0].astype(bf16)
    woo = w_out_odd[0].astype(bf16)
    wg = w_ffn_gate.astype(bf16)
    wu = w_ffn_up.astype(bf16)
    wd = w_ffn_down.astype(bf16)
    nm = norm_mix[:, None, :]
    nf = norm_ffn[:, None, :]
    nfin = norm_final[None, :]
    tril = jnp.asarray(np.tril(np.ones((GLA_CHUNK, GLA_CHUNK), np.float32)), dtype=bf16)
    tf = 256
    tm_p = 512
    q_scale, k_scale, gamma_c, gamma = _ret_decay(tm_p, RET_CHUNK)
    tables_p = _rope_tables(jnp.arange(seq, dtype=f32), True) + (q_scale, k_scale)
    tables_s = _rope_tables(jnp.full((n_s,), float(PAST_LEN), f32), False) + (
        _lane_replicated(np.ones((n_s, RET_HEADS))), _lane_replicated(np.full((n_s, RET_HEADS), RET_DK ** -0.5)))
    perm = _even_odd_perm()

    xs = x_sample.reshape(n_s, d)
    qkvg_s, loga_s, u_s = _in_even(xs, nm[0], we, shift, wgb, bg, tm=n_s)
    op_s, gla_s = _gla_pool_sample(qkvg_s, loga_s, u_s, state_gla[0], state_pool[0], gg, pw, ps, bb=8)
    xs = _proj_res(xs, op_s, woe, tm=n_s)
    xs = _ffn(xs, nf, wg, wu, wd, nfin, layer=0, tm=n_s, tf=tf, final_norm=False)
    qkvg2_s = _in_odd(xs, nm, wio, perm, tables_s, layer=1, tm=n_s, tn=512, split_halves=False)
    qkvg2_s = qkvg2_s.reshape(n_s, 1, -1)

    rows = n_s // (2 * (batch * seq // tm_p))
    xp = x_prompt.reshape(batch * seq, d)
    qkvg, loga, u_p = _in_even(xp, nm[0], we, shift, wgb, bg, tm=tm_p)
    xp, gla_p = _gla_pool_prompt(qkvg, loga, u_p, xp, tril, gg, pw, ps, woe, batch=batch, t=512)
    xp, og_s, ret_s = _ffn(xp, nf, wg, wu, wd, nfin, layer=0, tm=tm_p, tf=tf, final_norm=False,
                           rider=(qkvg2_s, state_ret[0], rows, gamma))
    qkvg2 = _in_odd(xp, nm, wio, perm, tables_p, layer=1, tm=tm_p, tn=512, split_halves=True)
    xp, ret_p = _ret_prompt(qkvg2, xp, woo, gamma_c, batch=batch, c=2 * RET_CHUNK, chunk=RET_CHUNK)
    y_prompt = _ffn(xp, nf, wg, wu, wd, nfin, layer=1, tm=tm_p, tf=tf, final_norm=True)
    pool_p = u_p.reshape(batch, seq, -1)[:, seq - POOL_BUF:, :]

    xs = _proj_res(xs, og_s.reshape(n_s, -1), woo, tm=n_s)
    y_sample = _ffn(xs, nf, wg, wu, wd, nfin, layer=1, tm=n_s, tf=tf, final_norm=True)
    pool_s = jnp.concatenate([state_pool[0][:, 1:, :], u_s[:, None, :]], axis=1)

    return (y_prompt.reshape(batch, seq, d), y_sample.reshape(n_s, 1, d),
            gla_p[None], gla_s[None], pool_p[None], pool_s[None], ret_p[None], ret_s[None])
```

```python
import functools

import numpy as np
import jax
import jax.numpy as jnp
from jax import lax
from jax.experimental import pallas as pl
from jax.experimental.pallas import tpu as pltpu

f32 = jnp.float32
bf16 = jnp.bfloat16

EPS = 1e-6
PAST_LEN = 16384
GLA_HEADS, GLA_DK, GLA_DV = 4, 64, 128
GLA_CHUNK = 64
GATE_RANK = 16
GATE_NORMALIZER = 16.0
POOL_WINDOWS = (2, 4, 8, 16)
POOL_GW = 128
POOL_BUF = max(POOL_WINDOWS) - 1
POOL_HIST = 32
RET_HEADS, RET_DK, RET_DV = 4, 256, 512
RET_CHUNK = 256
ROPE_BASE = 10000.0
LANES = 128
MIB = 1024 * 1024

NT_DIMS = (((1,), (1,)), ((), ()))
TN_DIMS = (((0,), (0,)), ((), ()))


def _params(semantics, vmem_mib):
    return pltpu.CompilerParams(dimension_semantics=semantics, vmem_limit_bytes=vmem_mib * MIB)


def _rms(x, gain=None):
    y = x * lax.rsqrt(jnp.mean(x * x, axis=-1, keepdims=True) + EPS)
    return y if gain is None else y * gain


def _silu(g):
    return g * jax.nn.sigmoid(g)


def _in_even_body(x_ref, gain_ref, w_ref, shift_ref, wgb_ref, bg_ref, qkvg_ref, loga_ref, u_ref, h_ref, wu_ref, *,
                  tn):
    nq = qkvg_ref.shape[1]
    nu = u_ref.shape[1]

    @pl.when(pl.program_id(0) == 0)
    def _():
        wu_ref[...] = jnp.dot(w_ref[:, nq:], shift_ref[...], preferred_element_type=f32).astype(bf16)

    h_ref[...] = _rms(x_ref[...], gain_ref[...]).astype(bf16)
    a = jnp.dot(h_ref[...], w_ref[:, nq:nq + LANES], preferred_element_type=f32)
    a = jnp.where(lax.broadcasted_iota(jnp.int32, a.shape, 1) < GATE_RANK, a, 0.0).astype(bf16)
    for c0 in range(0, nq, tn):
        qkvg_ref[:, c0:c0 + tn] = jnp.dot(h_ref[...], w_ref[:, c0:c0 + tn], preferred_element_type=f32).astype(bf16)
        if c0 == 0:
            z = jnp.dot(a, wgb_ref[...], preferred_element_type=f32) + bg_ref[...]
            loga_ref[...] = (jnp.minimum(z, 0.0) - jnp.log1p(jnp.exp(-jnp.abs(z)))) * (1.0 / GATE_NORMALIZER)
    for c0 in range(0, nu, tn):
        u_ref[:, c0:c0 + tn] = jnp.dot(h_ref[...], wu_ref[:, c0:c0 + tn], preferred_element_type=f32)


def _in_even(x, gain, w, shift, wgb, bg, *, tm):
    m, d = x.shape
    nq = 2 * GLA_HEADS * GLA_DK + 2 * GLA_HEADS * GLA_DV
    nu = POOL_GW * len(POOL_WINDOWS)
    nk = GLA_HEADS * GLA_DK
    const = lambda i: (0, 0)
    return pl.pallas_call(
        functools.partial(_in_even_body, tn=512),
        grid=(m // tm,),
        in_specs=[
            pl.BlockSpec((tm, d), lambda i: (i, 0)),
            pl.BlockSpec((1, d), const),
            pl.BlockSpec(w.shape, const, pipeline_mode=pl.Buffered(1)),
            pl.BlockSpec(shift.shape, const),
            pl.BlockSpec(wgb.shape, const),
            pl.BlockSpec((1, nk), const),
        ],
        out_specs=[
            pl.BlockSpec((tm, nq), lambda i: (i, 0)),
            pl.BlockSpec((tm, nk), lambda i: (i, 0)),
            pl.BlockSpec((tm, nu), lambda i: (i, 0)),
        ],
        out_shape=[
            jax.ShapeDtypeStruct((m, nq), bf16),
            jax.ShapeDtypeStruct((m, nk), f32),
            jax.ShapeDtypeStruct((m, nu), f32),
        ],
        scratch_shapes=[pltpu.VMEM((tm, d), bf16), pltpu.VMEM((d, nu), bf16)],
        compiler_params=_params(("arbitrary",), 48),
        name="in_even",
    )(x, gain, w, shift, wgb, bg)


def _gla_pool_prompt_body(qkvg_ref, loga_ref, u_ref, x_ref, tril_ref, gain_ref, pw_ref, ps_ref, wout_ref,
                          xo_ref, so_ref, st_ref, o_ref, e_ref, p_ref, q_ref, op_ref, after_chunk=None):
    t = x_ref.shape[0]
    ck = GLA_CHUNK
    kw = GLA_HEADS * GLA_DK
    vw = GLA_HEADS * GLA_DV
    pair_w = 2 * GLA_DK
    i = pl.program_id(1)

    @pl.when(i == 0)
    def _():
        st_ref[...] = jnp.zeros_like(st_ref)
        e_ref[0:POOL_HIST, :] = jnp.zeros((POOL_HIST, e_ref.shape[1]), f32)

    tril = tril_ref[...]
    row = lax.broadcasted_iota(jnp.int32, (2 * ck, pair_w), 0)
    lane = lax.broadcasted_iota(jnp.int32, (2 * ck, pair_w), 1)
    first_lanes = lane < GLA_DK
    first_lanes_ck = lax.broadcasted_iota(jnp.int32, (ck, pair_w), 1) < GLA_DK
    same_head = (row < ck) == first_lanes
    causal = same_head & ((row % ck) >= (lane % GLA_DK))
    pairs = range(GLA_HEADS // 2)
    chunks = range(t // ck)

    hist = POOL_HIST
    n = t + hist
    gw = POOL_GW
    u = u_ref[...]
    e_ref[hist:n, :] = u
    p_ref[8:n, :] = e_ref[8:n, :] + e_ref[7:n - 1, :]
    q_ref[16:n, gw:] = p_ref[16:n, gw:] + p_ref[14:n - 2, gw:]
    p_ref[24:n, 2 * gw:] = q_ref[24:n, 2 * gw:] + q_ref[20:n - 4, 2 * gw:]
    q_ref[32:n, 3 * gw:] = p_ref[32:n, 3 * gw:] + p_ref[24:n - 8, 3 * gw:]

    def rows_of(c):
        return slice(c * ck, (c + 1) * ck)

    def v_pair(c, p):
        va = qkvg_ref[rows_of(c), 2 * kw + (2 * p) * GLA_DV:2 * kw + (2 * p + 1) * GLA_DV]
        vb = qkvg_ref[rows_of(c), 2 * kw + (2 * p + 1) * GLA_DV:2 * kw + (2 * p + 2) * GLA_DV]
        return va, vb

    bcs = []
    for c in chunks:
        la = loga_ref[rows_of(c), :]
        la_hi = la.astype(bf16)
        la_lo = (la - la_hi.astype(f32)).astype(bf16)
        bcs.append(jnp.dot(tril, la_hi, preferred_element_type=f32) + jnp.dot(tril, la_lo, preferred_element_type=f32))
    lhs_q, ke2, kds, elast = [], [], [], []
    for c in chunks:
        bc = bcs[c]
        blast = bc[ck - 1:ck, :]
        q = qkvg_ref[rows_of(c), 0:kw].astype(f32) * (GLA_DK ** -0.5)
        k = qkvg_ref[rows_of(c), kw:2 * kw].astype(f32)
        qe = q * jnp.exp(bc)
        ke = (k * jnp.exp(-bc)).astype(bf16)
        kds.append((k * jnp.exp(blast - bc)).astype(bf16))
        elast.append(jnp.exp(blast))
        for p in pairs:
            pl_ = slice(p * pair_w, (p + 1) * pair_w)
            qe_p = qe[:, pl_]
            lhs_q.append(jnp.concatenate([jnp.where(first_lanes_ck, qe_p, 0.0),
                                          jnp.where(first_lanes_ck, 0.0, qe_p)], axis=0).astype(bf16))
            ke2.append(jnp.concatenate([ke[:, pl_], ke[:, pl_]], axis=0))
    att, upd = [], []
    for c in chunks:
        for p in pairs:
            idx = c * len(pairs) + p
            a = lax.dot_general(lhs_q[idx], ke2[idx], NT_DIMS, preferred_element_type=f32)
            att.append(jnp.where(causal, a, 0.0).astype(bf16))
            va, vb = v_pair(c, p)
            r = lax.dot_general(jnp.concatenate([va, vb], axis=1), kds[c][:, p * pair_w:(p + 1) * pair_w], TN_DIMS,
                                preferred_element_type=f32)
            upd.append(jnp.where(first_lanes, r[:GLA_DV], r[GLA_DV:]))
    st = [st_ref[p] for p in pairs]
    for c in chunks:
        for p in pairs:
            idx = c * len(pairs) + p
            va, vb = v_pair(c, p)
            o = lax.dot_general(lhs_q[idx], st[p].astype(bf16), NT_DIMS, preferred_element_type=f32)
            o = o + jnp.dot(att[idx], jnp.concatenate([va, vb], axis=0), preferred_element_type=f32)
            o_ref[rows_of(c), (2 * p) * GLA_DV:(2 * p + 1) * GLA_DV] = o[:ck]
            o_ref[rows_of(c), (2 * p + 1) * GLA_DV:(2 * p + 2) * GLA_DV] = o[ck:]
            st[p] = st[p] * elast[c][:, p * pair_w:(p + 1) * pair_w] + upd[idx]
    for p in pairs:
        st_ref[p] = st[p]

    sums = (p_ref, q_ref, p_ref, q_ref)
    pos = i * t + lax.broadcasted_iota(jnp.int32, (t, 1), 0)
    for gi, w in enumerate(POOL_WINDOWS):
        ls = slice(gi * gw, (gi + 1) * gw)
        cnt = jnp.minimum(w, pos + 1).astype(f32)
        pooled = (sums[gi][hist:n, ls] / cnt - u[:, ls]).astype(bf16)
        pg = jnp.dot(pooled, pw_ref[gi], preferred_element_type=f32) * ps_ref[:, ls]
        op_ref[:, vw + gi * gw:vw + (gi + 1) * gw] = pg.astype(bf16)
    e_ref[hist - 16:hist, :] = e_ref[n - 16:n, :]

    piece = 2 * GLA_DV
    y = x_ref[...]
    for c0 in (vw, vw + piece):
        y = y + jnp.dot(op_ref[:, c0:c0 + piece], wout_ref[c0:c0 + piece, :], preferred_element_type=f32)
    gain = gain_ref[...]
    for p in pairs:
        for h in (2 * p, 2 * p + 1):
            hs = slice(h * GLA_DV, (h + 1) * GLA_DV)
            g = qkvg_ref[:, 2 * kw + vw + h * GLA_DV:2 * kw + vw + (h + 1) * GLA_DV].astype(f32)
            op_ref[:, hs] = (_rms(o_ref[:, hs], gain) * _silu(g)).astype(bf16)
        c0 = p * piece
        y = y + jnp.dot(op_ref[:, c0:c0 + piece], wout_ref[c0:c0 + piece, :], preferred_element_type=f32)
    xo_ref[...] = y

    @pl.when(i == pl.num_programs(1) - 1)
    def _():
        for p in range(GLA_HEADS // 2):
            s_pair = st_ref[p].T
            so_ref[0, 2 * p] = s_pair[:GLA_DK]
            so_ref[0, 2 * p + 1] = s_pair[GLA_DK:]


def _gla_pool_prompt(qkvg, loga, u, x, tril, gain, pw, ps, wout, *, batch, t):
    m, d = x.shape
    nt = m // batch // t
    row = lambda b, i: (b * nt + i, 0)
    const2 = lambda b, i: (0, 0)
    vw = GLA_HEADS * GLA_DV
    uw = u.shape[1]
    return pl.pallas_call(
        _gla_pool_prompt_body,
        grid=(batch, nt),
        in_specs=[
            pl.BlockSpec((t, qkvg.shape[1]), row),
            pl.BlockSpec((t, loga.shape[1]), row),
            pl.BlockSpec((t, uw), row),
            pl.BlockSpec((t, d), row),
            pl.BlockSpec(tril.shape, const2),
            pl.BlockSpec(gain.shape, const2),
            pl.BlockSpec(pw.shape, lambda b, i: (0, 0, 0)),
            pl.BlockSpec(ps.shape, const2),
            pl.BlockSpec(wout.shape, const2),
        ],
        out_specs=[
            pl.BlockSpec((t, d), row),
            pl.BlockSpec((1, GLA_HEADS, GLA_DK, GLA_DV), lambda b, i: (b, 0, 0, 0)),
        ],
        out_shape=[
            jax.ShapeDtypeStruct((m, d), f32),
            jax.ShapeDtypeStruct((batch, GLA_HEADS, GLA_DK, GLA_DV), f32),
        ],
        scratch_shapes=[
            pltpu.VMEM((GLA_HEADS // 2, GLA_DV, 2 * GLA_DK), f32),
            pltpu.VMEM((t, vw), f32),
            pltpu.VMEM((POOL_HIST + t, uw), f32),
            pltpu.VMEM((POOL_HIST + t, uw), f32),
            pltpu.VMEM((POOL_HIST + t, uw), f32),
            pltpu.VMEM((t, vw + uw), bf16),
        ],
        compiler_params=_params(("arbitrary", "arbitrary"), 56),
        name="gla_pool_prompt",
    )(qkvg, loga, u, x, tril, gain, pw, ps, wout)


def _gla_pool_sample_body(qkvg_ref, loga_ref, u_ref, s_ref, buf_ref, gain_ref, pw_ref, ps_ref,
                          op_ref, so_ref, bo_ref):
    bb = u_ref.shape[0]
    kw = GLA_HEADS * GLA_DK
    vw = GLA_HEADS * GLA_DV
    gain = gain_ref[...]
    qkvg = qkvg_ref[...].astype(f32)
    alpha = jnp.exp(loga_ref[...])
    qs = qkvg[:, 0:kw] * (GLA_DK ** -0.5)
    k = qkvg[:, kw:2 * kw]

    def column(row):
        return jnp.broadcast_to(row, (LANES, kw)).T

    o_rows = []
    for b in range(bb):
        acol = column(alpha[b:b + 1, :])
        qcol = column(qs[b:b + 1, :])
        kcol = column(k[b:b + 1, :])
        o_heads = []
        for h in range(GLA_HEADS):
            ks = slice(h * GLA_DK, (h + 1) * GLA_DK)
            v = qkvg[b:b + 1, 2 * kw + h * GLA_DV:2 * kw + (h + 1) * GLA_DV]
            s_new = acol[ks, :] * s_ref[b, h] + kcol[ks, :] * v
            so_ref[b, h] = s_new
            o = jnp.sum(qcol[ks, :] * s_new, axis=0, keepdims=True)
            g = qkvg[b:b + 1, 2 * kw + vw + h * GLA_DV:2 * kw + vw + (h + 1) * GLA_DV]
            o_heads.append(_rms(o, gain) * _silu(g))
        o_rows.append(jnp.concatenate(o_heads, axis=1))
    op_ref[:, 0:vw] = jnp.concatenate(o_rows, axis=0).astype(bf16)

    u = u_ref[...]
    for gi, w in enumerate(POOL_WINDOWS):
        ls = slice(gi * POOL_GW, (gi + 1) * POOL_GW)
        s = u[:, ls] + jnp.sum(buf_ref[:, POOL_BUF - (w - 1):POOL_BUF, ls], axis=1)
        cnt = float(min(w, PAST_LEN + 1))
        pooled = (s / cnt - u[:, ls]).astype(bf16)
        pg = jnp.dot(pooled, pw_ref[gi], preferred_element_type=f32) * ps_ref[:, ls]
        op_ref[:, vw + gi * POOL_GW:vw + (gi + 1) * POOL_GW] = pg.astype(bf16)
    for b in range(bb):
        bo_ref[b, 0:POOL_BUF - 1, :] = buf_ref[b, 1:POOL_BUF, :]
        bo_ref[b, POOL_BUF - 1:POOL_BUF, :] = u[b:b + 1, :]


def _gla_pool_sample(qkvg, loga, u, s, buf, gain, pw, ps, *, bb):
    n = u.shape[0]
    row = lambda i: (i, 0)
    const2 = lambda i: (0, 0)
    ow = GLA_HEADS * GLA_DV + POOL_GW * len(POOL_WINDOWS)
    return pl.pallas_call(
        _gla_pool_sample_body,
        grid=(n // bb,),
        in_specs=[
            pl.BlockSpec((bb, qkvg.shape[1]), row),
            pl.BlockSpec((bb, loga.shape[1]), row),
            pl.BlockSpec((bb, u.shape[1]), row),
            pl.BlockSpec((bb,) + s.shape[1:], lambda i: (i, 0, 0, 0)),
            pl.BlockSpec((bb,) + buf.shape[1:], lambda i: (i, 0, 0)),
            pl.BlockSpec(gain.shape, const2),
            pl.BlockSpec(pw.shape, lambda i: (0, 0, 0)),
            pl.BlockSpec(ps.shape, const2),
        ],
        out_specs=[
            pl.BlockSpec((bb, ow), row),
            pl.BlockSpec((bb,) + s.shape[1:], lambda i: (i, 0, 0, 0)),
            pl.BlockSpec((bb,) + buf.shape[1:], lambda i: (i, 0, 0)),
        ],
        out_shape=[
            jax.ShapeDtypeStruct((n, ow), bf16),
            jax.ShapeDtypeStruct(s.shape, f32),
            jax.ShapeDtypeStruct(buf.shape, buf.dtype),
        ],
        compiler_params=_params(("arbitrary",), 32),
        name="gla_pool_sample",
    )(qkvg, loga, u, s, buf, gain, pw, ps)


def _proj_res_body(x_ref, a_ref, w_ref, o_ref):
    o_ref[...] = x_ref[...] + jnp.dot(a_ref[...], w_ref[...], preferred_element_type=f32)


def _proj_res(x, a, w, *, tm):
    m, d = x.shape
    return pl.pallas_call(
        _proj_res_body,
        grid=(m // tm,),
        in_specs=[
            pl.BlockSpec((tm, d), lambda i: (i, 0)),
            pl.BlockSpec((tm, a.shape[1]), lambda i: (i, 0)),
            pl.BlockSpec(w.shape, lambda i: (0, 0)),
        ],
        out_specs=pl.BlockSpec((tm, d), lambda i: (i, 0)),
        out_shape=jax.ShapeDtypeStruct((m, d), f32),
        compiler_params=_params(("arbitrary",), 32),
        name="proj_res",
    )(x, a, w)


def _ret_token_pieces(q_ref, k_ref, v_ref, g_ref, s_ref, og_ref, so_ref, gamma):
    def piece(j, h):
        def run():
            ks = slice(h * RET_DK, (h + 1) * RET_DK)
            vs = slice(h * RET_DV, (h + 1) * RET_DV)
            qcol = jnp.broadcast_to(q_ref[j, :, ks].astype(f32), (LANES, RET_DK)).T
            kcol = jnp.broadcast_to(k_ref[j, :, ks].astype(f32), (LANES, RET_DK)).T
            v = v_ref[j, :, vs].astype(f32)
            g = g_ref[j, :, vs].astype(f32)
            o_tiles = []
            for t in range(RET_DV // LANES):
                cs = slice(t * LANES, (t + 1) * LANES)
                s_new = gamma[h] * s_ref[j, h, :, cs] + kcol * v[:, cs]
                so_ref[j, h, :, cs] = s_new
                o_tiles.append(jnp.sum(qcol * s_new, axis=0, keepdims=True))
            o = jnp.concatenate(o_tiles, axis=1)
            og_ref[j, :, vs] = (_rms(o) * _silu(g)).astype(bf16)
        return run

    return [piece(j, h) for j in range(s_ref.shape[0]) for h in range(RET_HEADS)]


def _ffn_body(*refs, tf, n_sub, final_norm, rider_gamma):
    x_ref, gain_ref, wg_ref, wu_ref, wd_ref, fgain_ref = refs[:6]
    pieces = []
    if rider_gamma is None:
        o_ref, h_ref, acc_ref = refs[6:]
    else:
        rq_ref, rk_ref, rv_ref, rg_ref, rs_ref, o_ref, rog_ref, rso_ref, h_ref, acc_ref = refs[6:]
        pieces = _ret_token_pieces(rq_ref, rk_ref, rv_ref, rg_ref, rs_ref, rog_ref, rso_ref, rider_gamma)
    n_chunks = wg_ref.shape[1] // tf
    bounds = [n_chunks * s // n_sub for s in range(n_sub + 1)]

    def run_chunks(chunks):
        for n, c in enumerate(chunks):
            cs = slice(c * tf, (c + 1) * tf)
            g = jnp.dot(h_ref[...], wg_ref[:, cs], preferred_element_type=f32)
            u = jnp.dot(h_ref[...], wu_ref[:, cs], preferred_element_type=f32)
            a = (_silu(g) * u).astype(bf16)
            part = jnp.dot(a, wd_ref[cs, :], preferred_element_type=f32)
            if c == 0:
                acc_ref[...] = part
            else:
                acc_ref[...] += part
            for p in range(len(pieces)):
                if p * len(chunks) // len(pieces) == n:
                    pieces[p]()

    def sub_step(s):
        if s == 0:
            h_ref[...] = _rms(x_ref[...], gain_ref[...]).astype(bf16)
        run_chunks(range(bounds[s], bounds[s + 1]))
        if s == n_sub - 1:
            y = x_ref[...] + acc_ref[...]
            if final_norm:
                y = _rms(y, fgain_ref[...])
            o_ref[...] = y

    if n_sub == 1:
        sub_step(0)
    else:
        for s in range(n_sub):
            pl.when(pl.program_id(1) == s)(functools.partial(sub_step, s))


def _ffn(x, gain, wg, wu, wd, fgain, *, layer, tm, tf, final_norm, rider=None):
    m, d = x.shape
    ff = wg.shape[2]
    steps = m // tm
    n_sub = 1 if rider is None else 2
    resident = dict(pipeline_mode=pl.Buffered(1))
    in_specs = [
        pl.BlockSpec((tm, d), lambda i, s: (i, 0)),
        pl.BlockSpec((None, 1, d), lambda i, s: (layer, 0, 0)),
        pl.BlockSpec((None, d, ff), lambda i, s: (layer, 0, 0), **resident),
        pl.BlockSpec((None, d, ff), lambda i, s: (layer, 0, 0), **resident),
        pl.BlockSpec((None, ff, d), lambda i, s: (layer, 0, 0), **resident),
        pl.BlockSpec((1, d), lambda i, s: (0, 0)),
    ]
    args = [x, gain, wg, wu, wd, fgain]
    out_specs = [pl.BlockSpec((tm, d), lambda i, s: (i, 0))]
    out_shape = [jax.ShapeDtypeStruct((m, d), f32)]
    gamma = None
    vmem = 48
    if rider is not None:
        qkvg3, state, rows, gamma = rider
        assert 2 * steps * rows == state.shape[0]
        qw = RET_HEADS * RET_DK
        vw = RET_HEADS * RET_DV
        blk = lambda col: (lambda i, s: (2 * i + s, 0, col))
        state_spec = pl.BlockSpec((rows,) + state.shape[1:], lambda i, s: (2 * i + s, 0, 0, 0))
        in_specs += [
            pl.BlockSpec((rows, 1, qw), blk(0)),
            pl.BlockSpec((rows, 1, qw), blk(1)),
            pl.BlockSpec((rows, 1, vw), blk(1)),
            pl.BlockSpec((rows, 1, vw), blk(2)),
            state_spec,
        ]
        args += [qkvg3, qkvg3, qkvg3, qkvg3, state]
        out_specs += [pl.BlockSpec((rows, 1, vw), blk(0)), state_spec]
        out_shape += [
            jax.ShapeDtypeStruct((state.shape[0], 1, vw), bf16),
            jax.ShapeDtypeStruct(state.shape, f32),
        ]
        vmem = 56
    out = pl.pallas_call(
        functools.partial(_ffn_body, tf=tf, n_sub=n_sub, final_norm=final_norm, rider_gamma=gamma),
        grid=(steps, n_sub),
        in_specs=in_specs,
        out_specs=out_specs,
        out_shape=out_shape,
        scratch_shapes=[pltpu.VMEM((tm, d), bf16), pltpu.VMEM((tm, d), f32)],
        compiler_params=_params(("arbitrary", "arbitrary"), vmem),
        name="ffn_final" if final_norm else "ffn",
    )(*args)
    return out[0] if rider is None else out


def _in_odd_body(x_ref, gain_ref, w_ref, perm_ref, cos_ref, sin_ref, qsc_ref, ksc_ref, o_ref, h_ref, wqk_ref, *,
                 tn, split_halves):
    qw = RET_HEADS * RET_DK
    half = RET_DK // 2
    if split_halves:
        @pl.when(pl.program_id(0) == 0)
        def _():
            for hh in range(2 * RET_HEADS):
                hs = slice(hh * RET_DK, (hh + 1) * RET_DK)
                wqk_ref[:, hs] = jnp.dot(w_ref[:, hs], perm_ref[...], preferred_element_type=f32).astype(bf16)

    h_ref[...] = _rms(x_ref[...], gain_ref[...]).astype(bf16)
    cos = cos_ref[...]
    sin = sin_ref[...]
    for c in range(2 * qw // tn):
        c0 = c * tn
        w_chunk = wqk_ref[:, c0:c0 + tn] if split_halves else w_ref[:, c0:c0 + tn]
        p = jnp.dot(h_ref[...], w_chunk, preferred_element_type=f32)
        sc_ref = qsc_ref if c0 < qw else ksc_ref
        for hh in range(tn // RET_DK):
            h0 = hh * RET_DK
            head = (c0 % qw + h0) // RET_DK
            sc = sc_ref[:, head * LANES:(head + 1) * LANES]
            if split_halves:
                ev = p[:, h0:h0 + half]
                od = p[:, h0 + half:h0 + RET_DK]
                o_ref[:, c0 + h0:c0 + h0 + half] = ((ev * cos - od * sin) * sc).astype(bf16)
                o_ref[:, c0 + h0 + half:c0 + h0 + RET_DK] = ((od * cos + ev * sin) * sc).astype(bf16)
            else:
                xh = p[:, h0:h0 + RET_DK]
                even = lax.broadcasted_iota(jnp.int32, xh.shape, 1) % 2 == 0
                partner = jnp.where(even, pltpu.roll(xh, RET_DK - 1, 1), pltpu.roll(xh, 1, 1))
                r = xh * cos + partner * sin
                o_ref[:, c0 + h0:c0 + h0 + half] = (r[:, :half] * sc).astype(bf16)
                o_ref[:, c0 + h0 + half:c0 + h0 + RET_DK] = (r[:, half:] * sc).astype(bf16)
    for c0 in range(2 * qw, w_ref.shape[1], tn):
        p = jnp.dot(h_ref[...], w_ref[:, c0:c0 + tn], preferred_element_type=f32)
        o_ref[:, c0:c0 + tn] = p.astype(bf16)


def _in_odd(x, gain, w, perm, tables, *, layer, tm, tn, split_halves):
    m, d = x.shape
    n = w.shape[1]
    cos, sin, qsc, ksc = tables
    ntab = cos.shape[0] // tm
    qkw = 2 * RET_HEADS * RET_DK
    resident = dict(pipeline_mode=pl.Buffered(1))
    const = lambda i: (0, 0)
    rope_spec = pl.BlockSpec((tm, cos.shape[1]), lambda i: (i % ntab, 0))
    return pl.pallas_call(
        functools.partial(_in_odd_body, tn=tn, split_halves=split_halves),
        grid=(m // tm,),
        in_specs=[
            pl.BlockSpec((tm, d), lambda i: (i, 0)),
            pl.BlockSpec((None, 1, d), lambda i: (layer, 0, 0)),
            pl.BlockSpec(w.shape, const, **resident),
            pl.BlockSpec(perm.shape, const),
            rope_spec, rope_spec,
            pl.BlockSpec(qsc.shape, const),
            pl.BlockSpec(ksc.shape, const),
        ],
        out_specs=pl.BlockSpec((tm, n), lambda i: (i, 0)),
        out_shape=jax.ShapeDtypeStruct((m, n), bf16),
        scratch_shapes=[pltpu.VMEM((tm, d), bf16),
                        pltpu.VMEM((d, qkw) if split_halves else (8, LANES), bf16)],
        compiler_params=_params(("arbitrary",), 56),
        name="in_odd",
    )(x, gain, w, perm, cos, sin, qsc, ksc)


def _ret_prompt_body(q_ref, k_ref, v_ref, g_ref, x_ref, wout_ref, xo_ref, so_ref, s_ref, sb_ref, slab_ref, *,
                     gamma_c, n):
    c = pl.program_id(1)
    subs = [slice(j * n, (j + 1) * n) for j in range(q_ref.shape[0] // n)]

    @pl.when(c == 0)
    def _():
        s_ref[...] = jnp.zeros_like(s_ref)
        sb_ref[...] = jnp.zeros_like(sb_ref)

    causal = lax.broadcasted_iota(jnp.int32, (n, n), 0) >= lax.broadcasted_iota(jnp.int32, (n, n), 1)
    heads = range(RET_HEADS)
    ks = [slice(h * RET_DK, (h + 1) * RET_DK) for h in heads]
    vs = [slice(h * RET_DV, (h + 1) * RET_DV) for h in heads]
    att = [[jnp.where(causal, lax.dot_general(q_ref[r, ks[h]], k_ref[r, ks[h]], NT_DIMS,
                                              preferred_element_type=f32), 0.0).astype(bf16) for h in heads]
           for r in subs]
    o = []
    for j, r in enumerate(subs):
        o.append([jnp.dot(q_ref[r, ks[h]], sb_ref[h], preferred_element_type=f32)
                  + jnp.dot(att[j][h], v_ref[r, vs[h]], preferred_element_type=f32) for h in heads])
        for h in heads:
            kv = lax.dot_general(k_ref[r, ks[h]], v_ref[r, vs[h]], TN_DIMS, preferred_element_type=f32)
            s_new = gamma_c[h] * (s_ref[h] + kv)
            s_ref[h] = s_new
            sb_ref[h] = s_new.astype(bf16)
    for j, r in enumerate(subs):
        y = x_ref[r, :]
        for h in heads:
            og = (_rms(o[j][h]) * _silu(g_ref[r, vs[h]].astype(f32))).astype(bf16)
            y = y + jnp.dot(og, wout_ref[vs[h], :], preferred_element_type=f32)
        xo_ref[r, :] = y

    @pl.when(c == pl.num_programs(1) - 1)
    def _():
        half = RET_DK // 2
        for h in range(RET_HEADS):
            for t in range(RET_DV // LANES):
                ls = slice(t * LANES, (t + 1) * LANES)
                slab_ref[pl.ds(0, half, stride=2), :] = s_ref[h, 0:half, ls]
                slab_ref[pl.ds(1, half, stride=2), :] = s_ref[h, half:RET_DK, ls]
                so_ref[0, h, :, ls] = slab_ref[...]


def _ret_prompt(qkvg, x, wout, gamma_c, *, batch, c, chunk):
    m, d = x.shape
    nc = m // batch // c
    qw = RET_HEADS * RET_DK
    vw = RET_HEADS * RET_DV
    assert c % chunk == 0
    return pl.pallas_call(
        functools.partial(_ret_prompt_body, gamma_c=gamma_c, n=chunk),
        grid=(batch, nc),
        in_specs=[
            pl.BlockSpec((c, qw), lambda b, i: (b * nc + i, 0)),
            pl.BlockSpec((c, qw), lambda b, i: (b * nc + i, 1)),
            pl.BlockSpec((c, vw), lambda b, i: (b * nc + i, 1)),
            pl.BlockSpec((c, vw), lambda b, i: (b * nc + i, 2)),
            pl.BlockSpec((c, d), lambda b, i: (b * nc + i, 0)),
            pl.BlockSpec(wout.shape, lambda b, i: (0, 0), pipeline_mode=pl.Buffered(1)),
        ],
        out_specs=[
            pl.BlockSpec((c, d), lambda b, i: (b * nc + i, 0)),
            pl.BlockSpec((1, RET_HEADS, RET_DK, RET_DV), lambda b, i: (b, 0, 0, 0)),
        ],
        out_shape=[
            jax.ShapeDtypeStruct((m, d), f32),
            jax.ShapeDtypeStruct((batch, RET_HEADS, RET_DK, RET_DV), f32),
        ],
        scratch_shapes=[
            pltpu.VMEM((RET_HEADS, RET_DK, RET_DV), f32),
            pltpu.VMEM((RET_HEADS, RET_DK, RET_DV), bf16),
            pltpu.VMEM((RET_DK, LANES), f32),
        ],
        compiler_params=_params(("arbitrary", "arbitrary"), 48),
        name="ret_prompt",
    )(qkvg, qkvg, qkvg, qkvg, x, wout)


def _rope_tables(pos, per_pair):
    pair_angle = 1.0 / (ROPE_BASE ** jnp.linspace(0.0, 1.0, RET_DK // 2, dtype=f32))
    if per_pair:
        ang = pos[:, None] * pair_angle[None, :]
        return jnp.cos(ang), jnp.sin(ang)
    ang = pos[:, None] * jnp.repeat(pair_angle, 2)[None, :]
    sign = jnp.where(jnp.arange(RET_DK) % 2 == 0, -1.0, 1.0).astype(f32)
    return jnp.cos(ang), jnp.sin(ang) * sign


def _even_odd_perm():
    half = RET_DK // 2
    src = np.concatenate([2 * np.arange(half), 2 * np.arange(half) + 1])
    perm = np.zeros((RET_DK, RET_DK), np.float32)
    perm[src, np.arange(RET_DK)] = 1.0
    return jnp.asarray(perm, dtype=bf16)


def _lane_replicated(scale):
    return jnp.asarray(np.repeat(scale, LANES, axis=1), dtype=f32)


def _ret_decay(rows, c):
    gam = 1.0 - 2.0 ** (-5.0 - np.arange(RET_HEADS, dtype=np.float64))
    lg = np.log(gam)
    steps = (np.arange(rows) % c + 1.0)[:, None]
    q_scale = _lane_replicated(np.exp(lg[None, :] * steps))
    k_scale = _lane_replicated(np.exp(-lg[None, :] * steps) * RET_DK ** -0.5)
    gamma_c = tuple(float(x) for x in np.exp(lg * c))
    gamma = tuple(float(x) for x in gam)
    return q_scale, k_scale, gamma_c, gamma


def kernel(x_prompt, x_sample, state_gla, state_pool, state_ret, norm_mix, norm_ffn, norm_final, w_in_even,
           w_gate_b, b_gate, gla_gain, pool_w, pool_scale, w_out_even, w_in_odd, w_out_odd, w_ffn_gate,
           w_ffn_up, w_ffn_down):
    batch, seq, d = x_prompt.shape
    n_s = x_sample.shape[0]
    assert norm_mix.shape[0] == 2 and x_sample.shape[1] == 1

    we = jnp.pad(w_in_even[0], ((0, 0), (0, -w_in_even.shape[2] % LANES))).astype(bf16)
    nu = POOL_GW * len(POOL_WINDOWS)
    shift = np.zeros((nu + LANES, nu), np.float32)
    shift[GATE_RANK + np.arange(nu), np.arange(nu)] = 1.0
    shift = jnp.asarray(shift, dtype=bf16)
    wgb = jnp.concatenate([w_gate_b[0], jnp.zeros((LANES - GATE_RANK, w_gate_b.shape[2]), f32)], axis=0).astype(bf16)
    bg = b_gate[0][None, :]
    gg = gla_gain[0][None, :]
    pw = pool_w[0].astype(bf16)
    ps = pool_scale[0][None, :]
    woe = w_out_even[0].astype(bf16)
    wio = w_in_odd[0].astype(bf16)
    woo = w_out_odd[0].astype(bf16)
    wg = w_ffn_gate.astype(bf16)
    wu = w_ffn_up.astype(bf16)
    wd = w_ffn_down.astype(bf16)
    nm = norm_mix[:, None, :]
    nf = norm_ffn[:, None, :]
    nfin = norm_final[None, :]
    tril = jnp.asarray(np.tril(np.ones((GLA_CHUNK, GLA_CHUNK), np.float32)), dtype=bf16)
    tf = 256
    tm_p = 512
    q_scale, k_scale, gamma_c, gamma = _ret_decay(tm_p, RET_CHUNK)
    tables_p = _rope_tables(jnp.arange(seq, dtype=f32), True) + (q_scale, k_scale)
    tables_s = _rope_tables(jnp.full((n_s,), float(PAST_LEN), f32), False) + (
        _lane_replicated(np.ones((n_s, RET_HEADS))), _lane_replicated(np.full((n_s, RET_HEADS), RET_DK ** -0.5)))
    perm = _even_odd_perm()

    xs = x_sample.reshape(n_s, d)
    qkvg_s, loga_s, u_s = _in_even(xs, nm[0], we, shift, wgb, bg, tm=n_s)
    op_s, gla_s, pool_s = _gla_pool_sample(qkvg_s, loga_s, u_s, state_gla[0], state_pool[0], gg, pw, ps, bb=8)
    xs = _proj_res(xs, op_s, woe, tm=n_s)
    xs = _ffn(xs, nf, wg, wu, wd, nfin, layer=0, tm=n_s, tf=tf, final_norm=False)
    qkvg2_s = _in_odd(xs, nm, wio, perm, tables_s, layer=1, tm=n_s, tn=512, split_halves=False)
    qkvg2_s = qkvg2_s.reshape(n_s, 1, -1)

    rows = n_s // (2 * (batch * seq // tm_p))
    xp = x_prompt.reshape(batch * seq, d)
    qkvg, loga, u_p = _in_even(xp, nm[0], we, shift, wgb, bg, tm=tm_p)
    xp, gla_p = _gla_pool_prompt(qkvg, loga, u_p, xp, tril, gg, pw, ps, woe, batch=batch, t=1024)
    xp, og_s, ret_s = _ffn(xp, nf, wg, wu, wd, nfin, layer=0, tm=tm_p, tf=tf, final_norm=False,
                           rider=(qkvg2_s, state_ret[0], rows, gamma))
    qkvg2 = _in_odd(xp, nm, wio, perm, tables_p, layer=1, tm=tm_p, tn=512, split_halves=True)
    xp, ret_p = _ret_prompt(qkvg2, xp, woo, gamma_c, batch=batch, c=2 * RET_CHUNK, chunk=RET_CHUNK)
    y_prompt = _ffn(xp, nf, wg, wu, wd, nfin, layer=1, tm=tm_p, tf=tf, final_norm=True)
    pool_p = u_p.reshape(batch, seq, -1)[:, seq - POOL_BUF:, :]

    xs = _proj_res(xs, og_s.reshape(n_s, -1), woo, tm=n_s)
    y_sample = _ffn(xs, nf, wg, wu, wd, nfin, layer=1, tm=n_s, tf=tf, final_norm=True)

    return (y_prompt.reshape(batch, seq, d), y_sample.reshape(n_s, 1, d),
            gla_p[None], gla_s[None], pool_p[None], pool_s[None], ret_p[None], ret_s[None])
```

```python
import functools

import numpy as np
import jax
import jax.numpy as jnp
from jax import lax
from jax.experimental import pallas as pl
from jax.experimental.pallas import tpu as pltpu

f32 = jnp.float32
bf16 = jnp.bfloat16

EPS = 1e-6
PAST_LEN = 16384
GLA_HEADS, GLA_DK, GLA_DV = 4, 64, 128
GLA_CHUNK = 64
GATE_RANK = 16
GATE_NORMALIZER = 16.0
POOL_WINDOWS = (2, 4, 8, 16)
POOL_GW = 128
POOL_BUF = max(POOL_WINDOWS) - 1
POOL_HIST = 32
RET_HEADS, RET_DK, RET_DV = 4, 256, 512
RET_CHUNK = 256
ROPE_BASE = 10000.0
LANES = 128
MIB = 1024 * 1024

NT_DIMS = (((1,), (1,)), ((), ()))
TN_DIMS = (((0,), (0,)), ((), ()))


def _params(semantics, vmem_mib):
    return pltpu.CompilerParams(dimension_semantics=semantics, vmem_limit_bytes=vmem_mib * MIB)


def _rms(x, gain=None):
    y = x * lax.rsqrt(jnp.mean(x * x, axis=-1, keepdims=True) + EPS)
    return y if gain is None else y * gain


def _silu(g):
    return g * jax.nn.sigmoid(g)


def _in_even_body(x_ref, gain_ref, w_ref, shift_ref, wgb_ref, bg_ref, qkvg_ref, loga_ref, u_ref, h_ref, wu_ref, *,
                  tn):
    nq = qkvg_ref.shape[1]
    nu = u_ref.shape[1]

    @pl.when(pl.program_id(0) == 0)
    def _():
        wu_ref[...] = jnp.dot(w_ref[:, nq:], shift_ref[...], preferred_element_type=f32).astype(bf16)

    h_ref[...] = _rms(x_ref[...], gain_ref[...]).astype(bf16)
    a = jnp.dot(h_ref[...], w_ref[:, nq:nq + LANES], preferred_element_type=f32)
    a = jnp.where(lax.broadcasted_iota(jnp.int32, a.shape, 1) < GATE_RANK, a, 0.0).astype(bf16)
    for c0 in range(0, nq, tn):
        qkvg_ref[:, c0:c0 + tn] = jnp.dot(h_ref[...], w_ref[:, c0:c0 + tn], preferred_element_type=f32).astype(bf16)
        if c0 == 0:
            z = jnp.dot(a, wgb_ref[...], preferred_element_type=f32) + bg_ref[...]
            loga_ref[...] = (jnp.minimum(z, 0.0) - jnp.log1p(jnp.exp(-jnp.abs(z)))) * (1.0 / GATE_NORMALIZER)
    for c0 in range(0, nu, tn):
        u_ref[:, c0:c0 + tn] = jnp.dot(h_ref[...], wu_ref[:, c0:c0 + tn], preferred_element_type=f32)


def _in_even(x, gain, w, shift, wgb, bg, *, tm):
    m, d = x.shape
    nq = 2 * GLA_HEADS * GLA_DK + 2 * GLA_HEADS * GLA_DV
    nu = POOL_GW * len(POOL_WINDOWS)
    nk = GLA_HEADS * GLA_DK
    const = lambda i: (0, 0)
    return pl.pallas_call(
        functools.partial(_in_even_body, tn=512),
        grid=(m // tm,),
        in_specs=[
            pl.BlockSpec((tm, d), lambda i: (i, 0)),
            pl.BlockSpec((1, d), const),
            pl.BlockSpec(w.shape, const, pipeline_mode=pl.Buffered(1)),
            pl.BlockSpec(shift.shape, const),
            pl.BlockSpec(wgb.shape, const),
            pl.BlockSpec((1, nk), const),
        ],
        out_specs=[
            pl.BlockSpec((tm, nq), lambda i: (i, 0)),
            pl.BlockSpec((tm, nk), lambda i: (i, 0)),
            pl.BlockSpec((tm, nu), lambda i: (i, 0)),
        ],
        out_shape=[
            jax.ShapeDtypeStruct((m, nq), bf16),
            jax.ShapeDtypeStruct((m, nk), f32),
            jax.ShapeDtypeStruct((m, nu), f32),
        ],
        scratch_shapes=[pltpu.VMEM((tm, d), bf16), pltpu.VMEM((d, nu), bf16)],
        compiler_params=_params(("arbitrary",), 48),
        name="in_even",
    )(x, gain, w, shift, wgb, bg)


def _gla_pool_prompt_body(qkvg_ref, loga_ref, u_ref, x_ref, tril_ref, gain_ref, pw_ref, ps_ref, wout_ref,
                          xo_ref, so_ref, st_ref, o_ref, e_ref, p_ref, q_ref, op_ref, after_chunk=None):
    t = x_ref.shape[0]
    ck = GLA_CHUNK
    kw = GLA_HEADS * GLA_DK
    vw = GLA_HEADS * GLA_DV
    pair_w = 2 * GLA_DK
    i = pl.program_id(1)

    @pl.when(i == 0)
    def _():
        st_ref[...] = jnp.zeros_like(st_ref)
        e_ref[0:POOL_HIST, :] = jnp.zeros((POOL_HIST, e_ref.shape[1]), f32)

    tril = tril_ref[...]
    row = lax.broadcasted_iota(jnp.int32, (2 * ck, pair_w), 0)
    lane = lax.broadcasted_iota(jnp.int32, (2 * ck, pair_w), 1)
    first_lanes = lane < GLA_DK
    first_lanes_ck = lax.broadcasted_iota(jnp.int32, (ck, pair_w), 1) < GLA_DK
    same_head = (row < ck) == first_lanes
    causal = same_head & ((row % ck) >= (lane % GLA_DK))
    pairs = range(GLA_HEADS // 2)
    chunks = range(t // ck)

    hist = POOL_HIST
    n = t + hist
    gw = POOL_GW
    u = u_ref[...]
    e_ref[hist:n, :] = u
    p_ref[8:n, :] = e_ref[8:n, :] + e_ref[7:n - 1, :]
    q_ref[16:n, gw:] = p_ref[16:n, gw:] + p_ref[14:n - 2, gw:]
    p_ref[24:n, 2 * gw:] = q_ref[24:n, 2 * gw:] + q_ref[20:n - 4, 2 * gw:]
    q_ref[32:n, 3 * gw:] = p_ref[32:n, 3 * gw:] + p_ref[24:n - 8, 3 * gw:]

    def rows_of(c):
        return slice(c * ck, (c + 1) * ck)

    def v_pair(c, p):
        va = qkvg_ref[rows_of(c), 2 * kw + (2 * p) * GLA_DV:2 * kw + (2 * p + 1) * GLA_DV]
        vb = qkvg_ref[rows_of(c), 2 * kw + (2 * p + 1) * GLA_DV:2 * kw + (2 * p + 2) * GLA_DV]
        return va, vb

    bcs = []
    for c in chunks:
        la = loga_ref[rows_of(c), :]
        la_hi = la.astype(bf16)
        la_lo = (la - la_hi.astype(f32)).astype(bf16)
        bcs.append(jnp.dot(tril, la_hi, preferred_element_type=f32) + jnp.dot(tril, la_lo, preferred_element_type=f32))
    lhs_q, ke2, kds, elast = [], [], [], []
    for c in chunks:
        bc = bcs[c]
        blast = bc[ck - 1:ck, :]
        q = qkvg_ref[rows_of(c), 0:kw].astype(f32) * (GLA_DK ** -0.5)
        k = qkvg_ref[rows_of(c), kw:2 * kw].astype(f32)
        qe = q * jnp.exp(bc)
        ke = (k * jnp.exp(-bc)).astype(bf16)
        kds.append((k * jnp.exp(blast - bc)).astype(bf16))
        elast.append(jnp.exp(blast))
        for p in pairs:
            pl_ = slice(p * pair_w, (p + 1) * pair_w)
            qe_p = qe[:, pl_]
            lhs_q.append(jnp.concatenate([jnp.where(first_lanes_ck, qe_p, 0.0),
                                          jnp.where(first_lanes_ck, 0.0, qe_p)], axis=0).astype(bf16))
            ke2.append(jnp.concatenate([ke[:, pl_], ke[:, pl_]], axis=0))
    att, upd = [], []
    for c in chunks:
        for p in pairs:
            idx = c * len(pairs) + p
            a = lax.dot_general(lhs_q[idx], ke2[idx], NT_DIMS, preferred_element_type=f32)
            att.append(jnp.where(causal, a, 0.0).astype(bf16))
            va, vb = v_pair(c, p)
            r = lax.dot_general(jnp.concatenate([va, vb], axis=1), kds[c][:, p * pair_w:(p + 1) * pair_w], TN_DIMS,
                                preferred_element_type=f32)
            upd.append(jnp.where(first_lanes, r[:GLA_DV], r[GLA_DV:]))
    st = [st_ref[p] for p in pairs]
    for c in chunks:
        for p in pairs:
            idx = c * len(pairs) + p
            va, vb = v_pair(c, p)
            o = lax.dot_general(lhs_q[idx], st[p].astype(bf16), NT_DIMS, preferred_element_type=f32)
            o = o + jnp.dot(att[idx], jnp.concatenate([va, vb], axis=0), preferred_element_type=f32)
            o_ref[rows_of(c), (2 * p) * GLA_DV:(2 * p + 1) * GLA_DV] = o[:ck]
            o_ref[rows_of(c), (2 * p + 1) * GLA_DV:(2 * p + 2) * GLA_DV] = o[ck:]
            st[p] = st[p] * elast[c][:, p * pair_w:(p + 1) * pair_w] + upd[idx]
    for p in pairs:
        st_ref[p] = st[p]

    sums = (p_ref, q_ref, p_ref, q_ref)
    pos = i * t + lax.broadcasted_iota(jnp.int32, (t, 1), 0)
    for gi, w in enumerate(POOL_WINDOWS):
        ls = slice(gi * gw, (gi + 1) * gw)
        cnt = jnp.minimum(w, pos + 1).astype(f32)
        pooled = (sums[gi][hist:n, ls] / cnt - u[:, ls]).astype(bf16)
        pg = jnp.dot(pooled, pw_ref[gi], preferred_element_type=f32) * ps_ref[:, ls]
        op_ref[:, vw + gi * gw:vw + (gi + 1) * gw] = pg.astype(bf16)
    e_ref[hist - 16:hist, :] = e_ref[n - 16:n, :]

    piece = 2 * GLA_DV
    y = x_ref[...]
    for c0 in (vw, vw + piece):
        y = y + jnp.dot(op_ref[:, c0:c0 + piece], wout_ref[c0:c0 + piece, :], preferred_element_type=f32)
    gain = gain_ref[...]
    for p in pairs:
        for h in (2 * p, 2 * p + 1):
            hs = slice(h * GLA_DV, (h + 1) * GLA_DV)
            g = qkvg_ref[:, 2 * kw + vw + h * GLA_DV:2 * kw + vw + (h + 1) * GLA_DV].astype(f32)
            op_ref[:, hs] = (_rms(o_ref[:, hs], gain) * _silu(g)).astype(bf16)
        c0 = p * piece
        y = y + jnp.dot(op_ref[:, c0:c0 + piece], wout_ref[c0:c0 + piece, :], preferred_element_type=f32)
    xo_ref[...] = y

    @pl.when(i == pl.num_programs(1) - 1)
    def _():
        for p in range(GLA_HEADS // 2):
            s_pair = st_ref[p].T
            so_ref[0, 2 * p] = s_pair[:GLA_DK]
            so_ref[0, 2 * p + 1] = s_pair[GLA_DK:]


def _gla_pool_prompt(qkvg, loga, u, x, tril, gain, pw, ps, wout, *, batch, t):
    m, d = x.shape
    nt = m // batch // t
    row = lambda b, i: (b * nt + i, 0)
    const2 = lambda b, i: (0, 0)
    vw = GLA_HEADS * GLA_DV
    uw = u.shape[1]
    return pl.pallas_call(
        _gla_pool_prompt_body,
        grid=(batch, nt),
        in_specs=[
            pl.BlockSpec((t, qkvg.shape[1]), row),
            pl.BlockSpec((t, loga.shape[1]), row),
            pl.BlockSpec((t, uw), row),
            pl.BlockSpec((t, d), row),
            pl.BlockSpec(tril.shape, const2),
            pl.BlockSpec(gain.shape, const2),
            pl.BlockSpec(pw.shape, lambda b, i: (0, 0, 0)),
            pl.BlockSpec(ps.shape, const2),
            pl.BlockSpec(wout.shape, const2),
        ],
        out_specs=[
            pl.BlockSpec((t, d), row),
            pl.BlockSpec((1, GLA_HEADS, GLA_DK, GLA_DV), lambda b, i: (b, 0, 0, 0)),
        ],
        out_shape=[
            jax.ShapeDtypeStruct((m, d), f32),
            jax.ShapeDtypeStruct((batch, GLA_HEADS, GLA_DK, GLA_DV), f32),
        ],
        scratch_shapes=[
            pltpu.VMEM((GLA_HEADS // 2, GLA_DV, 2 * GLA_DK), f32),
            pltpu.VMEM((t, vw), f32),
            pltpu.VMEM((POOL_HIST + t, uw), f32),
            pltpu.VMEM((POOL_HIST + t, uw), f32),
            pltpu.VMEM((POOL_HIST + t, uw), f32),
            pltpu.VMEM((t, vw + uw), bf16),
        ],
        compiler_params=_params(("arbitrary", "arbitrary"), 56),
        name="gla_pool_prompt",
    )(qkvg, loga, u, x, tril, gain, pw, ps, wout)


def _gla_pool_sample_body(qkvg_ref, loga_ref, u_ref, s_ref, buf_ref, gain_ref, pw_ref, ps_ref,
                          op_ref, so_ref):
    bb = u_ref.shape[0]
    kw = GLA_HEADS * GLA_DK
    vw = GLA_HEADS * GLA_DV
    gain = gain_ref[...]
    qkvg = qkvg_ref[...].astype(f32)
    alpha = jnp.exp(loga_ref[...])
    qs = qkvg[:, 0:kw] * (GLA_DK ** -0.5)
    k = qkvg[:, kw:2 * kw]

    def column(row):
        return jnp.broadcast_to(row, (LANES, kw)).T

    o_rows = []
    for b in range(bb):
        acol = column(alpha[b:b + 1, :])
        qcol = column(qs[b:b + 1, :])
        kcol = column(k[b:b + 1, :])
        o_heads = []
        for h in range(GLA_HEADS):
            ks = slice(h * GLA_DK, (h + 1) * GLA_DK)
            v = qkvg[b:b + 1, 2 * kw + h * GLA_DV:2 * kw + (h + 1) * GLA_DV]
            s_new = acol[ks, :] * s_ref[b, h] + kcol[ks, :] * v
            so_ref[b, h] = s_new
            o = jnp.sum(qcol[ks, :] * s_new, axis=0, keepdims=True)
            g = qkvg[b:b + 1, 2 * kw + vw + h * GLA_DV:2 * kw + vw + (h + 1) * GLA_DV]
            o_heads.append(_rms(o, gain) * _silu(g))
        o_rows.append(jnp.concatenate(o_heads, axis=1))
    op_ref[:, 0:vw] = jnp.concatenate(o_rows, axis=0).astype(bf16)

    u = u_ref[...]
    for gi, w in enumerate(POOL_WINDOWS):
        ls = slice(gi * POOL_GW, (gi + 1) * POOL_GW)
        s = u[:, ls] + jnp.sum(buf_ref[:, POOL_BUF - (w - 1):POOL_BUF, ls], axis=1)
        cnt = float(min(w, PAST_LEN + 1))
        pooled = (s / cnt - u[:, ls]).astype(bf16)
        pg = jnp.dot(pooled, pw_ref[gi], preferred_element_type=f32) * ps_ref[:, ls]
        op_ref[:, vw + gi * POOL_GW:vw + (gi + 1) * POOL_GW] = pg.astype(bf16)


def _gla_pool_sample(qkvg, loga, u, s, buf, gain, pw, ps, *, bb):
    n = u.shape[0]
    row = lambda i: (i, 0)
    const2 = lambda i: (0, 0)
    ow = GLA_HEADS * GLA_DV + POOL_GW * len(POOL_WINDOWS)
    return pl.pallas_call(
        _gla_pool_sample_body,
        grid=(n // bb,),
        in_specs=[
            pl.BlockSpec((bb, qkvg.shape[1]), row),
            pl.BlockSpec((bb, loga.shape[1]), row),
            pl.BlockSpec((bb, u.shape[1]), row),
            pl.BlockSpec((bb,) + s.shape[1:], lambda i: (i, 0, 0, 0)),
            pl.BlockSpec((bb,) + buf.shape[1:], lambda i: (i, 0, 0)),
            pl.BlockSpec(gain.shape, const2),
            pl.BlockSpec(pw.shape, lambda i: (0, 0, 0)),
            pl.BlockSpec(ps.shape, const2),
        ],
        out_specs=[
            pl.BlockSpec((bb, ow), row),
            pl.BlockSpec((bb,) + s.shape[1:], lambda i: (i, 0, 0, 0)),
        ],
        out_shape=[
            jax.ShapeDtypeStruct((n, ow), bf16),
            jax.ShapeDtypeStruct(s.shape, f32),
        ],
        compiler_params=_params(("arbitrary",), 32),
        name="gla_pool_sample",
    )(qkvg, loga, u, s, buf, gain, pw, ps)


def _proj_res_body(x_ref, a_ref, w_ref, o_ref):
    o_ref[...] = x_ref[...] + jnp.dot(a_ref[...], w_ref[...], preferred_element_type=f32)


def _proj_res(x, a, w, *, tm):
    m, d = x.shape
    return pl.pallas_call(
        _proj_res_body,
        grid=(m // tm,),
        in_specs=[
            pl.BlockSpec((tm, d), lambda i: (i, 0)),
            pl.BlockSpec((tm, a.shape[1]), lambda i: (i, 0)),
            pl.BlockSpec(w.shape, lambda i: (0, 0)),
        ],
        out_specs=pl.BlockSpec((tm, d), lambda i: (i, 0)),
        out_shape=jax.ShapeDtypeStruct((m, d), f32),
        compiler_params=_params(("arbitrary",), 32),
        name="proj_res",
    )(x, a, w)


def _ret_token_pieces(q_ref, k_ref, v_ref, g_ref, s_ref, og_ref, so_ref, gamma):
    def piece(j, h):
        def run():
            ks = slice(h * RET_DK, (h + 1) * RET_DK)
            vs = slice(h * RET_DV, (h + 1) * RET_DV)
            qcol = jnp.broadcast_to(q_ref[j, :, ks].astype(f32), (LANES, RET_DK)).T
            kcol = jnp.broadcast_to(k_ref[j, :, ks].astype(f32), (LANES, RET_DK)).T
            v = v_ref[j, :, vs].astype(f32)
            g = g_ref[j, :, vs].astype(f32)
            o_tiles = []
            for t in range(RET_DV // LANES):
                cs = slice(t * LANES, (t + 1) * LANES)
                s_new = gamma[h] * s_ref[j, h, :, cs] + kcol * v[:, cs]
                so_ref[j, h, :, cs] = s_new
                o_tiles.append(jnp.sum(qcol * s_new, axis=0, keepdims=True))
            o = jnp.concatenate(o_tiles, axis=1)
            og_ref[j, :, vs] = (_rms(o) * _silu(g)).astype(bf16)
            return o
        return run

    return [piece(j, h) for j in range(s_ref.shape[0]) for h in range(RET_HEADS)]


def _ffn_body(*refs, tf, n_sub, final_norm, rider_gamma):
    x_ref, gain_ref, wg_ref, wu_ref, wd_ref, fgain_ref = refs[:6]
    pieces = []
    if rider_gamma is None:
        o_ref, h_ref, acc_ref = refs[6:]
    else:
        rq_ref, rk_ref, rv_ref, rg_ref, rs_ref, o_ref, rog_ref, rso_ref, h_ref, acc_ref = refs[6:]
        pieces = _ret_token_pieces(rq_ref, rk_ref, rv_ref, rg_ref, rs_ref, rog_ref, rso_ref, rider_gamma)
    n_chunks = wg_ref.shape[1] // tf
    bounds = [n_chunks * s // n_sub for s in range(n_sub + 1)]

    def exact_zero(v):
        bits = lax.bitcast_convert_type(v, jnp.uint32)
        return ((bits >> 16) >> 16).astype(f32)

    def run_chunks(chunks):
        pin = None
        for n, c in enumerate(chunks):
            cs = slice(c * tf, (c + 1) * tf)
            g = jnp.dot(h_ref[...], wg_ref[:, cs], preferred_element_type=f32)
            if pin is not None:
                g = g + pin
                pin = None
            u = jnp.dot(h_ref[...], wu_ref[:, cs], preferred_element_type=f32)
            a = (_silu(g) * u).astype(bf16)
            part = jnp.dot(a, wd_ref[cs, :], preferred_element_type=f32)
            if c == 0:
                acc_ref[...] = part
            else:
                acc_ref[...] += part
            for p in range(len(pieces)):
                if p * (len(chunks) - 1) // len(pieces) == n:
                    z = exact_zero(pieces[p]()[:, :tf])
                    pin = z if pin is None else pin + z

    def sub_step(s):
        if s == 0:
            h_ref[...] = _rms(x_ref[...], gain_ref[...]).astype(bf16)
        run_chunks(range(bounds[s], bounds[s + 1]))
        if s == n_sub - 1:
            y = x_ref[...] + acc_ref[...]
            if final_norm:
                y = _rms(y, fgain_ref[...])
            o_ref[...] = y

    if n_sub == 1:
        sub_step(0)
    else:
        for s in range(n_sub):
            pl.when(pl.program_id(1) == s)(functools.partial(sub_step, s))


def _ffn(x, gain, wg, wu, wd, fgain, *, layer, tm, tf, final_norm, rider=None):
    m, d = x.shape
    ff = wg.shape[2]
    steps = m // tm
    n_sub = 1 if rider is None else 2
    resident = dict(pipeline_mode=pl.Buffered(1))
    in_specs = [
        pl.BlockSpec((tm, d), lambda i, s: (i, 0)),
        pl.BlockSpec((None, 1, d), lambda i, s: (layer, 0, 0)),
        pl.BlockSpec((None, d, ff), lambda i, s: (layer, 0, 0), **resident),
        pl.BlockSpec((None, d, ff), lambda i, s: (layer, 0, 0), **resident),
        pl.BlockSpec((None, ff, d), lambda i, s: (layer, 0, 0), **resident),
        pl.BlockSpec((1, d), lambda i, s: (0, 0)),
    ]
    args = [x, gain, wg, wu, wd, fgain]
    out_specs = [pl.BlockSpec((tm, d), lambda i, s: (i, 0))]
    out_shape = [jax.ShapeDtypeStruct((m, d), f32)]
    gamma = None
    vmem = 48
    if rider is not None:
        qkvg3, state, rows, gamma = rider
        assert 2 * steps * rows == state.shape[0]
        qw = RET_HEADS * RET_DK
        vw = RET_HEADS * RET_DV
        blk = lambda col: (lambda i, s: (2 * i + s, 0, col))
        state_spec = pl.BlockSpec((rows,) + state.shape[1:], lambda i, s: (2 * i + s, 0, 0, 0))
        in_specs += [
            pl.BlockSpec((rows, 1, qw), blk(0)),
            pl.BlockSpec((rows, 1, qw), blk(1)),
            pl.BlockSpec((rows, 1, vw), blk(1)),
            pl.BlockSpec((rows, 1, vw), blk(2)),
            state_spec,
        ]
        args += [qkvg3, qkvg3, qkvg3, qkvg3, state]
        out_specs += [pl.BlockSpec((rows, 1, vw), blk(0)), state_spec]
        out_shape += [
            jax.ShapeDtypeStruct((state.shape[0], 1, vw), bf16),
            jax.ShapeDtypeStruct(state.shape, f32),
        ]
        vmem = 56
    out = pl.pallas_call(
        functools.partial(_ffn_body, tf=tf, n_sub=n_sub, final_norm=final_norm, rider_gamma=gamma),
        grid=(steps, n_sub),
        in_specs=in_specs,
        out_specs=out_specs,
        out_shape=out_shape,
        scratch_shapes=[pltpu.VMEM((tm, d), bf16), pltpu.VMEM((tm, d), f32)],
        compiler_params=_params(("arbitrary", "arbitrary"), vmem),
        name="ffn_final" if final_norm else "ffn",
    )(*args)
    return out[0] if rider is None else out


def _in_odd_body(x_ref, gain_ref, w_ref, perm_ref, cos_ref, sin_ref, qsc_ref, ksc_ref, o_ref, h_ref, wqk_ref, *,
                 tn, split_halves):
    qw = RET_HEADS * RET_DK
    half = RET_DK // 2
    if split_halves:
        @pl.when(pl.program_id(0) == 0)
        def _():
            for hh in range(2 * RET_HEADS):
                hs = slice(hh * RET_DK, (hh + 1) * RET_DK)
                wqk_ref[:, hs] = jnp.dot(w_ref[:, hs], perm_ref[...], preferred_element_type=f32).astype(bf16)

    h_ref[...] = _rms(x_ref[...], gain_ref[...]).astype(bf16)
    cos = cos_ref[...]
    sin = sin_ref[...]
    for c in range(2 * qw // tn):
        c0 = c * tn
        w_chunk = wqk_ref[:, c0:c0 + tn] if split_halves else w_ref[:, c0:c0 + tn]
        p = jnp.dot(h_ref[...], w_chunk, preferred_element_type=f32)
        sc_ref = qsc_ref if c0 < qw else ksc_ref
        for hh in range(tn // RET_DK):
            h0 = hh * RET_DK
            head = (c0 % qw + h0) // RET_DK
            sc = sc_ref[:, head * LANES:(head + 1) * LANES]
            if split_halves:
                ev = p[:, h0:h0 + half]
                od = p[:, h0 + half:h0 + RET_DK]
                o_ref[:, c0 + h0:c0 + h0 + half] = ((ev * cos - od * sin) * sc).astype(bf16)
                o_ref[:, c0 + h0 + half:c0 + h0 + RET_DK] = ((od * cos + ev * sin) * sc).astype(bf16)
            else:
                xh = p[:, h0:h0 + RET_DK]
                even = lax.broadcasted_iota(jnp.int32, xh.shape, 1) % 2 == 0
                partner = jnp.where(even, pltpu.roll(xh, RET_DK - 1, 1), pltpu.roll(xh, 1, 1))
                r = xh * cos + partner * sin
                o_ref[:, c0 + h0:c0 + h0 + half] = (r[:, :half] * sc).astype(bf16)
                o_ref[:, c0 + h0 + half:c0 + h0 + RET_DK] = (r[:, half:] * sc).astype(bf16)
    for c0 in range(2 * qw, w_ref.shape[1], tn):
        p = jnp.dot(h_ref[...], w_ref[:, c0:c0 + tn], preferred_element_type=f32)
        o_ref[:, c0:c0 + tn] = p.astype(bf16)


def _in_odd(x, gain, w, perm, tables, *, layer, tm, tn, split_halves):
    m, d = x.shape
    n = w.shape[1]
    cos, sin, qsc, ksc = tables
    ntab = cos.shape[0] // tm
    qkw = 2 * RET_HEADS * RET_DK
    resident = dict(pipeline_mode=pl.Buffered(1))
    const = lambda i: (0, 0)
    rope_spec = pl.BlockSpec((tm, cos.shape[1]), lambda i: (i % ntab, 0))
    return pl.pallas_call(
        functools.partial(_in_odd_body, tn=tn, split_halves=split_halves),
        grid=(m // tm,),
        in_specs=[
            pl.BlockSpec((tm, d), lambda i: (i, 0)),
            pl.BlockSpec((None, 1, d), lambda i: (layer, 0, 0)),
            pl.BlockSpec(w.shape, const, **resident),
            pl.BlockSpec(perm.shape, const),
            rope_spec, rope_spec,
            pl.BlockSpec(qsc.shape, const),
            pl.BlockSpec(ksc.shape, const),
        ],
        out_specs=pl.BlockSpec((tm, n), lambda i: (i, 0)),
        out_shape=jax.ShapeDtypeStruct((m, n), bf16),
        scratch_shapes=[pltpu.VMEM((tm, d), bf16),
                        pltpu.VMEM((d, qkw) if split_halves else (8, LANES), bf16)],
        compiler_params=_params(("arbitrary",), 56),
        name="in_odd",
    )(x, gain, w, perm, cos, sin, qsc, ksc)


def _ret_prompt_body(q_ref, k_ref, v_ref, g_ref, x_ref, wout_ref, xo_ref, so_ref, s_ref, sb_ref, slab_ref, *,
                     gamma_c, n):
    c = pl.program_id(1)
    subs = [slice(j * n, (j + 1) * n) for j in range(q_ref.shape[0] // n)]

    @pl.when(c == 0)
    def _():
        s_ref[...] = jnp.zeros_like(s_ref)
        sb_ref[...] = jnp.zeros_like(sb_ref)

    causal = lax.broadcasted_iota(jnp.int32, (n, n), 0) >= lax.broadcasted_iota(jnp.int32, (n, n), 1)
    heads = range(RET_HEADS)
    ks = [slice(h * RET_DK, (h + 1) * RET_DK) for h in heads]
    vs = [slice(h * RET_DV, (h + 1) * RET_DV) for h in heads]
    att = [[jnp.where(causal, lax.dot_general(q_ref[r, ks[h]], k_ref[r, ks[h]], NT_DIMS,
                                              preferred_element_type=f32), 0.0).astype(bf16) for h in heads]
           for r in subs]
    o = []
    for j, r in enumerate(subs):
        o.append([jnp.dot(q_ref[r, ks[h]], sb_ref[h], preferred_element_type=f32)
                  + jnp.dot(att[j][h], v_ref[r, vs[h]], preferred_element_type=f32) for h in heads])
        for h in heads:
            kv = lax.dot_general(k_ref[r, ks[h]], v_ref[r, vs[h]], TN_DIMS, preferred_element_type=f32)
            s_new = gamma_c[h] * (s_ref[h] + kv)
            s_ref[h] = s_new
            sb_ref[h] = s_new.astype(bf16)
    for j, r in enumerate(subs):
        y = x_ref[r, :]
        for h in heads:
            og = (_rms(o[j][h]) * _silu(g_ref[r, vs[h]].astype(f32))).astype(bf16)
            y = y + jnp.dot(og, wout_ref[vs[h], :], preferred_element_type=f32)
        xo_ref[r, :] = y

    @pl.when(c == pl.num_programs(1) - 1)
    def _():
        half = RET_DK // 2
        for h in range(RET_HEADS):
            for t in range(RET_DV // LANES):
                ls = slice(t * LANES, (t + 1) * LANES)
                slab_ref[pl.ds(0, half, stride=2), :] = s_ref[h, 0:half, ls]
                slab_ref[pl.ds(1, half, stride=2), :] = s_ref[h, half:RET_DK, ls]
                so_ref[0, h, :, ls] = slab_ref[...]


def _ret_prompt(qkvg, x, wout, gamma_c, *, batch, c, chunk):
    m, d = x.shape
    nc = m // batch // c
    qw = RET_HEADS * RET_DK
    vw = RET_HEADS * RET_DV
    assert c % chunk == 0
    return pl.pallas_call(
        functools.partial(_ret_prompt_body, gamma_c=gamma_c, n=chunk),
        grid=(batch, nc),
        in_specs=[
            pl.BlockSpec((c, qw), lambda b, i: (b * nc + i, 0)),
            pl.BlockSpec((c, qw), lambda b, i: (b * nc + i, 1)),
            pl.BlockSpec((c, vw), lambda b, i: (b * nc + i, 1)),
            pl.BlockSpec((c, vw), lambda b, i: (b * nc + i, 2)),
            pl.BlockSpec((c, d), lambda b, i: (b * nc + i, 0)),
            pl.BlockSpec(wout.shape, lambda b, i: (0, 0), pipeline_mode=pl.Buffered(1)),
        ],
        out_specs=[
            pl.BlockSpec((c, d), lambda b, i: (b * nc + i, 0)),
            pl.BlockSpec((1, RET_HEADS, RET_DK, RET_DV), lambda b, i: (b, 0, 0, 0)),
        ],
        out_shape=[
            jax.ShapeDtypeStruct((m, d), f32),
            jax.ShapeDtypeStruct((batch, RET_HEADS, RET_DK, RET_DV), f32),
        ],
        scratch_shapes=[
            pltpu.VMEM((RET_HEADS, RET_DK, RET_DV), f32),
            pltpu.VMEM((RET_HEADS, RET_DK, RET_DV), bf16),
            pltpu.VMEM((RET_DK, LANES), f32),
        ],
        compiler_params=_params(("arbitrary", "arbitrary"), 48),
        name="ret_prompt",
    )(qkvg, qkvg, qkvg, qkvg, x, wout)


def _rope_tables(pos, per_pair):
    pair_angle = 1.0 / (ROPE_BASE ** jnp.linspace(0.0, 1.0, RET_DK // 2, dtype=f32))
    if per_pair:
        ang = pos[:, None] * pair_angle[None, :]
        return jnp.cos(ang), jnp.sin(ang)
    ang = pos[:, None] * jnp.repeat(pair_angle, 2)[None, :]
    sign = jnp.where(jnp.arange(RET_DK) % 2 == 0, -1.0, 1.0).astype(f32)
    return jnp.cos(ang), jnp.sin(ang) * sign


def _even_odd_perm():
    half = RET_DK // 2
    src = np.concatenate([2 * np.arange(half), 2 * np.arange(half) + 1])
    perm = np.zeros((RET_DK, RET_DK), np.float32)
    perm[src, np.arange(RET_DK)] = 1.0
    return jnp.asarray(perm, dtype=bf16)


def _lane_replicated(scale):
    return jnp.asarray(np.repeat(scale, LANES, axis=1), dtype=f32)


def _ret_decay(rows, c):
    gam = 1.0 - 2.0 ** (-5.0 - np.arange(RET_HEADS, dtype=np.float64))
    lg = np.log(gam)
    steps = (np.arange(rows) % c + 1.0)[:, None]
    q_scale = _lane_replicated(np.exp(lg[None, :] * steps))
    k_scale = _lane_replicated(np.exp(-lg[None, :] * steps) * RET_DK ** -0.5)
    gamma_c = tuple(float(x) for x in np.exp(lg * c))
    gamma = tuple(float(x) for x in gam)
    return q_scale, k_scale, gamma_c, gamma


def kernel(x_prompt, x_sample, state_gla, state_pool, state_ret, norm_mix, norm_ffn, norm_final, w_in_even,
           w_gate_b, b_gate, gla_gain, pool_w, pool_scale, w_out_even, w_in_odd, w_out_odd, w_ffn_gate,
           w_ffn_up, w_ffn_down):
    batch, seq, d = x_prompt.shape
    n_s = x_sample.shape[0]
    assert norm_mix.shape[0] == 2 and x_sample.shape[1] == 1

    we = jnp.pad(w_in_even[0], ((0, 0), (0, -w_in_even.shape[2] % LANES))).astype(bf16)
    nu = POOL_GW * len(POOL_WINDOWS)
    shift = np.zeros((nu + LANES, nu), np.float32)
    shift[GATE_RANK + np.arange(nu), np.arange(nu)] = 1.0
    shift = jnp.asarray(shift, dtype=bf16)
    wgb = jnp.concatenate([w_gate_b[0], jnp.zeros((LANES - GATE_RANK, w_gate_b.shape[2]), f32)], axis=0).astype(bf16)
    bg = b_gate[0][None, :]
    gg = gla_gain[0][None, :]
    pw = pool_w[0].astype(bf16)
    ps = pool_scale[0][None, :]
    woe = w_out_even[0].astype(bf16)
    wio = w_in_odd[0].astype(bf16)
    woo = w_out_odd[0].astype(bf16)
    wg = w_ffn_gate.astype(bf16)
    wu = w_ffn_up.astype(bf16)
    wd = w_ffn_down.astype(bf16)
    nm = norm_mix[:, None, :]
    nf = norm_ffn[:, None, :]
    nfin = norm_final[None, :]
    tril = jnp.asarray(np.tril(np.ones((GLA_CHUNK, GLA_CHUNK), np.float32)), dtype=bf16)
    tf = 256
    tm_p = 512
    q_scale, k_scale, gamma_c, gamma = _ret_decay(tm_p, RET_CHUNK)
    tables_p = _rope_tables(jnp.arange(seq, dtype=f32), True) + (q_scale, k_scale)
    tables_s = _rope_tables(jnp.full((n_s,), float(PAST_LEN), f32), False) + (
        _lane_replicated(np.ones((n_s, RET_HEADS))), _lane_replicated(np.full((n_s, RET_HEADS), RET_DK ** -0.5)))
    perm = _even_odd_perm()

    xs = x_sample.reshape(n_s, d)
    qkvg_s, loga_s, u_s = _in_even(xs, nm[0], we, shift, wgb, bg, tm=n_s)
    op_s, gla_s = _gla_pool_sample(qkvg_s, loga_s, u_s, state_gla[0], state_pool[0], gg, pw, ps, bb=8)
    xs = _proj_res(xs, op_s, woe, tm=n_s)
    xs = _ffn(xs, nf, wg, wu, wd, nfin, layer=0, tm=n_s, tf=tf, final_norm=False)
    qkvg2_s = _in_odd(xs, nm, wio, perm, tables_s, layer=1, tm=n_s, tn=512, split_halves=False)
    qkvg2_s = qkvg2_s.reshape(n_s, 1, -1)

    rows = n_s // (2 * (batch * seq // tm_p))
    xp = x_prompt.reshape(batch * seq, d)
    qkvg, loga, u_p = _in_even(xp, nm[0], we, shift, wgb, bg, tm=tm_p)
    xp, gla_p = _gla_pool_prompt(qkvg, loga, u_p, xp, tril, gg, pw, ps, woe, batch=batch, t=512)
    xp, og_s, ret_s = _ffn(xp, nf, wg, wu, wd, nfin, layer=0, tm=tm_p, tf=tf, final_norm=False,
                           rider=(qkvg2_s, state_ret[0], rows, gamma))
    qkvg2 = _in_odd(xp, nm, wio, perm, tables_p, layer=1, tm=tm_p, tn=512, split_halves=True)
    xp, ret_p = _ret_prompt(qkvg2, xp, woo, gamma_c, batch=batch, c=2 * RET_CHUNK, chunk=RET_CHUNK)
    y_prompt = _ffn(xp, nf, wg, wu, wd, nfin, layer=1, tm=tm_p, tf=tf, final_norm=True)
    pool_p = u_p.reshape(batch, seq, -1)[:, seq - POOL_BUF:, :]

    xs = _proj_res(xs, og_s.reshape(n_s, -1), woo, tm=n_s)
    y_sample = _ffn(xs, nf, wg, wu, wd, nfin, layer=1, tm=n_s, tf=tf, final_norm=True)

    pool_s = jnp.concatenate([state_pool[0][:, 1:, :], u_s[:, None, :]], axis=1)

    return (y_prompt.reshape(batch, seq, d), y_sample.reshape(n_s, 1, d),
            gla_p[None], gla_s[None], pool_p[None], pool_s[None], ret_p[None], ret_s[None])
```

```python
import functools

import numpy as np
import jax
import jax.numpy as jnp
from jax import lax
from jax.experimental import pallas as pl
from jax.experimental.pallas import tpu as pltpu

f32 = jnp.float32
bf16 = jnp.bfloat16

EPS = 1e-6
PAST_LEN = 16384
GLA_HEADS, GLA_DK, GLA_DV = 4, 64, 128
GLA_CHUNK = 64
GATE_RANK = 16
GATE_NORMALIZER = 16.0
POOL_WINDOWS = (2, 4, 8, 16)
POOL_GW = 128
POOL_BUF = max(POOL_WINDOWS) - 1
POOL_HIST = 32
RET_HEADS, RET_DK, RET_DV = 4, 256, 512
RET_CHUNK = 256
ROPE_BASE = 10000.0
LANES = 128
MIB = 1024 * 1024

NT_DIMS = (((1,), (1,)), ((), ()))
TN_DIMS = (((0,), (0,)), ((), ()))


def _params(semantics, vmem_mib):
    return pltpu.CompilerParams(dimension_semantics=semantics, vmem_limit_bytes=vmem_mib * MIB)


def _rms(x, gain=None):
    y = x * lax.rsqrt(jnp.mean(x * x, axis=-1, keepdims=True) + EPS)
    return y if gain is None else y * gain


def _silu(g):
    return g * jax.nn.sigmoid(g)


def _in_even_body(x_ref, gain_ref, w_ref, shift_ref, wgb_ref, bg_ref, qkvg_ref, loga_ref, u_ref, h_ref, wu_ref, *,
                  tn):
    nq = qkvg_ref.shape[1]
    nu = u_ref.shape[1]

    @pl.when(pl.program_id(0) == 0)
    def _():
        wu_ref[...] = jnp.dot(w_ref[:, nq:], shift_ref[...], preferred_element_type=f32).astype(bf16)

    h_ref[...] = _rms(x_ref[...], gain_ref[...]).astype(bf16)
    a = jnp.dot(h_ref[...], w_ref[:, nq:nq + LANES], preferred_element_type=f32)
    a = jnp.where(lax.broadcasted_iota(jnp.int32, a.shape, 1) < GATE_RANK, a, 0.0).astype(bf16)
    for c0 in range(0, nq, tn):
        qkvg_ref[:, c0:c0 + tn] = jnp.dot(h_ref[...], w_ref[:, c0:c0 + tn], preferred_element_type=f32).astype(bf16)
        if c0 == 0:
            z = jnp.dot(a, wgb_ref[...], preferred_element_type=f32) + bg_ref[...]
            loga_ref[...] = (jnp.minimum(z, 0.0) - jnp.log1p(jnp.exp(-jnp.abs(z)))) * (1.0 / GATE_NORMALIZER)
    for c0 in range(0, nu, tn):
        u_ref[:, c0:c0 + tn] = jnp.dot(h_ref[...], wu_ref[:, c0:c0 + tn], preferred_element_type=f32)


def _in_even(x, gain, w, shift, wgb, bg, *, tm):
    m, d = x.shape
    nq = 2 * GLA_HEADS * GLA_DK + 2 * GLA_HEADS * GLA_DV
    nu = POOL_GW * len(POOL_WINDOWS)
    nk = GLA_HEADS * GLA_DK
    const = lambda i: (0, 0)
    return pl.pallas_call(
        functools.partial(_in_even_body, tn=512),
        grid=(m // tm,),
        in_specs=[
            pl.BlockSpec((tm, d), lambda i: (i, 0)),
            pl.BlockSpec((1, d), const),
            pl.BlockSpec(w.shape, const, pipeline_mode=pl.Buffered(1)),
            pl.BlockSpec(shift.shape, const),
            pl.BlockSpec(wgb.shape, const),
            pl.BlockSpec((1, nk), const),
        ],
        out_specs=[
            pl.BlockSpec((tm, nq), lambda i: (i, 0)),
            pl.BlockSpec((tm, nk), lambda i: (i, 0)),
            pl.BlockSpec((tm, nu), lambda i: (i, 0)),
        ],
        out_shape=[
            jax.ShapeDtypeStruct((m, nq), bf16),
            jax.ShapeDtypeStruct((m, nk), f32),
            jax.ShapeDtypeStruct((m, nu), f32),
        ],
        scratch_shapes=[pltpu.VMEM((tm, d), bf16), pltpu.VMEM((d, nu), bf16)],
        compiler_params=_params(("arbitrary",), 48),
        name="in_even",
    )(x, gain, w, shift, wgb, bg)


def _gla_pool_prompt_body(qkvg_ref, loga_ref, u_ref, x_ref, tril_ref, gain_ref, pw_ref, ps_ref, wout_ref,
                          xo_ref, so_ref, st_ref, o_ref, e_ref, p_ref, q_ref, op_ref, after_chunk=None):
    t = x_ref.shape[0]
    ck = GLA_CHUNK
    kw = GLA_HEADS * GLA_DK
    vw = GLA_HEADS * GLA_DV
    pair_w = 2 * GLA_DK
    i = pl.program_id(1)

    @pl.when(i == 0)
    def _():
        st_ref[...] = jnp.zeros_like(st_ref)
        e_ref[0:POOL_HIST, :] = jnp.zeros((POOL_HIST, e_ref.shape[1]), f32)

    tril = tril_ref[...]
    row = lax.broadcasted_iota(jnp.int32, (2 * ck, pair_w), 0)
    lane = lax.broadcasted_iota(jnp.int32, (2 * ck, pair_w), 1)
    first_lanes = lane < GLA_DK
    first_lanes_ck = lax.broadcasted_iota(jnp.int32, (ck, pair_w), 1) < GLA_DK
    same_head = (row < ck) == first_lanes
    causal = same_head & ((row % ck) >= (lane % GLA_DK))
    pairs = range(GLA_HEADS // 2)
    chunks = range(t // ck)

    hist = POOL_HIST
    n = t + hist
    gw = POOL_GW
    u = u_ref[...]
    e_ref[hist:n, :] = u
    p_ref[8:n, :] = e_ref[8:n, :] + e_ref[7:n - 1, :]
    q_ref[16:n, gw:] = p_ref[16:n, gw:] + p_ref[14:n - 2, gw:]
    p_ref[24:n, 2 * gw:] = q_ref[24:n, 2 * gw:] + q_ref[20:n - 4, 2 * gw:]
    q_ref[32:n, 3 * gw:] = p_ref[32:n, 3 * gw:] + p_ref[24:n - 8, 3 * gw:]

    def rows_of(c):
        return slice(c * ck, (c + 1) * ck)

    def v_pair(c, p):
        va = qkvg_ref[rows_of(c), 2 * kw + (2 * p) * GLA_DV:2 * kw + (2 * p + 1) * GLA_DV]
        vb = qkvg_ref[rows_of(c), 2 * kw + (2 * p + 1) * GLA_DV:2 * kw + (2 * p + 2) * GLA_DV]
        return va, vb

    bcs = []
    for c in chunks:
        la = loga_ref[rows_of(c), :]
        la_hi = la.astype(bf16)
        la_lo = (la - la_hi.astype(f32)).astype(bf16)
        bcs.append(jnp.dot(tril, la_hi, preferred_element_type=f32) + jnp.dot(tril, la_lo, preferred_element_type=f32))
    lhs_q, ke2, kds, elast = [], [], [], []
    for c in chunks:
        bc = bcs[c]
        blast = bc[ck - 1:ck, :]
        q = qkvg_ref[rows_of(c), 0:kw].astype(f32) * (GLA_DK ** -0.5)
        k = qkvg_ref[rows_of(c), kw:2 * kw].astype(f32)
        qe = q * jnp.exp(bc)
        ke = (k * jnp.exp(-bc)).astype(bf16)
        kds.append((k * jnp.exp(blast - bc)).astype(bf16))
        elast.append(jnp.exp(blast))
        for p in pairs:
            pl_ = slice(p * pair_w, (p + 1) * pair_w)
            qe_p = qe[:, pl_]
            lhs_q.append(jnp.concatenate([jnp.where(first_lanes_ck, qe_p, 0.0),
                                          jnp.where(first_lanes_ck, 0.0, qe_p)], axis=0).astype(bf16))
            ke2.append(jnp.concatenate([ke[:, pl_], ke[:, pl_]], axis=0))
    att, upd = [], []
    for c in chunks:
        for p in pairs:
            idx = c * len(pairs) + p
            a = lax.dot_general(lhs_q[idx], ke2[idx], NT_DIMS, preferred_element_type=f32)
            att.append(jnp.where(causal, a, 0.0).astype(bf16))
            va, vb = v_pair(c, p)
            r = lax.dot_general(jnp.concatenate([va, vb], axis=1), kds[c][:, p * pair_w:(p + 1) * pair_w], TN_DIMS,
                                preferred_element_type=f32)
            upd.append(jnp.where(first_lanes, r[:GLA_DV], r[GLA_DV:]))
    st = [st_ref[p] for p in pairs]
    for c in chunks:
        for p in pairs:
            idx = c * len(pairs) + p
            va, vb = v_pair(c, p)
            o = lax.dot_general(lhs_q[idx], st[p].astype(bf16), NT_DIMS, preferred_element_type=f32)
            o = o + jnp.dot(att[idx], jnp.concatenate([va, vb], axis=0), preferred_element_type=f32)
            o_ref[rows_of(c), (2 * p) * GLA_DV:(2 * p + 1) * GLA_DV] = o[:ck]
            o_ref[rows_of(c), (2 * p + 1) * GLA_DV:(2 * p + 2) * GLA_DV] = o[ck:]
            st[p] = st[p] * elast[c][:, p * pair_w:(p + 1) * pair_w] + upd[idx]
    for p in pairs:
        st_ref[p] = st[p]

    sums = (p_ref, q_ref, p_ref, q_ref)
    pos = i * t + lax.broadcasted_iota(jnp.int32, (t, 1), 0)
    for gi, w in enumerate(POOL_WINDOWS):
        ls = slice(gi * gw, (gi + 1) * gw)
        cnt = jnp.minimum(w, pos + 1).astype(f32)
        pooled = (sums[gi][hist:n, ls] / cnt - u[:, ls]).astype(bf16)
        pg = jnp.dot(pooled, pw_ref[gi], preferred_element_type=f32) * ps_ref[:, ls]
        op_ref[:, vw + gi * gw:vw + (gi + 1) * gw] = pg.astype(bf16)
    e_ref[hist - 16:hist, :] = e_ref[n - 16:n, :]

    piece = 2 * GLA_DV
    y = x_ref[...]
    for c0 in (vw, vw + piece):
        y = y + jnp.dot(op_ref[:, c0:c0 + piece], wout_ref[c0:c0 + piece, :], preferred_element_type=f32)
    gain = gain_ref[...]
    for p in pairs:
        for h in (2 * p, 2 * p + 1):
            hs = slice(h * GLA_DV, (h + 1) * GLA_DV)
            g = qkvg_ref[:, 2 * kw + vw + h * GLA_DV:2 * kw + vw + (h + 1) * GLA_DV].astype(f32)
            op_ref[:, hs] = (_rms(o_ref[:, hs], gain) * _silu(g)).astype(bf16)
        c0 = p * piece
        y = y + jnp.dot(op_ref[:, c0:c0 + piece], wout_ref[c0:c0 + piece, :], preferred_element_type=f32)
    xo_ref[...] = y

    @pl.when(i == pl.num_programs(1) - 1)
    def _():
        for p in range(GLA_HEADS // 2):
            s_pair = st_ref[p].T
            so_ref[0, 2 * p] = s_pair[:GLA_DK]
            so_ref[0, 2 * p + 1] = s_pair[GLA_DK:]


def _gla_pool_prompt(qkvg, loga, u, x, tril, gain, pw, ps, wout, *, batch, t):
    m, d = x.shape
    nt = m // batch // t
    row = lambda b, i: (b * nt + i, 0)
    const2 = lambda b, i: (0, 0)
    vw = GLA_HEADS * GLA_DV
    uw = u.shape[1]
    return pl.pallas_call(
        _gla_pool_prompt_body,
        grid=(batch, nt),
        in_specs=[
            pl.BlockSpec((t, qkvg.shape[1]), row),
            pl.BlockSpec((t, loga.shape[1]), row),
            pl.BlockSpec((t, uw), row),
            pl.BlockSpec((t, d), row),
            pl.BlockSpec(tril.shape, const2),
            pl.BlockSpec(gain.shape, const2),
            pl.BlockSpec(pw.shape, lambda b, i: (0, 0, 0)),
            pl.BlockSpec(ps.shape, const2),
            pl.BlockSpec(wout.shape, const2),
        ],
        out_specs=[
            pl.BlockSpec((t, d), row),
            pl.BlockSpec((1, GLA_HEADS, GLA_DK, GLA_DV), lambda b, i: (b, 0, 0, 0)),
        ],
        out_shape=[
            jax.ShapeDtypeStruct((m, d), f32),
            jax.ShapeDtypeStruct((batch, GLA_HEADS, GLA_DK, GLA_DV), f32),
        ],
        scratch_shapes=[
            pltpu.VMEM((GLA_HEADS // 2, GLA_DV, 2 * GLA_DK), f32),
            pltpu.VMEM((t, vw), f32),
            pltpu.VMEM((POOL_HIST + t, uw), f32),
            pltpu.VMEM((POOL_HIST + t, uw), f32),
            pltpu.VMEM((POOL_HIST + t, uw), f32),
            pltpu.VMEM((t, vw + uw), bf16),
        ],
        compiler_params=_params(("arbitrary", "arbitrary"), 56),
        name="gla_pool_prompt",
    )(qkvg, loga, u, x, tril, gain, pw, ps, wout)


def _gla_pool_sample_body(qkvg_ref, loga_ref, u_ref, s_ref, buf_ref, gain_ref, pw_ref, ps_ref,
                          op_ref, so_ref):
    bb = u_ref.shape[0]
    kw = GLA_HEADS * GLA_DK
    vw = GLA_HEADS * GLA_DV
    gain = gain_ref[...]
    qkvg = qkvg_ref[...].astype(f32)
    alpha = jnp.exp(loga_ref[...])
    qs = qkvg[:, 0:kw] * (GLA_DK ** -0.5)
    k = qkvg[:, kw:2 * kw]

    def column(row):
        return jnp.broadcast_to(row, (LANES, kw)).T

    o_rows = []
    for b in range(bb):
        acol = column(alpha[b:b + 1, :])
        qcol = column(qs[b:b + 1, :])
        kcol = column(k[b:b + 1, :])
        o_heads = []
        for h in range(GLA_HEADS):
            ks = slice(h * GLA_DK, (h + 1) * GLA_DK)
            v = qkvg[b:b + 1, 2 * kw + h * GLA_DV:2 * kw + (h + 1) * GLA_DV]
            s_new = acol[ks, :] * s_ref[b, h] + kcol[ks, :] * v
            so_ref[b, h] = s_new
            o = jnp.sum(qcol[ks, :] * s_new, axis=0, keepdims=True)
            g = qkvg[b:b + 1, 2 * kw + vw + h * GLA_DV:2 * kw + vw + (h + 1) * GLA_DV]
            o_heads.append(_rms(o, gain) * _silu(g))
        o_rows.append(jnp.concatenate(o_heads, axis=1))
    op_ref[:, 0:vw] = jnp.concatenate(o_rows, axis=0).astype(bf16)

    u = u_ref[...]
    for gi, w in enumerate(POOL_WINDOWS):
        ls = slice(gi * POOL_GW, (gi + 1) * POOL_GW)
        s = u[:, ls] + jnp.sum(buf_ref[:, POOL_BUF - (w - 1):POOL_BUF, ls], axis=1)
        cnt = float(min(w, PAST_LEN + 1))
        pooled = (s / cnt - u[:, ls]).astype(bf16)
        pg = jnp.dot(pooled, pw_ref[gi], preferred_element_type=f32) * ps_ref[:, ls]
        op_ref[:, vw + gi * POOL_GW:vw + (gi + 1) * POOL_GW] = pg.astype(bf16)


def _gla_pool_sample(qkvg, loga, u, s, buf, gain, pw, ps, *, bb):
    n = u.shape[0]
    row = lambda i: (i, 0)
    const2 = lambda i: (0, 0)
    ow = GLA_HEADS * GLA_DV + POOL_GW * len(POOL_WINDOWS)
    return pl.pallas_call(
        _gla_pool_sample_body,
        grid=(n // bb,),
        in_specs=[
            pl.BlockSpec((bb, qkvg.shape[1]), row),
            pl.BlockSpec((bb, loga.shape[1]), row),
            pl.BlockSpec((bb, u.shape[1]), row),
            pl.BlockSpec((bb,) + s.shape[1:], lambda i: (i, 0, 0, 0)),
            pl.BlockSpec((bb,) + buf.shape[1:], lambda i: (i, 0, 0)),
            pl.BlockSpec(gain.shape, const2),
            pl.BlockSpec(pw.shape, lambda i: (0, 0, 0)),
            pl.BlockSpec(ps.shape, const2),
        ],
        out_specs=[
            pl.BlockSpec((bb, ow), row),
            pl.BlockSpec((bb,) + s.shape[1:], lambda i: (i, 0, 0, 0)),
        ],
        out_shape=[
            jax.ShapeDtypeStruct((n, ow), bf16),
            jax.ShapeDtypeStruct(s.shape, f32),
        ],
        compiler_params=_params(("arbitrary",), 32),
        name="gla_pool_sample",
    )(qkvg, loga, u, s, buf, gain, pw, ps)


def _proj_res_body(x_ref, a_ref, w_ref, o_ref):
    o_ref[...] = x_ref[...] + jnp.dot(a_ref[...], w_ref[...], preferred_element_type=f32)


def _proj_res(x, a, w, *, tm):
    m, d = x.shape
    return pl.pallas_call(
        _proj_res_body,
        grid=(m // tm,),
        in_specs=[
            pl.BlockSpec((tm, d), lambda i: (i, 0)),
            pl.BlockSpec((tm, a.shape[1]), lambda i: (i, 0)),
            pl.BlockSpec(w.shape, lambda i: (0, 0)),
        ],
        out_specs=pl.BlockSpec((tm, d), lambda i: (i, 0)),
        out_shape=jax.ShapeDtypeStruct((m, d), f32),
        compiler_params=_params(("arbitrary",), 32),
        name="proj_res",
    )(x, a, w)


def _ret_token_pieces(q_ref, k_ref, v_ref, g_ref, s_ref, og_ref, so_ref, gamma):
    def piece(j, h):
        def run():
            ks = slice(h * RET_DK, (h + 1) * RET_DK)
            vs = slice(h * RET_DV, (h + 1) * RET_DV)
            qcol = jnp.broadcast_to(q_ref[j, :, ks].astype(f32), (LANES, RET_DK)).T
            kcol = jnp.broadcast_to(k_ref[j, :, ks].astype(f32), (LANES, RET_DK)).T
            v = v_ref[j, :, vs].astype(f32)
            g = g_ref[j, :, vs].astype(f32)
            o_tiles = []
            for t in range(RET_DV // LANES):
                cs = slice(t * LANES, (t + 1) * LANES)
                s_new = gamma[h] * s_ref[j, h, :, cs] + kcol * v[:, cs]
                so_ref[j, h, :, cs] = s_new
                o_tiles.append(jnp.sum(qcol * s_new, axis=0, keepdims=True))
            o = jnp.concatenate(o_tiles, axis=1)
            og_ref[j, :, vs] = (_rms(o) * _silu(g)).astype(bf16)
            return o
        return run

    return [piece(j, h) for j in range(s_ref.shape[0]) for h in range(RET_HEADS)]


def _ffn_body(*refs, tf, n_sub, final_norm, rider_gamma):
    x_ref, gain_ref, wg_ref, wu_ref, wd_ref, fgain_ref = refs[:6]
    pieces = []
    if rider_gamma is None:
        o_ref, h_ref, acc_ref = refs[6:]
    else:
        rq_ref, rk_ref, rv_ref, rg_ref, rs_ref, o_ref, rog_ref, rso_ref, h_ref, acc_ref = refs[6:]
        pieces = _ret_token_pieces(rq_ref, rk_ref, rv_ref, rg_ref, rs_ref, rog_ref, rso_ref, rider_gamma)
    n_chunks = wg_ref.shape[1] // tf
    bounds = [n_chunks * s // n_sub for s in range(n_sub + 1)]

    def exact_zero(v):
        bits = lax.bitcast_convert_type(v, jnp.uint32)
        return ((bits >> 16) >> 16).astype(f32)

    def run_chunks(chunks):
        pin = None
        for n, c in enumerate(chunks):
            cs = slice(c * tf, (c + 1) * tf)
            g = jnp.dot(h_ref[...], wg_ref[:, cs], preferred_element_type=f32)
            if pin is not None:
                g = g + pin
                pin = None
            u = jnp.dot(h_ref[...], wu_ref[:, cs], preferred_element_type=f32)
            a = (_silu(g) * u).astype(bf16)
            part = jnp.dot(a, wd_ref[cs, :], preferred_element_type=f32)
            if c == 0:
                acc_ref[...] = part
            else:
                acc_ref[...] += part
            for p in range(len(pieces)):
                if p * (len(chunks) - 1) // len(pieces) == n:
                    z = exact_zero(pieces[p]()[:, :tf])
                    pin = z if pin is None else pin + z

    def sub_step(s):
        if s == 0:
            h_ref[...] = _rms(x_ref[...], gain_ref[...]).astype(bf16)
        run_chunks(range(bounds[s], bounds[s + 1]))
        if s == n_sub - 1:
            y = x_ref[...] + acc_ref[...]
            if final_norm:
                y = _rms(y, fgain_ref[...])
            o_ref[...] = y

    if n_sub == 1:
        sub_step(0)
    else:
        for s in range(n_sub):
            pl.when(pl.program_id(1) == s)(functools.partial(sub_step, s))


def _ffn(x, gain, wg, wu, wd, fgain, *, layer, tm, tf, final_norm, rider=None):
    m, d = x.shape
    ff = wg.shape[2]
    steps = m // tm
    n_sub = 1 if rider is None else 2
    resident = dict(pipeline_mode=pl.Buffered(1))
    in_specs = [
        pl.BlockSpec((tm, d), lambda i, s: (i, 0)),
        pl.BlockSpec((None, 1, d), lambda i, s: (layer, 0, 0)),
        pl.BlockSpec((None, d, ff), lambda i, s: (layer, 0, 0), **resident),
        pl.BlockSpec((None, d, ff), lambda i, s: (layer, 0, 0), **resident),
        pl.BlockSpec((None, ff, d), lambda i, s: (layer, 0, 0), **resident),
        pl.BlockSpec((1, d), lambda i, s: (0, 0)),
    ]
    args = [x, gain, wg, wu, wd, fgain]
    out_specs = [pl.BlockSpec((tm, d), lambda i, s: (i, 0))]
    out_shape = [jax.ShapeDtypeStruct((m, d), f32)]
    gamma = None
    need = 3 * d * ff * 2 + 4 * tm * d * 4 + tm * d * (2 + 4) + 3 * tm * tf * 4 + tm * d * 4
    vmem = need // MIB + 6
    if rider is not None:
        qkvg3, state, rows, gamma = rider
        assert 2 * steps * rows == state.shape[0]
        qw = RET_HEADS * RET_DK
        vw = RET_HEADS * RET_DV
        blk = lambda col: (lambda i, s: (2 * i + s, 0, col))
        state_spec = pl.BlockSpec((rows,) + state.shape[1:], lambda i, s: (2 * i + s, 0, 0, 0))
        in_specs += [
            pl.BlockSpec((rows, 1, qw), blk(0)),
            pl.BlockSpec((rows, 1, qw), blk(1)),
            pl.BlockSpec((rows, 1, vw), blk(1)),
            pl.BlockSpec((rows, 1, vw), blk(2)),
            state_spec,
        ]
        args += [qkvg3, qkvg3, qkvg3, qkvg3, state]
        out_specs += [pl.BlockSpec((rows, 1, vw), blk(0)), state_spec]
        out_shape += [
            jax.ShapeDtypeStruct((state.shape[0], 1, vw), bf16),
            jax.ShapeDtypeStruct(state.shape, f32),
        ]
        vmem += 4 * rows * int(np.prod(state.shape[1:])) * 4 // MIB
    out = pl.pallas_call(
        functools.partial(_ffn_body, tf=tf, n_sub=n_sub, final_norm=final_norm, rider_gamma=gamma),
        grid=(steps, n_sub),
        in_specs=in_specs,
        out_specs=out_specs,
        out_shape=out_shape,
        scratch_shapes=[pltpu.VMEM((tm, d), bf16), pltpu.VMEM((tm, d), f32)],
        compiler_params=_params(("arbitrary", "arbitrary"), vmem),
        name="ffn_final" if final_norm else "ffn",
    )(*args)
    return out[0] if rider is None else out


def _ffn_stream_body(x_ref, gain_ref, wg_ref, wu_ref, wd_ref, fgain_ref, o_ref, h_ref, acc_ref, *, final_norm):
    j = pl.program_id(0)

    @pl.when(j == 0)
    def _():
        h_ref[...] = _rms(x_ref[...], gain_ref[...]).astype(bf16)
        acc_ref[...] = jnp.zeros_like(acc_ref)

    g = jnp.dot(h_ref[...], wg_ref[...], preferred_element_type=f32)
    u = jnp.dot(h_ref[...], wu_ref[...], preferred_element_type=f32)
    acc_ref[...] += jnp.dot((_silu(g) * u).astype(bf16), wd_ref[...], preferred_element_type=f32)

    @pl.when(j == pl.num_programs(0) - 1)
    def _():
        y = x_ref[...] + acc_ref[...]
        if final_norm:
            y = _rms(y, fgain_ref[...])
        o_ref[...] = y


def _ffn_stream(x, gain, wg, wu, wd, fgain, *, layer, tf, final_norm):
    m, d = x.shape
    ff = wg.shape[2]
    return pl.pallas_call(
        functools.partial(_ffn_stream_body, final_norm=final_norm),
        grid=(ff // tf,),
        in_specs=[
            pl.BlockSpec((m, d), lambda j: (0, 0)),
            pl.BlockSpec((None, 1, d), lambda j: (layer, 0, 0)),
            pl.BlockSpec((None, d, tf), lambda j: (layer, 0, j)),
            pl.BlockSpec((None, d, tf), lambda j: (layer, 0, j)),
            pl.BlockSpec((None, tf, d), lambda j: (layer, j, 0)),
            pl.BlockSpec((1, d), lambda j: (0, 0)),
        ],
        out_specs=pl.BlockSpec((m, d), lambda j: (0, 0)),
        out_shape=jax.ShapeDtypeStruct((m, d), f32),
        scratch_shapes=[pltpu.VMEM((m, d), bf16), pltpu.VMEM((m, d), f32)],
        compiler_params=_params(("arbitrary",), 16),
        name="ffn_final_stream" if final_norm else "ffn_stream",
    )(x, gain, wg, wu, wd, fgain)


def _in_odd_body(x_ref, gain_ref, w_ref, perm_ref, cos_ref, sin_ref, qsc_ref, ksc_ref, o_ref, h_ref, wqk_ref, *,
                 tn, split_halves):
    qw = RET_HEADS * RET_DK
    half = RET_DK // 2
    if split_halves:
        @pl.when(pl.program_id(0) == 0)
        def _():
            for hh in range(2 * RET_HEADS):
                hs = slice(hh * RET_DK, (hh + 1) * RET_DK)
                wqk_ref[:, hs] = jnp.dot(w_ref[:, hs], perm_ref[...], preferred_element_type=f32).astype(bf16)

    h_ref[...] = _rms(x_ref[...], gain_ref[...]).astype(bf16)
    cos = cos_ref[...]
    sin = sin_ref[...]
    for c in range(2 * qw // tn):
        c0 = c * tn
        w_chunk = wqk_ref[:, c0:c0 + tn] if split_halves else w_ref[:, c0:c0 + tn]
        p = jnp.dot(h_ref[...], w_chunk, preferred_element_type=f32)
        sc_ref = qsc_ref if c0 < qw else ksc_ref
        for hh in range(tn // RET_DK):
            h0 = hh * RET_DK
            head = (c0 % qw + h0) // RET_DK
            sc = sc_ref[:, head * LANES:(head + 1) * LANES]
            if split_halves:
                ev = p[:, h0:h0 + half]
                od = p[:, h0 + half:h0 + RET_DK]
                o_ref[:, c0 + h0:c0 + h0 + half] = ((ev * cos - od * sin) * sc).astype(bf16)
                o_ref[:, c0 + h0 + half:c0 + h0 + RET_DK] = ((od * cos + ev * sin) * sc).astype(bf16)
            else:
                xh = p[:, h0:h0 + RET_DK]
                even = lax.broadcasted_iota(jnp.int32, xh.shape, 1) % 2 == 0
                partner = jnp.where(even, pltpu.roll(xh, RET_DK - 1, 1), pltpu.roll(xh, 1, 1))
                r = xh * cos + partner * sin
                o_ref[:, c0 + h0:c0 + h0 + half] = (r[:, :half] * sc).astype(bf16)
                o_ref[:, c0 + h0 + half:c0 + h0 + RET_DK] = (r[:, half:] * sc).astype(bf16)
    for c0 in range(2 * qw, w_ref.shape[1], tn):
        p = jnp.dot(h_ref[...], w_ref[:, c0:c0 + tn], preferred_element_type=f32)
        o_ref[:, c0:c0 + tn] = p.astype(bf16)


def _in_odd(x, gain, w, perm, tables, *, layer, tm, tn, split_halves):
    m, d = x.shape
    n = w.shape[1]
    cos, sin, qsc, ksc = tables
    ntab = cos.shape[0] // tm
    qkw = 2 * RET_HEADS * RET_DK
    resident = dict(pipeline_mode=pl.Buffered(1))
    const = lambda i: (0, 0)
    rope_spec = pl.BlockSpec((tm, cos.shape[1]), lambda i: (i % ntab, 0))
    return pl.pallas_call(
        functools.partial(_in_odd_body, tn=tn, split_halves=split_halves),
        grid=(m // tm,),
        in_specs=[
            pl.BlockSpec((tm, d), lambda i: (i, 0)),
            pl.BlockSpec((None, 1, d), lambda i: (layer, 0, 0)),
            pl.BlockSpec(w.shape, const, **resident),
            pl.BlockSpec(perm.shape, const),
            rope_spec, rope_spec,
            pl.BlockSpec(qsc.shape, const),
            pl.BlockSpec(ksc.shape, const),
        ],
        out_specs=pl.BlockSpec((tm, n), lambda i: (i, 0)),
        out_shape=jax.ShapeDtypeStruct((m, n), bf16),
        scratch_shapes=[pltpu.VMEM((tm, d), bf16),
                        pltpu.VMEM((d, qkw) if split_halves else (8, LANES), bf16)],
        compiler_params=_params(("arbitrary",), 56),
        name="in_odd",
    )(x, gain, w, perm, cos, sin, qsc, ksc)


def _ret_prompt_body(q_ref, k_ref, v_ref, g_ref, x_ref, wout_ref, xo_ref, so_ref, s_ref, sb_ref, slab_ref, *,
                     gamma_c, n):
    c = pl.program_id(1)
    subs = [slice(j * n, (j + 1) * n) for j in range(q_ref.shape[0] // n)]

    @pl.when(c == 0)
    def _():
        s_ref[...] = jnp.zeros_like(s_ref)
        sb_ref[...] = jnp.zeros_like(sb_ref)

    causal = lax.broadcasted_iota(jnp.int32, (n, n), 0) >= lax.broadcasted_iota(jnp.int32, (n, n), 1)
    heads = range(RET_HEADS)
    ks = [slice(h * RET_DK, (h + 1) * RET_DK) for h in heads]
    vs = [slice(h * RET_DV, (h + 1) * RET_DV) for h in heads]
    att = [[jnp.where(causal, lax.dot_general(q_ref[r, ks[h]], k_ref[r, ks[h]], NT_DIMS,
                                              preferred_element_type=f32), 0.0).astype(bf16) for h in heads]
           for r in subs]
    o = []
    for j, r in enumerate(subs):
        o.append([jnp.dot(q_ref[r, ks[h]], sb_ref[h], preferred_element_type=f32)
                  + jnp.dot(att[j][h], v_ref[r, vs[h]], preferred_element_type=f32) for h in heads])
        for h in heads:
            kv = lax.dot_general(k_ref[r, ks[h]], v_ref[r, vs[h]], TN_DIMS, preferred_element_type=f32)
            s_new = gamma_c[h] * (s_ref[h] + kv)
            s_ref[h] = s_new
            sb_ref[h] = s_new.astype(bf16)
    for j, r in enumerate(subs):
        y = x_ref[r, :]
        for h in heads:
            og = (_rms(o[j][h]) * _silu(g_ref[r, vs[h]].astype(f32))).astype(bf16)
            y = y + jnp.dot(og, wout_ref[vs[h], :], preferred_element_type=f32)
        xo_ref[r, :] = y

    @pl.when(c == pl.num_programs(1) - 1)
    def _():
        half = RET_DK // 2
        for h in range(RET_HEADS):
            for t in range(RET_DV // LANES):
                ls = slice(t * LANES, (t + 1) * LANES)
                slab_ref[pl.ds(0, half, stride=2), :] = s_ref[h, 0:half, ls]
                slab_ref[pl.ds(1, half, stride=2), :] = s_ref[h, half:RET_DK, ls]
                so_ref[0, h, :, ls] = slab_ref[...]


def _ret_prompt(qkvg, x, wout, gamma_c, *, batch, c, chunk):
    m, d = x.shape
    nc = m // batch // c
    qw = RET_HEADS * RET_DK
    vw = RET_HEADS * RET_DV
    assert c % chunk == 0
    return pl.pallas_call(
        functools.partial(_ret_prompt_body, gamma_c=gamma_c, n=chunk),
        grid=(batch, nc),
        in_specs=[
            pl.BlockSpec((c, qw), lambda b, i: (b * nc + i, 0)),
            pl.BlockSpec((c, qw), lambda b, i: (b * nc + i, 1)),
            pl.BlockSpec((c, vw), lambda b, i: (b * nc + i, 1)),
            pl.BlockSpec((c, vw), lambda b, i: (b * nc + i, 2)),
            pl.BlockSpec((c, d), lambda b, i: (b * nc + i, 0)),
            pl.BlockSpec(wout.shape, lambda b, i: (0, 0), pipeline_mode=pl.Buffered(1)),
        ],
        out_specs=[
            pl.BlockSpec((c, d), lambda b, i: (b * nc + i, 0)),
            pl.BlockSpec((1, RET_HEADS, RET_DK, RET_DV), lambda b, i: (b, 0, 0, 0)),
        ],
        out_shape=[
            jax.ShapeDtypeStruct((m, d), f32),
            jax.ShapeDtypeStruct((batch, RET_HEADS, RET_DK, RET_DV), f32),
        ],
        scratch_shapes=[
            pltpu.VMEM((RET_HEADS, RET_DK, RET_DV), f32),
            pltpu.VMEM((RET_HEADS, RET_DK, RET_DV), bf16),
            pltpu.VMEM((RET_DK, LANES), f32),
        ],
        compiler_params=_params(("arbitrary", "arbitrary"), 48),
        name="ret_prompt",
    )(qkvg, qkvg, qkvg, qkvg, x, wout)


def _rope_tables(pos, per_pair):
    pos = np.asarray(pos, np.float64)
    pair_angle = 1.0 / (ROPE_BASE ** np.linspace(0.0, 1.0, RET_DK // 2))
    if per_pair:
        ang = pos[:, None] * pair_angle[None, :]
        return jnp.asarray(np.cos(ang), dtype=f32), jnp.asarray(np.sin(ang), dtype=f32)
    ang = pos[:, None] * np.repeat(pair_angle, 2)[None, :]
    sign = np.where(np.arange(RET_DK) % 2 == 0, -1.0, 1.0)
    return jnp.asarray(np.cos(ang), dtype=f32), jnp.asarray(np.sin(ang) * sign, dtype=f32)


def _even_odd_perm():
    half = RET_DK // 2
    src = np.concatenate([2 * np.arange(half), 2 * np.arange(half) + 1])
    perm = np.zeros((RET_DK, RET_DK), np.float32)
    perm[src, np.arange(RET_DK)] = 1.0
    return jnp.asarray(perm, dtype=bf16)


def _lane_replicated(scale):
    return jnp.asarray(np.repeat(scale, LANES, axis=1), dtype=f32)


def _ret_decay(rows, c):
    gam = 1.0 - 2.0 ** (-5.0 - np.arange(RET_HEADS, dtype=np.float64))
    lg = np.log(gam)
    steps = (np.arange(rows) % c + 1.0)[:, None]
    q_scale = _lane_replicated(np.exp(lg[None, :] * steps))
    k_scale = _lane_replicated(np.exp(-lg[None, :] * steps) * RET_DK ** -0.5)
    gamma_c = tuple(float(x) for x in np.exp(lg * c))
    gamma = tuple(float(x) for x in gam)
    return q_scale, k_scale, gamma_c, gamma


def kernel(x_prompt, x_sample, state_gla, state_pool, state_ret, norm_mix, norm_ffn, norm_final, w_in_even,
           w_gate_b, b_gate, gla_gain, pool_w, pool_scale, w_out_even, w_in_odd, w_out_odd, w_ffn_gate,
           w_ffn_up, w_ffn_down):
    batch, seq, d = x_prompt.shape
    n_s = x_sample.shape[0]
    assert norm_mix.shape[0] == 2 and x_sample.shape[1] == 1

    we = jnp.pad(w_in_even[0], ((0, 0), (0, -w_in_even.shape[2] % LANES))).astype(bf16)
    nu = POOL_GW * len(POOL_WINDOWS)
    shift = np.zeros((nu + LANES, nu), np.float32)
    shift[GATE_RANK + np.arange(nu), np.arange(nu)] = 1.0
    shift = jnp.asarray(shift, dtype=bf16)
    wgb = jnp.concatenate([w_gate_b[0], jnp.zeros((LANES - GATE_RANK, w_gate_b.shape[2]), f32)], axis=0).astype(bf16)
    bg = b_gate[0][None, :]
    gg = gla_gain[0][None, :]
    pw = pool_w[0].astype(bf16)
    ps = pool_scale[0][None, :]
    woe = w_out_even[0].astype(bf16)
    wio = w_in_odd[0].astype(bf16)
    woo = w_out_odd[0].astype(bf16)
    wg = w_ffn_gate.astype(bf16)
    wu = w_ffn_up.astype(bf16)
    wd = w_ffn_down.astype(bf16)
    nm = norm_mix[:, None, :]
    nf = norm_ffn[:, None, :]
    nfin = norm_final[None, :]
    tril = jnp.asarray(np.tril(np.ones((GLA_CHUNK, GLA_CHUNK), np.float32)), dtype=bf16)
    tf = 256
    tm_p = 512
    q_scale, k_scale, gamma_c, gamma = _ret_decay(tm_p, RET_CHUNK)
    tables_p = _rope_tables(np.arange(seq), True) + (q_scale, k_scale)
    tables_s = _rope_tables(np.full((n_s,), PAST_LEN), False) + (
        _lane_replicated(np.ones((n_s, RET_HEADS))), _lane_replicated(np.full((n_s, RET_HEADS), RET_DK ** -0.5)))
    perm = _even_odd_perm()

    xs = x_sample.reshape(n_s, d)
    qkvg_s, loga_s, u_s = _in_even(xs, nm[0], we, shift, wgb, bg, tm=n_s)
    op_s, gla_s = _gla_pool_sample(qkvg_s, loga_s, u_s, state_gla[0], state_pool[0], gg, pw, ps, bb=8)
    xs = _proj_res(xs, op_s, woe, tm=n_s)
    xs = _ffn_stream(xs, nf, wg, wu, wd, nfin, layer=0, tf=tf, final_norm=False)
    qkvg2_s = _in_odd(xs, nm, wio, perm, tables_s, layer=1, tm=n_s, tn=512, split_halves=False)
    qkvg2_s = qkvg2_s.reshape(n_s, 1, -1)

    rows = n_s // (2 * (batch * seq // tm_p))
    xp = x_prompt.reshape(batch * seq, d)
    qkvg, loga, u_p = _in_even(xp, nm[0], we, shift, wgb, bg, tm=tm_p)
    xp, gla_p = _gla_pool_prompt(qkvg, loga, u_p, xp, tril, gg, pw, ps, woe, batch=batch, t=512)
    xp, og_s, ret_s = _ffn(xp, nf, wg, wu, wd, nfin, layer=0, tm=tm_p, tf=tf, final_norm=False,
                           rider=(qkvg2_s, state_ret[0], rows, gamma))
    qkvg2 = _in_odd(xp, nm, wio, perm, tables_p, layer=1, tm=tm_p, tn=512, split_halves=True)
    xp, ret_p = _ret_prompt(qkvg2, xp, woo, gamma_c, batch=batch, c=2 * RET_CHUNK, chunk=RET_CHUNK)
    y_prompt = _ffn(xp, nf, wg, wu, wd, nfin, layer=1, tm=2 * tm_p, tf=tf, final_norm=True)
    pool_p = u_p.reshape(batch, seq, -1)[:, seq - POOL_BUF:, :]

    xs = _proj_res(xs, og_s.reshape(n_s, -1), woo, tm=n_s)
    y_sample = _ffn_stream(xs, nf, wg, wu, wd, nfin, layer=1, tf=tf, final_norm=True)

    pool_s = jnp.concatenate([state_pool[0][:, 1:, :], u_s[:, None, :]], axis=1)

    return (y_prompt.reshape(batch, seq, d), y_sample.reshape(n_s, 1, d),
            gla_p[None], gla_s[None], pool_p[None], pool_s[None], ret_p[None], ret_s[None])
```

```python
import functools

import numpy as np
import jax
import jax.numpy as jnp
from jax import lax
from jax.experimental import pallas as pl
from jax.experimental.pallas import tpu as pltpu

f32 = jnp.float32
bf16 = jnp.bfloat16

EPS = 1e-6
PAST_LEN = 16384
GLA_HEADS, GLA_DK, GLA_DV = 4, 64, 128
GLA_CHUNK = 64
GATE_RANK = 16
GATE_NORMALIZER = 16.0
POOL_WINDOWS = (2, 4, 8, 16)
POOL_GW = 128
POOL_BUF = max(POOL_WINDOWS) - 1
POOL_HIST = 32
RET_HEADS, RET_DK, RET_DV = 4, 256, 512
RET_CHUNK = 256
ROPE_BASE = 10000.0
LANES = 128
MIB = 1024 * 1024

NT_DIMS = (((1,), (1,)), ((), ()))
TN_DIMS = (((0,), (0,)), ((), ()))


def _params(semantics, vmem_mib):
    return pltpu.CompilerParams(dimension_semantics=semantics, vmem_limit_bytes=vmem_mib * MIB)


def _rms(x, gain=None):
    y = x * lax.rsqrt(jnp.mean(x * x, axis=-1, keepdims=True) + EPS)
    return y if gain is None else y * gain


def _silu(g):
    return g * jax.nn.sigmoid(g)


def _in_even_body(x_ref, gain_ref, w_ref, shift_ref, wgb_ref, bg_ref, qkvg_ref, loga_ref, u_ref, h_ref, wu_ref, *,
                  tn):
    nq = qkvg_ref.shape[1]
    nu = u_ref.shape[1]

    @pl.when(pl.program_id(0) == 0)
    def _():
        wu_ref[...] = jnp.dot(w_ref[:, nq:], shift_ref[...], preferred_element_type=f32).astype(bf16)

    h_ref[...] = _rms(x_ref[...], gain_ref[...]).astype(bf16)
    a = jnp.dot(h_ref[...], w_ref[:, nq:nq + LANES], preferred_element_type=f32)
    a = jnp.where(lax.broadcasted_iota(jnp.int32, a.shape, 1) < GATE_RANK, a, 0.0).astype(bf16)
    for c0 in range(0, nq, tn):
        qkvg_ref[:, c0:c0 + tn] = jnp.dot(h_ref[...], w_ref[:, c0:c0 + tn], preferred_element_type=f32).astype(bf16)
        if c0 == 0:
            z = jnp.dot(a, wgb_ref[...], preferred_element_type=f32) + bg_ref[...]
            loga_ref[...] = (jnp.minimum(z, 0.0) - jnp.log1p(jnp.exp(-jnp.abs(z)))) * (1.0 / GATE_NORMALIZER)
    for c0 in range(0, nu, tn):
        u_ref[:, c0:c0 + tn] = jnp.dot(h_ref[...], wu_ref[:, c0:c0 + tn], preferred_element_type=f32)


def _in_even(x, gain, w, shift, wgb, bg, *, tm):
    m, d = x.shape
    nq = 2 * GLA_HEADS * GLA_DK + 2 * GLA_HEADS * GLA_DV
    nu = POOL_GW * len(POOL_WINDOWS)
    nk = GLA_HEADS * GLA_DK
    const = lambda i: (0, 0)
    return pl.pallas_call(
        functools.partial(_in_even_body, tn=512),
        grid=(m // tm,),
        in_specs=[
            pl.BlockSpec((tm, d), lambda i: (i, 0)),
            pl.BlockSpec((1, d), const),
            pl.BlockSpec(w.shape, const, pipeline_mode=pl.Buffered(1)),
            pl.BlockSpec(shift.shape, const),
            pl.BlockSpec(wgb.shape, const),
            pl.BlockSpec((1, nk), const),
        ],
        out_specs=[
            pl.BlockSpec((tm, nq), lambda i: (i, 0)),
            pl.BlockSpec((tm, nk), lambda i: (i, 0)),
            pl.BlockSpec((tm, nu), lambda i: (i, 0)),
        ],
        out_shape=[
            jax.ShapeDtypeStruct((m, nq), bf16),
            jax.ShapeDtypeStruct((m, nk), f32),
            jax.ShapeDtypeStruct((m, nu), f32),
        ],
        scratch_shapes=[pltpu.VMEM((tm, d), bf16), pltpu.VMEM((d, nu), bf16)],
        compiler_params=_params(("arbitrary",), 48),
        name="in_even",
    )(x, gain, w, shift, wgb, bg)


def _gla_pool_prompt_body(qkvg_ref, loga_ref, u_ref, x_ref, tril_ref, gain_ref, pw_ref, ps_ref, wout_ref,
                          xo_ref, so_ref, st_ref, o_ref, e_ref, p_ref, q_ref, op_ref, after_chunk=None):
    t = x_ref.shape[0]
    ck = GLA_CHUNK
    kw = GLA_HEADS * GLA_DK
    vw = GLA_HEADS * GLA_DV
    pair_w = 2 * GLA_DK
    i = pl.program_id(1)

    @pl.when(i == 0)
    def _():
        st_ref[...] = jnp.zeros_like(st_ref)
        e_ref[0:POOL_HIST, :] = jnp.zeros((POOL_HIST, e_ref.shape[1]), f32)

    tril = tril_ref[...]
    row = lax.broadcasted_iota(jnp.int32, (2 * ck, pair_w), 0)
    lane = lax.broadcasted_iota(jnp.int32, (2 * ck, pair_w), 1)
    first_lanes = lane < GLA_DK
    first_lanes_ck = lax.broadcasted_iota(jnp.int32, (ck, pair_w), 1) < GLA_DK
    same_head = (row < ck) == first_lanes
    causal = same_head & ((row % ck) >= (lane % GLA_DK))
    pairs = range(GLA_HEADS // 2)
    chunks = range(t // ck)

    hist = POOL_HIST
    n = t + hist
    gw = POOL_GW
    u = u_ref[...]
    e_ref[hist:n, :] = u
    p_ref[8:n, :] = e_ref[8:n, :] + e_ref[7:n - 1, :]
    q_ref[16:n, gw:] = p_ref[16:n, gw:] + p_ref[14:n - 2, gw:]
    p_ref[24:n, 2 * gw:] = q_ref[24:n, 2 * gw:] + q_ref[20:n - 4, 2 * gw:]
    q_ref[32:n, 3 * gw:] = p_ref[32:n, 3 * gw:] + p_ref[24:n - 8, 3 * gw:]

    def rows_of(c):
        return slice(c * ck, (c + 1) * ck)

    def v_pair(c, p):
        va = qkvg_ref[rows_of(c), 2 * kw + (2 * p) * GLA_DV:2 * kw + (2 * p + 1) * GLA_DV]
        vb = qkvg_ref[rows_of(c), 2 * kw + (2 * p + 1) * GLA_DV:2 * kw + (2 * p + 2) * GLA_DV]
        return va, vb

    bcs = []
    for c in chunks:
        la = loga_ref[rows_of(c), :]
        la_hi = la.astype(bf16)
        la_lo = (la - la_hi.astype(f32)).astype(bf16)
        bcs.append(jnp.dot(tril, la_hi, preferred_element_type=f32) + jnp.dot(tril, la_lo, preferred_element_type=f32))
    lhs_q, ke2, kds, elast = [], [], [], []
    for c in chunks:
        bc = bcs[c]
        blast = bc[ck - 1:ck, :]
        q = qkvg_ref[rows_of(c), 0:kw].astype(f32) * (GLA_DK ** -0.5)
        k = qkvg_ref[rows_of(c), kw:2 * kw].astype(f32)
        qe = q * jnp.exp(bc)
        ke = (k * jnp.exp(-bc)).astype(bf16)
        kds.append((k * jnp.exp(blast - bc)).astype(bf16))
        elast.append(jnp.exp(blast))
        for p in pairs:
            pl_ = slice(p * pair_w, (p + 1) * pair_w)
            qe_p = qe[:, pl_]
            lhs_q.append(jnp.concatenate([jnp.where(first_lanes_ck, qe_p, 0.0),
                                          jnp.where(first_lanes_ck, 0.0, qe_p)], axis=0).astype(bf16))
            ke2.append(jnp.concatenate([ke[:, pl_], ke[:, pl_]], axis=0))
    att, upd = [], []
    for c in chunks:
        for p in pairs:
            idx = c * len(pairs) + p
            a = lax.dot_general(lhs_q[idx], ke2[idx], NT_DIMS, preferred_element_type=f32)
            att.append(jnp.where(causal, a, 0.0).astype(bf16))
            va, vb = v_pair(c, p)
            r = lax.dot_general(jnp.concatenate([va, vb], axis=1), kds[c][:, p * pair_w:(p + 1) * pair_w], TN_DIMS,
                                preferred_element_type=f32)
            upd.append(jnp.where(first_lanes, r[:GLA_DV], r[GLA_DV:]))
    st = [st_ref[p] for p in pairs]
    for c in chunks:
        for p in pairs:
            idx = c * len(pairs) + p
            va, vb = v_pair(c, p)
            o = lax.dot_general(lhs_q[idx], st[p].astype(bf16), NT_DIMS, preferred_element_type=f32)
            o = o + jnp.dot(att[idx], jnp.concatenate([va, vb], axis=0), preferred_element_type=f32)
            o_ref[rows_of(c), (2 * p) * GLA_DV:(2 * p + 1) * GLA_DV] = o[:ck]
            o_ref[rows_of(c), (2 * p + 1) * GLA_DV:(2 * p + 2) * GLA_DV] = o[ck:]
            st[p] = st[p] * elast[c][:, p * pair_w:(p + 1) * pair_w] + upd[idx]
    for p in pairs:
        st_ref[p] = st[p]

    sums = (p_ref, q_ref, p_ref, q_ref)
    pos = i * t + lax.broadcasted_iota(jnp.int32, (t, 1), 0)
    for gi, w in enumerate(POOL_WINDOWS):
        ls = slice(gi * gw, (gi + 1) * gw)
        cnt = jnp.minimum(w, pos + 1).astype(f32)
        pooled = (sums[gi][hist:n, ls] / cnt - u[:, ls]).astype(bf16)
        pg = jnp.dot(pooled, pw_ref[gi], preferred_element_type=f32) * ps_ref[:, ls]
        op_ref[:, vw + gi * gw:vw + (gi + 1) * gw] = pg.astype(bf16)
    e_ref[hist - 16:hist, :] = e_ref[n - 16:n, :]

    piece = 2 * GLA_DV
    y = x_ref[...]
    for c0 in (vw, vw + piece):
        y = y + jnp.dot(op_ref[:, c0:c0 + piece], wout_ref[c0:c0 + piece, :], preferred_element_type=f32)
    gain = gain_ref[...]
    for p in pairs:
        for h in (2 * p, 2 * p + 1):
            hs = slice(h * GLA_DV, (h + 1) * GLA_DV)
            g = qkvg_ref[:, 2 * kw + vw + h * GLA_DV:2 * kw + vw + (h + 1) * GLA_DV].astype(f32)
            op_ref[:, hs] = (_rms(o_ref[:, hs], gain) * _silu(g)).astype(bf16)
        c0 = p * piece
        y = y + jnp.dot(op_ref[:, c0:c0 + piece], wout_ref[c0:c0 + piece, :], preferred_element_type=f32)
    xo_ref[...] = y

    @pl.when(i == pl.num_programs(1) - 1)
    def _():
        for p in range(GLA_HEADS // 2):
            s_pair = st_ref[p].T
            so_ref[0, 2 * p] = s_pair[:GLA_DK]
            so_ref[0, 2 * p + 1] = s_pair[GLA_DK:]


def _gla_pool_prompt(qkvg, loga, u, x, tril, gain, pw, ps, wout, *, batch, t):
    m, d = x.shape
    nt = m // batch // t
    row = lambda b, i: (b * nt + i, 0)
    const2 = lambda b, i: (0, 0)
    vw = GLA_HEADS * GLA_DV
    uw = u.shape[1]
    return pl.pallas_call(
        _gla_pool_prompt_body,
        grid=(batch, nt),
        in_specs=[
            pl.BlockSpec((t, qkvg.shape[1]), row),
            pl.BlockSpec((t, loga.shape[1]), row),
            pl.BlockSpec((t, uw), row),
            pl.BlockSpec((t, d), row),
            pl.BlockSpec(tril.shape, const2),
            pl.BlockSpec(gain.shape, const2),
            pl.BlockSpec(pw.shape, lambda b, i: (0, 0, 0)),
            pl.BlockSpec(ps.shape, const2),
            pl.BlockSpec(wout.shape, const2),
        ],
        out_specs=[
            pl.BlockSpec((t, d), row),
            pl.BlockSpec((1, GLA_HEADS, GLA_DK, GLA_DV), lambda b, i: (b, 0, 0, 0)),
        ],
        out_shape=[
            jax.ShapeDtypeStruct((m, d), f32),
            jax.ShapeDtypeStruct((batch, GLA_HEADS, GLA_DK, GLA_DV), f32),
        ],
        scratch_shapes=[
            pltpu.VMEM((GLA_HEADS // 2, GLA_DV, 2 * GLA_DK), f32),
            pltpu.VMEM((t, vw), f32),
            pltpu.VMEM((POOL_HIST + t, uw), f32),
            pltpu.VMEM((POOL_HIST + t, uw), f32),
            pltpu.VMEM((POOL_HIST + t, uw), f32),
            pltpu.VMEM((t, vw + uw), bf16),
        ],
        compiler_params=_params(("arbitrary", "arbitrary"), 56),
        name="gla_pool_prompt",
    )(qkvg, loga, u, x, tril, gain, pw, ps, wout)


def _gla_pool_sample_body(qkvg_ref, loga_ref, u_ref, s_ref, buf_ref, gain_ref, pw_ref, ps_ref,
                          op_ref, so_ref):
    bb = u_ref.shape[0]
    kw = GLA_HEADS * GLA_DK
    vw = GLA_HEADS * GLA_DV
    gain = gain_ref[...]
    qkvg = qkvg_ref[...].astype(f32)
    alpha = jnp.exp(loga_ref[...])
    qs = qkvg[:, 0:kw] * (GLA_DK ** -0.5)
    k = qkvg[:, kw:2 * kw]

    def column(row):
        return jnp.broadcast_to(row, (LANES, kw)).T

    o_rows = []
    for b in range(bb):
        acol = column(alpha[b:b + 1, :])
        qcol = column(qs[b:b + 1, :])
        kcol = column(k[b:b + 1, :])
        o_heads = []
        for h in range(GLA_HEADS):
            ks = slice(h * GLA_DK, (h + 1) * GLA_DK)
            v = qkvg[b:b + 1, 2 * kw + h * GLA_DV:2 * kw + (h + 1) * GLA_DV]
            s_new = acol[ks, :] * s_ref[b, h] + kcol[ks, :] * v
            so_ref[b, h] = s_new
            o = jnp.sum(qcol[ks, :] * s_new, axis=0, keepdims=True)
            g = qkvg[b:b + 1, 2 * kw + vw + h * GLA_DV:2 * kw + vw + (h + 1) * GLA_DV]
            o_heads.append(_rms(o, gain) * _silu(g))
        o_rows.append(jnp.concatenate(o_heads, axis=1))
    op_ref[:, 0:vw] = jnp.concatenate(o_rows, axis=0).astype(bf16)

    u = u_ref[...]
    for gi, w in enumerate(POOL_WINDOWS):
        ls = slice(gi * POOL_GW, (gi + 1) * POOL_GW)
        s = u[:, ls] + jnp.sum(buf_ref[:, POOL_BUF - (w - 1):POOL_BUF, ls], axis=1)
        cnt = float(min(w, PAST_LEN + 1))
        pooled = (s / cnt - u[:, ls]).astype(bf16)
        pg = jnp.dot(pooled, pw_ref[gi], preferred_element_type=f32) * ps_ref[:, ls]
        op_ref[:, vw + gi * POOL_GW:vw + (gi + 1) * POOL_GW] = pg.astype(bf16)


def _gla_pool_sample(qkvg, loga, u, s, buf, gain, pw, ps, *, bb):
    n = u.shape[0]
    row = lambda i: (i, 0)
    const2 = lambda i: (0, 0)
    ow = GLA_HEADS * GLA_DV + POOL_GW * len(POOL_WINDOWS)
    return pl.pallas_call(
        _gla_pool_sample_body,
        grid=(n // bb,),
        in_specs=[
            pl.BlockSpec((bb, qkvg.shape[1]), row),
            pl.BlockSpec((bb, loga.shape[1]), row),
            pl.BlockSpec((bb, u.shape[1]), row),
            pl.BlockSpec((bb,) + s.shape[1:], lambda i: (i, 0, 0, 0)),
            pl.BlockSpec((bb,) + buf.shape[1:], lambda i: (i, 0, 0)),
            pl.BlockSpec(gain.shape, const2),
            pl.BlockSpec(pw.shape, lambda i: (0, 0, 0)),
            pl.BlockSpec(ps.shape, const2),
        ],
        out_specs=[
            pl.BlockSpec((bb, ow), row),
            pl.BlockSpec((bb,) + s.shape[1:], lambda i: (i, 0, 0, 0)),
        ],
        out_shape=[
            jax.ShapeDtypeStruct((n, ow), bf16),
            jax.ShapeDtypeStruct(s.shape, f32),
        ],
        compiler_params=_params(("arbitrary",), 32),
        name="gla_pool_sample",
    )(qkvg, loga, u, s, buf, gain, pw, ps)


def _proj_res_body(x_ref, a_ref, w_ref, o_ref):
    o_ref[...] = x_ref[...] + jnp.dot(a_ref[...], w_ref[...], preferred_element_type=f32)


def _proj_res(x, a, w, *, tm):
    m, d = x.shape
    return pl.pallas_call(
        _proj_res_body,
        grid=(m // tm,),
        in_specs=[
            pl.BlockSpec((tm, d), lambda i: (i, 0)),
            pl.BlockSpec((tm, a.shape[1]), lambda i: (i, 0)),
            pl.BlockSpec(w.shape, lambda i: (0, 0)),
        ],
        out_specs=pl.BlockSpec((tm, d), lambda i: (i, 0)),
        out_shape=jax.ShapeDtypeStruct((m, d), f32),
        compiler_params=_params(("arbitrary",), 32),
        name="proj_res",
    )(x, a, w)


def _ret_token_pieces(q_ref, k_ref, v_ref, g_ref, s_ref, og_ref, so_ref, gamma):
    def piece(j, h):
        def run():
            ks = slice(h * RET_DK, (h + 1) * RET_DK)
            vs = slice(h * RET_DV, (h + 1) * RET_DV)
            qcol = jnp.broadcast_to(q_ref[j, :, ks].astype(f32), (LANES, RET_DK)).T
            kcol = jnp.broadcast_to(k_ref[j, :, ks].astype(f32), (LANES, RET_DK)).T
            v = v_ref[j, :, vs].astype(f32)
            g = g_ref[j, :, vs].astype(f32)
            o_tiles = []
            for t in range(RET_DV // LANES):
                cs = slice(t * LANES, (t + 1) * LANES)
                s_new = gamma[h] * s_ref[j, h, :, cs] + kcol * v[:, cs]
                so_ref[j, h, :, cs] = s_new
                o_tiles.append(jnp.sum(qcol * s_new, axis=0, keepdims=True))
            o = jnp.concatenate(o_tiles, axis=1)
            og_ref[j, :, vs] = (_rms(o) * _silu(g)).astype(bf16)
            return o
        return run

    return [piece(j, h) for j in range(s_ref.shape[0]) for h in range(RET_HEADS)]


def _ffn_body(*refs, tf, n_sub, final_norm, rider_gamma):
    x_ref, gain_ref, wg_ref, wu_ref, wd_ref, fgain_ref = refs[:6]
    pieces = []
    if rider_gamma is None:
        o_ref, h_ref, acc_ref = refs[6:]
    else:
        rq_ref, rk_ref, rv_ref, rg_ref, rs_ref, o_ref, rog_ref, rso_ref, h_ref, acc_ref = refs[6:]
        pieces = _ret_token_pieces(rq_ref, rk_ref, rv_ref, rg_ref, rs_ref, rog_ref, rso_ref, rider_gamma)
    n_chunks = wg_ref.shape[1] // tf
    bounds = [n_chunks * s // n_sub for s in range(n_sub + 1)]

    def exact_zero(v):
        bits = lax.bitcast_convert_type(v, jnp.uint32)
        return ((bits >> 16) >> 16).astype(f32)

    def run_chunks(chunks):
        pin = None
        for n, c in enumerate(chunks):
            cs = slice(c * tf, (c + 1) * tf)
            g = jnp.dot(h_ref[...], wg_ref[:, cs], preferred_element_type=f32)
            if pin is not None:
                g = g + pin
                pin = None
            u = jnp.dot(h_ref[...], wu_ref[:, cs], preferred_element_type=f32)
            a = (_silu(g) * u).astype(bf16)
            part = jnp.dot(a, wd_ref[cs, :], preferred_element_type=f32)
            if c == 0:
                acc_ref[...] = part
            else:
                acc_ref[...] += part
            for p in range(len(pieces)):
                if p * (len(chunks) - 1) // len(pieces) == n:
                    z = exact_zero(pieces[p]()[:, :tf])
                    pin = z if pin is None else pin + z

    def sub_step(s):
        if s == 0:
            h_ref[...] = _rms(x_ref[...], gain_ref[...]).astype(bf16)
        run_chunks(range(bounds[s], bounds[s + 1]))
        if s == n_sub - 1:
            y = x_ref[...] + acc_ref[...]
            if final_norm:
                y = _rms(y, fgain_ref[...])
            o_ref[...] = y

    if n_sub == 1:
        sub_step(0)
    else:
        for s in range(n_sub):
            pl.when(pl.program_id(1) == s)(functools.partial(sub_step, s))


def _ffn(x, gain, wg, wu, wd, fgain, *, layer, tm, tf, final_norm, rider=None):
    m, d = x.shape
    ff = wg.shape[1]
    steps = m // tm
    n_sub = 1 if rider is None else 2
    resident = dict(pipeline_mode=pl.Buffered(1))
    in_specs = [
        pl.BlockSpec((tm, d), lambda i, s: (i, 0)),
        pl.BlockSpec((None, 1, d), lambda i, s: (layer, 0, 0)),
        pl.BlockSpec((d, ff), lambda i, s: (0, 0), **resident),
        pl.BlockSpec((d, ff), lambda i, s: (0, 0), **resident),
        pl.BlockSpec((ff, d), lambda i, s: (0, 0), **resident),
        pl.BlockSpec((1, d), lambda i, s: (0, 0)),
    ]
    args = [x, gain, wg, wu, wd, fgain]
    out_specs = [pl.BlockSpec((tm, d), lambda i, s: (i, 0))]
    out_shape = [jax.ShapeDtypeStruct((m, d), f32)]
    gamma = None
    need = 3 * d * ff * 2 + 4 * tm * d * 4 + tm * d * (2 + 4) + 3 * tm * tf * 4 + tm * d * 4
    vmem = need // MIB + 6
    if rider is not None:
        qkvg3, state, rows, gamma = rider
        assert 2 * steps * rows == state.shape[0]
        qw = RET_HEADS * RET_DK
        vw = RET_HEADS * RET_DV
        blk = lambda col: (lambda i, s: (2 * i + s, 0, col))
        state_spec = pl.BlockSpec((rows,) + state.shape[1:], lambda i, s: (2 * i + s, 0, 0, 0))
        in_specs += [
            pl.BlockSpec((rows, 1, qw), blk(0)),
            pl.BlockSpec((rows, 1, qw), blk(1)),
            pl.BlockSpec((rows, 1, vw), blk(1)),
            pl.BlockSpec((rows, 1, vw), blk(2)),
            state_spec,
        ]
        args += [qkvg3, qkvg3, qkvg3, qkvg3, state]
        out_specs += [pl.BlockSpec((rows, 1, vw), blk(0)), state_spec]
        out_shape += [
            jax.ShapeDtypeStruct((state.shape[0], 1, vw), bf16),
            jax.ShapeDtypeStruct(state.shape, f32),
        ]
        vmem += 4 * rows * int(np.prod(state.shape[1:])) * 4 // MIB
    out = pl.pallas_call(
        functools.partial(_ffn_body, tf=tf, n_sub=n_sub, final_norm=final_norm, rider_gamma=gamma),
        grid=(steps, n_sub),
        in_specs=in_specs,
        out_specs=out_specs,
        out_shape=out_shape,
        scratch_shapes=[pltpu.VMEM((tm, d), bf16), pltpu.VMEM((tm, d), f32)],
        compiler_params=_params(("arbitrary", "arbitrary"), vmem),
        name="ffn_final" if final_norm else "ffn",
    )(*args)
    return out[0] if rider is None else out


def _ffn_stream_body(x_ref, gain_ref, wg_ref, wu_ref, wd_ref, fgain_ref, o_ref, wg16_ref, wu16_ref, wd16_ref,
                     h_ref, acc_ref, *, final_norm):
    j = pl.program_id(0)

    @pl.when(j == 0)
    def _():
        h_ref[...] = _rms(x_ref[...], gain_ref[...]).astype(bf16)
        acc_ref[...] = jnp.zeros_like(acc_ref)

    wg16_ref[...] = wg_ref[...].astype(bf16)
    wu16_ref[...] = wu_ref[...].astype(bf16)
    wd16_ref[...] = wd_ref[...].astype(bf16)
    g = jnp.dot(h_ref[...], wg16_ref[...], preferred_element_type=f32)
    u = jnp.dot(h_ref[...], wu16_ref[...], preferred_element_type=f32)
    acc_ref[...] += jnp.dot((_silu(g) * u).astype(bf16), wd16_ref[...], preferred_element_type=f32)

    @pl.when(j == pl.num_programs(0) - 1)
    def _():
        y = x_ref[...] + acc_ref[...]
        if final_norm:
            y = _rms(y, fgain_ref[...])
        o_ref[...] = y


def _ffn_stream(x, gain, wg, wu, wd, fgain, *, layer, tf, final_norm):
    m, d = x.shape
    ff = wg.shape[2]
    return pl.pallas_call(
        functools.partial(_ffn_stream_body, final_norm=final_norm),
        grid=(ff // tf,),
        in_specs=[
            pl.BlockSpec((m, d), lambda j: (0, 0)),
            pl.BlockSpec((None, 1, d), lambda j: (layer, 0, 0)),
            pl.BlockSpec((None, d, tf), lambda j: (layer, 0, j)),
            pl.BlockSpec((None, d, tf), lambda j: (layer, 0, j)),
            pl.BlockSpec((None, tf, d), lambda j: (layer, j, 0)),
            pl.BlockSpec((1, d), lambda j: (0, 0)),
        ],
        out_specs=[
            pl.BlockSpec((m, d), lambda j: (0, 0)),
            pl.BlockSpec((d, tf), lambda j: (0, j)),
            pl.BlockSpec((d, tf), lambda j: (0, j)),
            pl.BlockSpec((tf, d), lambda j: (j, 0)),
        ],
        out_shape=[
            jax.ShapeDtypeStruct((m, d), f32),
            jax.ShapeDtypeStruct((d, ff), bf16),
            jax.ShapeDtypeStruct((d, ff), bf16),
            jax.ShapeDtypeStruct((ff, d), bf16),
        ],
        scratch_shapes=[pltpu.VMEM((m, d), bf16), pltpu.VMEM((m, d), f32)],
        compiler_params=_params(("arbitrary",), 24),
        name="ffn_final_stream" if final_norm else "ffn_stream",
    )(x, gain, wg, wu, wd, fgain)


def _in_odd_body(x_ref, gain_ref, w_ref, perm_ref, cos_ref, sin_ref, qsc_ref, ksc_ref, o_ref, h_ref, wqk_ref, *,
                 tn, split_halves):
    qw = RET_HEADS * RET_DK
    half = RET_DK // 2
    if split_halves:
        @pl.when(pl.program_id(0) == 0)
        def _():
            for hh in range(2 * RET_HEADS):
                hs = slice(hh * RET_DK, (hh + 1) * RET_DK)
                wqk_ref[:, hs] = jnp.dot(w_ref[:, hs], perm_ref[...], preferred_element_type=f32).astype(bf16)

    h_ref[...] = _rms(x_ref[...], gain_ref[...]).astype(bf16)
    cos = cos_ref[...]
    sin = sin_ref[...]
    for c in range(2 * qw // tn):
        c0 = c * tn
        w_chunk = wqk_ref[:, c0:c0 + tn] if split_halves else w_ref[:, c0:c0 + tn]
        p = jnp.dot(h_ref[...], w_chunk, preferred_element_type=f32)
        sc_ref = qsc_ref if c0 < qw else ksc_ref
        for hh in range(tn // RET_DK):
            h0 = hh * RET_DK
            head = (c0 % qw + h0) // RET_DK
            sc = sc_ref[:, head * LANES:(head + 1) * LANES]
            if split_halves:
                ev = p[:, h0:h0 + half]
                od = p[:, h0 + half:h0 + RET_DK]
                o_ref[:, c0 + h0:c0 + h0 + half] = ((ev * cos - od * sin) * sc).astype(bf16)
                o_ref[:, c0 + h0 + half:c0 + h0 + RET_DK] = ((od * cos + ev * sin) * sc).astype(bf16)
            else:
                xh = p[:, h0:h0 + RET_DK]
                even = lax.broadcasted_iota(jnp.int32, xh.shape, 1) % 2 == 0
                partner = jnp.where(even, pltpu.roll(xh, RET_DK - 1, 1), pltpu.roll(xh, 1, 1))
                r = xh * cos + partner * sin
                o_ref[:, c0 + h0:c0 + h0 + half] = (r[:, :half] * sc).astype(bf16)
                o_ref[:, c0 + h0 + half:c0 + h0 + RET_DK] = (r[:, half:] * sc).astype(bf16)
    for c0 in range(2 * qw, w_ref.shape[1], tn):
        p = jnp.dot(h_ref[...], w_ref[:, c0:c0 + tn], preferred_element_type=f32)
        o_ref[:, c0:c0 + tn] = p.astype(bf16)


def _in_odd(x, gain, w, perm, tables, *, layer, tm, tn, split_halves):
    m, d = x.shape
    n = w.shape[1]
    cos, sin, qsc, ksc = tables
    ntab = cos.shape[0] // tm
    qkw = 2 * RET_HEADS * RET_DK
    resident = dict(pipeline_mode=pl.Buffered(1))
    const = lambda i: (0, 0)
    rope_spec = pl.BlockSpec((tm, cos.shape[1]), lambda i: (i % ntab, 0))
    return pl.pallas_call(
        functools.partial(_in_odd_body, tn=tn, split_halves=split_halves),
        grid=(m // tm,),
        in_specs=[
            pl.BlockSpec((tm, d), lambda i: (i, 0)),
            pl.BlockSpec((None, 1, d), lambda i: (layer, 0, 0)),
            pl.BlockSpec(w.shape, const, **resident),
            pl.BlockSpec(perm.shape, const),
            rope_spec, rope_spec,
            pl.BlockSpec(qsc.shape, const),
            pl.BlockSpec(ksc.shape, const),
        ],
        out_specs=pl.BlockSpec((tm, n), lambda i: (i, 0)),
        out_shape=jax.ShapeDtypeStruct((m, n), bf16),
        scratch_shapes=[pltpu.VMEM((tm, d), bf16),
                        pltpu.VMEM((d, qkw) if split_halves else (8, LANES), bf16)],
        compiler_params=_params(("arbitrary",), 56),
        name="in_odd",
    )(x, gain, w, perm, cos, sin, qsc, ksc)


def _ret_prompt_body(q_ref, k_ref, v_ref, g_ref, x_ref, wout_ref, xo_ref, so_ref, s_ref, sb_ref, slab_ref, *,
                     gamma_c, n):
    c = pl.program_id(1)
    subs = [slice(j * n, (j + 1) * n) for j in range(q_ref.shape[0] // n)]

    @pl.when(c == 0)
    def _():
        s_ref[...] = jnp.zeros_like(s_ref)
        sb_ref[...] = jnp.zeros_like(sb_ref)

    causal = lax.broadcasted_iota(jnp.int32, (n, n), 0) >= lax.broadcasted_iota(jnp.int32, (n, n), 1)
    heads = range(RET_HEADS)
    ks = [slice(h * RET_DK, (h + 1) * RET_DK) for h in heads]
    vs = [slice(h * RET_DV, (h + 1) * RET_DV) for h in heads]
    att = [[jnp.where(causal, lax.dot_general(q_ref[r, ks[h]], k_ref[r, ks[h]], NT_DIMS,
                                              preferred_element_type=f32), 0.0).astype(bf16) for h in heads]
           for r in subs]
    o = []
    for j, r in enumerate(subs):
        o.append([jnp.dot(q_ref[r, ks[h]], sb_ref[h], preferred_element_type=f32)
                  + jnp.dot(att[j][h], v_ref[r, vs[h]], preferred_element_type=f32) for h in heads])
        for h in heads:
            kv = lax.dot_general(k_ref[r, ks[h]], v_ref[r, vs[h]], TN_DIMS, preferred_element_type=f32)
            s_new = gamma_c[h] * (s_ref[h] + kv)
            s_ref[h] = s_new
            sb_ref[h] = s_new.astype(bf16)
    for j, r in enumerate(subs):
        y = x_ref[r, :]
        for h in heads:
            og = (_rms(o[j][h]) * _silu(g_ref[r, vs[h]].astype(f32))).astype(bf16)
            y = y + jnp.dot(og, wout_ref[vs[h], :], preferred_element_type=f32)
        xo_ref[r, :] = y

    @pl.when(c == pl.num_programs(1) - 1)
    def _():
        half = RET_DK // 2
        for h in range(RET_HEADS):
            for t in range(RET_DV // LANES):
                ls = slice(t * LANES, (t + 1) * LANES)
                slab_ref[pl.ds(0, half, stride=2), :] = s_ref[h, 0:half, ls]
                slab_ref[pl.ds(1, half, stride=2), :] = s_ref[h, half:RET_DK, ls]
                so_ref[0, h, :, ls] = slab_ref[...]


def _ret_prompt(qkvg, x, wout, gamma_c, *, batch, c, chunk):
    m, d = x.shape
    nc = m // batch // c
    qw = RET_HEADS * RET_DK
    vw = RET_HEADS * RET_DV
    assert c % chunk == 0
    return pl.pallas_call(
        functools.partial(_ret_prompt_body, gamma_c=gamma_c, n=chunk),
        grid=(batch, nc),
        in_specs=[
            pl.BlockSpec((c, qw), lambda b, i: (b * nc + i, 0)),
            pl.BlockSpec((c, qw), lambda b, i: (b * nc + i, 1)),
            pl.BlockSpec((c, vw), lambda b, i: (b * nc + i, 1)),
            pl.BlockSpec((c, vw), lambda b, i: (b * nc + i, 2)),
            pl.BlockSpec((c, d), lambda b, i: (b * nc + i, 0)),
            pl.BlockSpec(wout.shape, lambda b, i: (0, 0), pipeline_mode=pl.Buffered(1)),
        ],
        out_specs=[
            pl.BlockSpec((c, d), lambda b, i: (b * nc + i, 0)),
            pl.BlockSpec((1, RET_HEADS, RET_DK, RET_DV), lambda b, i: (b, 0, 0, 0)),
        ],
        out_shape=[
            jax.ShapeDtypeStruct((m, d), f32),
            jax.ShapeDtypeStruct((batch, RET_HEADS, RET_DK, RET_DV), f32),
        ],
        scratch_shapes=[
            pltpu.VMEM((RET_HEADS, RET_DK, RET_DV), f32),
            pltpu.VMEM((RET_HEADS, RET_DK, RET_DV), bf16),
            pltpu.VMEM((RET_DK, LANES), f32),
        ],
        compiler_params=_params(("arbitrary", "arbitrary"), 48),
        name="ret_prompt",
    )(qkvg, qkvg, qkvg, qkvg, x, wout)


def _rope_tables(pos, per_pair):
    pair_angle = 1.0 / (ROPE_BASE ** jnp.linspace(0.0, 1.0, RET_DK // 2, dtype=f32))
    if per_pair:
        ang = pos[:, None] * pair_angle[None, :]
        return jnp.cos(ang), jnp.sin(ang)
    ang = pos[:, None] * jnp.repeat(pair_angle, 2)[None, :]
    sign = jnp.where(jnp.arange(RET_DK) % 2 == 0, -1.0, 1.0).astype(f32)
    return jnp.cos(ang), jnp.sin(ang) * sign


def _even_odd_perm():
    half = RET_DK // 2
    src = np.concatenate([2 * np.arange(half), 2 * np.arange(half) + 1])
    perm = np.zeros((RET_DK, RET_DK), np.float32)
    perm[src, np.arange(RET_DK)] = 1.0
    return jnp.asarray(perm, dtype=bf16)


def _lane_replicated(scale):
    return jnp.asarray(np.repeat(scale, LANES, axis=1), dtype=f32)


def _ret_decay(rows, c):
    gam = 1.0 - 2.0 ** (-5.0 - np.arange(RET_HEADS, dtype=np.float64))
    lg = np.log(gam)
    steps = (np.arange(rows) % c + 1.0)[:, None]
    q_scale = _lane_replicated(np.exp(lg[None, :] * steps))
    k_scale = _lane_replicated(np.exp(-lg[None, :] * steps) * RET_DK ** -0.5)
    gamma_c = tuple(float(x) for x in np.exp(lg * c))
    gamma = tuple(float(x) for x in gam)
    return q_scale, k_scale, gamma_c, gamma


def kernel(x_prompt, x_sample, state_gla, state_pool, state_ret, norm_mix, norm_ffn, norm_final, w_in_even,
           w_gate_b, b_gate, gla_gain, pool_w, pool_scale, w_out_even, w_in_odd, w_out_odd, w_ffn_gate,
           w_ffn_up, w_ffn_down):
    batch, seq, d = x_prompt.shape
    n_s = x_sample.shape[0]
    assert norm_mix.shape[0] == 2 and x_sample.shape[1] == 1

    we = jnp.pad(w_in_even[0], ((0, 0), (0, -w_in_even.shape[2] % LANES))).astype(bf16)
    nu = POOL_GW * len(POOL_WINDOWS)
    shift = np.zeros((nu + LANES, nu), np.float32)
    shift[GATE_RANK + np.arange(nu), np.arange(nu)] = 1.0
    shift = jnp.asarray(shift, dtype=bf16)
    wgb = jnp.concatenate([w_gate_b[0], jnp.zeros((LANES - GATE_RANK, w_gate_b.shape[2]), f32)], axis=0).astype(bf16)
    bg = b_gate[0][None, :]
    gg = gla_gain[0][None, :]
    pw = pool_w[0].astype(bf16)
    ps = pool_scale[0][None, :]
    woe = w_out_even[0].astype(bf16)
    wio = w_in_odd[0].astype(bf16)
    woo = w_out_odd[0].astype(bf16)
    nm = norm_mix[:, None, :]
    nf = norm_ffn[:, None, :]
    nfin = norm_final[None, :]
    tril = jnp.asarray(np.tril(np.ones((GLA_CHUNK, GLA_CHUNK), np.float32)), dtype=bf16)
    tf = 256
    tm_p = 512
    q_scale, k_scale, gamma_c, gamma = _ret_decay(tm_p, RET_CHUNK)
    tables_p = _rope_tables(jnp.arange(seq, dtype=f32), True) + (q_scale, k_scale)
    tables_s = _rope_tables(jnp.full((n_s,), float(PAST_LEN), f32), False) + (
        _lane_replicated(np.ones((n_s, RET_HEADS))), _lane_replicated(np.full((n_s, RET_HEADS), RET_DK ** -0.5)))
    perm = _even_odd_perm()

    xs = x_sample.reshape(n_s, d)
    qkvg_s, loga_s, u_s = _in_even(xs, nm[0], we, shift, wgb, bg, tm=n_s)
    op_s, gla_s = _gla_pool_sample(qkvg_s, loga_s, u_s, state_gla[0], state_pool[0], gg, pw, ps, bb=8)
    xs = _proj_res(xs, op_s, woe, tm=n_s)
    xs, *ffn_w0 = _ffn_stream(xs, nf, w_ffn_gate, w_ffn_up, w_ffn_down, nfin, layer=0, tf=tf, final_norm=False)
    qkvg2_s = _in_odd(xs, nm, wio, perm, tables_s, layer=1, tm=n_s, tn=512, split_halves=False)
    qkvg2_s = qkvg2_s.reshape(n_s, 1, -1)

    rows = n_s // (2 * (batch * seq // tm_p))
    xp = x_prompt.reshape(batch * seq, d)
    qkvg, loga, u_p = _in_even(xp, nm[0], we, shift, wgb, bg, tm=tm_p)
    xp, gla_p = _gla_pool_prompt(qkvg, loga, u_p, xp, tril, gg, pw, ps, woe, batch=batch, t=512)
    xp, og_s, ret_s = _ffn(xp, nf, *ffn_w0, nfin, layer=0, tm=tm_p, tf=tf, final_norm=False,
                           rider=(qkvg2_s, state_ret[0], rows, gamma))

    xs = _proj_res(xs, og_s.reshape(n_s, -1), woo, tm=n_s)
    y_sample, *ffn_w1 = _ffn_stream(xs, nf, w_ffn_gate, w_ffn_up, w_ffn_down, nfin, layer=1, tf=tf, final_norm=True)

    qkvg2 = _in_odd(xp, nm, wio, perm, tables_p, layer=1, tm=tm_p, tn=512, split_halves=True)
    xp, ret_p = _ret_prompt(qkvg2, xp, woo, gamma_c, batch=batch, c=2 * RET_CHUNK, chunk=RET_CHUNK)
    y_prompt = _ffn(xp, nf, *ffn_w1, nfin, layer=1, tm=tm_p, tf=tf, final_norm=True)
    pool_p = u_p.reshape(batch, seq, -1)[:, seq - POOL_BUF:, :]

    pool_s = jnp.concatenate([state_pool[0][:, 1:, :], u_s[:, None, :]], axis=1)

    return (y_prompt.reshape(batch, seq, d), y_sample.reshape(n_s, 1, d),
            gla_p[None], gla_s[None], pool_p[None], pool_s[None], ret_p[None], ret_s[None])
```

```python
import functools

import numpy as np
import jax
import jax.numpy as jnp
from jax import lax
from jax.experimental import pallas as pl
from jax.experimental.pallas import tpu as pltpu

f32 = jnp.float32
bf16 = jnp.bfloat16

EPS = 1e-6
PAST_LEN = 16384
GLA_HEADS, GLA_DK, GLA_DV = 4, 64, 128
GLA_CHUNK = 64
GATE_RANK = 16
GATE_NORMALIZER = 16.0
POOL_WINDOWS = (2, 4, 8, 16)
POOL_GW = 128
POOL_BUF = max(POOL_WINDOWS) - 1
POOL_HIST = 32
POOL_TAIL = 16
RET_HEADS, RET_DK, RET_DV = 4, 256, 512
RET_CHUNK = 256
ROPE_BASE = 10000.0
LANES = 128
MIB = 1024 * 1024
VMEM_COMPILER_ALLOWANCE = 8 * MIB
F32_BYTES, BF16_BYTES = 4, 2

NT_DIMS = (((1,), (1,)), ((), ()))
TN_DIMS = (((0,), (0,)), ((), ()))


def _params(semantics, pipelined_bytes, resident_bytes):
    limit = 2 * pipelined_bytes + resident_bytes + VMEM_COMPILER_ALLOWANCE
    return pltpu.CompilerParams(dimension_semantics=semantics, vmem_limit_bytes=int(limit))


def _rms(x, gain=None):
    y = x * lax.rsqrt(jnp.mean(x * x, axis=-1, keepdims=True) + EPS)
    return y if gain is None else y * gain


def _silu(g):
    return g * jax.nn.sigmoid(g)


def _in_even_body(x_ref, gain_ref, w_ref, shift_ref, wgb_ref, bg_ref, qkvg_ref, loga_ref, u_ref, h_ref, wu_ref, *,
                  tn):
    nq = qkvg_ref.shape[1]
    nu = u_ref.shape[1]

    @pl.when(pl.program_id(0) == 0)
    def _():
        wu_ref[...] = jnp.dot(w_ref[:, nq:], shift_ref[...], preferred_element_type=f32).astype(bf16)

    h_ref[...] = _rms(x_ref[...], gain_ref[...]).astype(bf16)
    a = jnp.dot(h_ref[...], w_ref[:, nq:nq + LANES], preferred_element_type=f32)
    a = jnp.where(lax.broadcasted_iota(jnp.int32, a.shape, 1) < GATE_RANK, a, 0.0).astype(bf16)
    for c0 in range(0, nq, tn):
        qkvg_ref[:, c0:c0 + tn] = jnp.dot(h_ref[...], w_ref[:, c0:c0 + tn], preferred_element_type=f32).astype(bf16)
        if c0 == 0:
            z = jnp.dot(a, wgb_ref[...], preferred_element_type=f32) + bg_ref[...]
            loga_ref[...] = (jnp.minimum(z, 0.0) - jnp.log1p(jnp.exp(-jnp.abs(z)))) * (1.0 / GATE_NORMALIZER)
    for c0 in range(0, nu, tn):
        u_ref[:, c0:c0 + tn] = jnp.dot(h_ref[...], wu_ref[:, c0:c0 + tn], preferred_element_type=f32)


def _in_even(x, gain, w, shift, wgb, bg, *, tm):
    m, d = x.shape
    nq = 2 * GLA_HEADS * GLA_DK + 2 * GLA_HEADS * GLA_DV
    nu = POOL_GW * len(POOL_WINDOWS)
    nk = GLA_HEADS * GLA_DK
    const = lambda i: (0, 0)
    tn = 512
    pipelined = tm * (d * F32_BYTES + nq * BF16_BYTES + nk * F32_BYTES + nu * F32_BYTES)
    resident = ((w.size + 2 * shift.size + 2 * wgb.size + tm * d + d * nu) * BF16_BYTES
                + 2 * tm * tn * F32_BYTES)
    return pl.pallas_call(
        functools.partial(_in_even_body, tn=tn),
        grid=(m // tm,),
        in_specs=[
            pl.BlockSpec((tm, d), lambda i: (i, 0)),
            pl.BlockSpec((1, d), const),
            pl.BlockSpec(w.shape, const, pipeline_mode=pl.Buffered(1)),
            pl.BlockSpec(shift.shape, const),
            pl.BlockSpec(wgb.shape, const),
            pl.BlockSpec((1, nk), const),
        ],
        out_specs=[
            pl.BlockSpec((tm, nq), lambda i: (i, 0)),
            pl.BlockSpec((tm, nk), lambda i: (i, 0)),
            pl.BlockSpec((tm, nu), lambda i: (i, 0)),
        ],
        out_shape=[
            jax.ShapeDtypeStruct((m, nq), bf16),
            jax.ShapeDtypeStruct((m, nk), f32),
            jax.ShapeDtypeStruct((m, nu), f32),
        ],
        scratch_shapes=[pltpu.VMEM((tm, d), bf16), pltpu.VMEM((d, nu), bf16)],
        compiler_params=_params(("arbitrary",), pipelined, resident),
        name="in_even",
    )(x, gain, w, shift, wgb, bg)


def _gla_pool_prompt_body(qkvg_ref, loga_ref, u_ref, x_ref, tril_ref, gain_ref, pw_ref, ps_ref, wout_ref,
                          xo_ref, so_ref, st_ref, o_ref, e_ref, p_ref, q_ref, op_ref):
    t = x_ref.shape[0]
    ck = GLA_CHUNK
    kw = GLA_HEADS * GLA_DK
    vw = GLA_HEADS * GLA_DV
    pair_w = 2 * GLA_DK
    i = pl.program_id(1)

    @pl.when(i == 0)
    def _():
        st_ref[...] = jnp.zeros_like(st_ref)
        e_ref[0:POOL_HIST, :] = jnp.zeros((POOL_HIST, e_ref.shape[1]), f32)

    tril = tril_ref[...]
    row = lax.broadcasted_iota(jnp.int32, (2 * ck, pair_w), 0)
    lane = lax.broadcasted_iota(jnp.int32, (2 * ck, pair_w), 1)
    first_lanes = lane < GLA_DK
    first_lanes_ck = lax.broadcasted_iota(jnp.int32, (ck, pair_w), 1) < GLA_DK
    same_head = (row < ck) == first_lanes
    causal = same_head & ((row % ck) >= (lane % GLA_DK))
    pairs = range(GLA_HEADS // 2)
    chunks = range(t // ck)

    hist = POOL_HIST
    n = t + hist
    gw = POOL_GW
    u = u_ref[...]
    e_ref[hist:n, :] = u
    p_ref[8:n, :] = e_ref[8:n, :] + e_ref[7:n - 1, :]
    q_ref[16:n, gw:] = p_ref[16:n, gw:] + p_ref[14:n - 2, gw:]
    p_ref[24:n, 2 * gw:] = q_ref[24:n, 2 * gw:] + q_ref[20:n - 4, 2 * gw:]
    q_ref[32:n, 3 * gw:] = p_ref[32:n, 3 * gw:] + p_ref[24:n - 8, 3 * gw:]

    def rows_of(c):
        return slice(c * ck, (c + 1) * ck)

    def v_pair(c, p):
        va = qkvg_ref[rows_of(c), 2 * kw + (2 * p) * GLA_DV:2 * kw + (2 * p + 1) * GLA_DV]
        vb = qkvg_ref[rows_of(c), 2 * kw + (2 * p + 1) * GLA_DV:2 * kw + (2 * p + 2) * GLA_DV]
        return va, vb

    bcs = []
    for c in chunks:
        la = loga_ref[rows_of(c), :]
        la_hi = la.astype(bf16)
        la_lo = (la - la_hi.astype(f32)).astype(bf16)
        bcs.append(jnp.dot(tril, la_hi, preferred_element_type=f32) + jnp.dot(tril, la_lo, preferred_element_type=f32))
    lhs_q, ke2, kds, elast = [], [], [], []
    for c in chunks:
        bc = bcs[c]
        blast = bc[ck - 1:ck, :]
        q = qkvg_ref[rows_of(c), 0:kw].astype(f32) * (GLA_DK ** -0.5)
        k = qkvg_ref[rows_of(c), kw:2 * kw].astype(f32)
        qe = q * jnp.exp(bc)
        ke = (k * jnp.exp(-bc)).astype(bf16)
        kds.append((k * jnp.exp(blast - bc)).astype(bf16))
        elast.append(jnp.exp(blast))
        for p in pairs:
            pl_ = slice(p * pair_w, (p + 1) * pair_w)
            qe_p = qe[:, pl_]
            lhs_q.append(jnp.concatenate([jnp.where(first_lanes_ck, qe_p, 0.0),
                                          jnp.where(first_lanes_ck, 0.0, qe_p)], axis=0).astype(bf16))
            ke2.append(jnp.concatenate([ke[:, pl_], ke[:, pl_]], axis=0))
    att, upd = [], []
    for c in chunks:
        for p in pairs:
            idx = c * len(pairs) + p
            a = lax.dot_general(lhs_q[idx], ke2[idx], NT_DIMS, preferred_element_type=f32)
            att.append(jnp.where(causal, a, 0.0).astype(bf16))
            va, vb = v_pair(c, p)
            r = lax.dot_general(jnp.concatenate([va, vb], axis=1), kds[c][:, p * pair_w:(p + 1) * pair_w], TN_DIMS,
                                preferred_element_type=f32)
            upd.append(jnp.where(first_lanes, r[:GLA_DV], r[GLA_DV:]))
    st = [st_ref[p] for p in pairs]
    for c in chunks:
        for p in pairs:
            idx = c * len(pairs) + p
            va, vb = v_pair(c, p)
            o = lax.dot_general(lhs_q[idx], st[p].astype(bf16), NT_DIMS, preferred_element_type=f32)
            o = o + jnp.dot(att[idx], jnp.concatenate([va, vb], axis=0), preferred_element_type=f32)
            o_ref[rows_of(c), (2 * p) * GLA_DV:(2 * p + 1) * GLA_DV] = o[:ck]
            o_ref[rows_of(c), (2 * p + 1) * GLA_DV:(2 * p + 2) * GLA_DV] = o[ck:]
            st[p] = st[p] * elast[c][:, p * pair_w:(p + 1) * pair_w] + upd[idx]
    for p in pairs:
        st_ref[p] = st[p]

    sums = (p_ref, q_ref, p_ref, q_ref)
    pos = i * t + lax.broadcasted_iota(jnp.int32, (t, 1), 0)
    for gi, w in enumerate(POOL_WINDOWS):
        ls = slice(gi * gw, (gi + 1) * gw)
        cnt = jnp.minimum(w, pos + 1).astype(f32)
        pooled = (sums[gi][hist:n, ls] / cnt - u[:, ls]).astype(bf16)
        pg = jnp.dot(pooled, pw_ref[gi], preferred_element_type=f32) * ps_ref[:, ls]
        op_ref[:, vw + gi * gw:vw + (gi + 1) * gw] = pg.astype(bf16)
    e_ref[hist - POOL_TAIL:hist, :] = e_ref[n - POOL_TAIL:n, :]

    piece = 2 * GLA_DV
    y = x_ref[...]
    for c0 in (vw, vw + piece):
        y = y + jnp.dot(op_ref[:, c0:c0 + piece], wout_ref[c0:c0 + piece, :], preferred_element_type=f32)
    gain = gain_ref[...]
    for p in pairs:
        for h in (2 * p, 2 * p + 1):
            hs = slice(h * GLA_DV, (h + 1) * GLA_DV)
            g = qkvg_ref[:, 2 * kw + vw + h * GLA_DV:2 * kw + vw + (h + 1) * GLA_DV].astype(f32)
            op_ref[:, hs] = (_rms(o_ref[:, hs], gain) * _silu(g)).astype(bf16)
        c0 = p * piece
        y = y + jnp.dot(op_ref[:, c0:c0 + piece], wout_ref[c0:c0 + piece, :], preferred_element_type=f32)
    xo_ref[...] = y

    @pl.when(i == pl.num_programs(1) - 1)
    def _():
        for p in range(GLA_HEADS // 2):
            s_pair = st_ref[p].T
            so_ref[0, 2 * p] = s_pair[:GLA_DK]
            so_ref[0, 2 * p + 1] = s_pair[GLA_DK:]


def _gla_pool_prompt(qkvg, loga, u, x, tril, gain, pw, ps, wout, *, batch, t):
    m, d = x.shape
    nt = m // batch // t
    row = lambda b, i: (b * nt + i, 0)
    const2 = lambda b, i: (0, 0)
    vw = GLA_HEADS * GLA_DV
    uw = u.shape[1]
    pipelined = t * (qkvg.shape[1] * BF16_BYTES + (loga.shape[1] + uw + 2 * d) * F32_BYTES)
    scratch = (t * vw + 3 * (POOL_HIST + t) * uw) * F32_BYTES + t * (vw + uw) * BF16_BYTES
    waves = (t // GLA_CHUNK) * (GLA_HEADS // 2) * (3 * LANES * LANES * BF16_BYTES + LANES * LANES * F32_BYTES)
    resident = 2 * (wout.size + pw.size) * BF16_BYTES + scratch + waves + t * d * F32_BYTES
    return pl.pallas_call(
        _gla_pool_prompt_body,
        grid=(batch, nt),
        in_specs=[
            pl.BlockSpec((t, qkvg.shape[1]), row),
            pl.BlockSpec((t, loga.shape[1]), row),
            pl.BlockSpec((t, uw), row),
            pl.BlockSpec((t, d), row),
            pl.BlockSpec(tril.shape, const2),
            pl.BlockSpec(gain.shape, const2),
            pl.BlockSpec(pw.shape, lambda b, i: (0, 0, 0)),
            pl.BlockSpec(ps.shape, const2),
            pl.BlockSpec(wout.shape, const2),
        ],
        out_specs=[
            pl.BlockSpec((t, d), row),
            pl.BlockSpec((1, GLA_HEADS, GLA_DK, GLA_DV), lambda b, i: (b, 0, 0, 0)),
        ],
        out_shape=[
            jax.ShapeDtypeStruct((m, d), f32),
            jax.ShapeDtypeStruct((batch, GLA_HEADS, GLA_DK, GLA_DV), f32),
        ],
        scratch_shapes=[
            pltpu.VMEM((GLA_HEADS // 2, GLA_DV, 2 * GLA_DK), f32),
            pltpu.VMEM((t, vw), f32),
            pltpu.VMEM((POOL_HIST + t, uw), f32),
            pltpu.VMEM((POOL_HIST + t, uw), f32),
            pltpu.VMEM((POOL_HIST + t, uw), f32),
            pltpu.VMEM((t, vw + uw), bf16),
        ],
        compiler_params=_params(("arbitrary", "arbitrary"), pipelined, resident),
        name="gla_pool_prompt",
    )(qkvg, loga, u, x, tril, gain, pw, ps, wout)


def _gla_pool_sample_body(qkvg_ref, loga_ref, u_ref, s_ref, buf_ref, gain_ref, pw_ref, ps_ref,
                          op_ref, so_ref):
    bb = u_ref.shape[0]
    kw = GLA_HEADS * GLA_DK
    vw = GLA_HEADS * GLA_DV
    gain = gain_ref[...]
    qkvg = qkvg_ref[...].astype(f32)
    alpha = jnp.exp(loga_ref[...])
    qs = qkvg[:, 0:kw] * (GLA_DK ** -0.5)
    k = qkvg[:, kw:2 * kw]

    def column(row):
        return jnp.broadcast_to(row, (LANES, kw)).T

    o_rows = []
    for b in range(bb):
        acol = column(alpha[b:b + 1, :])
        qcol = column(qs[b:b + 1, :])
        kcol = column(k[b:b + 1, :])
        o_heads = []
        for h in range(GLA_HEADS):
            ks = slice(h * GLA_DK, (h + 1) * GLA_DK)
            v = qkvg[b:b + 1, 2 * kw + h * GLA_DV:2 * kw + (h + 1) * GLA_DV]
            s_new = acol[ks, :] * s_ref[b, h] + kcol[ks, :] * v
            so_ref[b, h] = s_new
            o = jnp.sum(qcol[ks, :] * s_new, axis=0, keepdims=True)
            g = qkvg[b:b + 1, 2 * kw + vw + h * GLA_DV:2 * kw + vw + (h + 1) * GLA_DV]
            o_heads.append(_rms(o, gain) * _silu(g))
        o_rows.append(jnp.concatenate(o_heads, axis=1))
    op_ref[:, 0:vw] = jnp.concatenate(o_rows, axis=0).astype(bf16)

    u = u_ref[...]
    for gi, w in enumerate(POOL_WINDOWS):
        ls = slice(gi * POOL_GW, (gi + 1) * POOL_GW)
        s = u[:, ls] + jnp.sum(buf_ref[:, POOL_BUF - (w - 1):POOL_BUF, ls], axis=1)
        cnt = float(min(w, PAST_LEN + 1))
        pooled = (s / cnt - u[:, ls]).astype(bf16)
        pg = jnp.dot(pooled, pw_ref[gi], preferred_element_type=f32) * ps_ref[:, ls]
        op_ref[:, vw + gi * POOL_GW:vw + (gi + 1) * POOL_GW] = pg.astype(bf16)


def _gla_pool_sample(qkvg, loga, u, s, buf, gain, pw, ps, *, bb):
    n = u.shape[0]
    row = lambda i: (i, 0)
    const2 = lambda i: (0, 0)
    ow = GLA_HEADS * GLA_DV + POOL_GW * len(POOL_WINDOWS)
    state_rows = int(np.prod(s.shape[1:]))
    pipelined = bb * ((qkvg.shape[1] + ow) * BF16_BYTES
                      + (loga.shape[1] + u.shape[1] + 2 * state_rows + POOL_TAIL * buf.shape[2]) * F32_BYTES)
    resident = 2 * pw.size * BF16_BYTES + 3 * LANES * loga.shape[1] * F32_BYTES
    return pl.pallas_call(
        _gla_pool_sample_body,
        grid=(n // bb,),
        in_specs=[
            pl.BlockSpec((bb, qkvg.shape[1]), row),
            pl.BlockSpec((bb, loga.shape[1]), row),
            pl.BlockSpec((bb, u.shape[1]), row),
            pl.BlockSpec((bb,) + s.shape[1:], lambda i: (i, 0, 0, 0)),
            pl.BlockSpec((bb,) + buf.shape[1:], lambda i: (i, 0, 0)),
            pl.BlockSpec(gain.shape, const2),
            pl.BlockSpec(pw.shape, lambda i: (0, 0, 0)),
            pl.BlockSpec(ps.shape, const2),
        ],
        out_specs=[
            pl.BlockSpec((bb, ow), row),
            pl.BlockSpec((bb,) + s.shape[1:], lambda i: (i, 0, 0, 0)),
        ],
        out_shape=[
            jax.ShapeDtypeStruct((n, ow), bf16),
            jax.ShapeDtypeStruct(s.shape, f32),
        ],
        compiler_params=_params(("arbitrary",), pipelined, resident),
        name="gla_pool_sample",
    )(qkvg, loga, u, s, buf, gain, pw, ps)


def _proj_res_body(x_ref, a_ref, w_ref, o_ref):
    o_ref[...] = x_ref[...] + jnp.dot(a_ref[...], w_ref[...], preferred_element_type=f32)


def _proj_res(x, a, w, *, tm):
    m, d = x.shape
    return pl.pallas_call(
        _proj_res_body,
        grid=(m // tm,),
        in_specs=[
            pl.BlockSpec((tm, d), lambda i: (i, 0)),
            pl.BlockSpec((tm, a.shape[1]), lambda i: (i, 0)),
            pl.BlockSpec(w.shape, lambda i: (0, 0)),
        ],
        out_specs=pl.BlockSpec((tm, d), lambda i: (i, 0)),
        out_shape=jax.ShapeDtypeStruct((m, d), f32),
        compiler_params=_params(("arbitrary",), tm * (2 * d * F32_BYTES + a.shape[1] * BF16_BYTES),
                                2 * w.size * BF16_BYTES),
        name="proj_res",
    )(x, a, w)


def _ret_token_pieces(q_ref, k_ref, v_ref, g_ref, s_ref, og_ref, so_ref, gamma):
    def piece(j, h):
        def run():
            ks = slice(h * RET_DK, (h + 1) * RET_DK)
            vs = slice(h * RET_DV, (h + 1) * RET_DV)
            qcol = jnp.broadcast_to(q_ref[j, :, ks].astype(f32), (LANES, RET_DK)).T
            kcol = jnp.broadcast_to(k_ref[j, :, ks].astype(f32), (LANES, RET_DK)).T
            v = v_ref[j, :, vs].astype(f32)
            g = g_ref[j, :, vs].astype(f32)
            o_tiles = []
            for t in range(RET_DV // LANES):
                cs = slice(t * LANES, (t + 1) * LANES)
                s_new = gamma[h] * s_ref[j, h, :, cs] + kcol * v[:, cs]
                so_ref[j, h, :, cs] = s_new
                o_tiles.append(jnp.sum(qcol * s_new, axis=0, keepdims=True))
            o = jnp.concatenate(o_tiles, axis=1)
            og_ref[j, :, vs] = (_rms(o) * _silu(g)).astype(bf16)
            return o
        return run

    return [piece(j, h) for j in range(s_ref.shape[0]) for h in range(RET_HEADS)]


def _ffn_body(*refs, tf, n_sub, final_norm, rider_gamma):
    x_ref, gain_ref, wg_ref, wu_ref, wd_ref, fgain_ref = refs[:6]
    pieces = []
    if rider_gamma is None:
        o_ref, h_ref, acc_ref = refs[6:]
    else:
        rq_ref, rk_ref, rv_ref, rg_ref, rs_ref, o_ref, rog_ref, rso_ref, h_ref, acc_ref = refs[6:]
        pieces = _ret_token_pieces(rq_ref, rk_ref, rv_ref, rg_ref, rs_ref, rog_ref, rso_ref, rider_gamma)
    n_chunks = wg_ref.shape[1] // tf
    bounds = [n_chunks * s // n_sub for s in range(n_sub + 1)]

    def exact_zero(v):
        bits = lax.bitcast_convert_type(v, jnp.uint32)
        return ((bits >> 16) >> 16).astype(f32)

    def run_chunks(chunks):
        pin = None
        for n, c in enumerate(chunks):
            cs = slice(c * tf, (c + 1) * tf)
            g = jnp.dot(h_ref[...], wg_ref[:, cs], preferred_element_type=f32)
            if pin is not None:
                g = g + pin
                pin = None
            u = jnp.dot(h_ref[...], wu_ref[:, cs], preferred_element_type=f32)
            a = (_silu(g) * u).astype(bf16)
            part = jnp.dot(a, wd_ref[cs, :], preferred_element_type=f32)
            if c == 0:
                acc_ref[...] = part
            else:
                acc_ref[...] += part
            for p in range(len(pieces)):
                if p * (len(chunks) - 1) // len(pieces) == n:
                    z = exact_zero(pieces[p]()[:, :tf])
                    pin = z if pin is None else pin + z

    def sub_step(s):
        if s == 0:
            h_ref[...] = _rms(x_ref[...], gain_ref[...]).astype(bf16)
        run_chunks(range(bounds[s], bounds[s + 1]))
        if s == n_sub - 1:
            y = x_ref[...] + acc_ref[...]
            if final_norm:
                y = _rms(y, fgain_ref[...])
            o_ref[...] = y

    if n_sub == 1:
        sub_step(0)
    else:
        for s in range(n_sub):
            pl.when(pl.program_id(1) == s)(functools.partial(sub_step, s))


def _ffn(x, gain, wg, wu, wd, fgain, *, layer, tm, tf, final_norm, rider=None):
    m, d = x.shape
    ff = wg.shape[2]
    steps = m // tm
    n_sub = 1 if rider is None else 2
    single_buffered = dict(pipeline_mode=pl.Buffered(1))
    in_specs = [
        pl.BlockSpec((tm, d), lambda i, s: (i, 0)),
        pl.BlockSpec((None, 1, d), lambda i, s: (layer, 0, 0)),
        pl.BlockSpec((None, d, ff), lambda i, s: (layer, 0, 0), **single_buffered),
        pl.BlockSpec((None, d, ff), lambda i, s: (layer, 0, 0), **single_buffered),
        pl.BlockSpec((None, ff, d), lambda i, s: (layer, 0, 0), **single_buffered),
        pl.BlockSpec((1, d), lambda i, s: (0, 0)),
    ]
    args = [x, gain, wg, wu, wd, fgain]
    out_specs = [pl.BlockSpec((tm, d), lambda i, s: (i, 0))]
    out_shape = [jax.ShapeDtypeStruct((m, d), f32)]
    gamma = None
    pipelined = 2 * tm * d * F32_BYTES
    resident = (3 * d * ff * BF16_BYTES + tm * d * (BF16_BYTES + F32_BYTES)
                + 3 * tm * tf * F32_BYTES + tm * d * F32_BYTES)
    if rider is not None:
        qkvg3, state, rows, gamma = rider
        assert 2 * steps * rows == state.shape[0]
        qw = RET_HEADS * RET_DK
        vw = RET_HEADS * RET_DV
        blk = lambda col: (lambda i, s: (2 * i + s, 0, col))
        state_spec = pl.BlockSpec((rows,) + state.shape[1:], lambda i, s: (2 * i + s, 0, 0, 0))
        in_specs += [
            pl.BlockSpec((rows, 1, qw), blk(0)),
            pl.BlockSpec((rows, 1, qw), blk(1)),
            pl.BlockSpec((rows, 1, vw), blk(1)),
            pl.BlockSpec((rows, 1, vw), blk(2)),
            state_spec,
        ]
        args += [qkvg3, qkvg3, qkvg3, qkvg3, state]
        out_specs += [pl.BlockSpec((rows, 1, vw), blk(0)), state_spec]
        out_shape += [
            jax.ShapeDtypeStruct((state.shape[0], 1, vw), bf16),
            jax.ShapeDtypeStruct(state.shape, f32),
        ]
        pipelined += 2 * rows * int(np.prod(state.shape[1:])) * F32_BYTES
    out = pl.pallas_call(
        functools.partial(_ffn_body, tf=tf, n_sub=n_sub, final_norm=final_norm, rider_gamma=gamma),
        grid=(steps, n_sub),
        in_specs=in_specs,
        out_specs=out_specs,
        out_shape=out_shape,
        scratch_shapes=[pltpu.VMEM((tm, d), bf16), pltpu.VMEM((tm, d), f32)],
        compiler_params=_params(("arbitrary", "arbitrary"), pipelined, resident),
        name="ffn_final" if final_norm else "ffn",
    )(*args)
    return out[0] if rider is None else out


def _in_odd_body(x_ref, gain_ref, w_ref, perm_ref, cos_ref, sin_ref, qsc_ref, ksc_ref, o_ref, h_ref, wqk_ref, *,
                 tn, split_halves):
    qw = RET_HEADS * RET_DK
    half = RET_DK // 2
    if split_halves:
        @pl.when(pl.program_id(0) == 0)
        def _():
            for hh in range(2 * RET_HEADS):
                hs = slice(hh * RET_DK, (hh + 1) * RET_DK)
                wqk_ref[:, hs] = jnp.dot(w_ref[:, hs], perm_ref[...], preferred_element_type=f32).astype(bf16)

    h_ref[...] = _rms(x_ref[...], gain_ref[...]).astype(bf16)
    cos = cos_ref[...]
    sin = sin_ref[...]
    for c in range(2 * qw // tn):
        c0 = c * tn
        w_chunk = wqk_ref[:, c0:c0 + tn] if split_halves else w_ref[:, c0:c0 + tn]
        p = jnp.dot(h_ref[...], w_chunk, preferred_element_type=f32)
        sc_ref = qsc_ref if c0 < qw else ksc_ref
        for hh in range(tn // RET_DK):
            h0 = hh * RET_DK
            head = (c0 % qw + h0) // RET_DK
            sc = sc_ref[:, head * LANES:(head + 1) * LANES]
            if split_halves:
                ev = p[:, h0:h0 + half]
                od = p[:, h0 + half:h0 + RET_DK]
                o_ref[:, c0 + h0:c0 + h0 + half] = ((ev * cos - od * sin) * sc).astype(bf16)
                o_ref[:, c0 + h0 + half:c0 + h0 + RET_DK] = ((od * cos + ev * sin) * sc).astype(bf16)
            else:
                xh = p[:, h0:h0 + RET_DK]
                even = lax.broadcasted_iota(jnp.int32, xh.shape, 1) % 2 == 0
                partner = jnp.where(even, pltpu.roll(xh, RET_DK - 1, 1), pltpu.roll(xh, 1, 1))
                r = xh * cos + partner * sin
                o_ref[:, c0 + h0:c0 + h0 + half] = (r[:, :half] * sc).astype(bf16)
                o_ref[:, c0 + h0 + half:c0 + h0 + RET_DK] = (r[:, half:] * sc).astype(bf16)
    for c0 in range(2 * qw, w_ref.shape[1], tn):
        p = jnp.dot(h_ref[...], w_ref[:, c0:c0 + tn], preferred_element_type=f32)
        o_ref[:, c0:c0 + tn] = p.astype(bf16)


def _in_odd(x, gain, w, perm, tables, *, layer, tm, tn, split_halves):
    m, d = x.shape
    n = w.shape[1]
    cos, sin, qsc, ksc = tables
    ntab = cos.shape[0] // tm
    qkw = 2 * RET_HEADS * RET_DK
    single_buffered = dict(pipeline_mode=pl.Buffered(1))
    const = lambda i: (0, 0)
    rope_spec = pl.BlockSpec((tm, cos.shape[1]), lambda i: (i % ntab, 0))
    wqk_shape = (d, qkw) if split_halves else (8, LANES)
    pipelined = tm * (d * F32_BYTES + n * BF16_BYTES + 2 * cos.shape[1] * F32_BYTES)
    resident = ((w.size + 2 * perm.size + tm * d + wqk_shape[0] * wqk_shape[1]) * BF16_BYTES
                + 2 * (qsc.size + ksc.size) * F32_BYTES + 2 * tm * tn * F32_BYTES)
    return pl.pallas_call(
        functools.partial(_in_odd_body, tn=tn, split_halves=split_halves),
        grid=(m // tm,),
        in_specs=[
            pl.BlockSpec((tm, d), lambda i: (i, 0)),
            pl.BlockSpec((None, 1, d), lambda i: (layer, 0, 0)),
            pl.BlockSpec(w.shape, const, **single_buffered),
            pl.BlockSpec(perm.shape, const),
            rope_spec, rope_spec,
            pl.BlockSpec(qsc.shape, const),
            pl.BlockSpec(ksc.shape, const),
        ],
        out_specs=pl.BlockSpec((tm, n), lambda i: (i, 0)),
        out_shape=jax.ShapeDtypeStruct((m, n), bf16),
        scratch_shapes=[pltpu.VMEM((tm, d), bf16), pltpu.VMEM(wqk_shape, bf16)],
        compiler_params=_params(("arbitrary",), pipelined, resident),
        name="in_odd",
    )(x, gain, w, perm, cos, sin, qsc, ksc)


def _ret_prompt_body(q_ref, k_ref, v_ref, g_ref, x_ref, wout_ref, xo_ref, so_ref, s_ref, sb_ref, slab_ref, *,
                     gamma_c, n):
    c = pl.program_id(1)
    subs = [slice(j * n, (j + 1) * n) for j in range(q_ref.shape[0] // n)]

    @pl.when(c == 0)
    def _():
        s_ref[...] = jnp.zeros_like(s_ref)
        sb_ref[...] = jnp.zeros_like(sb_ref)

    causal = lax.broadcasted_iota(jnp.int32, (n, n), 0) >= lax.broadcasted_iota(jnp.int32, (n, n), 1)
    heads = range(RET_HEADS)
    ks = [slice(h * RET_DK, (h + 1) * RET_DK) for h in heads]
    vs = [slice(h * RET_DV, (h + 1) * RET_DV) for h in heads]
    att = [[jnp.where(causal, lax.dot_general(q_ref[r, ks[h]], k_ref[r, ks[h]], NT_DIMS,
                                              preferred_element_type=f32), 0.0).astype(bf16) for h in heads]
           for r in subs]
    o = []
    for j, r in enumerate(subs):
        o.append([jnp.dot(q_ref[r, ks[h]], sb_ref[h], preferred_element_type=f32)
                  + jnp.dot(att[j][h], v_ref[r, vs[h]], preferred_element_type=f32) for h in heads])
        for h in heads:
            kv = lax.dot_general(k_ref[r, ks[h]], v_ref[r, vs[h]], TN_DIMS, preferred_element_type=f32)
            s_new = gamma_c[h] * (s_ref[h] + kv)
            s_ref[h] = s_new
            sb_ref[h] = s_new.astype(bf16)
    for j, r in enumerate(subs):
        y = x_ref[r, :]
        for h in heads:
            og = (_rms(o[j][h]) * _silu(g_ref[r, vs[h]].astype(f32))).astype(bf16)
            y = y + jnp.dot(og, wout_ref[vs[h], :], preferred_element_type=f32)
        xo_ref[r, :] = y

    @pl.when(c == pl.num_programs(1) - 1)
    def _():
        half = RET_DK // 2
        for h in range(RET_HEADS):
            for t in range(RET_DV // LANES):
                ls = slice(t * LANES, (t + 1) * LANES)
                slab_ref[pl.ds(0, half, stride=2), :] = s_ref[h, 0:half, ls]
                slab_ref[pl.ds(1, half, stride=2), :] = s_ref[h, half:RET_DK, ls]
                so_ref[0, h, :, ls] = slab_ref[...]


def _ret_prompt(qkvg, x, wout, gamma_c, *, batch, c, chunk):
    m, d = x.shape
    nc = m // batch // c
    qw = RET_HEADS * RET_DK
    vw = RET_HEADS * RET_DV
    assert c % chunk == 0
    state = RET_HEADS * RET_DK * RET_DV
    pipelined = c * (2 * (qw + vw) * BF16_BYTES + 2 * d * F32_BYTES) + state * F32_BYTES
    resident = (wout.size * BF16_BYTES + state * (F32_BYTES + BF16_BYTES) + RET_DK * LANES * F32_BYTES
                + c * RET_HEADS * (chunk * BF16_BYTES + RET_DV * F32_BYTES)
                + RET_DK * RET_DV * F32_BYTES + c * d * F32_BYTES)
    return pl.pallas_call(
        functools.partial(_ret_prompt_body, gamma_c=gamma_c, n=chunk),
        grid=(batch, nc),
        in_specs=[
            pl.BlockSpec((c, qw), lambda b, i: (b * nc + i, 0)),
            pl.BlockSpec((c, qw), lambda b, i: (b * nc + i, 1)),
            pl.BlockSpec((c, vw), lambda b, i: (b * nc + i, 1)),
            pl.BlockSpec((c, vw), lambda b, i: (b * nc + i, 2)),
            pl.BlockSpec((c, d), lambda b, i: (b * nc + i, 0)),
            pl.BlockSpec(wout.shape, lambda b, i: (0, 0), pipeline_mode=pl.Buffered(1)),
        ],
        out_specs=[
            pl.BlockSpec((c, d), lambda b, i: (b * nc + i, 0)),
            pl.BlockSpec((1, RET_HEADS, RET_DK, RET_DV), lambda b, i: (b, 0, 0, 0)),
        ],
        out_shape=[
            jax.ShapeDtypeStruct((m, d), f32),
            jax.ShapeDtypeStruct((batch, RET_HEADS, RET_DK, RET_DV), f32),
        ],
        scratch_shapes=[
            pltpu.VMEM((RET_HEADS, RET_DK, RET_DV), f32),
            pltpu.VMEM((RET_HEADS, RET_DK, RET_DV), bf16),
            pltpu.VMEM((RET_DK, LANES), f32),
        ],
        compiler_params=_params(("arbitrary", "arbitrary"), pipelined, resident),
        name="ret_prompt",
    )(qkvg, qkvg, qkvg, qkvg, x, wout)


def _rope_tables(pos, per_pair):
    pair_angle = 1.0 / (ROPE_BASE ** jnp.linspace(0.0, 1.0, RET_DK // 2, dtype=f32))
    if per_pair:
        ang = pos[:, None] * pair_angle[None, :]
        return jnp.cos(ang), jnp.sin(ang)
    ang = pos[:, None] * jnp.repeat(pair_angle, 2)[None, :]
    sign = jnp.where(jnp.arange(RET_DK) % 2 == 0, -1.0, 1.0).astype(f32)
    return jnp.cos(ang), jnp.sin(ang) * sign


def _even_odd_perm():
    half = RET_DK // 2
    src = np.concatenate([2 * np.arange(half), 2 * np.arange(half) + 1])
    perm = np.zeros((RET_DK, RET_DK), np.float32)
    perm[src, np.arange(RET_DK)] = 1.0
    return jnp.asarray(perm, dtype=bf16)


def _lane_replicated(scale):
    return jnp.asarray(np.repeat(scale, LANES, axis=1), dtype=f32)


def _ret_decay(rows, c):
    gam = 1.0 - 2.0 ** (-5.0 - np.arange(RET_HEADS, dtype=np.float64))
    lg = np.log(gam)
    steps = (np.arange(rows) % c + 1.0)[:, None]
    q_scale = _lane_replicated(np.exp(lg[None, :] * steps))
    k_scale = _lane_replicated(np.exp(-lg[None, :] * steps) * RET_DK ** -0.5)
    gamma_c = tuple(float(x) for x in np.exp(lg * c))
    gamma = tuple(float(x) for x in gam)
    return q_scale, k_scale, gamma_c, gamma


def kernel(x_prompt, x_sample, state_gla, state_pool, state_ret, norm_mix, norm_ffn, norm_final, w_in_even,
           w_gate_b, b_gate, gla_gain, pool_w, pool_scale, w_out_even, w_in_odd, w_out_odd, w_ffn_gate,
           w_ffn_up, w_ffn_down):
    batch, seq, d = x_prompt.shape
    n_s = x_sample.shape[0]
    assert norm_mix.shape[0] == 2 and x_sample.shape[1] == 1

    we = jnp.pad(w_in_even[0], ((0, 0), (0, -w_in_even.shape[2] % LANES))).astype(bf16)
    nu = POOL_GW * len(POOL_WINDOWS)
    shift = np.zeros((nu + LANES, nu), np.float32)
    shift[GATE_RANK + np.arange(nu), np.arange(nu)] = 1.0
    shift = jnp.asarray(shift, dtype=bf16)
    wgb = jnp.concatenate([w_gate_b[0], jnp.zeros((LANES - GATE_RANK, w_gate_b.shape[2]), f32)], axis=0).astype(bf16)
    bg = b_gate[0][None, :]
    gg = gla_gain[0][None, :]
    pw = pool_w[0].astype(bf16)
    ps = pool_scale[0][None, :]
    woe = w_out_even[0].astype(bf16)
    wio = w_in_odd[0].astype(bf16)
    woo = w_out_odd[0].astype(bf16)
    wg = w_ffn_gate.astype(bf16)
    wu = w_ffn_up.astype(bf16)
    wd = w_ffn_down.astype(bf16)
    nm = norm_mix[:, None, :]
    nf = norm_ffn[:, None, :]
    nfin = norm_final[None, :]
    tril = jnp.asarray(np.tril(np.ones((GLA_CHUNK, GLA_CHUNK), np.float32)), dtype=bf16)
    tf = 256
    tm_p = 512
    q_scale, k_scale, gamma_c, gamma = _ret_decay(tm_p, RET_CHUNK)
    tables_p = _rope_tables(jnp.arange(seq, dtype=f32), True) + (q_scale, k_scale)
    tables_s = _rope_tables(jnp.full((n_s,), float(PAST_LEN), f32), False) + (
        _lane_replicated(np.ones((n_s, RET_HEADS))), _lane_replicated(np.full((n_s, RET_HEADS), RET_DK ** -0.5)))
    perm = _even_odd_perm()

    xs = x_sample.reshape(n_s, d)
    qkvg_s, loga_s, u_s = _in_even(xs, nm[0], we, shift, wgb, bg, tm=n_s)
    op_s, gla_s = _gla_pool_sample(qkvg_s, loga_s, u_s, state_gla[0], state_pool[0], gg, pw, ps, bb=8)
    xs = _proj_res(xs, op_s, woe, tm=n_s)
    xs = _ffn(xs, nf, wg, wu, wd, nfin, layer=0, tm=n_s, tf=tf, final_norm=False)
    qkvg2_s = _in_odd(xs, nm, wio, perm, tables_s, layer=1, tm=n_s, tn=512, split_halves=False)
    qkvg2_s = qkvg2_s.reshape(n_s, 1, -1)

    rows = n_s // (2 * (batch * seq // tm_p))
    xp = x_prompt.reshape(batch * seq, d)
    qkvg, loga, u_p = _in_even(xp, nm[0], we, shift, wgb, bg, tm=tm_p)
    xp, gla_p = _gla_pool_prompt(qkvg, loga, u_p, xp, tril, gg, pw, ps, woe, batch=batch, t=512)
    xp, og_s, ret_s = _ffn(xp, nf, wg, wu, wd, nfin, layer=0, tm=tm_p, tf=tf, final_norm=False,
                           rider=(qkvg2_s, state_ret[0], rows, gamma))
    qkvg2 = _in_odd(xp, nm, wio, perm, tables_p, layer=1, tm=tm_p, tn=512, split_halves=True)
    xp, ret_p = _ret_prompt(qkvg2, xp, woo, gamma_c, batch=batch, c=2 * RET_CHUNK, chunk=RET_CHUNK)
    y_prompt = _ffn(xp, nf, wg, wu, wd, nfin, layer=1, tm=tm_p, tf=tf, final_norm=True)
    pool_p = u_p.reshape(batch, seq, -1)[:, seq - POOL_BUF:, :]

    xs = _proj_res(xs, og_s.reshape(n_s, -1), woo, tm=n_s)
    y_sample = _ffn(xs, nf, wg, wu, wd, nfin, layer=1, tm=n_s, tf=tf, final_norm=True)

    pool_s = jnp.concatenate([state_pool[0][:, 1:, :], u_s[:, None, :]], axis=1)

    return (y_prompt.reshape(batch, seq, d), y_sample.reshape(n_s, 1, d),
            gla_p[None], gla_s[None], pool_p[None], pool_s[None], ret_p[None], ret_s[None])
```

```python
import functools

import numpy as np
import jax
import jax.numpy as jnp
from jax import lax
from jax.experimental import pallas as pl
from jax.experimental.pallas import tpu as pltpu

f32 = jnp.float32
bf16 = jnp.bfloat16

EPS = 1e-6
PAST_LEN = 16384
GLA_HEADS, GLA_DK, GLA_DV = 4, 64, 128
GLA_CHUNK = 64
GATE_RANK = 16
GATE_NORMALIZER = 16.0
POOL_WINDOWS = (2, 4, 8, 16)
POOL_GW = 128
POOL_BUF = max(POOL_WINDOWS) - 1
POOL_HIST = 32
POOL_TAIL = 16
RET_HEADS, RET_DK, RET_DV = 4, 256, 512
RET_CHUNK = 256
ROPE_BASE = 10000.0
LANES = 128
MIB = 1024 * 1024
VMEM_COMPILER_ALLOWANCE = 8 * MIB
F32_BYTES, BF16_BYTES = 4, 2

NT_DIMS = (((1,), (1,)), ((), ()))
TN_DIMS = (((0,), (0,)), ((), ()))


def _params(semantics, pipelined_bytes, resident_bytes):
    limit = 2 * pipelined_bytes + resident_bytes + VMEM_COMPILER_ALLOWANCE
    return pltpu.CompilerParams(dimension_semantics=semantics, vmem_limit_bytes=int(limit))


def _rms(x, gain=None):
    y = x * lax.rsqrt(jnp.mean(x * x, axis=-1, keepdims=True) + EPS)
    return y if gain is None else y * gain


def _silu(g):
    return g * jax.nn.sigmoid(g)


def _in_even_body(x_ref, gain_ref, w_ref, shift_ref, wgb_ref, bg_ref, qkvg_ref, loga_ref, u_ref, h_ref, wu_ref, *,
                  tn):
    nq = qkvg_ref.shape[1]
    nu = u_ref.shape[1]

    @pl.when(pl.program_id(0) == 0)
    def _():
        wu_ref[...] = jnp.dot(w_ref[:, nq:], shift_ref[...], preferred_element_type=f32).astype(bf16)

    h_ref[...] = _rms(x_ref[...], gain_ref[...]).astype(bf16)
    a = jnp.dot(h_ref[...], w_ref[:, nq:nq + LANES], preferred_element_type=f32)
    a = jnp.where(lax.broadcasted_iota(jnp.int32, a.shape, 1) < GATE_RANK, a, 0.0).astype(bf16)
    for c0 in range(0, nq, tn):
        qkvg_ref[:, c0:c0 + tn] = jnp.dot(h_ref[...], w_ref[:, c0:c0 + tn], preferred_element_type=f32).astype(bf16)
        if c0 == 0:
            z = jnp.dot(a, wgb_ref[...], preferred_element_type=f32) + bg_ref[...]
            loga_ref[...] = (jnp.minimum(z, 0.0) - jnp.log1p(jnp.exp(-jnp.abs(z)))) * (1.0 / GATE_NORMALIZER)
    for c0 in range(0, nu, tn):
        u_ref[:, c0:c0 + tn] = jnp.dot(h_ref[...], wu_ref[:, c0:c0 + tn], preferred_element_type=f32)


def _in_even_casting_body(*refs, tn, n_cast):
    ins, outs = refs[:6 + n_cast], refs[6 + n_cast:]
    _in_even_body(*ins[:6], *outs[:3], *outs[3 + n_cast:], tn=tn)
    for src_ref, dst_ref in zip(ins[6:], outs[3:3 + n_cast]):
        dst_ref[...] = src_ref[...].astype(bf16)


def _in_even(x, gain, w, shift, wgb, bg, *, tm, cast=()):
    m, d = x.shape
    steps = m // tm
    nq = 2 * GLA_HEADS * GLA_DK + 2 * GLA_HEADS * GLA_DV
    nu = POOL_GW * len(POOL_WINDOWS)
    nk = GLA_HEADS * GLA_DK
    const = lambda i: (0, 0)
    tn = 512
    cast_blocks = [(c.shape[0] // steps, c.shape[1]) for c in cast]
    assert all(c.shape[0] == b[0] * steps and b[0] % 16 == 0 for c, b in zip(cast, cast_blocks))
    pipelined = tm * (d * F32_BYTES + nq * BF16_BYTES + nk * F32_BYTES + nu * F32_BYTES)
    pipelined += sum(r * c * (F32_BYTES + BF16_BYTES) for r, c in cast_blocks)
    resident = ((w.size + 2 * shift.size + 2 * wgb.size + tm * d + d * nu) * BF16_BYTES
                + 2 * tm * tn * F32_BYTES)
    body = functools.partial(_in_even_casting_body, tn=tn, n_cast=len(cast)) if cast else functools.partial(
        _in_even_body, tn=tn)
    return pl.pallas_call(
        body,
        grid=(steps,),
        in_specs=[
            pl.BlockSpec((tm, d), lambda i: (i, 0)),
            pl.BlockSpec((1, d), const),
            pl.BlockSpec(w.shape, const, pipeline_mode=pl.Buffered(1)),
            pl.BlockSpec(shift.shape, const),
            pl.BlockSpec(wgb.shape, const),
            pl.BlockSpec((1, nk), const),
        ] + [pl.BlockSpec(b, lambda i: (i, 0)) for b in cast_blocks],
        out_specs=[
            pl.BlockSpec((tm, nq), lambda i: (i, 0)),
            pl.BlockSpec((tm, nk), lambda i: (i, 0)),
            pl.BlockSpec((tm, nu), lambda i: (i, 0)),
        ] + [pl.BlockSpec(b, lambda i: (i, 0)) for b in cast_blocks],
        out_shape=[
            jax.ShapeDtypeStruct((m, nq), bf16),
            jax.ShapeDtypeStruct((m, nk), f32),
            jax.ShapeDtypeStruct((m, nu), f32),
        ] + [jax.ShapeDtypeStruct(c.shape, bf16) for c in cast],
        scratch_shapes=[pltpu.VMEM((tm, d), bf16), pltpu.VMEM((d, nu), bf16)],
        compiler_params=_params(("arbitrary",), pipelined, resident),
        name="in_even",
    )(x, gain, w, shift, wgb, bg, *cast)


def _gla_pool_prompt_body(qkvg_ref, loga_ref, u_ref, x_ref, tril_ref, gain_ref, pw_ref, ps_ref, wout_ref,
                          xo_ref, so_ref, st_ref, o_ref, e_ref, p_ref, q_ref, op_ref):
    t = x_ref.shape[0]
    ck = GLA_CHUNK
    kw = GLA_HEADS * GLA_DK
    vw = GLA_HEADS * GLA_DV
    pair_w = 2 * GLA_DK
    i = pl.program_id(1)

    @pl.when(i == 0)
    def _():
        st_ref[...] = jnp.zeros_like(st_ref)
        e_ref[0:POOL_HIST, :] = jnp.zeros((POOL_HIST, e_ref.shape[1]), f32)

    tril = tril_ref[...]
    row = lax.broadcasted_iota(jnp.int32, (2 * ck, pair_w), 0)
    lane = lax.broadcasted_iota(jnp.int32, (2 * ck, pair_w), 1)
    first_lanes = lane < GLA_DK
    first_lanes_ck = lax.broadcasted_iota(jnp.int32, (ck, pair_w), 1) < GLA_DK
    same_head = (row < ck) == first_lanes
    causal = same_head & ((row % ck) >= (lane % GLA_DK))
    pairs = range(GLA_HEADS // 2)
    chunks = range(t // ck)

    hist = POOL_HIST
    n = t + hist
    gw = POOL_GW
    u = u_ref[...]
    e_ref[hist:n, :] = u
    p_ref[8:n, :] = e_ref[8:n, :] + e_ref[7:n - 1, :]
    q_ref[16:n, gw:] = p_ref[16:n, gw:] + p_ref[14:n - 2, gw:]
    p_ref[24:n, 2 * gw:] = q_ref[24:n, 2 * gw:] + q_ref[20:n - 4, 2 * gw:]
    q_ref[32:n, 3 * gw:] = p_ref[32:n, 3 * gw:] + p_ref[24:n - 8, 3 * gw:]

    def rows_of(c):
        return slice(c * ck, (c + 1) * ck)

    def v_pair(c, p):
        va = qkvg_ref[rows_of(c), 2 * kw + (2 * p) * GLA_DV:2 * kw + (2 * p + 1) * GLA_DV]
        vb = qkvg_ref[rows_of(c), 2 * kw + (2 * p + 1) * GLA_DV:2 * kw + (2 * p + 2) * GLA_DV]
        return va, vb

    bcs = []
    for c in chunks:
        la = loga_ref[rows_of(c), :]
        la_hi = la.astype(bf16)
        la_lo = (la - la_hi.astype(f32)).astype(bf16)
        bcs.append(jnp.dot(tril, la_hi, preferred_element_type=f32) + jnp.dot(tril, la_lo, preferred_element_type=f32))
    lhs_q, ke2, kds, elast = [], [], [], []
    for c in chunks:
        bc = bcs[c]
        blast = bc[ck - 1:ck, :]
        q = qkvg_ref[rows_of(c), 0:kw].astype(f32) * (GLA_DK ** -0.5)
        k = qkvg_ref[rows_of(c), kw:2 * kw].astype(f32)
        qe = q * jnp.exp(bc)
        ke = (k * jnp.exp(-bc)).astype(bf16)
        kds.append((k * jnp.exp(blast - bc)).astype(bf16))
        elast.append(jnp.exp(blast))
        for p in pairs:
            pl_ = slice(p * pair_w, (p + 1) * pair_w)
            qe_p = qe[:, pl_]
            lhs_q.append(jnp.concatenate([jnp.where(first_lanes_ck, qe_p, 0.0),
                                          jnp.where(first_lanes_ck, 0.0, qe_p)], axis=0).astype(bf16))
            ke2.append(jnp.concatenate([ke[:, pl_], ke[:, pl_]], axis=0))
    att, upd = [], []
    for c in chunks:
        for p in pairs:
            idx = c * len(pairs) + p
            a = lax.dot_general(lhs_q[idx], ke2[idx], NT_DIMS, preferred_element_type=f32)
            att.append(jnp.where(causal, a, 0.0).astype(bf16))
            va, vb = v_pair(c, p)
            r = lax.dot_general(jnp.concatenate([va, vb], axis=1), kds[c][:, p * pair_w:(p + 1) * pair_w], TN_DIMS,
                                preferred_element_type=f32)
            upd.append(jnp.where(first_lanes, r[:GLA_DV], r[GLA_DV:]))
    st = [st_ref[p] for p in pairs]
    for c in chunks:
        for p in pairs:
            idx = c * len(pairs) + p
            va, vb = v_pair(c, p)
            o = lax.dot_general(lhs_q[idx], st[p].astype(bf16), NT_DIMS, preferred_element_type=f32)
            o = o + jnp.dot(att[idx], jnp.concatenate([va, vb], axis=0), preferred_element_type=f32)
            o_ref[rows_of(c), (2 * p) * GLA_DV:(2 * p + 1) * GLA_DV] = o[:ck]
            o_ref[rows_of(c), (2 * p + 1) * GLA_DV:(2 * p + 2) * GLA_DV] = o[ck:]
            st[p] = st[p] * elast[c][:, p * pair_w:(p + 1) * pair_w] + upd[idx]
    for p in pairs:
        st_ref[p] = st[p]

    sums = (p_ref, q_ref, p_ref, q_ref)
    pos = i * t + lax.broadcasted_iota(jnp.int32, (t, 1), 0)
    for gi, w in enumerate(POOL_WINDOWS):
        ls = slice(gi * gw, (gi + 1) * gw)
        cnt = jnp.minimum(w, pos + 1).astype(f32)
        pooled = (sums[gi][hist:n, ls] / cnt - u[:, ls]).astype(bf16)
        pg = jnp.dot(pooled, pw_ref[gi], preferred_element_type=f32) * ps_ref[:, ls]
        op_ref[:, vw + gi * gw:vw + (gi + 1) * gw] = pg.astype(bf16)
    e_ref[hist - POOL_TAIL:hist, :] = e_ref[n - POOL_TAIL:n, :]

    piece = 2 * GLA_DV
    y = x_ref[...]
    for c0 in (vw, vw + piece):
        y = y + jnp.dot(op_ref[:, c0:c0 + piece], wout_ref[c0:c0 + piece, :], preferred_element_type=f32)
    gain = gain_ref[...]
    for p in pairs:
        for h in (2 * p, 2 * p + 1):
            hs = slice(h * GLA_DV, (h + 1) * GLA_DV)
            g = qkvg_ref[:, 2 * kw + vw + h * GLA_DV:2 * kw + vw + (h + 1) * GLA_DV].astype(f32)
            op_ref[:, hs] = (_rms(o_ref[:, hs], gain) * _silu(g)).astype(bf16)
        c0 = p * piece
        y = y + jnp.dot(op_ref[:, c0:c0 + piece], wout_ref[c0:c0 + piece, :], preferred_element_type=f32)
    xo_ref[...] = y

    @pl.when(i == pl.num_programs(1) - 1)
    def _():
        for p in range(GLA_HEADS // 2):
            s_pair = st_ref[p].T
            so_ref[0, 2 * p] = s_pair[:GLA_DK]
            so_ref[0, 2 * p + 1] = s_pair[GLA_DK:]


def _gla_pool_prompt(qkvg, loga, u, x, tril, gain, pw, ps, wout, *, batch, t):
    m, d = x.shape
    nt = m // batch // t
    row = lambda b, i: (b * nt + i, 0)
    const2 = lambda b, i: (0, 0)
    vw = GLA_HEADS * GLA_DV
    uw = u.shape[1]
    pipelined = t * (qkvg.shape[1] * BF16_BYTES + (loga.shape[1] + uw + 2 * d) * F32_BYTES)
    scratch = (t * vw + 3 * (POOL_HIST + t) * uw) * F32_BYTES + t * (vw + uw) * BF16_BYTES
    waves = (t // GLA_CHUNK) * (GLA_HEADS // 2) * (3 * LANES * LANES * BF16_BYTES + LANES * LANES * F32_BYTES)
    resident = 2 * (wout.size + pw.size) * BF16_BYTES + scratch + waves + t * d * F32_BYTES
    return pl.pallas_call(
        _gla_pool_prompt_body,
        grid=(batch, nt),
        in_specs=[
            pl.BlockSpec((t, qkvg.shape[1]), row),
            pl.BlockSpec((t, loga.shape[1]), row),
            pl.BlockSpec((t, uw), row),
            pl.BlockSpec((t, d), row),
            pl.BlockSpec(tril.shape, const2),
            pl.BlockSpec(gain.shape, const2),
            pl.BlockSpec(pw.shape, lambda b, i: (0, 0, 0)),
            pl.BlockSpec(ps.shape, const2),
            pl.BlockSpec(wout.shape, const2),
        ],
        out_specs=[
            pl.BlockSpec((t, d), row),
            pl.BlockSpec((1, GLA_HEADS, GLA_DK, GLA_DV), lambda b, i: (b, 0, 0, 0)),
        ],
        out_shape=[
            jax.ShapeDtypeStruct((m, d), f32),
            jax.ShapeDtypeStruct((batch, GLA_HEADS, GLA_DK, GLA_DV), f32),
        ],
        scratch_shapes=[
            pltpu.VMEM((GLA_HEADS // 2, GLA_DV, 2 * GLA_DK), f32),
            pltpu.VMEM((t, vw), f32),
            pltpu.VMEM((POOL_HIST + t, uw), f32),
            pltpu.VMEM((POOL_HIST + t, uw), f32),
            pltpu.VMEM((POOL_HIST + t, uw), f32),
            pltpu.VMEM((t, vw + uw), bf16),
        ],
        compiler_params=_params(("arbitrary", "arbitrary"), pipelined, resident),
        name="gla_pool_prompt",
    )(qkvg, loga, u, x, tril, gain, pw, ps, wout)


def _gla_pool_sample_body(qkvg_ref, loga_ref, u_ref, s_ref, buf_ref, gain_ref, pw_ref, ps_ref,
                          op_ref, so_ref):
    bb = u_ref.shape[0]
    kw = GLA_HEADS * GLA_DK
    vw = GLA_HEADS * GLA_DV
    gain = gain_ref[...]
    qkvg = qkvg_ref[...].astype(f32)
    alpha = jnp.exp(loga_ref[...])
    qs = qkvg[:, 0:kw] * (GLA_DK ** -0.5)
    k = qkvg[:, kw:2 * kw]

    def column(row):
        return jnp.broadcast_to(row, (LANES, kw)).T

    o_rows = []
    for b in range(bb):
        acol = column(alpha[b:b + 1, :])
        qcol = column(qs[b:b + 1, :])
        kcol = column(k[b:b + 1, :])
        o_heads = []
        for h in range(GLA_HEADS):
            ks = slice(h * GLA_DK, (h + 1) * GLA_DK)
            v = qkvg[b:b + 1, 2 * kw + h * GLA_DV:2 * kw + (h + 1) * GLA_DV]
            s_new = acol[ks, :] * s_ref[b, h] + kcol[ks, :] * v
            so_ref[b, h] = s_new
            o = jnp.sum(qcol[ks, :] * s_new, axis=0, keepdims=True)
            g = qkvg[b:b + 1, 2 * kw + vw + h * GLA_DV:2 * kw + vw + (h + 1) * GLA_DV]
            o_heads.append(_rms(o, gain) * _silu(g))
        o_rows.append(jnp.concatenate(o_heads, axis=1))
    op_ref[:, 0:vw] = jnp.concatenate(o_rows, axis=0).astype(bf16)

    u = u_ref[...]
    for gi, w in enumerate(POOL_WINDOWS):
        ls = slice(gi * POOL_GW, (gi + 1) * POOL_GW)
        s = u[:, ls] + jnp.sum(buf_ref[:, POOL_BUF - (w - 1):POOL_BUF, ls], axis=1)
        cnt = float(min(w, PAST_LEN + 1))
        pooled = (s / cnt - u[:, ls]).astype(bf16)
        pg = jnp.dot(pooled, pw_ref[gi], preferred_element_type=f32) * ps_ref[:, ls]
        op_ref[:, vw + gi * POOL_GW:vw + (gi + 1) * POOL_GW] = pg.astype(bf16)


def _gla_pool_sample(qkvg, loga, u, s, buf, gain, pw, ps, *, bb):
    n = u.shape[0]
    row = lambda i: (i, 0)
    const2 = lambda i: (0, 0)
    ow = GLA_HEADS * GLA_DV + POOL_GW * len(POOL_WINDOWS)
    state_rows = int(np.prod(s.shape[1:]))
    pipelined = bb * ((qkvg.shape[1] + ow) * BF16_BYTES
                      + (loga.shape[1] + u.shape[1] + 2 * state_rows + POOL_TAIL * buf.shape[2]) * F32_BYTES)
    resident = 2 * pw.size * BF16_BYTES + 3 * LANES * loga.shape[1] * F32_BYTES
    return pl.pallas_call(
        _gla_pool_sample_body,
        grid=(n // bb,),
        in_specs=[
            pl.BlockSpec((bb, qkvg.shape[1]), row),
            pl.BlockSpec((bb, loga.shape[1]), row),
            pl.BlockSpec((bb, u.shape[1]), row),
            pl.BlockSpec((bb,) + s.shape[1:], lambda i: (i, 0, 0, 0)),
            pl.BlockSpec((bb,) + buf.shape[1:], lambda i: (i, 0, 0)),
            pl.BlockSpec(gain.shape, const2),
            pl.BlockSpec(pw.shape, lambda i: (0, 0, 0)),
            pl.BlockSpec(ps.shape, const2),
        ],
        out_specs=[
            pl.BlockSpec((bb, ow), row),
            pl.BlockSpec((bb,) + s.shape[1:], lambda i: (i, 0, 0, 0)),
        ],
        out_shape=[
            jax.ShapeDtypeStruct((n, ow), bf16),
            jax.ShapeDtypeStruct(s.shape, f32),
        ],
        compiler_params=_params(("arbitrary",), pipelined, resident),
        name="gla_pool_sample",
    )(qkvg, loga, u, s, buf, gain, pw, ps)


def _proj_res_body(x_ref, a_ref, w_ref, o_ref):
    o_ref[...] = x_ref[...] + jnp.dot(a_ref[...], w_ref[...], preferred_element_type=f32)


def _proj_res(x, a, w, *, tm):
    m, d = x.shape
    return pl.pallas_call(
        _proj_res_body,
        grid=(m // tm,),
        in_specs=[
            pl.BlockSpec((tm, d), lambda i: (i, 0)),
            pl.BlockSpec((tm, a.shape[1]), lambda i: (i, 0)),
            pl.BlockSpec(w.shape, lambda i: (0, 0)),
        ],
        out_specs=pl.BlockSpec((tm, d), lambda i: (i, 0)),
        out_shape=jax.ShapeDtypeStruct((m, d), f32),
        compiler_params=_params(("arbitrary",), tm * (2 * d * F32_BYTES + a.shape[1] * BF16_BYTES),
                                2 * w.size * BF16_BYTES),
        name="proj_res",
    )(x, a, w)


def _ret_token_pieces(q_ref, k_ref, v_ref, g_ref, s_ref, og_ref, so_ref, gamma):
    def piece(j, h):
        def run():
            ks = slice(h * RET_DK, (h + 1) * RET_DK)
            vs = slice(h * RET_DV, (h + 1) * RET_DV)
            qcol = jnp.broadcast_to(q_ref[j, :, ks].astype(f32), (LANES, RET_DK)).T
            kcol = jnp.broadcast_to(k_ref[j, :, ks].astype(f32), (LANES, RET_DK)).T
            v = v_ref[j, :, vs].astype(f32)
            g = g_ref[j, :, vs].astype(f32)
            o_tiles = []
            for t in range(RET_DV // LANES):
                cs = slice(t * LANES, (t + 1) * LANES)
                s_new = gamma[h] * s_ref[j, h, :, cs] + kcol * v[:, cs]
                so_ref[j, h, :, cs] = s_new
                o_tiles.append(jnp.sum(qcol * s_new, axis=0, keepdims=True))
            o = jnp.concatenate(o_tiles, axis=1)
            og_ref[j, :, vs] = (_rms(o) * _silu(g)).astype(bf16)
            return o
        return run

    return [piece(j, h) for j in range(s_ref.shape[0]) for h in range(RET_HEADS)]


def _ffn_body(*refs, tf, n_sub, final_norm, rider_gamma):
    x_ref, gain_ref, wg_ref, wu_ref, wd_ref, fgain_ref = refs[:6]
    pieces = []
    if rider_gamma is None:
        o_ref, h_ref, acc_ref = refs[6:]
    else:
        rq_ref, rk_ref, rv_ref, rg_ref, rs_ref, o_ref, rog_ref, rso_ref, h_ref, acc_ref = refs[6:]
        pieces = _ret_token_pieces(rq_ref, rk_ref, rv_ref, rg_ref, rs_ref, rog_ref, rso_ref, rider_gamma)
    n_chunks = wg_ref.shape[1] // tf
    bounds = [n_chunks * s // n_sub for s in range(n_sub + 1)]

    def exact_zero(v):
        bits = lax.bitcast_convert_type(v, jnp.uint32)
        return ((bits >> 16) >> 16).astype(f32)

    def run_chunks(chunks):
        pin = None
        for n, c in enumerate(chunks):
            cs = slice(c * tf, (c + 1) * tf)
            g = jnp.dot(h_ref[...], wg_ref[:, cs], preferred_element_type=f32)
            if pin is not None:
                g = g + pin
                pin = None
            u = jnp.dot(h_ref[...], wu_ref[:, cs], preferred_element_type=f32)
            a = (_silu(g) * u).astype(bf16)
            part = jnp.dot(a, wd_ref[cs, :], preferred_element_type=f32)
            if c == 0:
                acc_ref[...] = part
            else:
                acc_ref[...] += part
            for p in range(len(pieces)):
                if p * (len(chunks) - 1) // len(pieces) == n:
                    z = exact_zero(pieces[p]()[:, :tf])
                    pin = z if pin is None else pin + z

    def sub_step(s):
        if s == 0:
            h_ref[...] = _rms(x_ref[...], gain_ref[...]).astype(bf16)
        run_chunks(range(bounds[s], bounds[s + 1]))
        if s == n_sub - 1:
            y = x_ref[...] + acc_ref[...]
            if final_norm:
                y = _rms(y, fgain_ref[...])
            o_ref[...] = y

    if n_sub == 1:
        sub_step(0)
    else:
        for s in range(n_sub):
            pl.when(pl.program_id(1) == s)(functools.partial(sub_step, s))


def _ffn(x, gain, wg, wu, wd, fgain, *, layer, tm, tf, final_norm, rider=None):
    m, d = x.shape
    ff = wg.shape[2]
    steps = m // tm
    n_sub = 1 if rider is None else 2
    single_buffered = dict(pipeline_mode=pl.Buffered(1))
    in_specs = [
        pl.BlockSpec((tm, d), lambda i, s: (i, 0)),
        pl.BlockSpec((None, 1, d), lambda i, s: (layer, 0, 0)),
        pl.BlockSpec((None, d, ff), lambda i, s: (layer, 0, 0), **single_buffered),
        pl.BlockSpec((None, d, ff), lambda i, s: (layer, 0, 0), **single_buffered),
        pl.BlockSpec((None, ff, d), lambda i, s: (layer, 0, 0), **single_buffered),
        pl.BlockSpec((1, d), lambda i, s: (0, 0)),
    ]
    args = [x, gain, wg, wu, wd, fgain]
    out_specs = [pl.BlockSpec((tm, d), lambda i, s: (i, 0))]
    out_shape = [jax.ShapeDtypeStruct((m, d), f32)]
    gamma = None
    pipelined = 2 * tm * d * F32_BYTES
    resident = (3 * d * ff * BF16_BYTES + tm * d * (BF16_BYTES + F32_BYTES)
                + 3 * tm * tf * F32_BYTES + tm * d * F32_BYTES)
    if rider is not None:
        qkvg3, state, rows, gamma = rider
        assert 2 * steps * rows == state.shape[0]
        qw = RET_HEADS * RET_DK
        vw = RET_HEADS * RET_DV
        blk = lambda col: (lambda i, s: (2 * i + s, 0, col))
        state_spec = pl.BlockSpec((rows,) + state.shape[1:], lambda i, s: (2 * i + s, 0, 0, 0))
        in_specs += [
            pl.BlockSpec((rows, 1, qw), blk(0)),
            pl.BlockSpec((rows, 1, qw), blk(1)),
            pl.BlockSpec((rows, 1, vw), blk(1)),
            pl.BlockSpec((rows, 1, vw), blk(2)),
            state_spec,
        ]
        args += [qkvg3, qkvg3, qkvg3, qkvg3, state]
        out_specs += [pl.BlockSpec((rows, 1, vw), blk(0)), state_spec]
        out_shape += [
            jax.ShapeDtypeStruct((state.shape[0], 1, vw), bf16),
            jax.ShapeDtypeStruct(state.shape, f32),
        ]
        pipelined += 2 * rows * int(np.prod(state.shape[1:])) * F32_BYTES
    out = pl.pallas_call(
        functools.partial(_ffn_body, tf=tf, n_sub=n_sub, final_norm=final_norm, rider_gamma=gamma),
        grid=(steps, n_sub),
        in_specs=in_specs,
        out_specs=out_specs,
        out_shape=out_shape,
        scratch_shapes=[pltpu.VMEM((tm, d), bf16), pltpu.VMEM((tm, d), f32)],
        compiler_params=_params(("arbitrary", "arbitrary"), pipelined, resident),
        name="ffn_final" if final_norm else "ffn",
    )(*args)
    return out[0] if rider is None else out


def _in_odd_body(x_ref, gain_ref, w_ref, perm_ref, cos_ref, sin_ref, qsc_ref, ksc_ref, o_ref, h_ref, wqk_ref, *,
                 tn, split_halves):
    qw = RET_HEADS * RET_DK
    half = RET_DK // 2
    if split_halves:
        @pl.when(pl.program_id(0) == 0)
        def _():
            for hh in range(2 * RET_HEADS):
                hs = slice(hh * RET_DK, (hh + 1) * RET_DK)
                wqk_ref[:, hs] = jnp.dot(w_ref[:, hs], perm_ref[...], preferred_element_type=f32).astype(bf16)

    h_ref[...] = _rms(x_ref[...], gain_ref[...]).astype(bf16)
    cos = cos_ref[...]
    sin = sin_ref[...]
    for c in range(2 * qw // tn):
        c0 = c * tn
        w_chunk = wqk_ref[:, c0:c0 + tn] if split_halves else w_ref[:, c0:c0 + tn]
        p = jnp.dot(h_ref[...], w_chunk, preferred_element_type=f32)
        sc_ref = qsc_ref if c0 < qw else ksc_ref
        for hh in range(tn // RET_DK):
            h0 = hh * RET_DK
            head = (c0 % qw + h0) // RET_DK
            sc = sc_ref[:, head * LANES:(head + 1) * LANES]
            if split_halves:
                ev = p[:, h0:h0 + half]
                od = p[:, h0 + half:h0 + RET_DK]
                o_ref[:, c0 + h0:c0 + h0 + half] = ((ev * cos - od * sin) * sc).astype(bf16)
                o_ref[:, c0 + h0 + half:c0 + h0 + RET_DK] = ((od * cos + ev * sin) * sc).astype(bf16)
            else:
                xh = p[:, h0:h0 + RET_DK]
                even = lax.broadcasted_iota(jnp.int32, xh.shape, 1) % 2 == 0
                partner = jnp.where(even, pltpu.roll(xh, RET_DK - 1, 1), pltpu.roll(xh, 1, 1))
                r = xh * cos + partner * sin
                o_ref[:, c0 + h0:c0 + h0 + half] = (r[:, :half] * sc).astype(bf16)
                o_ref[:, c0 + h0 + half:c0 + h0 + RET_DK] = (r[:, half:] * sc).astype(bf16)
    for c0 in range(2 * qw, w_ref.shape[1], tn):
        p = jnp.dot(h_ref[...], w_ref[:, c0:c0 + tn], preferred_element_type=f32)
        o_ref[:, c0:c0 + tn] = p.astype(bf16)


def _in_odd(x, gain, w, perm, tables, *, layer, tm, tn, split_halves):
    m, d = x.shape
    n = w.shape[1]
    cos, sin, qsc, ksc = tables
    ntab = cos.shape[0] // tm
    qkw = 2 * RET_HEADS * RET_DK
    single_buffered = dict(pipeline_mode=pl.Buffered(1))
    const = lambda i: (0, 0)
    rope_spec = pl.BlockSpec((tm, cos.shape[1]), lambda i: (i % ntab, 0))
    wqk_shape = (d, qkw) if split_halves else (8, LANES)
    pipelined = tm * (d * F32_BYTES + n * BF16_BYTES + 2 * cos.shape[1] * F32_BYTES)
    resident = ((w.size + 2 * perm.size + tm * d + wqk_shape[0] * wqk_shape[1]) * BF16_BYTES
                + 2 * (qsc.size + ksc.size) * F32_BYTES + 2 * tm * tn * F32_BYTES)
    return pl.pallas_call(
        functools.partial(_in_odd_body, tn=tn, split_halves=split_halves),
        grid=(m // tm,),
        in_specs=[
            pl.BlockSpec((tm, d), lambda i: (i, 0)),
            pl.BlockSpec((None, 1, d), lambda i: (layer, 0, 0)),
            pl.BlockSpec(w.shape, const, **single_buffered),
            pl.BlockSpec(perm.shape, const),
            rope_spec, rope_spec,
            pl.BlockSpec(qsc.shape, const),
            pl.BlockSpec(ksc.shape, const),
        ],
        out_specs=pl.BlockSpec((tm, n), lambda i: (i, 0)),
        out_shape=jax.ShapeDtypeStruct((m, n), bf16),
        scratch_shapes=[pltpu.VMEM((tm, d), bf16), pltpu.VMEM(wqk_shape, bf16)],
        compiler_params=_params(("arbitrary",), pipelined, resident),
        name="in_odd",
    )(x, gain, w, perm, cos, sin, qsc, ksc)


def _ret_prompt_body(q_ref, k_ref, v_ref, g_ref, x_ref, wout_ref, xo_ref, so_ref, s_ref, sb_ref, slab_ref, *,
                     gamma_c, n):
    c = pl.program_id(1)
    subs = [slice(j * n, (j + 1) * n) for j in range(q_ref.shape[0] // n)]

    @pl.when(c == 0)
    def _():
        s_ref[...] = jnp.zeros_like(s_ref)
        sb_ref[...] = jnp.zeros_like(sb_ref)

    causal = lax.broadcasted_iota(jnp.int32, (n, n), 0) >= lax.broadcasted_iota(jnp.int32, (n, n), 1)
    heads = range(RET_HEADS)
    ks = [slice(h * RET_DK, (h + 1) * RET_DK) for h in heads]
    vs = [slice(h * RET_DV, (h + 1) * RET_DV) for h in heads]
    att = [[jnp.where(causal, lax.dot_general(q_ref[r, ks[h]], k_ref[r, ks[h]], NT_DIMS,
                                              preferred_element_type=f32), 0.0).astype(bf16) for h in heads]
           for r in subs]
    o = []
    for j, r in enumerate(subs):
        o.append([jnp.dot(q_ref[r, ks[h]], sb_ref[h], preferred_element_type=f32)
                  + jnp.dot(att[j][h], v_ref[r, vs[h]], preferred_element_type=f32) for h in heads])
        for h in heads:
            kv = lax.dot_general(k_ref[r, ks[h]], v_ref[r, vs[h]], TN_DIMS, preferred_element_type=f32)
            s_new = gamma_c[h] * (s_ref[h] + kv)
            s_ref[h] = s_new
            sb_ref[h] = s_new.astype(bf16)
    for j, r in enumerate(subs):
        y = x_ref[r, :]
        for h in heads:
            og = (_rms(o[j][h]) * _silu(g_ref[r, vs[h]].astype(f32))).astype(bf16)
            y = y + jnp.dot(og, wout_ref[vs[h], :], preferred_element_type=f32)
        xo_ref[r, :] = y

    @pl.when(c == pl.num_programs(1) - 1)
    def _():
        half = RET_DK // 2
        for h in range(RET_HEADS):
            for t in range(RET_DV // LANES):
                ls = slice(t * LANES, (t + 1) * LANES)
                slab_ref[pl.ds(0, half, stride=2), :] = s_ref[h, 0:half, ls]
                slab_ref[pl.ds(1, half, stride=2), :] = s_ref[h, half:RET_DK, ls]
                so_ref[0, h, :, ls] = slab_ref[...]


def _ret_prompt(qkvg, x, wout, gamma_c, *, batch, c, chunk):
    m, d = x.shape
    nc = m // batch // c
    qw = RET_HEADS * RET_DK
    vw = RET_HEADS * RET_DV
    assert c % chunk == 0
    state = RET_HEADS * RET_DK * RET_DV
    pipelined = c * (2 * (qw + vw) * BF16_BYTES + 2 * d * F32_BYTES) + state * F32_BYTES
    resident = (wout.size * BF16_BYTES + state * (F32_BYTES + BF16_BYTES) + RET_DK * LANES * F32_BYTES
                + c * RET_HEADS * (chunk * BF16_BYTES + RET_DV * F32_BYTES)
                + RET_DK * RET_DV * F32_BYTES + c * d * F32_BYTES)
    return pl.pallas_call(
        functools.partial(_ret_prompt_body, gamma_c=gamma_c, n=chunk),
        grid=(batch, nc),
        in_specs=[
            pl.BlockSpec((c, qw), lambda b, i: (b * nc + i, 0)),
            pl.BlockSpec((c, qw), lambda b, i: (b * nc + i, 1)),
            pl.BlockSpec((c, vw), lambda b, i: (b * nc + i, 1)),
            pl.BlockSpec((c, vw), lambda b, i: (b * nc + i, 2)),
            pl.BlockSpec((c, d), lambda b, i: (b * nc + i, 0)),
            pl.BlockSpec(wout.shape, lambda b, i: (0, 0), pipeline_mode=pl.Buffered(1)),
        ],
        out_specs=[
            pl.BlockSpec((c, d), lambda b, i: (b * nc + i, 0)),
            pl.BlockSpec((1, RET_HEADS, RET_DK, RET_DV), lambda b, i: (b, 0, 0, 0)),
        ],
        out_shape=[
            jax.ShapeDtypeStruct((m, d), f32),
            jax.ShapeDtypeStruct((batch, RET_HEADS, RET_DK, RET_DV), f32),
        ],
        scratch_shapes=[
            pltpu.VMEM((RET_HEADS, RET_DK, RET_DV), f32),
            pltpu.VMEM((RET_HEADS, RET_DK, RET_DV), bf16),
            pltpu.VMEM((RET_DK, LANES), f32),
        ],
        compiler_params=_params(("arbitrary", "arbitrary"), pipelined, resident),
        name="ret_prompt",
    )(qkvg, qkvg, qkvg, qkvg, x, wout)


def _rope_tables(pos, per_pair):
    pair_angle = 1.0 / (ROPE_BASE ** jnp.linspace(0.0, 1.0, RET_DK // 2, dtype=f32))
    if per_pair:
        ang = pos[:, None] * pair_angle[None, :]
        return jnp.cos(ang), jnp.sin(ang)
    ang = pos[:, None] * jnp.repeat(pair_angle, 2)[None, :]
    sign = jnp.where(jnp.arange(RET_DK) % 2 == 0, -1.0, 1.0).astype(f32)
    return jnp.cos(ang), jnp.sin(ang) * sign


def _even_odd_perm():
    half = RET_DK // 2
    src = np.concatenate([2 * np.arange(half), 2 * np.arange(half) + 1])
    perm = np.zeros((RET_DK, RET_DK), np.float32)
    perm[src, np.arange(RET_DK)] = 1.0
    return jnp.asarray(perm, dtype=bf16)


def _lane_replicated(scale):
    return jnp.asarray(np.repeat(scale, LANES, axis=1), dtype=f32)


def _ret_decay(rows, c):
    gam = 1.0 - 2.0 ** (-5.0 - np.arange(RET_HEADS, dtype=np.float64))
    lg = np.log(gam)
    steps = (np.arange(rows) % c + 1.0)[:, None]
    q_scale = _lane_replicated(np.exp(lg[None, :] * steps))
    k_scale = _lane_replicated(np.exp(-lg[None, :] * steps) * RET_DK ** -0.5)
    gamma_c = tuple(float(x) for x in np.exp(lg * c))
    gamma = tuple(float(x) for x in gam)
    return q_scale, k_scale, gamma_c, gamma


def kernel(x_prompt, x_sample, state_gla, state_pool, state_ret, norm_mix, norm_ffn, norm_final, w_in_even,
           w_gate_b, b_gate, gla_gain, pool_w, pool_scale, w_out_even, w_in_odd, w_out_odd, w_ffn_gate,
           w_ffn_up, w_ffn_down):
    batch, seq, d = x_prompt.shape
    n_s = x_sample.shape[0]
    assert norm_mix.shape[0] == 2 and x_sample.shape[1] == 1

    we = jnp.pad(w_in_even[0], ((0, 0), (0, -w_in_even.shape[2] % LANES))).astype(bf16)
    nu = POOL_GW * len(POOL_WINDOWS)
    shift = np.zeros((nu + LANES, nu), np.float32)
    shift[GATE_RANK + np.arange(nu), np.arange(nu)] = 1.0
    shift = jnp.asarray(shift, dtype=bf16)
    wgb = jnp.concatenate([w_gate_b[0], jnp.zeros((LANES - GATE_RANK, w_gate_b.shape[2]), f32)], axis=0).astype(bf16)
    bg = b_gate[0][None, :]
    gg = gla_gain[0][None, :]
    pw = pool_w[0].astype(bf16)
    ps = pool_scale[0][None, :]
    woe = w_out_even[0].astype(bf16)
    woo = w_out_odd[0].astype(bf16)
    nm = norm_mix[:, None, :]
    nf = norm_ffn[:, None, :]
    nfin = norm_final[None, :]
    tril = jnp.asarray(np.tril(np.ones((GLA_CHUNK, GLA_CHUNK), np.float32)), dtype=bf16)
    tf = 256
    tm_p = 512
    q_scale, k_scale, gamma_c, gamma = _ret_decay(tm_p, RET_CHUNK)
    tables_p = _rope_tables(jnp.arange(seq, dtype=f32), True) + (q_scale, k_scale)
    tables_s = _rope_tables(jnp.full((n_s,), float(PAST_LEN), f32), False) + (
        _lane_replicated(np.ones((n_s, RET_HEADS))), _lane_replicated(np.full((n_s, RET_HEADS), RET_DK ** -0.5)))
    perm = _even_odd_perm()

    ff = w_ffn_gate.shape[2]
    xp = x_prompt.reshape(batch * seq, d)
    qkvg, loga, u_p, wg, wu, wd, wio = _in_even(
        xp, nm[0], we, shift, wgb, bg, tm=tm_p,
        cast=(w_ffn_gate.reshape(-1, ff), w_ffn_up.reshape(-1, ff), w_ffn_down.reshape(-1, d), w_in_odd[0]))
    wg, wu, wd = wg.reshape(-1, d, ff), wu.reshape(-1, d, ff), wd.reshape(-1, ff, d)
    xp, gla_p = _gla_pool_prompt(qkvg, loga, u_p, xp, tril, gg, pw, ps, woe, batch=batch, t=512)

    xs = x_sample.reshape(n_s, d)
    qkvg_s, loga_s, u_s = _in_even(xs, nm[0], we, shift, wgb, bg, tm=n_s)
    op_s, gla_s = _gla_pool_sample(qkvg_s, loga_s, u_s, state_gla[0], state_pool[0], gg, pw, ps, bb=8)
    xs = _proj_res(xs, op_s, woe, tm=n_s)
    xs = _ffn(xs, nf, wg, wu, wd, nfin, layer=0, tm=n_s, tf=tf, final_norm=False)
    qkvg2_s = _in_odd(xs, nm, wio, perm, tables_s, layer=1, tm=n_s, tn=512, split_halves=False)
    qkvg2_s = qkvg2_s.reshape(n_s, 1, -1)

    rows = n_s // (2 * (batch * seq // tm_p))
    xp, og_s, ret_s = _ffn(xp, nf, wg, wu, wd, nfin, layer=0, tm=tm_p, tf=tf, final_norm=False,
                           rider=(qkvg2_s, state_ret[0], rows, gamma))
    qkvg2 = _in_odd(xp, nm, wio, perm, tables_p, layer=1, tm=tm_p, tn=512, split_halves=True)
    xp, ret_p = _ret_prompt(qkvg2, xp, woo, gamma_c, batch=batch, c=2 * RET_CHUNK, chunk=RET_CHUNK)
    y_prompt = _ffn(xp, nf, wg, wu, wd, nfin, layer=1, tm=tm_p, tf=tf, final_norm=True)
    pool_p = u_p.reshape(batch, seq, -1)[:, seq - POOL_BUF:, :]

    xs = _proj_res(xs, og_s.reshape(n_s, -1), woo, tm=n_s)
    y_sample = _ffn(xs, nf, wg, wu, wd, nfin, layer=1, tm=n_s, tf=tf, final_norm=True)

    pool_s = jnp.concatenate([state_pool[0][:, 1:, :], u_s[:, None, :]], axis=1)

    return (y_prompt.reshape(batch, seq, d), y_sample.reshape(n_s, 1, d),
            gla_p[None], gla_s[None], pool_p[None], pool_s[None], ret_p[None], ret_s[None])
```

```python
import functools

import numpy as np
import jax
import jax.numpy as jnp
from jax import lax
from jax.experimental import pallas as pl
from jax.experimental.pallas import tpu as pltpu

f32 = jnp.float32
bf16 = jnp.bfloat16

EPS = 1e-6
PAST_LEN = 16384
GLA_HEADS, GLA_DK, GLA_DV = 4, 64, 128
GLA_CHUNK = 64
GATE_RANK = 16
GATE_NORMALIZER = 16.0
POOL_WINDOWS = (2, 4, 8, 16)
POOL_GW = 128
POOL_BUF = max(POOL_WINDOWS) - 1
POOL_HIST = 32
POOL_TAIL = 16
RET_HEADS, RET_DK, RET_DV = 4, 256, 512
RET_CHUNK = 256
ROPE_BASE = 10000.0
LANES = 128
MIB = 1024 * 1024
VMEM_COMPILER_ALLOWANCE = 8 * MIB
F32_BYTES, BF16_BYTES = 4, 2

NT_DIMS = (((1,), (1,)), ((), ()))
TN_DIMS = (((0,), (0,)), ((), ()))


def _params(semantics, pipelined_bytes, resident_bytes):
    limit = 2 * pipelined_bytes + resident_bytes + VMEM_COMPILER_ALLOWANCE
    return pltpu.CompilerParams(dimension_semantics=semantics, vmem_limit_bytes=int(limit))


def _rms(x, gain=None):
    y = x * lax.rsqrt(jnp.mean(x * x, axis=-1, keepdims=True) + EPS)
    return y if gain is None else y * gain


def _silu(g):
    return g * jax.nn.sigmoid(g)


def _in_even_body(x_ref, gain_ref, w_ref, shift_ref, wgb_ref, bg_ref, qkvg_ref, loga_ref, u_ref, h_ref, wu_ref, *,
                  tn):
    nq = qkvg_ref.shape[1]
    nu = u_ref.shape[1]

    @pl.when(pl.program_id(0) == 0)
    def _():
        wu_ref[...] = jnp.dot(w_ref[:, nq:], shift_ref[...], preferred_element_type=f32).astype(bf16)

    h_ref[...] = _rms(x_ref[...], gain_ref[...]).astype(bf16)
    a = jnp.dot(h_ref[...], w_ref[:, nq:nq + LANES], preferred_element_type=f32)
    a = jnp.where(lax.broadcasted_iota(jnp.int32, a.shape, 1) < GATE_RANK, a, 0.0).astype(bf16)
    for c0 in range(0, nq, tn):
        qkvg_ref[:, c0:c0 + tn] = jnp.dot(h_ref[...], w_ref[:, c0:c0 + tn], preferred_element_type=f32).astype(bf16)
        if c0 == 0:
            z = jnp.dot(a, wgb_ref[...], preferred_element_type=f32) + bg_ref[...]
            loga_ref[...] = (jnp.minimum(z, 0.0) - jnp.log1p(jnp.exp(-jnp.abs(z)))) * (1.0 / GATE_NORMALIZER)
    for c0 in range(0, nu, tn):
        u_ref[:, c0:c0 + tn] = jnp.dot(h_ref[...], wu_ref[:, c0:c0 + tn], preferred_element_type=f32)


def _cast_rider(cast, steps):
    arrays, in_specs, out_specs, out_shapes, nbytes = [], [], [], [], 0
    for arr, rows, first, count in cast:
        assert count <= steps and rows % 16 == 0 and (first + count) * rows <= arr.shape[0]
        cols = arr.shape[1]
        arrays.append(arr)
        in_specs.append(pl.BlockSpec(
            (rows, cols), lambda i, first=first, count=count: (first + jnp.minimum(i, count - 1), 0)))
        out_specs.append(pl.BlockSpec((rows, cols), lambda i, count=count: (jnp.minimum(i, count - 1), 0)))
        out_shapes.append(jax.ShapeDtypeStruct((count * rows, cols), bf16))
        nbytes += rows * cols * (F32_BYTES + BF16_BYTES)
    return arrays, in_specs, out_specs, out_shapes, nbytes


def _with_cast_rider(body, n_in, n_out, n_cast):
    def wrapped(*refs):
        rest = refs[n_in + n_cast:]
        body(*refs[:n_in], *rest[:n_out], *rest[n_out + n_cast:])
        for src_ref, dst_ref in zip(refs[n_in:n_in + n_cast], rest[n_out:n_out + n_cast]):
            dst_ref[...] = src_ref[...].astype(bf16)
    return wrapped


def _in_even(x, gain, w, shift, wgb, bg, *, tm, cast=()):
    m, d = x.shape
    steps = m // tm
    nq = 2 * GLA_HEADS * GLA_DK + 2 * GLA_HEADS * GLA_DV
    nu = POOL_GW * len(POOL_WINDOWS)
    nk = GLA_HEADS * GLA_DK
    const = lambda i: (0, 0)
    tn = 512
    cast_args, cast_in, cast_out, cast_shapes, cast_bytes = _cast_rider(cast, steps)
    pipelined = tm * (d * F32_BYTES + nq * BF16_BYTES + nk * F32_BYTES + nu * F32_BYTES) + cast_bytes
    resident = ((w.size + 2 * shift.size + 2 * wgb.size + tm * d + d * nu) * BF16_BYTES
                + 2 * tm * tn * F32_BYTES)
    return pl.pallas_call(
        _with_cast_rider(functools.partial(_in_even_body, tn=tn), 6, 3, len(cast)),
        grid=(steps,),
        in_specs=[
            pl.BlockSpec((tm, d), lambda i: (i, 0)),
            pl.BlockSpec((1, d), const),
            pl.BlockSpec(w.shape, const, pipeline_mode=pl.Buffered(1)),
            pl.BlockSpec(shift.shape, const),
            pl.BlockSpec(wgb.shape, const),
            pl.BlockSpec((1, nk), const),
        ] + cast_in,
        out_specs=[
            pl.BlockSpec((tm, nq), lambda i: (i, 0)),
            pl.BlockSpec((tm, nk), lambda i: (i, 0)),
            pl.BlockSpec((tm, nu), lambda i: (i, 0)),
        ] + cast_out,
        out_shape=[
            jax.ShapeDtypeStruct((m, nq), bf16),
            jax.ShapeDtypeStruct((m, nk), f32),
            jax.ShapeDtypeStruct((m, nu), f32),
        ] + cast_shapes,
        scratch_shapes=[pltpu.VMEM((tm, d), bf16), pltpu.VMEM((d, nu), bf16)],
        compiler_params=_params(("arbitrary",), pipelined, resident),
        name="in_even",
    )(x, gain, w, shift, wgb, bg, *cast_args)


def _gla_pool_prompt_body(qkvg_ref, loga_ref, u_ref, x_ref, tril_ref, gain_ref, pw_ref, ps_ref, wout_ref,
                          xo_ref, so_ref, st_ref, o_ref, e_ref, p_ref, q_ref, op_ref):
    t = x_ref.shape[0]
    ck = GLA_CHUNK
    kw = GLA_HEADS * GLA_DK
    vw = GLA_HEADS * GLA_DV
    pair_w = 2 * GLA_DK
    i = pl.program_id(1)

    @pl.when(i == 0)
    def _():
        st_ref[...] = jnp.zeros_like(st_ref)
        e_ref[0:POOL_HIST, :] = jnp.zeros((POOL_HIST, e_ref.shape[1]), f32)

    tril = tril_ref[...]
    row = lax.broadcasted_iota(jnp.int32, (2 * ck, pair_w), 0)
    lane = lax.broadcasted_iota(jnp.int32, (2 * ck, pair_w), 1)
    first_lanes = lane < GLA_DK
    first_lanes_ck = lax.broadcasted_iota(jnp.int32, (ck, pair_w), 1) < GLA_DK
    same_head = (row < ck) == first_lanes
    causal = same_head & ((row % ck) >= (lane % GLA_DK))
    pairs = range(GLA_HEADS // 2)
    chunks = range(t // ck)

    hist = POOL_HIST
    n = t + hist
    gw = POOL_GW
    u = u_ref[...]
    e_ref[hist:n, :] = u
    p_ref[8:n, :] = e_ref[8:n, :] + e_ref[7:n - 1, :]
    q_ref[16:n, gw:] = p_ref[16:n, gw:] + p_ref[14:n - 2, gw:]
    p_ref[24:n, 2 * gw:] = q_ref[24:n, 2 * gw:] + q_ref[20:n - 4, 2 * gw:]
    q_ref[32:n, 3 * gw:] = p_ref[32:n, 3 * gw:] + p_ref[24:n - 8, 3 * gw:]

    def rows_of(c):
        return slice(c * ck, (c + 1) * ck)

    def v_pair(c, p):
        va = qkvg_ref[rows_of(c), 2 * kw + (2 * p) * GLA_DV:2 * kw + (2 * p + 1) * GLA_DV]
        vb = qkvg_ref[rows_of(c), 2 * kw + (2 * p + 1) * GLA_DV:2 * kw + (2 * p + 2) * GLA_DV]
        return va, vb

    bcs = []
    for c in chunks:
        la = loga_ref[rows_of(c), :]
        la_hi = la.astype(bf16)
        la_lo = (la - la_hi.astype(f32)).astype(bf16)
        bcs.append(jnp.dot(tril, la_hi, preferred_element_type=f32) + jnp.dot(tril, la_lo, preferred_element_type=f32))
    lhs_q, ke2, kds, elast = [], [], [], []
    for c in chunks:
        bc = bcs[c]
        blast = bc[ck - 1:ck, :]
        q = qkvg_ref[rows_of(c), 0:kw].astype(f32) * (GLA_DK ** -0.5)
        k = qkvg_ref[rows_of(c), kw:2 * kw].astype(f32)
        qe = q * jnp.exp(bc)
        ke = (k * jnp.exp(-bc)).astype(bf16)
        kds.append((k * jnp.exp(blast - bc)).astype(bf16))
        elast.append(jnp.exp(blast))
        for p in pairs:
            pl_ = slice(p * pair_w, (p + 1) * pair_w)
            qe_p = qe[:, pl_]
            lhs_q.append(jnp.concatenate([jnp.where(first_lanes_ck, qe_p, 0.0),
                                          jnp.where(first_lanes_ck, 0.0, qe_p)], axis=0).astype(bf16))
            ke2.append(jnp.concatenate([ke[:, pl_], ke[:, pl_]], axis=0))
    att, upd = [], []
    for c in chunks:
        for p in pairs:
            idx = c * len(pairs) + p
            a = lax.dot_general(lhs_q[idx], ke2[idx], NT_DIMS, preferred_element_type=f32)
            att.append(jnp.where(causal, a, 0.0).astype(bf16))
            va, vb = v_pair(c, p)
            r = lax.dot_general(jnp.concatenate([va, vb], axis=1), kds[c][:, p * pair_w:(p + 1) * pair_w], TN_DIMS,
                                preferred_element_type=f32)
            upd.append(jnp.where(first_lanes, r[:GLA_DV], r[GLA_DV:]))
    st = [st_ref[p] for p in pairs]
    for c in chunks:
        for p in pairs:
            idx = c * len(pairs) + p
            va, vb = v_pair(c, p)
            o = lax.dot_general(lhs_q[idx], st[p].astype(bf16), NT_DIMS, preferred_element_type=f32)
            o = o + jnp.dot(att[idx], jnp.concatenate([va, vb], axis=0), preferred_element_type=f32)
            o_ref[rows_of(c), (2 * p) * GLA_DV:(2 * p + 1) * GLA_DV] = o[:ck]
            o_ref[rows_of(c), (2 * p + 1) * GLA_DV:(2 * p + 2) * GLA_DV] = o[ck:]
            st[p] = st[p] * elast[c][:, p * pair_w:(p + 1) * pair_w] + upd[idx]
    for p in pairs:
        st_ref[p] = st[p]

    sums = (p_ref, q_ref, p_ref, q_ref)
    pos = i * t + lax.broadcasted_iota(jnp.int32, (t, 1), 0)
    for gi, w in enumerate(POOL_WINDOWS):
        ls = slice(gi * gw, (gi + 1) * gw)
        cnt = jnp.minimum(w, pos + 1).astype(f32)
        pooled = (sums[gi][hist:n, ls] / cnt - u[:, ls]).astype(bf16)
        pg = jnp.dot(pooled, pw_ref[gi], preferred_element_type=f32) * ps_ref[:, ls]
        op_ref[:, vw + gi * gw:vw + (gi + 1) * gw] = pg.astype(bf16)
    e_ref[hist - POOL_TAIL:hist, :] = e_ref[n - POOL_TAIL:n, :]

    piece = 2 * GLA_DV
    y = x_ref[...]
    for c0 in (vw, vw + piece):
        y = y + jnp.dot(op_ref[:, c0:c0 + piece], wout_ref[c0:c0 + piece, :], preferred_element_type=f32)
    gain = gain_ref[...]
    for p in pairs:
        for h in (2 * p, 2 * p + 1):
            hs = slice(h * GLA_DV, (h + 1) * GLA_DV)
            g = qkvg_ref[:, 2 * kw + vw + h * GLA_DV:2 * kw + vw + (h + 1) * GLA_DV].astype(f32)
            op_ref[:, hs] = (_rms(o_ref[:, hs], gain) * _silu(g)).astype(bf16)
        c0 = p * piece
        y = y + jnp.dot(op_ref[:, c0:c0 + piece], wout_ref[c0:c0 + piece, :], preferred_element_type=f32)
    xo_ref[...] = y

    @pl.when(i == pl.num_programs(1) - 1)
    def _():
        for p in range(GLA_HEADS // 2):
            s_pair = st_ref[p].T
            so_ref[0, 2 * p] = s_pair[:GLA_DK]
            so_ref[0, 2 * p + 1] = s_pair[GLA_DK:]


def _gla_pool_prompt(qkvg, loga, u, x, tril, gain, pw, ps, wout, *, batch, t):
    m, d = x.shape
    nt = m // batch // t
    row = lambda b, i: (b * nt + i, 0)
    const2 = lambda b, i: (0, 0)
    vw = GLA_HEADS * GLA_DV
    uw = u.shape[1]
    pipelined = t * (qkvg.shape[1] * BF16_BYTES + (loga.shape[1] + uw + 2 * d) * F32_BYTES)
    scratch = (t * vw + 3 * (POOL_HIST + t) * uw) * F32_BYTES + t * (vw + uw) * BF16_BYTES
    waves = (t // GLA_CHUNK) * (GLA_HEADS // 2) * (3 * LANES * LANES * BF16_BYTES + LANES * LANES * F32_BYTES)
    resident = 2 * (wout.size + pw.size) * BF16_BYTES + scratch + waves + t * d * F32_BYTES
    return pl.pallas_call(
        _gla_pool_prompt_body,
        grid=(batch, nt),
        in_specs=[
            pl.BlockSpec((t, qkvg.shape[1]), row),
            pl.BlockSpec((t, loga.shape[1]), row),
            pl.BlockSpec((t, uw), row),
            pl.BlockSpec((t, d), row),
            pl.BlockSpec(tril.shape, const2),
            pl.BlockSpec(gain.shape, const2),
            pl.BlockSpec(pw.shape, lambda b, i: (0, 0, 0)),
            pl.BlockSpec(ps.shape, const2),
            pl.BlockSpec(wout.shape, const2),
        ],
        out_specs=[
            pl.BlockSpec((t, d), row),
            pl.BlockSpec((1, GLA_HEADS, GLA_DK, GLA_DV), lambda b, i: (b, 0, 0, 0)),
        ],
        out_shape=[
            jax.ShapeDtypeStruct((m, d), f32),
            jax.ShapeDtypeStruct((batch, GLA_HEADS, GLA_DK, GLA_DV), f32),
        ],
        scratch_shapes=[
            pltpu.VMEM((GLA_HEADS // 2, GLA_DV, 2 * GLA_DK), f32),
            pltpu.VMEM((t, vw), f32),
            pltpu.VMEM((POOL_HIST + t, uw), f32),
            pltpu.VMEM((POOL_HIST + t, uw), f32),
            pltpu.VMEM((POOL_HIST + t, uw), f32),
            pltpu.VMEM((t, vw + uw), bf16),
        ],
        compiler_params=_params(("arbitrary", "arbitrary"), pipelined, resident),
        name="gla_pool_prompt",
    )(qkvg, loga, u, x, tril, gain, pw, ps, wout)


def _gla_pool_sample_body(qkvg_ref, loga_ref, u_ref, s_ref, buf_ref, gain_ref, pw_ref, ps_ref,
                          op_ref, so_ref):
    bb = u_ref.shape[0]
    kw = GLA_HEADS * GLA_DK
    vw = GLA_HEADS * GLA_DV
    gain = gain_ref[...]
    qkvg = qkvg_ref[...].astype(f32)
    alpha = jnp.exp(loga_ref[...])
    qs = qkvg[:, 0:kw] * (GLA_DK ** -0.5)
    k = qkvg[:, kw:2 * kw]

    def column(row):
        return jnp.broadcast_to(row, (LANES, kw)).T

    o_rows = []
    for b in range(bb):
        acol = column(alpha[b:b + 1, :])
        qcol = column(qs[b:b + 1, :])
        kcol = column(k[b:b + 1, :])
        o_heads = []
        for h in range(GLA_HEADS):
            ks = slice(h * GLA_DK, (h + 1) * GLA_DK)
            v = qkvg[b:b + 1, 2 * kw + h * GLA_DV:2 * kw + (h + 1) * GLA_DV]
            s_new = acol[ks, :] * s_ref[b, h] + kcol[ks, :] * v
            so_ref[b, h] = s_new
            o = jnp.sum(qcol[ks, :] * s_new, axis=0, keepdims=True)
            g = qkvg[b:b + 1, 2 * kw + vw + h * GLA_DV:2 * kw + vw + (h + 1) * GLA_DV]
            o_heads.append(_rms(o, gain) * _silu(g))
        o_rows.append(jnp.concatenate(o_heads, axis=1))
    op_ref[:, 0:vw] = jnp.concatenate(o_rows, axis=0).astype(bf16)

    u = u_ref[...]
    for gi, w in enumerate(POOL_WINDOWS):
        ls = slice(gi * POOL_GW, (gi + 1) * POOL_GW)
        s = u[:, ls] + jnp.sum(buf_ref[:, POOL_BUF - (w - 1):POOL_BUF, ls], axis=1)
        cnt = float(min(w, PAST_LEN + 1))
        pooled = (s / cnt - u[:, ls]).astype(bf16)
        pg = jnp.dot(pooled, pw_ref[gi], preferred_element_type=f32) * ps_ref[:, ls]
        op_ref[:, vw + gi * POOL_GW:vw + (gi + 1) * POOL_GW] = pg.astype(bf16)


def _gla_pool_sample(qkvg, loga, u, s, buf, gain, pw, ps, *, bb):
    n = u.shape[0]
    row = lambda i: (i, 0)
    const2 = lambda i: (0, 0)
    ow = GLA_HEADS * GLA_DV + POOL_GW * len(POOL_WINDOWS)
    state_rows = int(np.prod(s.shape[1:]))
    pipelined = bb * ((qkvg.shape[1] + ow) * BF16_BYTES
                      + (loga.shape[1] + u.shape[1] + 2 * state_rows + POOL_TAIL * buf.shape[2]) * F32_BYTES)
    resident = 2 * pw.size * BF16_BYTES + 3 * LANES * loga.shape[1] * F32_BYTES
    return pl.pallas_call(
        _gla_pool_sample_body,
        grid=(n // bb,),
        in_specs=[
            pl.BlockSpec((bb, qkvg.shape[1]), row),
            pl.BlockSpec((bb, loga.shape[1]), row),
            pl.BlockSpec((bb, u.shape[1]), row),
            pl.BlockSpec((bb,) + s.shape[1:], lambda i: (i, 0, 0, 0)),
            pl.BlockSpec((bb,) + buf.shape[1:], lambda i: (i, 0, 0)),
            pl.BlockSpec(gain.shape, const2),
            pl.BlockSpec(pw.shape, lambda i: (0, 0, 0)),
            pl.BlockSpec(ps.shape, const2),
        ],
        out_specs=[
            pl.BlockSpec((bb, ow), row),
            pl.BlockSpec((bb,) + s.shape[1:], lambda i: (i, 0, 0, 0)),
        ],
        out_shape=[
            jax.ShapeDtypeStruct((n, ow), bf16),
            jax.ShapeDtypeStruct(s.shape, f32),
        ],
        compiler_params=_params(("arbitrary",), pipelined, resident),
        name="gla_pool_sample",
    )(qkvg, loga, u, s, buf, gain, pw, ps)


def _proj_res_body(x_ref, a_ref, w_ref, o_ref):
    o_ref[...] = x_ref[...] + jnp.dot(a_ref[...], w_ref[...], preferred_element_type=f32)


def _proj_res(x, a, w, *, tm):
    m, d = x.shape
    return pl.pallas_call(
        _proj_res_body,
        grid=(m // tm,),
        in_specs=[
            pl.BlockSpec((tm, d), lambda i: (i, 0)),
            pl.BlockSpec((tm, a.shape[1]), lambda i: (i, 0)),
            pl.BlockSpec(w.shape, lambda i: (0, 0)),
        ],
        out_specs=pl.BlockSpec((tm, d), lambda i: (i, 0)),
        out_shape=jax.ShapeDtypeStruct((m, d), f32),
        compiler_params=_params(("arbitrary",), tm * (2 * d * F32_BYTES + a.shape[1] * BF16_BYTES),
                                2 * w.size * BF16_BYTES),
        name="proj_res",
    )(x, a, w)


def _ret_token_pieces(q_ref, k_ref, v_ref, g_ref, s_ref, og_ref, so_ref, gamma):
    def piece(j, h):
        def run():
            ks = slice(h * RET_DK, (h + 1) * RET_DK)
            vs = slice(h * RET_DV, (h + 1) * RET_DV)
            qcol = jnp.broadcast_to(q_ref[j, :, ks].astype(f32), (LANES, RET_DK)).T
            kcol = jnp.broadcast_to(k_ref[j, :, ks].astype(f32), (LANES, RET_DK)).T
            v = v_ref[j, :, vs].astype(f32)
            g = g_ref[j, :, vs].astype(f32)
            o_tiles = []
            for t in range(RET_DV // LANES):
                cs = slice(t * LANES, (t + 1) * LANES)
                s_new = gamma[h] * s_ref[j, h, :, cs] + kcol * v[:, cs]
                so_ref[j, h, :, cs] = s_new
                o_tiles.append(jnp.sum(qcol * s_new, axis=0, keepdims=True))
            o = jnp.concatenate(o_tiles, axis=1)
            og_ref[j, :, vs] = (_rms(o) * _silu(g)).astype(bf16)
            return o
        return run

    return [piece(j, h) for j in range(s_ref.shape[0]) for h in range(RET_HEADS)]


def _ffn_body(*refs, tf, n_sub, final_norm, rider_gamma):
    x_ref, gain_ref, wg_ref, wu_ref, wd_ref, fgain_ref = refs[:6]
    pieces = []
    if rider_gamma is None:
        o_ref, h_ref, acc_ref = refs[6:]
    else:
        rq_ref, rk_ref, rv_ref, rg_ref, rs_ref, o_ref, rog_ref, rso_ref, h_ref, acc_ref = refs[6:]
        pieces = _ret_token_pieces(rq_ref, rk_ref, rv_ref, rg_ref, rs_ref, rog_ref, rso_ref, rider_gamma)
    n_chunks = wg_ref.shape[1] // tf
    bounds = [n_chunks * s // n_sub for s in range(n_sub + 1)]

    def exact_zero(v):
        bits = lax.bitcast_convert_type(v, jnp.uint32)
        return ((bits >> 16) >> 16).astype(f32)

    def run_chunks(chunks):
        pin = None
        for n, c in enumerate(chunks):
            cs = slice(c * tf, (c + 1) * tf)
            g = jnp.dot(h_ref[...], wg_ref[:, cs], preferred_element_type=f32)
            if pin is not None:
                g = g + pin
                pin = None
            u = jnp.dot(h_ref[...], wu_ref[:, cs], preferred_element_type=f32)
            a = (_silu(g) * u).astype(bf16)
            part = jnp.dot(a, wd_ref[cs, :], preferred_element_type=f32)
            if c == 0:
                acc_ref[...] = part
            else:
                acc_ref[...] += part
            for p in range(len(pieces)):
                if p * (len(chunks) - 1) // len(pieces) == n:
                    z = exact_zero(pieces[p]()[:, :tf])
                    pin = z if pin is None else pin + z

    def sub_step(s):
        if s == 0:
            h_ref[...] = _rms(x_ref[...], gain_ref[...]).astype(bf16)
        run_chunks(range(bounds[s], bounds[s + 1]))
        if s == n_sub - 1:
            y = x_ref[...] + acc_ref[...]
            if final_norm:
                y = _rms(y, fgain_ref[...])
            o_ref[...] = y

    if n_sub == 1:
        sub_step(0)
    else:
        for s in range(n_sub):
            pl.when(pl.program_id(1) == s)(functools.partial(sub_step, s))


def _ffn(x, gain, wg, wu, wd, fgain, *, layer, tm, tf, final_norm, rider=None):
    m, d = x.shape
    ff = wg.shape[1]
    steps = m // tm
    n_sub = 1 if rider is None else 2
    single_buffered = dict(pipeline_mode=pl.Buffered(1))
    in_specs = [
        pl.BlockSpec((tm, d), lambda i, s: (i, 0)),
        pl.BlockSpec((None, 1, d), lambda i, s: (layer, 0, 0)),
        pl.BlockSpec((d, ff), lambda i, s: (0, 0), **single_buffered),
        pl.BlockSpec((d, ff), lambda i, s: (0, 0), **single_buffered),
        pl.BlockSpec((ff, d), lambda i, s: (0, 0), **single_buffered),
        pl.BlockSpec((1, d), lambda i, s: (0, 0)),
    ]
    args = [x, gain, wg, wu, wd, fgain]
    out_specs = [pl.BlockSpec((tm, d), lambda i, s: (i, 0))]
    out_shape = [jax.ShapeDtypeStruct((m, d), f32)]
    gamma = None
    pipelined = 2 * tm * d * F32_BYTES
    resident = (3 * d * ff * BF16_BYTES + tm * d * (BF16_BYTES + F32_BYTES)
                + 3 * tm * tf * F32_BYTES + tm * d * F32_BYTES)
    if rider is not None:
        qkvg3, state, rows, gamma = rider
        assert 2 * steps * rows == state.shape[0]
        qw = RET_HEADS * RET_DK
        vw = RET_HEADS * RET_DV
        blk = lambda col: (lambda i, s: (2 * i + s, 0, col))
        state_spec = pl.BlockSpec((rows,) + state.shape[1:], lambda i, s: (2 * i + s, 0, 0, 0))
        in_specs += [
            pl.BlockSpec((rows, 1, qw), blk(0)),
            pl.BlockSpec((rows, 1, qw), blk(1)),
            pl.BlockSpec((rows, 1, vw), blk(1)),
            pl.BlockSpec((rows, 1, vw), blk(2)),
            state_spec,
        ]
        args += [qkvg3, qkvg3, qkvg3, qkvg3, state]
        out_specs += [pl.BlockSpec((rows, 1, vw), blk(0)), state_spec]
        out_shape += [
            jax.ShapeDtypeStruct((state.shape[0], 1, vw), bf16),
            jax.ShapeDtypeStruct(state.shape, f32),
        ]
        pipelined += 2 * rows * int(np.prod(state.shape[1:])) * F32_BYTES
    out = pl.pallas_call(
        functools.partial(_ffn_body, tf=tf, n_sub=n_sub, final_norm=final_norm, rider_gamma=gamma),
        grid=(steps, n_sub),
        in_specs=in_specs,
        out_specs=out_specs,
        out_shape=out_shape,
        scratch_shapes=[pltpu.VMEM((tm, d), bf16), pltpu.VMEM((tm, d), f32)],
        compiler_params=_params(("arbitrary", "arbitrary"), pipelined, resident),
        name="ffn_final" if final_norm else "ffn",
    )(*args)
    return out[0] if rider is None else out


def _in_odd_body(x_ref, gain_ref, w_ref, perm_ref, cos_ref, sin_ref, qsc_ref, ksc_ref, o_ref, h_ref, wqk_ref, *,
                 tn, split_halves):
    qw = RET_HEADS * RET_DK
    half = RET_DK // 2
    if split_halves:
        @pl.when(pl.program_id(0) == 0)
        def _():
            for hh in range(2 * RET_HEADS):
                hs = slice(hh * RET_DK, (hh + 1) * RET_DK)
                wqk_ref[:, hs] = jnp.dot(w_ref[:, hs], perm_ref[...], preferred_element_type=f32).astype(bf16)

    h_ref[...] = _rms(x_ref[...], gain_ref[...]).astype(bf16)
    cos = cos_ref[...]
    sin = sin_ref[...]
    for c in range(2 * qw // tn):
        c0 = c * tn
        w_chunk = wqk_ref[:, c0:c0 + tn] if split_halves else w_ref[:, c0:c0 + tn]
        p = jnp.dot(h_ref[...], w_chunk, preferred_element_type=f32)
        sc_ref = qsc_ref if c0 < qw else ksc_ref
        for hh in range(tn // RET_DK):
            h0 = hh * RET_DK
            head = (c0 % qw + h0) // RET_DK
            sc = sc_ref[:, head * LANES:(head + 1) * LANES]
            if split_halves:
                ev = p[:, h0:h0 + half]
                od = p[:, h0 + half:h0 + RET_DK]
                o_ref[:, c0 + h0:c0 + h0 + half] = ((ev * cos - od * sin) * sc).astype(bf16)
                o_ref[:, c0 + h0 + half:c0 + h0 + RET_DK] = ((od * cos + ev * sin) * sc).astype(bf16)
            else:
                xh = p[:, h0:h0 + RET_DK]
                even = lax.broadcasted_iota(jnp.int32, xh.shape, 1) % 2 == 0
                partner = jnp.where(even, pltpu.roll(xh, RET_DK - 1, 1), pltpu.roll(xh, 1, 1))
                r = xh * cos + partner * sin
                o_ref[:, c0 + h0:c0 + h0 + half] = (r[:, :half] * sc).astype(bf16)
                o_ref[:, c0 + h0 + half:c0 + h0 + RET_DK] = (r[:, half:] * sc).astype(bf16)
    for c0 in range(2 * qw, w_ref.shape[1], tn):
        p = jnp.dot(h_ref[...], w_ref[:, c0:c0 + tn], preferred_element_type=f32)
        o_ref[:, c0:c0 + tn] = p.astype(bf16)


def _in_odd(x, gain, w, perm, tables, *, layer, tm, tn, split_halves, cast=()):
    m, d = x.shape
    n = w.shape[1]
    cos, sin, qsc, ksc = tables
    ntab = cos.shape[0] // tm
    qkw = 2 * RET_HEADS * RET_DK
    single_buffered = dict(pipeline_mode=pl.Buffered(1))
    const = lambda i: (0, 0)
    rope_spec = pl.BlockSpec((tm, cos.shape[1]), lambda i: (i % ntab, 0))
    wqk_shape = (d, qkw) if split_halves else (8, LANES)
    cast_args, cast_in, cast_out, cast_shapes, cast_bytes = _cast_rider(cast, m // tm)
    pipelined = tm * (d * F32_BYTES + n * BF16_BYTES + 2 * cos.shape[1] * F32_BYTES) + cast_bytes
    resident = ((w.size + 2 * perm.size + tm * d + wqk_shape[0] * wqk_shape[1]) * BF16_BYTES
                + 2 * (qsc.size + ksc.size) * F32_BYTES + 2 * tm * tn * F32_BYTES)
    outs = pl.pallas_call(
        _with_cast_rider(functools.partial(_in_odd_body, tn=tn, split_halves=split_halves), 8, 1, len(cast)),
        grid=(m // tm,),
        in_specs=[
            pl.BlockSpec((tm, d), lambda i: (i, 0)),
            pl.BlockSpec((None, 1, d), lambda i: (layer, 0, 0)),
            pl.BlockSpec(w.shape, const, **single_buffered),
            pl.BlockSpec(perm.shape, const),
            rope_spec, rope_spec,
            pl.BlockSpec(qsc.shape, const),
            pl.BlockSpec(ksc.shape, const),
        ] + cast_in,
        out_specs=[pl.BlockSpec((tm, n), lambda i: (i, 0))] + cast_out,
        out_shape=[jax.ShapeDtypeStruct((m, n), bf16)] + cast_shapes,
        scratch_shapes=[pltpu.VMEM((tm, d), bf16), pltpu.VMEM(wqk_shape, bf16)],
        compiler_params=_params(("arbitrary",), pipelined, resident),
        name="in_odd",
    )(x, gain, w, perm, cos, sin, qsc, ksc, *cast_args)
    return tuple(outs) if cast else outs[0]


def _ret_prompt_body(q_ref, k_ref, v_ref, g_ref, x_ref, wout_ref, xo_ref, so_ref, s_ref, sb_ref, slab_ref, *,
                     gamma_c, n):
    c = pl.program_id(1)
    subs = [slice(j * n, (j + 1) * n) for j in range(q_ref.shape[0] // n)]

    @pl.when(c == 0)
    def _():
        s_ref[...] = jnp.zeros_like(s_ref)
        sb_ref[...] = jnp.zeros_like(sb_ref)

    causal = lax.broadcasted_iota(jnp.int32, (n, n), 0) >= lax.broadcasted_iota(jnp.int32, (n, n), 1)
    heads = range(RET_HEADS)
    ks = [slice(h * RET_DK, (h + 1) * RET_DK) for h in heads]
    vs = [slice(h * RET_DV, (h + 1) * RET_DV) for h in heads]
    att = [[jnp.where(causal, lax.dot_general(q_ref[r, ks[h]], k_ref[r, ks[h]], NT_DIMS,
                                              preferred_element_type=f32), 0.0).astype(bf16) for h in heads]
           for r in subs]
    o = []
    for j, r in enumerate(subs):
        o.append([jnp.dot(q_ref[r, ks[h]], sb_ref[h], preferred_element_type=f32)
                  + jnp.dot(att[j][h], v_ref[r, vs[h]], preferred_element_type=f32) for h in heads])
        for h in heads:
            kv = lax.dot_general(k_ref[r, ks[h]], v_ref[r, vs[h]], TN_DIMS, preferred_element_type=f32)
            s_new = gamma_c[h] * (s_ref[h] + kv)
            s_ref[h] = s_new
            sb_ref[h] = s_new.astype(bf16)
    for j, r in enumerate(subs):
        y = x_ref[r, :]
        for h in heads:
            og = (_rms(o[j][h]) * _silu(g_ref[r, vs[h]].astype(f32))).astype(bf16)
            y = y + jnp.dot(og, wout_ref[vs[h], :], preferred_element_type=f32)
        xo_ref[r, :] = y

    @pl.when(c == pl.num_programs(1) - 1)
    def _():
        half = RET_DK // 2
        for h in range(RET_HEADS):
            for t in range(RET_DV // LANES):
                ls = slice(t * LANES, (t + 1) * LANES)
                slab_ref[pl.ds(0, half, stride=2), :] = s_ref[h, 0:half, ls]
                slab_ref[pl.ds(1, half, stride=2), :] = s_ref[h, half:RET_DK, ls]
                so_ref[0, h, :, ls] = slab_ref[...]


def _ret_prompt(qkvg, x, wout, gamma_c, *, batch, c, chunk):
    m, d = x.shape
    nc = m // batch // c
    qw = RET_HEADS * RET_DK
    vw = RET_HEADS * RET_DV
    assert c % chunk == 0
    state = RET_HEADS * RET_DK * RET_DV
    pipelined = c * (2 * (qw + vw) * BF16_BYTES + 2 * d * F32_BYTES) + state * F32_BYTES
    resident = (wout.size * BF16_BYTES + state * (F32_BYTES + BF16_BYTES) + RET_DK * LANES * F32_BYTES
                + c * RET_HEADS * (chunk * BF16_BYTES + RET_DV * F32_BYTES)
                + RET_DK * RET_DV * F32_BYTES + c * d * F32_BYTES)
    return pl.pallas_call(
        functools.partial(_ret_prompt_body, gamma_c=gamma_c, n=chunk),
        grid=(batch, nc),
        in_specs=[
            pl.BlockSpec((c, qw), lambda b, i: (b * nc + i, 0)),
            pl.BlockSpec((c, qw), lambda b, i: (b * nc + i, 1)),
            pl.BlockSpec((c, vw), lambda b, i: (b * nc + i, 1)),
            pl.BlockSpec((c, vw), lambda b, i: (b * nc + i, 2)),
            pl.BlockSpec((c, d), lambda b, i: (b * nc + i, 0)),
            pl.BlockSpec(wout.shape, lambda b, i: (0, 0), pipeline_mode=pl.Buffered(1)),
        ],
        out_specs=[
            pl.BlockSpec((c, d), lambda b, i: (b * nc + i, 0)),
            pl.BlockSpec((1, RET_HEADS, RET_DK, RET_DV), lambda b, i: (b, 0, 0, 0)),
        ],
        out_shape=[
            jax.ShapeDtypeStruct((m, d), f32),
            jax.ShapeDtypeStruct((batch, RET_HEADS, RET_DK, RET_DV), f32),
        ],
        scratch_shapes=[
            pltpu.VMEM((RET_HEADS, RET_DK, RET_DV), f32),
            pltpu.VMEM((RET_HEADS, RET_DK, RET_DV), bf16),
            pltpu.VMEM((RET_DK, LANES), f32),
        ],
        compiler_params=_params(("arbitrary", "arbitrary"), pipelined, resident),
        name="ret_prompt",
    )(qkvg, qkvg, qkvg, qkvg, x, wout)


def _rope_tables(pos, per_pair):
    pair_angle = 1.0 / (ROPE_BASE ** jnp.linspace(0.0, 1.0, RET_DK // 2, dtype=f32))
    if per_pair:
        ang = pos[:, None] * pair_angle[None, :]
        return jnp.cos(ang), jnp.sin(ang)
    ang = pos[:, None] * jnp.repeat(pair_angle, 2)[None, :]
    sign = jnp.where(jnp.arange(RET_DK) % 2 == 0, -1.0, 1.0).astype(f32)
    return jnp.cos(ang), jnp.sin(ang) * sign


def _even_odd_perm():
    half = RET_DK // 2
    src = np.concatenate([2 * np.arange(half), 2 * np.arange(half) + 1])
    perm = np.zeros((RET_DK, RET_DK), np.float32)
    perm[src, np.arange(RET_DK)] = 1.0
    return jnp.asarray(perm, dtype=bf16)


def _lane_replicated(scale):
    return jnp.asarray(np.repeat(scale, LANES, axis=1), dtype=f32)


def _ret_decay(rows, c):
    gam = 1.0 - 2.0 ** (-5.0 - np.arange(RET_HEADS, dtype=np.float64))
    lg = np.log(gam)
    steps = (np.arange(rows) % c + 1.0)[:, None]
    q_scale = _lane_replicated(np.exp(lg[None, :] * steps))
    k_scale = _lane_replicated(np.exp(-lg[None, :] * steps) * RET_DK ** -0.5)
    gamma_c = tuple(float(x) for x in np.exp(lg * c))
    gamma = tuple(float(x) for x in gam)
    return q_scale, k_scale, gamma_c, gamma


def kernel(x_prompt, x_sample, state_gla, state_pool, state_ret, norm_mix, norm_ffn, norm_final, w_in_even,
           w_gate_b, b_gate, gla_gain, pool_w, pool_scale, w_out_even, w_in_odd, w_out_odd, w_ffn_gate,
           w_ffn_up, w_ffn_down):
    batch, seq, d = x_prompt.shape
    n_s = x_sample.shape[0]
    assert norm_mix.shape[0] == 2 and x_sample.shape[1] == 1

    we = jnp.pad(w_in_even[0], ((0, 0), (0, -w_in_even.shape[2] % LANES))).astype(bf16)
    nu = POOL_GW * len(POOL_WINDOWS)
    shift = np.zeros((nu + LANES, nu), np.float32)
    shift[GATE_RANK + np.arange(nu), np.arange(nu)] = 1.0
    shift = jnp.asarray(shift, dtype=bf16)
    wgb = jnp.concatenate([w_gate_b[0], jnp.zeros((LANES - GATE_RANK, w_gate_b.shape[2]), f32)], axis=0).astype(bf16)
    bg = b_gate[0][None, :]
    gg = gla_gain[0][None, :]
    pw = pool_w[0].astype(bf16)
    ps = pool_scale[0][None, :]
    nm = norm_mix[:, None, :]
    nf = norm_ffn[:, None, :]
    nfin = norm_final[None, :]
    tril = jnp.asarray(np.tril(np.ones((GLA_CHUNK, GLA_CHUNK), np.float32)), dtype=bf16)
    tf = 256
    tm_p = 512
    q_scale, k_scale, gamma_c, gamma = _ret_decay(tm_p, RET_CHUNK)
    tables_p = _rope_tables(jnp.arange(seq, dtype=f32), True) + (q_scale, k_scale)
    tables_s = _rope_tables(jnp.full((n_s,), float(PAST_LEN), f32), False) + (
        _lane_replicated(np.ones((n_s, RET_HEADS))), _lane_replicated(np.full((n_s, RET_HEADS), RET_DK ** -0.5)))
    perm = _even_odd_perm()

    ff = w_ffn_gate.shape[2]
    xp = x_prompt.reshape(batch * seq, d)
    steps_p = batch * seq // tm_p
    ffn_cast = lambda layer: (
        (w_ffn_gate.reshape(-1, ff), d // steps_p, layer * steps_p, steps_p),
        (w_ffn_up.reshape(-1, ff), d // steps_p, layer * steps_p, steps_p),
        (w_ffn_down.reshape(-1, d), 2 * ff // steps_p, layer * steps_p // 2, steps_p // 2))
    whole = lambda w: (w, w.shape[0] // steps_p, 0, steps_p)
    qkvg, loga, u_p, wg0, wu0, wd0, wio, woe, woo = _in_even(
        xp, nm[0], we, shift, wgb, bg, tm=tm_p,
        cast=ffn_cast(0) + (whole(w_in_odd[0]), whole(w_out_even[0]), whole(w_out_odd[0])))
    xp, gla_p = _gla_pool_prompt(qkvg, loga, u_p, xp, tril, gg, pw, ps, woe, batch=batch, t=512)

    xs = x_sample.reshape(n_s, d)
    qkvg_s, loga_s, u_s = _in_even(xs, nm[0], we, shift, wgb, bg, tm=n_s)
    op_s, gla_s = _gla_pool_sample(qkvg_s, loga_s, u_s, state_gla[0], state_pool[0], gg, pw, ps, bb=8)
    xs = _proj_res(xs, op_s, woe, tm=n_s)
    xs = _ffn(xs, nf, wg0, wu0, wd0, nfin, layer=0, tm=n_s, tf=tf, final_norm=False)
    qkvg2_s = _in_odd(xs, nm, wio, perm, tables_s, layer=1, tm=n_s, tn=512, split_halves=False)
    qkvg2_s = qkvg2_s.reshape(n_s, 1, -1)

    rows = n_s // (2 * steps_p)
    xp, og_s, ret_s = _ffn(xp, nf, wg0, wu0, wd0, nfin, layer=0, tm=tm_p, tf=tf, final_norm=False,
                           rider=(qkvg2_s, state_ret[0], rows, gamma))
    qkvg2, wg1, wu1, wd1 = _in_odd(xp, nm, wio, perm, tables_p, layer=1, tm=tm_p, tn=512, split_halves=True,
                                   cast=ffn_cast(1))
    xp, ret_p = _ret_prompt(qkvg2, xp, woo, gamma_c, batch=batch, c=2 * RET_CHUNK, chunk=RET_CHUNK)
    y_prompt = _ffn(xp, nf, wg1, wu1, wd1, nfin, layer=1, tm=tm_p, tf=tf, final_norm=True)
    pool_p = u_p.reshape(batch, seq, -1)[:, seq - POOL_BUF:, :]

    xs = _proj_res(xs, og_s.reshape(n_s, -1), woo, tm=n_s)
    y_sample = _ffn(xs, nf, wg1, wu1, wd1, nfin, layer=1, tm=n_s, tf=tf, final_norm=True)

    pool_s = jnp.concatenate([state_pool[0][:, 1:, :], u_s[:, None, :]], axis=1)

    return (y_prompt.reshape(batch, seq, d), y_sample.reshape(n_s, 1, d),
            gla_p[None], gla_s[None], pool_p[None], pool_s[None], ret_p[None], ret_s[None])
```

```python
import functools

import numpy as np
import jax
import jax.numpy as jnp
from jax import lax
from jax.experimental import pallas as pl
from jax.experimental.pallas import tpu as pltpu

f32 = jnp.float32
bf16 = jnp.bfloat16

EPS = 1e-6
PAST_LEN = 16384
GLA_HEADS, GLA_DK, GLA_DV = 4, 64, 128
GLA_CHUNK = 64
GATE_RANK = 16
GATE_NORMALIZER = 16.0
POOL_WINDOWS = (2, 4, 8, 16)
POOL_GW = 128
POOL_BUF = max(POOL_WINDOWS) - 1
POOL_HIST = 32
POOL_TAIL = 16
RET_HEADS, RET_DK, RET_DV = 4, 256, 512
RET_CHUNK = 256
ROPE_BASE = 10000.0
LANES = 128
MIB = 1024 * 1024
VMEM_BYTES = 64 * MIB
VMEM_SMALL_OPERAND_ROOM = 4 * MIB
VMEM_COMPILER_ALLOWANCE = 8 * MIB
F32_BYTES, BF16_BYTES = 4, 2

NT_DIMS = (((1,), (1,)), ((), ()))
TN_DIMS = (((0,), (0,)), ((), ()))


def _params(semantics, pipelined_bytes, resident_bytes, exclusive=False):
    limit = 2 * pipelined_bytes + resident_bytes + VMEM_COMPILER_ALLOWANCE
    assert limit <= VMEM_BYTES - VMEM_SMALL_OPERAND_ROOM
    if exclusive:
        limit = VMEM_BYTES - VMEM_SMALL_OPERAND_ROOM
    return pltpu.CompilerParams(dimension_semantics=semantics, vmem_limit_bytes=int(limit))


def _rms(x, gain=None):
    y = x * lax.rsqrt(jnp.mean(x * x, axis=-1, keepdims=True) + EPS)
    return y if gain is None else y * gain


def _silu(g):
    return g * jax.nn.sigmoid(g)


def _in_even_body(x_ref, gain_ref, w_ref, shift_ref, wgb_ref, bg_ref, qkvg_ref, loga_ref, u_ref, h_ref, wu_ref, *,
                  tn):
    nq = qkvg_ref.shape[1]
    nu = u_ref.shape[1]

    @pl.when(pl.program_id(0) == 0)
    def _():
        wu_ref[...] = jnp.dot(w_ref[:, nq:], shift_ref[...], preferred_element_type=f32).astype(bf16)

    h_ref[...] = _rms(x_ref[...], gain_ref[...]).astype(bf16)
    a = jnp.dot(h_ref[...], w_ref[:, nq:nq + LANES], preferred_element_type=f32)
    a = jnp.where(lax.broadcasted_iota(jnp.int32, a.shape, 1) < GATE_RANK, a, 0.0).astype(bf16)
    for c0 in range(0, nq, tn):
        qkvg_ref[:, c0:c0 + tn] = jnp.dot(h_ref[...], w_ref[:, c0:c0 + tn], preferred_element_type=f32).astype(bf16)
        if c0 == 0:
            z = jnp.dot(a, wgb_ref[...], preferred_element_type=f32) + bg_ref[...]
            loga_ref[...] = (jnp.minimum(z, 0.0) - jnp.log1p(jnp.exp(-jnp.abs(z)))) * (1.0 / GATE_NORMALIZER)
    for c0 in range(0, nu, tn):
        u_ref[:, c0:c0 + tn] = jnp.dot(h_ref[...], wu_ref[:, c0:c0 + tn], preferred_element_type=f32)


def _cast_rider(cast, steps):
    arrays, in_specs, out_specs, out_shapes, nbytes = [], [], [], [], 0
    for arr, rows, first, count in cast:
        assert count <= steps and rows % 16 == 0 and (first + count) * rows <= arr.shape[0]
        cols = arr.shape[1]
        arrays.append(arr)
        in_specs.append(pl.BlockSpec(
            (rows, cols), lambda i, first=first, count=count: (first + jnp.minimum(i, count - 1), 0)))
        out_specs.append(pl.BlockSpec((rows, cols), lambda i, count=count: (jnp.minimum(i, count - 1), 0)))
        out_shapes.append(jax.ShapeDtypeStruct((count * rows, cols), bf16))
        nbytes += rows * cols * (F32_BYTES + BF16_BYTES)
    return arrays, in_specs, out_specs, out_shapes, nbytes


def _with_cast_rider(body, n_in, n_out, n_cast):
    def wrapped(*refs):
        rest = refs[n_in + n_cast:]
        body(*refs[:n_in], *rest[:n_out], *rest[n_out + n_cast:])
        for src_ref, dst_ref in zip(refs[n_in:n_in + n_cast], rest[n_out:n_out + n_cast]):
            dst_ref[...] = src_ref[...].astype(bf16)
    return wrapped


def _in_even(x, gain, w, shift, wgb, bg, *, tm, cast=()):
    m, d = x.shape
    steps = m // tm
    nq = 2 * GLA_HEADS * GLA_DK + 2 * GLA_HEADS * GLA_DV
    nu = POOL_GW * len(POOL_WINDOWS)
    nk = GLA_HEADS * GLA_DK
    const = lambda i: (0, 0)
    tn = 512
    cast_args, cast_in, cast_out, cast_shapes, cast_bytes = _cast_rider(cast, steps)
    pipelined = tm * (d * F32_BYTES + nq * BF16_BYTES + nk * F32_BYTES + nu * F32_BYTES) + cast_bytes
    resident = ((w.size + 2 * shift.size + 2 * wgb.size + tm * d + d * nu) * BF16_BYTES
                + 2 * tm * tn * F32_BYTES)
    return pl.pallas_call(
        _with_cast_rider(functools.partial(_in_even_body, tn=tn), 6, 3, len(cast)),
        grid=(steps,),
        in_specs=[
            pl.BlockSpec((tm, d), lambda i: (i, 0)),
            pl.BlockSpec((1, d), const),
            pl.BlockSpec(w.shape, const, pipeline_mode=pl.Buffered(1)),
            pl.BlockSpec(shift.shape, const),
            pl.BlockSpec(wgb.shape, const),
            pl.BlockSpec((1, nk), const),
        ] + cast_in,
        out_specs=[
            pl.BlockSpec((tm, nq), lambda i: (i, 0)),
            pl.BlockSpec((tm, nk), lambda i: (i, 0)),
            pl.BlockSpec((tm, nu), lambda i: (i, 0)),
        ] + cast_out,
        out_shape=[
            jax.ShapeDtypeStruct((m, nq), bf16),
            jax.ShapeDtypeStruct((m, nk), f32),
            jax.ShapeDtypeStruct((m, nu), f32),
        ] + cast_shapes,
        scratch_shapes=[pltpu.VMEM((tm, d), bf16), pltpu.VMEM((d, nu), bf16)],
        compiler_params=_params(("arbitrary",), pipelined, resident),
        name="in_even",
    )(x, gain, w, shift, wgb, bg, *cast_args)


def _gla_pool_prompt_body(qkvg_ref, loga_ref, u_ref, x_ref, tril_ref, gain_ref, pw_ref, ps_ref, wout_ref,
                          xo_ref, so_ref, st_ref, o_ref, e_ref, p_ref, q_ref, op_ref):
    t = x_ref.shape[0]
    ck = GLA_CHUNK
    kw = GLA_HEADS * GLA_DK
    vw = GLA_HEADS * GLA_DV
    pair_w = 2 * GLA_DK
    i = pl.program_id(1)

    @pl.when(i == 0)
    def _():
        st_ref[...] = jnp.zeros_like(st_ref)
        e_ref[0:POOL_HIST, :] = jnp.zeros((POOL_HIST, e_ref.shape[1]), f32)

    tril = tril_ref[...]
    row = lax.broadcasted_iota(jnp.int32, (2 * ck, pair_w), 0)
    lane = lax.broadcasted_iota(jnp.int32, (2 * ck, pair_w), 1)
    first_lanes = lane < GLA_DK
    first_lanes_ck = lax.broadcasted_iota(jnp.int32, (ck, pair_w), 1) < GLA_DK
    same_head = (row < ck) == first_lanes
    causal = same_head & ((row % ck) >= (lane % GLA_DK))
    pairs = range(GLA_HEADS // 2)
    chunks = range(t // ck)

    hist = POOL_HIST
    n = t + hist
    gw = POOL_GW
    u = u_ref[...]
    e_ref[hist:n, :] = u
    p_ref[8:n, :] = e_ref[8:n, :] + e_ref[7:n - 1, :]
    q_ref[16:n, gw:] = p_ref[16:n, gw:] + p_ref[14:n - 2, gw:]
    p_ref[24:n, 2 * gw:] = q_ref[24:n, 2 * gw:] + q_ref[20:n - 4, 2 * gw:]
    q_ref[32:n, 3 * gw:] = p_ref[32:n, 3 * gw:] + p_ref[24:n - 8, 3 * gw:]

    def rows_of(c):
        return slice(c * ck, (c + 1) * ck)

    def v_pair(c, p):
        va = qkvg_ref[rows_of(c), 2 * kw + (2 * p) * GLA_DV:2 * kw + (2 * p + 1) * GLA_DV]
        vb = qkvg_ref[rows_of(c), 2 * kw + (2 * p + 1) * GLA_DV:2 * kw + (2 * p + 2) * GLA_DV]
        return va, vb

    bcs = []
    for c in chunks:
        la = loga_ref[rows_of(c), :]
        la_hi = la.astype(bf16)
        la_lo = (la - la_hi.astype(f32)).astype(bf16)
        bcs.append(jnp.dot(tril, la_hi, preferred_element_type=f32) + jnp.dot(tril, la_lo, preferred_element_type=f32))
    lhs_q, ke2, kds, elast = [], [], [], []
    for c in chunks:
        bc = bcs[c]
        blast = bc[ck - 1:ck, :]
        q = qkvg_ref[rows_of(c), 0:kw].astype(f32) * (GLA_DK ** -0.5)
        k = qkvg_ref[rows_of(c), kw:2 * kw].astype(f32)
        qe = q * jnp.exp(bc)
        ke = (k * jnp.exp(-bc)).astype(bf16)
        kds.append((k * jnp.exp(blast - bc)).astype(bf16))
        elast.append(jnp.exp(blast))
        for p in pairs:
            pl_ = slice(p * pair_w, (p + 1) * pair_w)
            qe_p = qe[:, pl_]
            lhs_q.append(jnp.concatenate([jnp.where(first_lanes_ck, qe_p, 0.0),
                                          jnp.where(first_lanes_ck, 0.0, qe_p)], axis=0).astype(bf16))
            ke2.append(jnp.concatenate([ke[:, pl_], ke[:, pl_]], axis=0))
    att, upd = [], []
    for c in chunks:
        for p in pairs:
            idx = c * len(pairs) + p
            a = lax.dot_general(lhs_q[idx], ke2[idx], NT_DIMS, preferred_element_type=f32)
            att.append(jnp.where(causal, a, 0.0).astype(bf16))
            va, vb = v_pair(c, p)
            r = lax.dot_general(jnp.concatenate([va, vb], axis=1), kds[c][:, p * pair_w:(p + 1) * pair_w], TN_DIMS,
                                preferred_element_type=f32)
            upd.append(jnp.where(first_lanes, r[:GLA_DV], r[GLA_DV:]))
    st = [st_ref[p] for p in pairs]
    for c in chunks:
        for p in pairs:
            idx = c * len(pairs) + p
            va, vb = v_pair(c, p)
            o = lax.dot_general(lhs_q[idx], st[p].astype(bf16), NT_DIMS, preferred_element_type=f32)
            o = o + jnp.dot(att[idx], jnp.concatenate([va, vb], axis=0), preferred_element_type=f32)
            o_ref[rows_of(c), (2 * p) * GLA_DV:(2 * p + 1) * GLA_DV] = o[:ck]
            o_ref[rows_of(c), (2 * p + 1) * GLA_DV:(2 * p + 2) * GLA_DV] = o[ck:]
            st[p] = st[p] * elast[c][:, p * pair_w:(p + 1) * pair_w] + upd[idx]
    for p in pairs:
        st_ref[p] = st[p]

    sums = (p_ref, q_ref, p_ref, q_ref)
    pos = i * t + lax.broadcasted_iota(jnp.int32, (t, 1), 0)
    for gi, w in enumerate(POOL_WINDOWS):
        ls = slice(gi * gw, (gi + 1) * gw)
        cnt = jnp.minimum(w, pos + 1).astype(f32)
        pooled = (sums[gi][hist:n, ls] / cnt - u[:, ls]).astype(bf16)
        pg = jnp.dot(pooled, pw_ref[gi], preferred_element_type=f32) * ps_ref[:, ls]
        op_ref[:, vw + gi * gw:vw + (gi + 1) * gw] = pg.astype(bf16)
    e_ref[hist - POOL_TAIL:hist, :] = e_ref[n - POOL_TAIL:n, :]

    piece = 2 * GLA_DV
    y = x_ref[...]
    for c0 in (vw, vw + piece):
        y = y + jnp.dot(op_ref[:, c0:c0 + piece], wout_ref[c0:c0 + piece, :], preferred_element_type=f32)
    gain = gain_ref[...]
    for p in pairs:
        for h in (2 * p, 2 * p + 1):
            hs = slice(h * GLA_DV, (h + 1) * GLA_DV)
            g = qkvg_ref[:, 2 * kw + vw + h * GLA_DV:2 * kw + vw + (h + 1) * GLA_DV].astype(f32)
            op_ref[:, hs] = (_rms(o_ref[:, hs], gain) * _silu(g)).astype(bf16)
        c0 = p * piece
        y = y + jnp.dot(op_ref[:, c0:c0 + piece], wout_ref[c0:c0 + piece, :], preferred_element_type=f32)
    xo_ref[...] = y

    @pl.when(i == pl.num_programs(1) - 1)
    def _():
        for p in range(GLA_HEADS // 2):
            s_pair = st_ref[p].T
            so_ref[0, 2 * p] = s_pair[:GLA_DK]
            so_ref[0, 2 * p + 1] = s_pair[GLA_DK:]


def _gla_pool_prompt(qkvg, loga, u, x, tril, gain, pw, ps, wout, *, batch, t):
    m, d = x.shape
    nt = m // batch // t
    row = lambda b, i: (b * nt + i, 0)
    const2 = lambda b, i: (0, 0)
    vw = GLA_HEADS * GLA_DV
    uw = u.shape[1]
    pipelined = t * (qkvg.shape[1] * BF16_BYTES + (loga.shape[1] + uw + 2 * d) * F32_BYTES)
    scratch = (t * vw + 3 * (POOL_HIST + t) * uw) * F32_BYTES + t * (vw + uw) * BF16_BYTES
    waves = (t // GLA_CHUNK) * (GLA_HEADS // 2) * (3 * LANES * LANES * BF16_BYTES + LANES * LANES * F32_BYTES)
    resident = 2 * (wout.size + pw.size) * BF16_BYTES + scratch + waves + t * d * F32_BYTES
    return pl.pallas_call(
        _gla_pool_prompt_body,
        grid=(batch, nt),
        in_specs=[
            pl.BlockSpec((t, qkvg.shape[1]), row),
            pl.BlockSpec((t, loga.shape[1]), row),
            pl.BlockSpec((t, uw), row),
            pl.BlockSpec((t, d), row),
            pl.BlockSpec(tril.shape, const2),
            pl.BlockSpec(gain.shape, const2),
            pl.BlockSpec(pw.shape, lambda b, i: (0, 0, 0)),
            pl.BlockSpec(ps.shape, const2),
            pl.BlockSpec(wout.shape, const2),
        ],
        out_specs=[
            pl.BlockSpec((t, d), row),
            pl.BlockSpec((1, GLA_HEADS, GLA_DK, GLA_DV), lambda b, i: (b, 0, 0, 0)),
        ],
        out_shape=[
            jax.ShapeDtypeStruct((m, d), f32),
            jax.ShapeDtypeStruct((batch, GLA_HEADS, GLA_DK, GLA_DV), f32),
        ],
        scratch_shapes=[
            pltpu.VMEM((GLA_HEADS // 2, GLA_DV, 2 * GLA_DK), f32),
            pltpu.VMEM((t, vw), f32),
            pltpu.VMEM((POOL_HIST + t, uw), f32),
            pltpu.VMEM((POOL_HIST + t, uw), f32),
            pltpu.VMEM((POOL_HIST + t, uw), f32),
            pltpu.VMEM((t, vw + uw), bf16),
        ],
        compiler_params=_params(("arbitrary", "arbitrary"), pipelined, resident),
        name="gla_pool_prompt",
    )(qkvg, loga, u, x, tril, gain, pw, ps, wout)


def _gla_pool_sample_body(qkvg_ref, loga_ref, u_ref, s_ref, buf_ref, gain_ref, pw_ref, ps_ref,
                          op_ref, so_ref):
    bb = u_ref.shape[0]
    kw = GLA_HEADS * GLA_DK
    vw = GLA_HEADS * GLA_DV
    gain = gain_ref[...]
    qkvg = qkvg_ref[...].astype(f32)
    alpha = jnp.exp(loga_ref[...])
    qs = qkvg[:, 0:kw] * (GLA_DK ** -0.5)
    k = qkvg[:, kw:2 * kw]

    def column(row):
        return jnp.broadcast_to(row, (LANES, kw)).T

    o_rows = []
    for b in range(bb):
        acol = column(alpha[b:b + 1, :])
        qcol = column(qs[b:b + 1, :])
        kcol = column(k[b:b + 1, :])
        o_heads = []
        for h in range(GLA_HEADS):
            ks = slice(h * GLA_DK, (h + 1) * GLA_DK)
            v = qkvg[b:b + 1, 2 * kw + h * GLA_DV:2 * kw + (h + 1) * GLA_DV]
            s_new = acol[ks, :] * s_ref[b, h] + kcol[ks, :] * v
            so_ref[b, h] = s_new
            o = jnp.sum(qcol[ks, :] * s_new, axis=0, keepdims=True)
            g = qkvg[b:b + 1, 2 * kw + vw + h * GLA_DV:2 * kw + vw + (h + 1) * GLA_DV]
            o_heads.append(_rms(o, gain) * _silu(g))
        o_rows.append(jnp.concatenate(o_heads, axis=1))
    op_ref[:, 0:vw] = jnp.concatenate(o_rows, axis=0).astype(bf16)

    u = u_ref[...]
    for gi, w in enumerate(POOL_WINDOWS):
        ls = slice(gi * POOL_GW, (gi + 1) * POOL_GW)
        s = u[:, ls] + jnp.sum(buf_ref[:, POOL_BUF - (w - 1):POOL_BUF, ls], axis=1)
        cnt = float(min(w, PAST_LEN + 1))
        pooled = (s / cnt - u[:, ls]).astype(bf16)
        pg = jnp.dot(pooled, pw_ref[gi], preferred_element_type=f32) * ps_ref[:, ls]
        op_ref[:, vw + gi * POOL_GW:vw + (gi + 1) * POOL_GW] = pg.astype(bf16)


def _gla_pool_sample(qkvg, loga, u, s, buf, gain, pw, ps, *, bb):
    n = u.shape[0]
    row = lambda i: (i, 0)
    const2 = lambda i: (0, 0)
    ow = GLA_HEADS * GLA_DV + POOL_GW * len(POOL_WINDOWS)
    state_rows = int(np.prod(s.shape[1:]))
    pipelined = bb * ((qkvg.shape[1] + ow) * BF16_BYTES
                      + (loga.shape[1] + u.shape[1] + 2 * state_rows + POOL_TAIL * buf.shape[2]) * F32_BYTES)
    resident = 2 * pw.size * BF16_BYTES + 3 * LANES * loga.shape[1] * F32_BYTES
    return pl.pallas_call(
        _gla_pool_sample_body,
        grid=(n // bb,),
        in_specs=[
            pl.BlockSpec((bb, qkvg.shape[1]), row),
            pl.BlockSpec((bb, loga.shape[1]), row),
            pl.BlockSpec((bb, u.shape[1]), row),
            pl.BlockSpec((bb,) + s.shape[1:], lambda i: (i, 0, 0, 0)),
            pl.BlockSpec((bb,) + buf.shape[1:], lambda i: (i, 0, 0)),
            pl.BlockSpec(gain.shape, const2),
            pl.BlockSpec(pw.shape, lambda i: (0, 0, 0)),
            pl.BlockSpec(ps.shape, const2),
        ],
        out_specs=[
            pl.BlockSpec((bb, ow), row),
            pl.BlockSpec((bb,) + s.shape[1:], lambda i: (i, 0, 0, 0)),
        ],
        out_shape=[
            jax.ShapeDtypeStruct((n, ow), bf16),
            jax.ShapeDtypeStruct(s.shape, f32),
        ],
        compiler_params=_params(("arbitrary",), pipelined, resident, exclusive=True),
        name="gla_pool_sample",
    )(qkvg, loga, u, s, buf, gain, pw, ps)


def _proj_res_body(x_ref, a_ref, w_ref, o_ref):
    o_ref[...] = x_ref[...] + jnp.dot(a_ref[...], w_ref[...], preferred_element_type=f32)


def _proj_res(x, a, w, *, tm):
    m, d = x.shape
    return pl.pallas_call(
        _proj_res_body,
        grid=(m // tm,),
        in_specs=[
            pl.BlockSpec((tm, d), lambda i: (i, 0)),
            pl.BlockSpec((tm, a.shape[1]), lambda i: (i, 0)),
            pl.BlockSpec(w.shape, lambda i: (0, 0)),
        ],
        out_specs=pl.BlockSpec((tm, d), lambda i: (i, 0)),
        out_shape=jax.ShapeDtypeStruct((m, d), f32),
        compiler_params=_params(("arbitrary",), tm * (2 * d * F32_BYTES + a.shape[1] * BF16_BYTES),
                                2 * w.size * BF16_BYTES),
        name="proj_res",
    )(x, a, w)


def _ret_token_pieces(q_ref, k_ref, v_ref, g_ref, s_ref, og_ref, so_ref, gamma):
    def piece(j, h):
        def run():
            ks = slice(h * RET_DK, (h + 1) * RET_DK)
            vs = slice(h * RET_DV, (h + 1) * RET_DV)
            qcol = jnp.broadcast_to(q_ref[j, :, ks].astype(f32), (LANES, RET_DK)).T
            kcol = jnp.broadcast_to(k_ref[j, :, ks].astype(f32), (LANES, RET_DK)).T
            v = v_ref[j, :, vs].astype(f32)
            g = g_ref[j, :, vs].astype(f32)
            o_tiles = []
            for t in range(RET_DV // LANES):
                cs = slice(t * LANES, (t + 1) * LANES)
                s_new = gamma[h] * s_ref[j, h, :, cs] + kcol * v[:, cs]
                so_ref[j, h, :, cs] = s_new
                o_tiles.append(jnp.sum(qcol * s_new, axis=0, keepdims=True))
            o = jnp.concatenate(o_tiles, axis=1)
            og_ref[j, :, vs] = (_rms(o) * _silu(g)).astype(bf16)
            return o
        return run

    return [piece(j, h) for j in range(s_ref.shape[0]) for h in range(RET_HEADS)]


def _ffn_body(*refs, tf, n_sub, final_norm, rider_gamma):
    x_ref, gain_ref, wg_ref, wu_ref, wd_ref, fgain_ref = refs[:6]
    pieces = []
    if rider_gamma is None:
        o_ref, h_ref, acc_ref = refs[6:]
    else:
        rq_ref, rk_ref, rv_ref, rg_ref, rs_ref, o_ref, rog_ref, rso_ref, h_ref, acc_ref = refs[6:]
        pieces = _ret_token_pieces(rq_ref, rk_ref, rv_ref, rg_ref, rs_ref, rog_ref, rso_ref, rider_gamma)
    n_chunks = wg_ref.shape[1] // tf
    bounds = [n_chunks * s // n_sub for s in range(n_sub + 1)]

    def exact_zero(v):
        bits = lax.bitcast_convert_type(v, jnp.uint32)
        return ((bits >> 16) >> 16).astype(f32)

    def run_chunks(chunks):
        pin = None
        for n, c in enumerate(chunks):
            cs = slice(c * tf, (c + 1) * tf)
            g = jnp.dot(h_ref[...], wg_ref[:, cs], preferred_element_type=f32)
            if pin is not None:
                g = g + pin
                pin = None
            u = jnp.dot(h_ref[...], wu_ref[:, cs], preferred_element_type=f32)
            a = (_silu(g) * u).astype(bf16)
            part = jnp.dot(a, wd_ref[cs, :], preferred_element_type=f32)
            if c == 0:
                acc_ref[...] = part
            else:
                acc_ref[...] += part
            for p in range(len(pieces)):
                if p * (len(chunks) - 1) // len(pieces) == n:
                    z = exact_zero(pieces[p]()[:, :tf])
                    pin = z if pin is None else pin + z

    def sub_step(s):
        if s == 0:
            h_ref[...] = _rms(x_ref[...], gain_ref[...]).astype(bf16)
        run_chunks(range(bounds[s], bounds[s + 1]))
        if s == n_sub - 1:
            y = x_ref[...] + acc_ref[...]
            if final_norm:
                y = _rms(y, fgain_ref[...])
            o_ref[...] = y

    if n_sub == 1:
        sub_step(0)
    else:
        for s in range(n_sub):
            pl.when(pl.program_id(1) == s)(functools.partial(sub_step, s))


def _ffn(x, gain, wg, wu, wd, fgain, *, layer, tm, tf, final_norm, rider=None):
    m, d = x.shape
    ff = wg.shape[1]
    steps = m // tm
    n_sub = 1 if rider is None else 2
    single_buffered = dict(pipeline_mode=pl.Buffered(1))
    in_specs = [
        pl.BlockSpec((tm, d), lambda i, s: (i, 0)),
        pl.BlockSpec((None, 1, d), lambda i, s: (layer, 0, 0)),
        pl.BlockSpec((d, ff), lambda i, s: (0, 0), **single_buffered),
        pl.BlockSpec((d, ff), lambda i, s: (0, 0), **single_buffered),
        pl.BlockSpec((ff, d), lambda i, s: (0, 0), **single_buffered),
        pl.BlockSpec((1, d), lambda i, s: (0, 0)),
    ]
    args = [x, gain, wg, wu, wd, fgain]
    out_specs = [pl.BlockSpec((tm, d), lambda i, s: (i, 0))]
    out_shape = [jax.ShapeDtypeStruct((m, d), f32)]
    gamma = None
    pipelined = 2 * tm * d * F32_BYTES
    resident = (3 * d * ff * BF16_BYTES + tm * d * (BF16_BYTES + F32_BYTES)
                + 3 * tm * tf * F32_BYTES + tm * d * F32_BYTES)
    if rider is not None:
        qkvg3, state, rows, gamma = rider
        assert 2 * steps * rows == state.shape[0]
        qw = RET_HEADS * RET_DK
        vw = RET_HEADS * RET_DV
        blk = lambda col: (lambda i, s: (2 * i + s, 0, col))
        state_spec = pl.BlockSpec((rows,) + state.shape[1:], lambda i, s: (2 * i + s, 0, 0, 0))
        in_specs += [
            pl.BlockSpec((rows, 1, qw), blk(0)),
            pl.BlockSpec((rows, 1, qw), blk(1)),
            pl.BlockSpec((rows, 1, vw), blk(1)),
            pl.BlockSpec((rows, 1, vw), blk(2)),
            state_spec,
        ]
        args += [qkvg3, qkvg3, qkvg3, qkvg3, state]
        out_specs += [pl.BlockSpec((rows, 1, vw), blk(0)), state_spec]
        out_shape += [
            jax.ShapeDtypeStruct((state.shape[0], 1, vw), bf16),
            jax.ShapeDtypeStruct(state.shape, f32),
        ]
        pipelined += 2 * rows * int(np.prod(state.shape[1:])) * F32_BYTES
    out = pl.pallas_call(
        functools.partial(_ffn_body, tf=tf, n_sub=n_sub, final_norm=final_norm, rider_gamma=gamma),
        grid=(steps, n_sub),
        in_specs=in_specs,
        out_specs=out_specs,
        out_shape=out_shape,
        scratch_shapes=[pltpu.VMEM((tm, d), bf16), pltpu.VMEM((tm, d), f32)],
        compiler_params=_params(("arbitrary", "arbitrary"), pipelined, resident, exclusive=steps == 1),
        name="ffn_final" if final_norm else "ffn",
    )(*args)
    return out[0] if rider is None else out


def _in_odd_body(x_ref, gain_ref, w_ref, perm_ref, cos_ref, sin_ref, qsc_ref, ksc_ref, o_ref, h_ref, wqk_ref, *,
                 tn, split_halves):
    qw = RET_HEADS * RET_DK
    half = RET_DK // 2
    if split_halves:
        @pl.when(pl.program_id(0) == 0)
        def _():
            for hh in range(2 * RET_HEADS):
                hs = slice(hh * RET_DK, (hh + 1) * RET_DK)
                wqk_ref[:, hs] = jnp.dot(w_ref[:, hs], perm_ref[...], preferred_element_type=f32).astype(bf16)

    h_ref[...] = _rms(x_ref[...], gain_ref[...]).astype(bf16)
    cos = cos_ref[...]
    sin = sin_ref[...]
    for c in range(2 * qw // tn):
        c0 = c * tn
        w_chunk = wqk_ref[:, c0:c0 + tn] if split_halves else w_ref[:, c0:c0 + tn]
        p = jnp.dot(h_ref[...], w_chunk, preferred_element_type=f32)
        sc_ref = qsc_ref if c0 < qw else ksc_ref
        for hh in range(tn // RET_DK):
            h0 = hh * RET_DK
            head = (c0 % qw + h0) // RET_DK
            sc = sc_ref[:, head * LANES:(head + 1) * LANES]
            if split_halves:
                ev = p[:, h0:h0 + half]
                od = p[:, h0 + half:h0 + RET_DK]
                o_ref[:, c0 + h0:c0 + h0 + half] = ((ev * cos - od * sin) * sc).astype(bf16)
                o_ref[:, c0 + h0 + half:c0 + h0 + RET_DK] = ((od * cos + ev * sin) * sc).astype(bf16)
            else:
                xh = p[:, h0:h0 + RET_DK]
                even = lax.broadcasted_iota(jnp.int32, xh.shape, 1) % 2 == 0
                partner = jnp.where(even, pltpu.roll(xh, RET_DK - 1, 1), pltpu.roll(xh, 1, 1))
                r = xh * cos + partner * sin
                o_ref[:, c0 + h0:c0 + h0 + half] = (r[:, :half] * sc).astype(bf16)
                o_ref[:, c0 + h0 + half:c0 + h0 + RET_DK] = (r[:, half:] * sc).astype(bf16)
    for c0 in range(2 * qw, w_ref.shape[1], tn):
        p = jnp.dot(h_ref[...], w_ref[:, c0:c0 + tn], preferred_element_type=f32)
        o_ref[:, c0:c0 + tn] = p.astype(bf16)


def _in_odd(x, gain, w, perm, tables, *, layer, tm, tn, split_halves, cast=()):
    m, d = x.shape
    n = w.shape[1]
    cos, sin, qsc, ksc = tables
    ntab = cos.shape[0] // tm
    qkw = 2 * RET_HEADS * RET_DK
    single_buffered = dict(pipeline_mode=pl.Buffered(1))
    const = lambda i: (0, 0)
    rope_spec = pl.BlockSpec((tm, cos.shape[1]), lambda i: (i % ntab, 0))
    wqk_shape = (d, qkw) if split_halves else (8, LANES)
    cast_args, cast_in, cast_out, cast_shapes, cast_bytes = _cast_rider(cast, m // tm)
    pipelined = tm * (d * F32_BYTES + n * BF16_BYTES + 2 * cos.shape[1] * F32_BYTES) + cast_bytes
    resident = ((w.size + 2 * perm.size + tm * d + wqk_shape[0] * wqk_shape[1]) * BF16_BYTES
                + 2 * (qsc.size + ksc.size) * F32_BYTES + 2 * tm * tn * F32_BYTES)
    outs = pl.pallas_call(
        _with_cast_rider(functools.partial(_in_odd_body, tn=tn, split_halves=split_halves), 8, 1, len(cast)),
        grid=(m // tm,),
        in_specs=[
            pl.BlockSpec((tm, d), lambda i: (i, 0)),
            pl.BlockSpec((None, 1, d), lambda i: (layer, 0, 0)),
            pl.BlockSpec(w.shape, const, **single_buffered),
            pl.BlockSpec(perm.shape, const),
            rope_spec, rope_spec,
            pl.BlockSpec(qsc.shape, const),
            pl.BlockSpec(ksc.shape, const),
        ] + cast_in,
        out_specs=[pl.BlockSpec((tm, n), lambda i: (i, 0))] + cast_out,
        out_shape=[jax.ShapeDtypeStruct((m, n), bf16)] + cast_shapes,
        scratch_shapes=[pltpu.VMEM((tm, d), bf16), pltpu.VMEM(wqk_shape, bf16)],
        compiler_params=_params(("arbitrary",), pipelined, resident),
        name="in_odd",
    )(x, gain, w, perm, cos, sin, qsc, ksc, *cast_args)
    return tuple(outs) if cast else outs[0]


def _ret_prompt_body(q_ref, k_ref, v_ref, g_ref, x_ref, wout_ref, xo_ref, so_ref, s_ref, sb_ref, slab_ref, *,
                     gamma_c, n):
    c = pl.program_id(1)
    subs = [slice(j * n, (j + 1) * n) for j in range(q_ref.shape[0] // n)]

    @pl.when(c == 0)
    def _():
        s_ref[...] = jnp.zeros_like(s_ref)
        sb_ref[...] = jnp.zeros_like(sb_ref)

    causal = lax.broadcasted_iota(jnp.int32, (n, n), 0) >= lax.broadcasted_iota(jnp.int32, (n, n), 1)
    heads = range(RET_HEADS)
    ks = [slice(h * RET_DK, (h + 1) * RET_DK) for h in heads]
    vs = [slice(h * RET_DV, (h + 1) * RET_DV) for h in heads]
    att = [[jnp.where(causal, lax.dot_general(q_ref[r, ks[h]], k_ref[r, ks[h]], NT_DIMS,
                                              preferred_element_type=f32), 0.0).astype(bf16) for h in heads]
           for r in subs]
    o = []
    for j, r in enumerate(subs):
        o.append([jnp.dot(q_ref[r, ks[h]], sb_ref[h], preferred_element_type=f32)
                  + jnp.dot(att[j][h], v_ref[r, vs[h]], preferred_element_type=f32) for h in heads])
        for h in heads:
            kv = lax.dot_general(k_ref[r, ks[h]], v_ref[r, vs[h]], TN_DIMS, preferred_element_type=f32)
            s_new = gamma_c[h] * (s_ref[h] + kv)
            s_ref[h] = s_new
            sb_ref[h] = s_new.astype(bf16)
    for j, r in enumerate(subs):
        y = x_ref[r, :]
        for h in heads:
            og = (_rms(o[j][h]) * _silu(g_ref[r, vs[h]].astype(f32))).astype(bf16)
            y = y + jnp.dot(og, wout_ref[vs[h], :], preferred_element_type=f32)
        xo_ref[r, :] = y

    @pl.when(c == pl.num_programs(1) - 1)
    def _():
        half = RET_DK // 2
        for h in range(RET_HEADS):
            for t in range(RET_DV // LANES):
                ls = slice(t * LANES, (t + 1) * LANES)
                slab_ref[pl.ds(0, half, stride=2), :] = s_ref[h, 0:half, ls]
                slab_ref[pl.ds(1, half, stride=2), :] = s_ref[h, half:RET_DK, ls]
                so_ref[0, h, :, ls] = slab_ref[...]


def _ret_prompt(qkvg, x, wout, gamma_c, *, batch, c, chunk):
    m, d = x.shape
    nc = m // batch // c
    qw = RET_HEADS * RET_DK
    vw = RET_HEADS * RET_DV
    assert c % chunk == 0
    state = RET_HEADS * RET_DK * RET_DV
    pipelined = c * (2 * (qw + vw) * BF16_BYTES + 2 * d * F32_BYTES) + state * F32_BYTES
    resident = (wout.size * BF16_BYTES + state * (F32_BYTES + BF16_BYTES) + RET_DK * LANES * F32_BYTES
                + c * RET_HEADS * (chunk * BF16_BYTES + RET_DV * F32_BYTES)
                + RET_DK * RET_DV * F32_BYTES + c * d * F32_BYTES)
    return pl.pallas_call(
        functools.partial(_ret_prompt_body, gamma_c=gamma_c, n=chunk),
        grid=(batch, nc),
        in_specs=[
            pl.BlockSpec((c, qw), lambda b, i: (b * nc + i, 0)),
            pl.BlockSpec((c, qw), lambda b, i: (b * nc + i, 1)),
            pl.BlockSpec((c, vw), lambda b, i: (b * nc + i, 1)),
            pl.BlockSpec((c, vw), lambda b, i: (b * nc + i, 2)),
            pl.BlockSpec((c, d), lambda b, i: (b * nc + i, 0)),
            pl.BlockSpec(wout.shape, lambda b, i: (0, 0), pipeline_mode=pl.Buffered(1)),
        ],
        out_specs=[
            pl.BlockSpec((c, d), lambda b, i: (b * nc + i, 0)),
            pl.BlockSpec((1, RET_HEADS, RET_DK, RET_DV), lambda b, i: (b, 0, 0, 0)),
        ],
        out_shape=[
            jax.ShapeDtypeStruct((m, d), f32),
            jax.ShapeDtypeStruct((batch, RET_HEADS, RET_DK, RET_DV), f32),
        ],
        scratch_shapes=[
            pltpu.VMEM((RET_HEADS, RET_DK, RET_DV), f32),
            pltpu.VMEM((RET_HEADS, RET_DK, RET_DV), bf16),
            pltpu.VMEM((RET_DK, LANES), f32),
        ],
        compiler_params=_params(("arbitrary", "arbitrary"), pipelined, resident),
        name="ret_prompt",
    )(qkvg, qkvg, qkvg, qkvg, x, wout)


def _rope_tables(pos, per_pair):
    pair_angle = 1.0 / (ROPE_BASE ** jnp.linspace(0.0, 1.0, RET_DK // 2, dtype=f32))
    if per_pair:
        ang = pos[:, None] * pair_angle[None, :]
        return jnp.cos(ang), jnp.sin(ang)
    ang = pos[:, None] * jnp.repeat(pair_angle, 2)[None, :]
    sign = jnp.where(jnp.arange(RET_DK) % 2 == 0, -1.0, 1.0).astype(f32)
    return jnp.cos(ang), jnp.sin(ang) * sign


def _even_odd_perm():
    half = RET_DK // 2
    src = np.concatenate([2 * np.arange(half), 2 * np.arange(half) + 1])
    perm = np.zeros((RET_DK, RET_DK), np.float32)
    perm[src, np.arange(RET_DK)] = 1.0
    return jnp.asarray(perm, dtype=bf16)


def _lane_replicated(scale):
    return jnp.asarray(np.repeat(scale, LANES, axis=1), dtype=f32)


def _ret_decay(rows, c):
    gam = 1.0 - 2.0 ** (-5.0 - np.arange(RET_HEADS, dtype=np.float64))
    lg = np.log(gam)
    steps = (np.arange(rows) % c + 1.0)[:, None]
    q_scale = _lane_replicated(np.exp(lg[None, :] * steps))
    k_scale = _lane_replicated(np.exp(-lg[None, :] * steps) * RET_DK ** -0.5)
    gamma_c = tuple(float(x) for x in np.exp(lg * c))
    gamma = tuple(float(x) for x in gam)
    return q_scale, k_scale, gamma_c, gamma


def kernel(x_prompt, x_sample, state_gla, state_pool, state_ret, norm_mix, norm_ffn, norm_final, w_in_even,
           w_gate_b, b_gate, gla_gain, pool_w, pool_scale, w_out_even, w_in_odd, w_out_odd, w_ffn_gate,
           w_ffn_up, w_ffn_down):
    batch, seq, d = x_prompt.shape
    n_s = x_sample.shape[0]
    assert norm_mix.shape[0] == 2 and x_sample.shape[1] == 1

    we = jnp.pad(w_in_even[0], ((0, 0), (0, -w_in_even.shape[2] % LANES))).astype(bf16)
    nu = POOL_GW * len(POOL_WINDOWS)
    shift = np.zeros((nu + LANES, nu), np.float32)
    shift[GATE_RANK + np.arange(nu), np.arange(nu)] = 1.0
    shift = jnp.asarray(shift, dtype=bf16)
    wgb = jnp.concatenate([w_gate_b[0], jnp.zeros((LANES - GATE_RANK, w_gate_b.shape[2]), f32)], axis=0).astype(bf16)
    bg = b_gate[0][None, :]
    gg = gla_gain[0][None, :]
    pw = pool_w[0].astype(bf16)
    ps = pool_scale[0][None, :]
    nm = norm_mix[:, None, :]
    nf = norm_ffn[:, None, :]
    nfin = norm_final[None, :]
    tril = jnp.asarray(np.tril(np.ones((GLA_CHUNK, GLA_CHUNK), np.float32)), dtype=bf16)
    tf = 256
    tm_p = 512
    q_scale, k_scale, gamma_c, gamma = _ret_decay(tm_p, RET_CHUNK)
    tables_p = _rope_tables(jnp.arange(seq, dtype=f32), True) + (q_scale, k_scale)
    tables_s = _rope_tables(jnp.full((n_s,), float(PAST_LEN), f32), False) + (
        _lane_replicated(np.ones((n_s, RET_HEADS))), _lane_replicated(np.full((n_s, RET_HEADS), RET_DK ** -0.5)))
    perm = _even_odd_perm()

    ff = w_ffn_gate.shape[2]
    xp = x_prompt.reshape(batch * seq, d)
    steps_p = batch * seq // tm_p
    ffn_cast = lambda layer: (
        (w_ffn_gate.reshape(-1, ff), d // steps_p, layer * steps_p, steps_p),
        (w_ffn_up.reshape(-1, ff), d // steps_p, layer * steps_p, steps_p),
        (w_ffn_down.reshape(-1, d), 2 * ff // steps_p, layer * steps_p // 2, steps_p // 2))
    whole = lambda w: (w, w.shape[0] // steps_p, 0, steps_p)
    qkvg, loga, u_p, wg0, wu0, wd0, wio, woe, woo = _in_even(
        xp, nm[0], we, shift, wgb, bg, tm=tm_p,
        cast=ffn_cast(0) + (whole(w_in_odd[0]), whole(w_out_even[0]), whole(w_out_odd[0])))
    xp, gla_p = _gla_pool_prompt(qkvg, loga, u_p, xp, tril, gg, pw, ps, woe, batch=batch, t=512)

    xs = x_sample.reshape(n_s, d)
    qkvg_s, loga_s, u_s = _in_even(xs, nm[0], we, shift, wgb, bg, tm=n_s)
    op_s, gla_s = _gla_pool_sample(qkvg_s, loga_s, u_s, state_gla[0], state_pool[0], gg, pw, ps, bb=8)
    xs = _proj_res(xs, op_s, woe, tm=n_s)
    xs = _ffn(xs, nf, wg0, wu0, wd0, nfin, layer=0, tm=n_s, tf=tf, final_norm=False)
    qkvg2_s = _in_odd(xs, nm, wio, perm, tables_s, layer=1, tm=n_s, tn=512, split_halves=False)
    qkvg2_s = qkvg2_s.reshape(n_s, 1, -1)

    rows = n_s // (2 * steps_p)
    xp, og_s, ret_s = _ffn(xp, nf, wg0, wu0, wd0, nfin, layer=0, tm=tm_p, tf=tf, final_norm=False,
                           rider=(qkvg2_s, state_ret[0], rows, gamma))
    qkvg2, wg1, wu1, wd1 = _in_odd(xp, nm, wio, perm, tables_p, layer=1, tm=tm_p, tn=512, split_halves=True,
                                   cast=ffn_cast(1))
    xp, ret_p = _ret_prompt(qkvg2, xp, woo, gamma_c, batch=batch, c=2 * RET_CHUNK, chunk=RET_CHUNK)
    y_prompt = _ffn(xp, nf, wg1, wu1, wd1, nfin, layer=1, tm=tm_p, tf=tf, final_norm=True)
    pool_p = u_p.reshape(batch, seq, -1)[:, seq - POOL_BUF:, :]

    xs = _proj_res(xs, og_s.reshape(n_s, -1), woo, tm=n_s)
    y_sample = _ffn(xs, nf, wg1, wu1, wd1, nfin, layer=1, tm=n_s, tf=tf, final_norm=True)

    pool_s = jnp.concatenate([state_pool[0][:, 1:, :], u_s[:, None, :]], axis=1)

    return (y_prompt.reshape(batch, seq, d), y_sample.reshape(n_s, 1, d),
            gla_p[None], gla_s[None], pool_p[None], pool_s[None], ret_p[None], ret_s[None])
```

```python
import functools

import numpy as np
import jax
import jax.numpy as jnp
from jax import lax
from jax.experimental import pallas as pl
from jax.experimental.pallas import tpu as pltpu

f32 = jnp.float32
bf16 = jnp.bfloat16

EPS = 1e-6
PAST_LEN = 16384
GLA_HEADS, GLA_DK, GLA_DV = 4, 64, 128
GLA_CHUNK = 64
GATE_RANK = 16
GATE_NORMALIZER = 16.0
POOL_WINDOWS = (2, 4, 8, 16)
POOL_GW = 128
POOL_BUF = max(POOL_WINDOWS) - 1
POOL_HIST = 32
POOL_TAIL = 16
RET_HEADS, RET_DK, RET_DV = 4, 256, 512
RET_CHUNK = 256
ROPE_BASE = 10000.0
LANES = 128
MIB = 1024 * 1024
VMEM_COMPILER_ALLOWANCE = 8 * MIB
F32_BYTES, BF16_BYTES = 4, 2

NT_DIMS = (((1,), (1,)), ((), ()))
TN_DIMS = (((0,), (0,)), ((), ()))


def _params(semantics, pipelined_bytes, resident_bytes):
    limit = 2 * pipelined_bytes + resident_bytes + VMEM_COMPILER_ALLOWANCE
    return pltpu.CompilerParams(dimension_semantics=semantics, vmem_limit_bytes=int(limit))


def _rms(x, gain=None):
    y = x * lax.rsqrt(jnp.mean(x * x, axis=-1, keepdims=True) + EPS)
    return y if gain is None else y * gain


def _silu(g):
    return g * jax.nn.sigmoid(g)


def _in_even_body(x_ref, gain_ref, w_ref, wgb_ref, bg_ref, qkvg_ref, loga_ref, u_ref, h_ref, wq_ref, wu_ref, *, tn):
    nq = qkvg_ref.shape[1]
    nu = u_ref.shape[1]

    @pl.when(pl.program_id(0) == 0)
    def _():
        wq_ref[...] = w_ref[:, :nq + LANES].astype(bf16)
        wu_ref[...] = w_ref[:, nq + GATE_RANK:nq + GATE_RANK + nu].astype(bf16)

    h_ref[...] = _rms(x_ref[...], gain_ref[...]).astype(bf16)
    a = jnp.dot(h_ref[...], wq_ref[:, nq:nq + LANES], preferred_element_type=f32)
    a = jnp.where(lax.broadcasted_iota(jnp.int32, a.shape, 1) < GATE_RANK, a, 0.0).astype(bf16)
    for c0 in range(0, nq, tn):
        qkvg_ref[:, c0:c0 + tn] = jnp.dot(h_ref[...], wq_ref[:, c0:c0 + tn], preferred_element_type=f32).astype(bf16)
        if c0 == 0:
            z = jnp.dot(a, wgb_ref[...], preferred_element_type=f32) + bg_ref[...]
            loga_ref[...] = (jnp.minimum(z, 0.0) - jnp.log1p(jnp.exp(-jnp.abs(z)))) * (1.0 / GATE_NORMALIZER)
    for c0 in range(0, nu, tn):
        u_ref[:, c0:c0 + tn] = jnp.dot(h_ref[...], wu_ref[:, c0:c0 + tn], preferred_element_type=f32)


def _cast_rider(cast, steps):
    arrays, in_specs, out_specs, out_shapes, nbytes = [], [], [], [], 0
    for arr, rows, first, count in cast:
        assert count <= steps and rows % 16 == 0 and (first + count) * rows <= arr.shape[0]
        cols = arr.shape[1]
        arrays.append(arr)
        in_specs.append(pl.BlockSpec(
            (rows, cols), lambda i, first=first, count=count: (first + jnp.minimum(i, count - 1), 0)))
        out_specs.append(pl.BlockSpec((rows, cols), lambda i, count=count: (jnp.minimum(i, count - 1), 0)))
        out_shapes.append(jax.ShapeDtypeStruct((count * rows, cols), bf16))
        nbytes += rows * cols * (F32_BYTES + BF16_BYTES)
    return arrays, in_specs, out_specs, out_shapes, nbytes


def _with_cast_rider(body, n_in, n_out, n_cast):
    def wrapped(*refs):
        rest = refs[n_in + n_cast:]
        body(*refs[:n_in], *rest[:n_out], *rest[n_out + n_cast:])
        for src_ref, dst_ref in zip(refs[n_in:n_in + n_cast], rest[n_out:n_out + n_cast]):
            dst_ref[...] = src_ref[...].astype(bf16)
    return wrapped


def _in_even(x, gain, w, wgb, bg, *, tm, cast=()):
    m, d = x.shape
    steps = m // tm
    nq = 2 * GLA_HEADS * GLA_DK + 2 * GLA_HEADS * GLA_DV
    nu = POOL_GW * len(POOL_WINDOWS)
    nk = GLA_HEADS * GLA_DK
    assert w.shape == (d, nq + GATE_RANK + nu)
    const = lambda i: (0, 0)
    tn = 512
    w_lanes = -(-w.shape[1] // LANES) * LANES
    cast_args, cast_in, cast_out, cast_shapes, cast_bytes = _cast_rider(cast, steps)
    pipelined = tm * (d * F32_BYTES + nq * BF16_BYTES + nk * F32_BYTES + nu * F32_BYTES) + cast_bytes
    resident = (d * w_lanes * F32_BYTES + (2 * wgb.size + tm * d + d * (nq + LANES) + d * nu) * BF16_BYTES
                + 2 * tm * tn * F32_BYTES)
    return pl.pallas_call(
        _with_cast_rider(functools.partial(_in_even_body, tn=tn), 5, 3, len(cast)),
        grid=(steps,),
        in_specs=[
            pl.BlockSpec((tm, d), lambda i: (i, 0)),
            pl.BlockSpec((1, d), const),
            pl.BlockSpec(w.shape, const, pipeline_mode=pl.Buffered(1)),
            pl.BlockSpec(wgb.shape, const),
            pl.BlockSpec((1, nk), const),
        ] + cast_in,
        out_specs=[
            pl.BlockSpec((tm, nq), lambda i: (i, 0)),
            pl.BlockSpec((tm, nk), lambda i: (i, 0)),
            pl.BlockSpec((tm, nu), lambda i: (i, 0)),
        ] + cast_out,
        out_shape=[
            jax.ShapeDtypeStruct((m, nq), bf16),
            jax.ShapeDtypeStruct((m, nk), f32),
            jax.ShapeDtypeStruct((m, nu), f32),
        ] + cast_shapes,
        scratch_shapes=[pltpu.VMEM((tm, d), bf16), pltpu.VMEM((d, nq + LANES), bf16), pltpu.VMEM((d, nu), bf16)],
        compiler_params=_params(("arbitrary",), pipelined, resident),
        name="in_even",
    )(x, gain, w, wgb, bg, *cast_args)


def _gla_pool_prompt_body(qkvg_ref, loga_ref, u_ref, x_ref, tril_ref, gain_ref, pw_ref, ps_ref, wout_ref,
                          xo_ref, so_ref, st_ref, o_ref, e_ref, p_ref, q_ref, op_ref):
    t = x_ref.shape[0]
    ck = GLA_CHUNK
    kw = GLA_HEADS * GLA_DK
    vw = GLA_HEADS * GLA_DV
    pair_w = 2 * GLA_DK
    i = pl.program_id(1)

    @pl.when(i == 0)
    def _():
        st_ref[...] = jnp.zeros_like(st_ref)
        e_ref[0:POOL_HIST, :] = jnp.zeros((POOL_HIST, e_ref.shape[1]), f32)

    tril = tril_ref[...]
    row = lax.broadcasted_iota(jnp.int32, (2 * ck, pair_w), 0)
    lane = lax.broadcasted_iota(jnp.int32, (2 * ck, pair_w), 1)
    first_lanes = lane < GLA_DK
    first_lanes_ck = lax.broadcasted_iota(jnp.int32, (ck, pair_w), 1) < GLA_DK
    same_head = (row < ck) == first_lanes
    causal = same_head & ((row % ck) >= (lane % GLA_DK))
    pairs = range(GLA_HEADS // 2)
    chunks = range(t // ck)

    hist = POOL_HIST
    n = t + hist
    gw = POOL_GW
    u = u_ref[...]
    e_ref[hist:n, :] = u
    p_ref[8:n, :] = e_ref[8:n, :] + e_ref[7:n - 1, :]
    q_ref[16:n, gw:] = p_ref[16:n, gw:] + p_ref[14:n - 2, gw:]
    p_ref[24:n, 2 * gw:] = q_ref[24:n, 2 * gw:] + q_ref[20:n - 4, 2 * gw:]
    q_ref[32:n, 3 * gw:] = p_ref[32:n, 3 * gw:] + p_ref[24:n - 8, 3 * gw:]

    def rows_of(c):
        return slice(c * ck, (c + 1) * ck)

    def v_pair(c, p):
        va = qkvg_ref[rows_of(c), 2 * kw + (2 * p) * GLA_DV:2 * kw + (2 * p + 1) * GLA_DV]
        vb = qkvg_ref[rows_of(c), 2 * kw + (2 * p + 1) * GLA_DV:2 * kw + (2 * p + 2) * GLA_DV]
        return va, vb

    bcs = []
    for c in chunks:
        la = loga_ref[rows_of(c), :]
        la_hi = la.astype(bf16)
        la_lo = (la - la_hi.astype(f32)).astype(bf16)
        bcs.append(jnp.dot(tril, la_hi, preferred_element_type=f32) + jnp.dot(tril, la_lo, preferred_element_type=f32))
    lhs_q, ke2, kds, elast = [], [], [], []
    for c in chunks:
        bc = bcs[c]
        blast = bc[ck - 1:ck, :]
        q = qkvg_ref[rows_of(c), 0:kw].astype(f32) * (GLA_DK ** -0.5)
        k = qkvg_ref[rows_of(c), kw:2 * kw].astype(f32)
        qe = q * jnp.exp(bc)
        ke = (k * jnp.exp(-bc)).astype(bf16)
        kds.append((k * jnp.exp(blast - bc)).astype(bf16))
        elast.append(jnp.exp(blast))
        for p in pairs:
            pl_ = slice(p * pair_w, (p + 1) * pair_w)
            qe_p = qe[:, pl_]
            lhs_q.append(jnp.concatenate([jnp.where(first_lanes_ck, qe_p, 0.0),
                                          jnp.where(first_lanes_ck, 0.0, qe_p)], axis=0).astype(bf16))
            ke2.append(jnp.concatenate([ke[:, pl_], ke[:, pl_]], axis=0))
    att, upd = [], []
    for c in chunks:
        for p in pairs:
            idx = c * len(pairs) + p
            a = lax.dot_general(lhs_q[idx], ke2[idx], NT_DIMS, preferred_element_type=f32)
            att.append(jnp.where(causal, a, 0.0).astype(bf16))
            va, vb = v_pair(c, p)
            r = lax.dot_general(jnp.concatenate([va, vb], axis=1), kds[c][:, p * pair_w:(p + 1) * pair_w], TN_DIMS,
                                preferred_element_type=f32)
            upd.append(jnp.where(first_lanes, r[:GLA_DV], r[GLA_DV:]))
    st = [st_ref[p] for p in pairs]
    for c in chunks:
        for p in pairs:
            idx = c * len(pairs) + p
            va, vb = v_pair(c, p)
            o = lax.dot_general(lhs_q[idx], st[p].astype(bf16), NT_DIMS, preferred_element_type=f32)
            o = o + jnp.dot(att[idx], jnp.concatenate([va, vb], axis=0), preferred_element_type=f32)
            o_ref[rows_of(c), (2 * p) * GLA_DV:(2 * p + 1) * GLA_DV] = o[:ck]
            o_ref[rows_of(c), (2 * p + 1) * GLA_DV:(2 * p + 2) * GLA_DV] = o[ck:]
            st[p] = st[p] * elast[c][:, p * pair_w:(p + 1) * pair_w] + upd[idx]
    for p in pairs:
        st_ref[p] = st[p]

    sums = (p_ref, q_ref, p_ref, q_ref)
    pos = i * t + lax.broadcasted_iota(jnp.int32, (t, 1), 0)
    for gi, w in enumerate(POOL_WINDOWS):
        ls = slice(gi * gw, (gi + 1) * gw)
        cnt = jnp.minimum(w, pos + 1).astype(f32)
        pooled = (sums[gi][hist:n, ls] / cnt - u[:, ls]).astype(bf16)
        pg = jnp.dot(pooled, pw_ref[gi], preferred_element_type=f32) * ps_ref[:, ls]
        op_ref[:, vw + gi * gw:vw + (gi + 1) * gw] = pg.astype(bf16)
    e_ref[hist - POOL_TAIL:hist, :] = e_ref[n - POOL_TAIL:n, :]

    piece = 2 * GLA_DV
    y = x_ref[...]
    for c0 in (vw, vw + piece):
        y = y + jnp.dot(op_ref[:, c0:c0 + piece], wout_ref[c0:c0 + piece, :], preferred_element_type=f32)
    gain = gain_ref[...]
    for p in pairs:
        for h in (2 * p, 2 * p + 1):
            hs = slice(h * GLA_DV, (h + 1) * GLA_DV)
            g = qkvg_ref[:, 2 * kw + vw + h * GLA_DV:2 * kw + vw + (h + 1) * GLA_DV].astype(f32)
            op_ref[:, hs] = (_rms(o_ref[:, hs], gain) * _silu(g)).astype(bf16)
        c0 = p * piece
        y = y + jnp.dot(op_ref[:, c0:c0 + piece], wout_ref[c0:c0 + piece, :], preferred_element_type=f32)
    xo_ref[...] = y

    @pl.when(i == pl.num_programs(1) - 1)
    def _():
        for p in range(GLA_HEADS // 2):
            s_pair = st_ref[p].T
            so_ref[0, 2 * p] = s_pair[:GLA_DK]
            so_ref[0, 2 * p + 1] = s_pair[GLA_DK:]


def _gla_pool_prompt(qkvg, loga, u, x, tril, gain, pw, ps, wout, *, batch, t):
    m, d = x.shape
    nt = m // batch // t
    row = lambda b, i: (b * nt + i, 0)
    const2 = lambda b, i: (0, 0)
    vw = GLA_HEADS * GLA_DV
    uw = u.shape[1]
    pipelined = t * (qkvg.shape[1] * BF16_BYTES + (loga.shape[1] + uw + 2 * d) * F32_BYTES)
    scratch = (t * vw + 3 * (POOL_HIST + t) * uw) * F32_BYTES + t * (vw + uw) * BF16_BYTES
    waves = (t // GLA_CHUNK) * (GLA_HEADS // 2) * (3 * LANES * LANES * BF16_BYTES + LANES * LANES * F32_BYTES)
    resident = 2 * (wout.size + pw.size) * BF16_BYTES + scratch + waves + t * d * F32_BYTES
    return pl.pallas_call(
        _gla_pool_prompt_body,
        grid=(batch, nt),
        in_specs=[
            pl.BlockSpec((t, qkvg.shape[1]), row),
            pl.BlockSpec((t, loga.shape[1]), row),
            pl.BlockSpec((t, uw), row),
            pl.BlockSpec((t, d), row),
            pl.BlockSpec(tril.shape, const2),
            pl.BlockSpec(gain.shape, const2),
            pl.BlockSpec(pw.shape, lambda b, i: (0, 0, 0)),
            pl.BlockSpec(ps.shape, const2),
            pl.BlockSpec(wout.shape, const2),
        ],
        out_specs=[
            pl.BlockSpec((t, d), row),
            pl.BlockSpec((1, GLA_HEADS, GLA_DK, GLA_DV), lambda b, i: (b, 0, 0, 0)),
        ],
        out_shape=[
            jax.ShapeDtypeStruct((m, d), f32),
            jax.ShapeDtypeStruct((batch, GLA_HEADS, GLA_DK, GLA_DV), f32),
        ],
        scratch_shapes=[
            pltpu.VMEM((GLA_HEADS // 2, GLA_DV, 2 * GLA_DK), f32),
            pltpu.VMEM((t, vw), f32),
            pltpu.VMEM((POOL_HIST + t, uw), f32),
            pltpu.VMEM((POOL_HIST + t, uw), f32),
            pltpu.VMEM((POOL_HIST + t, uw), f32),
            pltpu.VMEM((t, vw + uw), bf16),
        ],
        compiler_params=_params(("arbitrary", "arbitrary"), pipelined, resident),
        name="gla_pool_prompt",
    )(qkvg, loga, u, x, tril, gain, pw, ps, wout)


def _gla_pool_sample_body(qkvg_ref, loga_ref, u_ref, s_ref, buf_ref, gain_ref, pw_ref, ps_ref,
                          op_ref, so_ref):
    bb = u_ref.shape[0]
    kw = GLA_HEADS * GLA_DK
    vw = GLA_HEADS * GLA_DV
    gain = gain_ref[...]
    qkvg = qkvg_ref[...].astype(f32)
    alpha = jnp.exp(loga_ref[...])
    qs = qkvg[:, 0:kw] * (GLA_DK ** -0.5)
    k = qkvg[:, kw:2 * kw]

    def column(row):
        return jnp.broadcast_to(row, (LANES, kw)).T

    o_rows = []
    for b in range(bb):
        acol = column(alpha[b:b + 1, :])
        qcol = column(qs[b:b + 1, :])
        kcol = column(k[b:b + 1, :])
        o_heads = []
        for h in range(GLA_HEADS):
            ks = slice(h * GLA_DK, (h + 1) * GLA_DK)
            v = qkvg[b:b + 1, 2 * kw + h * GLA_DV:2 * kw + (h + 1) * GLA_DV]
            s_new = acol[ks, :] * s_ref[b, h] + kcol[ks, :] * v
            so_ref[b, h] = s_new
            o = jnp.sum(qcol[ks, :] * s_new, axis=0, keepdims=True)
            g = qkvg[b:b + 1, 2 * kw + vw + h * GLA_DV:2 * kw + vw + (h + 1) * GLA_DV]
            o_heads.append(_rms(o, gain) * _silu(g))
        o_rows.append(jnp.concatenate(o_heads, axis=1))
    op_ref[:, 0:vw] = jnp.concatenate(o_rows, axis=0).astype(bf16)

    u = u_ref[...]
    for gi, w in enumerate(POOL_WINDOWS):
        ls = slice(gi * POOL_GW, (gi + 1) * POOL_GW)
        s = u[:, ls] + jnp.sum(buf_ref[:, POOL_BUF - (w - 1):POOL_BUF, ls], axis=1)
        cnt = float(min(w, PAST_LEN + 1))
        pooled = (s / cnt - u[:, ls]).astype(bf16)
        pg = jnp.dot(pooled, pw_ref[gi], preferred_element_type=f32) * ps_ref[:, ls]
        op_ref[:, vw + gi * POOL_GW:vw + (gi + 1) * POOL_GW] = pg.astype(bf16)


def _gla_pool_sample(qkvg, loga, u, s, buf, gain, pw, ps, *, bb):
    n = u.shape[0]
    row = lambda i: (i, 0)
    const2 = lambda i: (0, 0)
    ow = GLA_HEADS * GLA_DV + POOL_GW * len(POOL_WINDOWS)
    state_rows = int(np.prod(s.shape[1:]))
    pipelined = bb * ((qkvg.shape[1] + ow) * BF16_BYTES
                      + (loga.shape[1] + u.shape[1] + 2 * state_rows + POOL_TAIL * buf.shape[2]) * F32_BYTES)
    resident = 2 * pw.size * BF16_BYTES + 3 * LANES * loga.shape[1] * F32_BYTES
    return pl.pallas_call(
        _gla_pool_sample_body,
        grid=(n // bb,),
        in_specs=[
            pl.BlockSpec((bb, qkvg.shape[1]), row),
            pl.BlockSpec((bb, loga.shape[1]), row),
            pl.BlockSpec((bb, u.shape[1]), row),
            pl.BlockSpec((bb,) + s.shape[1:], lambda i: (i, 0, 0, 0)),
            pl.BlockSpec((bb,) + buf.shape[1:], lambda i: (i, 0, 0)),
            pl.BlockSpec(gain.shape, const2),
            pl.BlockSpec(pw.shape, lambda i: (0, 0, 0)),
            pl.BlockSpec(ps.shape, const2),
        ],
        out_specs=[
            pl.BlockSpec((bb, ow), row),
            pl.BlockSpec((bb,) + s.shape[1:], lambda i: (i, 0, 0, 0)),
        ],
        out_shape=[
            jax.ShapeDtypeStruct((n, ow), bf16),
            jax.ShapeDtypeStruct(s.shape, f32),
        ],
        compiler_params=_params(("arbitrary",), pipelined, resident),
        name="gla_pool_sample",
    )(qkvg, loga, u, s, buf, gain, pw, ps)


def _proj_res_body(x_ref, a_ref, w_ref, o_ref):
    o_ref[...] = x_ref[...] + jnp.dot(a_ref[...], w_ref[...], preferred_element_type=f32)


def _proj_res(x, a, w, *, tm):
    m, d = x.shape
    return pl.pallas_call(
        _proj_res_body,
        grid=(m // tm,),
        in_specs=[
            pl.BlockSpec((tm, d), lambda i: (i, 0)),
            pl.BlockSpec((tm, a.shape[1]), lambda i: (i, 0)),
            pl.BlockSpec(w.shape, lambda i: (0, 0)),
        ],
        out_specs=pl.BlockSpec((tm, d), lambda i: (i, 0)),
        out_shape=jax.ShapeDtypeStruct((m, d), f32),
        compiler_params=_params(("arbitrary",), tm * (2 * d * F32_BYTES + a.shape[1] * BF16_BYTES),
                                2 * w.size * BF16_BYTES),
        name="proj_res",
    )(x, a, w)


def _ret_token_pieces(q_ref, k_ref, v_ref, g_ref, s_ref, og_ref, so_ref, gamma):
    def piece(j, h):
        def run():
            ks = slice(h * RET_DK, (h + 1) * RET_DK)
            vs = slice(h * RET_DV, (h + 1) * RET_DV)
            qcol = jnp.broadcast_to(q_ref[j, :, ks].astype(f32), (LANES, RET_DK)).T
            kcol = jnp.broadcast_to(k_ref[j, :, ks].astype(f32), (LANES, RET_DK)).T
            v = v_ref[j, :, vs].astype(f32)
            g = g_ref[j, :, vs].astype(f32)
            o_tiles = []
            for t in range(RET_DV // LANES):
                cs = slice(t * LANES, (t + 1) * LANES)
                s_new = gamma[h] * s_ref[j, h, :, cs] + kcol * v[:, cs]
                so_ref[j, h, :, cs] = s_new
                o_tiles.append(jnp.sum(qcol * s_new, axis=0, keepdims=True))
            o = jnp.concatenate(o_tiles, axis=1)
            og_ref[j, :, vs] = (_rms(o) * _silu(g)).astype(bf16)
            return o
        return run

    return [piece(j, h) for j in range(s_ref.shape[0]) for h in range(RET_HEADS)]


def _ffn_body(*refs, tf, n_sub, final_norm, rider_gamma):
    x_ref, gain_ref, wg_ref, wu_ref, wd_ref, fgain_ref = refs[:6]
    pieces = []
    if rider_gamma is None:
        o_ref, h_ref, acc_ref = refs[6:]
    else:
        rq_ref, rk_ref, rv_ref, rg_ref, rs_ref, o_ref, rog_ref, rso_ref, h_ref, acc_ref = refs[6:]
        pieces = _ret_token_pieces(rq_ref, rk_ref, rv_ref, rg_ref, rs_ref, rog_ref, rso_ref, rider_gamma)
    n_chunks = wg_ref.shape[1] // tf
    bounds = [n_chunks * s // n_sub for s in range(n_sub + 1)]

    def exact_zero(v):
        bits = lax.bitcast_convert_type(v, jnp.uint32)
        return ((bits >> 16) >> 16).astype(f32)

    def run_chunks(chunks):
        pin = None
        for n, c in enumerate(chunks):
            cs = slice(c * tf, (c + 1) * tf)
            g = jnp.dot(h_ref[...], wg_ref[:, cs], preferred_element_type=f32)
            if pin is not None:
                g = g + pin
                pin = None
            u = jnp.dot(h_ref[...], wu_ref[:, cs], preferred_element_type=f32)
            a = (_silu(g) * u).astype(bf16)
            part = jnp.dot(a, wd_ref[cs, :], preferred_element_type=f32)
            if c == 0:
                acc_ref[...] = part
            else:
                acc_ref[...] += part
            for p in range(len(pieces)):
                if p * (len(chunks) - 1) // len(pieces) == n:
                    z = exact_zero(pieces[p]()[:, :tf])
                    pin = z if pin is None else pin + z

    def sub_step(s):
        if s == 0:
            h_ref[...] = _rms(x_ref[...], gain_ref[...]).astype(bf16)
        run_chunks(range(bounds[s], bounds[s + 1]))
        if s == n_sub - 1:
            y = x_ref[...] + acc_ref[...]
            if final_norm:
                y = _rms(y, fgain_ref[...])
            o_ref[...] = y

    if n_sub == 1:
        sub_step(0)
    else:
        for s in range(n_sub):
            pl.when(pl.program_id(1) == s)(functools.partial(sub_step, s))


def _ffn(x, gain, wg, wu, wd, fgain, *, layer, tm, tf, final_norm, rider=None):
    m, d = x.shape
    ff = wg.shape[1]
    steps = m // tm
    n_sub = 1 if rider is None else 2
    single_buffered = dict(pipeline_mode=pl.Buffered(1))
    in_specs = [
        pl.BlockSpec((tm, d), lambda i, s: (i, 0)),
        pl.BlockSpec((None, 1, d), lambda i, s: (layer, 0, 0)),
        pl.BlockSpec((d, ff), lambda i, s: (0, 0), **single_buffered),
        pl.BlockSpec((d, ff), lambda i, s: (0, 0), **single_buffered),
        pl.BlockSpec((ff, d), lambda i, s: (0, 0), **single_buffered),
        pl.BlockSpec((1, d), lambda i, s: (0, 0)),
    ]
    args = [x, gain, wg, wu, wd, fgain]
    out_specs = [pl.BlockSpec((tm, d), lambda i, s: (i, 0))]
    out_shape = [jax.ShapeDtypeStruct((m, d), f32)]
    gamma = None
    pipelined = 2 * tm * d * F32_BYTES
    resident = (3 * d * ff * BF16_BYTES + tm * d * (BF16_BYTES + F32_BYTES)
                + 3 * tm * tf * F32_BYTES + tm * d * F32_BYTES)
    if rider is not None:
        qkvg3, state, rows, gamma = rider
        assert 2 * steps * rows == state.shape[0]
        qw = RET_HEADS * RET_DK
        vw = RET_HEADS * RET_DV
        blk = lambda col: (lambda i, s: (2 * i + s, 0, col))
        state_spec = pl.BlockSpec((rows,) + state.shape[1:], lambda i, s: (2 * i + s, 0, 0, 0))
        in_specs += [
            pl.BlockSpec((rows, 1, qw), blk(0)),
            pl.BlockSpec((rows, 1, qw), blk(1)),
            pl.BlockSpec((rows, 1, vw), blk(1)),
            pl.BlockSpec((rows, 1, vw), blk(2)),
            state_spec,
        ]
        args += [qkvg3, qkvg3, qkvg3, qkvg3, state]
        out_specs += [pl.BlockSpec((rows, 1, vw), blk(0)), state_spec]
        out_shape += [
            jax.ShapeDtypeStruct((state.shape[0], 1, vw), bf16),
            jax.ShapeDtypeStruct(state.shape, f32),
        ]
        pipelined += 2 * rows * int(np.prod(state.shape[1:])) * F32_BYTES
    out = pl.pallas_call(
        functools.partial(_ffn_body, tf=tf, n_sub=n_sub, final_norm=final_norm, rider_gamma=gamma),
        grid=(steps, n_sub),
        in_specs=in_specs,
        out_specs=out_specs,
        out_shape=out_shape,
        scratch_shapes=[pltpu.VMEM((tm, d), bf16), pltpu.VMEM((tm, d), f32)],
        compiler_params=_params(("arbitrary", "arbitrary"), pipelined, resident),
        name="ffn_final" if final_norm else "ffn",
    )(*args)
    return out[0] if rider is None else out


def _in_odd_body(x_ref, gain_ref, w_ref, perm_ref, cos_ref, sin_ref, qsc_ref, ksc_ref, o_ref, h_ref, wqk_ref, *,
                 tn, split_halves):
    qw = RET_HEADS * RET_DK
    half = RET_DK // 2
    if split_halves:
        @pl.when(pl.program_id(0) == 0)
        def _():
            for hh in range(2 * RET_HEADS):
                hs = slice(hh * RET_DK, (hh + 1) * RET_DK)
                wqk_ref[:, hs] = jnp.dot(w_ref[:, hs], perm_ref[...], preferred_element_type=f32).astype(bf16)

    h_ref[...] = _rms(x_ref[...], gain_ref[...]).astype(bf16)
    cos = cos_ref[...]
    sin = sin_ref[...]
    for c in range(2 * qw // tn):
        c0 = c * tn
        w_chunk = wqk_ref[:, c0:c0 + tn] if split_halves else w_ref[:, c0:c0 + tn]
        p = jnp.dot(h_ref[...], w_chunk, preferred_element_type=f32)
        sc_ref = qsc_ref if c0 < qw else ksc_ref
        for hh in range(tn // RET_DK):
            h0 = hh * RET_DK
            head = (c0 % qw + h0) // RET_DK
            sc = sc_ref[:, head * LANES:(head + 1) * LANES]
            if split_halves:
                ev = p[:, h0:h0 + half]
                od = p[:, h0 + half:h0 + RET_DK]
                o_ref[:, c0 + h0:c0 + h0 + half] = ((ev * cos - od * sin) * sc).astype(bf16)
                o_ref[:, c0 + h0 + half:c0 + h0 + RET_DK] = ((od * cos + ev * sin) * sc).astype(bf16)
            else:
                xh = p[:, h0:h0 + RET_DK]
                even = lax.broadcasted_iota(jnp.int32, xh.shape, 1) % 2 == 0
                partner = jnp.where(even, pltpu.roll(xh, RET_DK - 1, 1), pltpu.roll(xh, 1, 1))
                r = xh * cos + partner * sin
                o_ref[:, c0 + h0:c0 + h0 + half] = (r[:, :half] * sc).astype(bf16)
                o_ref[:, c0 + h0 + half:c0 + h0 + RET_DK] = (r[:, half:] * sc).astype(bf16)
    for c0 in range(2 * qw, w_ref.shape[1], tn):
        p = jnp.dot(h_ref[...], w_ref[:, c0:c0 + tn], preferred_element_type=f32)
        o_ref[:, c0:c0 + tn] = p.astype(bf16)


def _in_odd(x, gain, w, perm, tables, *, layer, tm, tn, split_halves, cast=()):
    m, d = x.shape
    n = w.shape[1]
    cos, sin, qsc, ksc = tables
    ntab = cos.shape[0] // tm
    qkw = 2 * RET_HEADS * RET_DK
    single_buffered = dict(pipeline_mode=pl.Buffered(1))
    const = lambda i: (0, 0)
    rope_spec = pl.BlockSpec((tm, cos.shape[1]), lambda i: (i % ntab, 0))
    wqk_shape = (d, qkw) if split_halves else (8, LANES)
    cast_args, cast_in, cast_out, cast_shapes, cast_bytes = _cast_rider(cast, m // tm)
    pipelined = tm * (d * F32_BYTES + n * BF16_BYTES + 2 * cos.shape[1] * F32_BYTES) + cast_bytes
    resident = ((w.size + 2 * perm.size + tm * d + wqk_shape[0] * wqk_shape[1]) * BF16_BYTES
                + 2 * (qsc.size + ksc.size) * F32_BYTES + 2 * tm * tn * F32_BYTES)
    outs = pl.pallas_call(
        _with_cast_rider(functools.partial(_in_odd_body, tn=tn, split_halves=split_halves), 8, 1, len(cast)),
        grid=(m // tm,),
        in_specs=[
            pl.BlockSpec((tm, d), lambda i: (i, 0)),
            pl.BlockSpec((None, 1, d), lambda i: (layer, 0, 0)),
            pl.BlockSpec(w.shape, const, **single_buffered),
            pl.BlockSpec(perm.shape, const),
            rope_spec, rope_spec,
            pl.BlockSpec(qsc.shape, const),
            pl.BlockSpec(ksc.shape, const),
        ] + cast_in,
        out_specs=[pl.BlockSpec((tm, n), lambda i: (i, 0))] + cast_out,
        out_shape=[jax.ShapeDtypeStruct((m, n), bf16)] + cast_shapes,
        scratch_shapes=[pltpu.VMEM((tm, d), bf16), pltpu.VMEM(wqk_shape, bf16)],
        compiler_params=_params(("arbitrary",), pipelined, resident),
        name="in_odd",
    )(x, gain, w, perm, cos, sin, qsc, ksc, *cast_args)
    return tuple(outs) if cast else outs[0]


def _ret_prompt_body(q_ref, k_ref, v_ref, g_ref, x_ref, wout_ref, xo_ref, so_ref, s_ref, sb_ref, slab_ref, *,
                     gamma_c, n):
    c = pl.program_id(1)
    subs = [slice(j * n, (j + 1) * n) for j in range(q_ref.shape[0] // n)]

    @pl.when(c == 0)
    def _():
        s_ref[...] = jnp.zeros_like(s_ref)
        sb_ref[...] = jnp.zeros_like(sb_ref)

    causal = lax.broadcasted_iota(jnp.int32, (n, n), 0) >= lax.broadcasted_iota(jnp.int32, (n, n), 1)
    heads = range(RET_HEADS)
    ks = [slice(h * RET_DK, (h + 1) * RET_DK) for h in heads]
    vs = [slice(h * RET_DV, (h + 1) * RET_DV) for h in heads]
    att = [[jnp.where(causal, lax.dot_general(q_ref[r, ks[h]], k_ref[r, ks[h]], NT_DIMS,
                                              preferred_element_type=f32), 0.0).astype(bf16) for h in heads]
           for r in subs]
    o = []
    for j, r in enumerate(subs):
        o.append([jnp.dot(q_ref[r, ks[h]], sb_ref[h], preferred_element_type=f32)
                  + jnp.dot(att[j][h], v_ref[r, vs[h]], preferred_element_type=f32) for h in heads])
        for h in heads:
            kv = lax.dot_general(k_ref[r, ks[h]], v_ref[r, vs[h]], TN_DIMS, preferred_element_type=f32)
            s_new = gamma_c[h] * (s_ref[h] + kv)
            s_ref[h] = s_new
            sb_ref[h] = s_new.astype(bf16)
    for j, r in enumerate(subs):
        y = x_ref[r, :]
        for h in heads:
            og = (_rms(o[j][h]) * _silu(g_ref[r, vs[h]].astype(f32))).astype(bf16)
            y = y + jnp.dot(og, wout_ref[vs[h], :], preferred_element_type=f32)
        xo_ref[r, :] = y

    @pl.when(c == pl.num_programs(1) - 1)
    def _():
        half = RET_DK // 2
        for h in range(RET_HEADS):
            for t in range(RET_DV // LANES):
                ls = slice(t * LANES, (t + 1) * LANES)
                slab_ref[pl.ds(0, half, stride=2), :] = s_ref[h, 0:half, ls]
                slab_ref[pl.ds(1, half, stride=2), :] = s_ref[h, half:RET_DK, ls]
                so_ref[0, h, :, ls] = slab_ref[...]


def _ret_prompt(qkvg, x, wout, gamma_c, *, batch, c, chunk):
    m, d = x.shape
    nc = m // batch // c
    qw = RET_HEADS * RET_DK
    vw = RET_HEADS * RET_DV
    assert c % chunk == 0
    state = RET_HEADS * RET_DK * RET_DV
    pipelined = c * (2 * (qw + vw) * BF16_BYTES + 2 * d * F32_BYTES) + state * F32_BYTES
    resident = (wout.size * BF16_BYTES + state * (F32_BYTES + BF16_BYTES) + RET_DK * LANES * F32_BYTES
                + c * RET_HEADS * (chunk * BF16_BYTES + RET_DV * F32_BYTES)
                + RET_DK * RET_DV * F32_BYTES + c * d * F32_BYTES)
    return pl.pallas_call(
        functools.partial(_ret_prompt_body, gamma_c=gamma_c, n=chunk),
        grid=(batch, nc),
        in_specs=[
            pl.BlockSpec((c, qw), lambda b, i: (b * nc + i, 0)),
            pl.BlockSpec((c, qw), lambda b, i: (b * nc + i, 1)),
            pl.BlockSpec((c, vw), lambda b, i: (b * nc + i, 1)),
            pl.BlockSpec((c, vw), lambda b, i: (b * nc + i, 2)),
            pl.BlockSpec((c, d), lambda b, i: (b * nc + i, 0)),
            pl.BlockSpec(wout.shape, lambda b, i: (0, 0), pipeline_mode=pl.Buffered(1)),
        ],
        out_specs=[
            pl.BlockSpec((c, d), lambda b, i: (b * nc + i, 0)),
            pl.BlockSpec((1, RET_HEADS, RET_DK, RET_DV), lambda b, i: (b, 0, 0, 0)),
        ],
        out_shape=[
            jax.ShapeDtypeStruct((m, d), f32),
            jax.ShapeDtypeStruct((batch, RET_HEADS, RET_DK, RET_DV), f32),
        ],
        scratch_shapes=[
            pltpu.VMEM((RET_HEADS, RET_DK, RET_DV), f32),
            pltpu.VMEM((RET_HEADS, RET_DK, RET_DV), bf16),
            pltpu.VMEM((RET_DK, LANES), f32),
        ],
        compiler_params=_params(("arbitrary", "arbitrary"), pipelined, resident),
        name="ret_prompt",
    )(qkvg, qkvg, qkvg, qkvg, x, wout)


def _rope_tables(pos, per_pair):
    pair_angle = 1.0 / (ROPE_BASE ** jnp.linspace(0.0, 1.0, RET_DK // 2, dtype=f32))
    if per_pair:
        ang = pos[:, None] * pair_angle[None, :]
        return jnp.cos(ang), jnp.sin(ang)
    ang = pos[:, None] * jnp.repeat(pair_angle, 2)[None, :]
    sign = jnp.where(jnp.arange(RET_DK) % 2 == 0, -1.0, 1.0).astype(f32)
    return jnp.cos(ang), jnp.sin(ang) * sign


def _even_odd_perm():
    half = RET_DK // 2
    src = np.concatenate([2 * np.arange(half), 2 * np.arange(half) + 1])
    perm = np.zeros((RET_DK, RET_DK), np.float32)
    perm[src, np.arange(RET_DK)] = 1.0
    return jnp.asarray(perm, dtype=bf16)


def _lane_replicated(scale):
    return jnp.asarray(np.repeat(scale, LANES, axis=1), dtype=f32)


def _ret_decay(rows, c):
    gam = 1.0 - 2.0 ** (-5.0 - np.arange(RET_HEADS, dtype=np.float64))
    lg = np.log(gam)
    steps = (np.arange(rows) % c + 1.0)[:, None]
    q_scale = _lane_replicated(np.exp(lg[None, :] * steps))
    k_scale = _lane_replicated(np.exp(-lg[None, :] * steps) * RET_DK ** -0.5)
    gamma_c = tuple(float(x) for x in np.exp(lg * c))
    gamma = tuple(float(x) for x in gam)
    return q_scale, k_scale, gamma_c, gamma


def kernel(x_prompt, x_sample, state_gla, state_pool, state_ret, norm_mix, norm_ffn, norm_final, w_in_even,
           w_gate_b, b_gate, gla_gain, pool_w, pool_scale, w_out_even, w_in_odd, w_out_odd, w_ffn_gate,
           w_ffn_up, w_ffn_down):
    batch, seq, d = x_prompt.shape
    n_s = x_sample.shape[0]
    assert norm_mix.shape[0] == 2 and x_sample.shape[1] == 1

    we = w_in_even[0]
    wgb = jnp.concatenate([w_gate_b[0], jnp.zeros((LANES - GATE_RANK, w_gate_b.shape[2]), f32)], axis=0).astype(bf16)
    bg = b_gate[0][None, :]
    gg = gla_gain[0][None, :]
    pw = pool_w[0].astype(bf16)
    ps = pool_scale[0][None, :]
    nm = norm_mix[:, None, :]
    nf = norm_ffn[:, None, :]
    nfin = norm_final[None, :]
    tril = jnp.asarray(np.tril(np.ones((GLA_CHUNK, GLA_CHUNK), np.float32)), dtype=bf16)
    tf = 256
    tm_p = 512
    q_scale, k_scale, gamma_c, gamma = _ret_decay(tm_p, RET_CHUNK)
    tables_p = _rope_tables(jnp.arange(seq, dtype=f32), True) + (q_scale, k_scale)
    tables_s = _rope_tables(jnp.full((n_s,), float(PAST_LEN), f32), False) + (
        _lane_replicated(np.ones((n_s, RET_HEADS))), _lane_replicated(np.full((n_s, RET_HEADS), RET_DK ** -0.5)))
    perm = _even_odd_perm()

    ff = w_ffn_gate.shape[2]
    xp = x_prompt.reshape(batch * seq, d)
    steps_p = batch * seq // tm_p
    ffn_cast = lambda layer: (
        (w_ffn_gate.reshape(-1, ff), d // steps_p, layer * steps_p, steps_p),
        (w_ffn_up.reshape(-1, ff), d // steps_p, layer * steps_p, steps_p),
        (w_ffn_down.reshape(-1, d), 2 * ff // steps_p, layer * steps_p // 2, steps_p // 2))
    whole = lambda w: (w, w.shape[0] // steps_p, 0, steps_p)
    qkvg, loga, u_p, wg0, wu0, wd0, wio, woe, woo = _in_even(
        xp, nm[0], we, wgb, bg, tm=tm_p,
        cast=ffn_cast(0) + (whole(w_in_odd[0]), whole(w_out_even[0]), whole(w_out_odd[0])))
    xp, gla_p = _gla_pool_prompt(qkvg, loga, u_p, xp, tril, gg, pw, ps, woe, batch=batch, t=512)

    xs = x_sample.reshape(n_s, d)
    qkvg_s, loga_s, u_s = _in_even(xs, nm[0], we, wgb, bg, tm=n_s)
    op_s, gla_s = _gla_pool_sample(qkvg_s, loga_s, u_s, state_gla[0], state_pool[0], gg, pw, ps, bb=8)
    xs = _proj_res(xs, op_s, woe, tm=n_s)
    xs = _ffn(xs, nf, wg0, wu0, wd0, nfin, layer=0, tm=n_s, tf=tf, final_norm=False)
    qkvg2_s = _in_odd(xs, nm, wio, perm, tables_s, layer=1, tm=n_s, tn=512, split_halves=False)
    qkvg2_s = qkvg2_s.reshape(n_s, 1, -1)

    rows = n_s // (2 * steps_p)
    xp, og_s, ret_s = _ffn(xp, nf, wg0, wu0, wd0, nfin, layer=0, tm=tm_p, tf=tf, final_norm=False,
                           rider=(qkvg2_s, state_ret[0], rows, gamma))
    qkvg2, wg1, wu1, wd1 = _in_odd(xp, nm, wio, perm, tables_p, layer=1, tm=tm_p, tn=512, split_halves=True,
                                   cast=ffn_cast(1))
    xp, ret_p = _ret_prompt(qkvg2, xp, woo, gamma_c, batch=batch, c=2 * RET_CHUNK, chunk=RET_CHUNK)
    y_prompt = _ffn(xp, nf, wg1, wu1, wd1, nfin, layer=1, tm=tm_p, tf=tf, final_norm=True)
    pool_p = u_p.reshape(batch, seq, -1)[:, seq - POOL_BUF:, :]

    xs = _proj_res(xs, og_s.reshape(n_s, -1), woo, tm=n_s)
    y_sample = _ffn(xs, nf, wg1, wu1, wd1, nfin, layer=1, tm=n_s, tf=tf, final_norm=True)

    pool_s = jnp.concatenate([state_pool[0][:, 1:, :], u_s[:, None, :]], axis=1)

    return (y_prompt.reshape(batch, seq, d), y_sample.reshape(n_s, 1, d),
            gla_p[None], gla_s[None], pool_p[None], pool_s[None], ret_p[None], ret_s[None])
```

```python
import functools

import numpy as np
import jax
import jax.numpy as jnp
from jax import lax
from jax.experimental import pallas as pl
from jax.experimental.pallas import tpu as pltpu

f32 = jnp.float32
bf16 = jnp.bfloat16

EPS = 1e-6
PAST_LEN = 16384
GLA_HEADS, GLA_DK, GLA_DV = 4, 64, 128
GLA_CHUNK = 64
GATE_RANK = 16
GATE_NORMALIZER = 16.0
POOL_WINDOWS = (2, 4, 8, 16)
POOL_GW = 128
POOL_BUF = max(POOL_WINDOWS) - 1
POOL_HIST = 32
POOL_TAIL = 16
RET_HEADS, RET_DK, RET_DV = 4, 256, 512
RET_CHUNK = 256
ROPE_BASE = 10000.0
LANES = 128
MIB = 1024 * 1024
VMEM_COMPILER_ALLOWANCE = 8 * MIB
F32_BYTES, BF16_BYTES = 4, 2

NT_DIMS = (((1,), (1,)), ((), ()))
TN_DIMS = (((0,), (0,)), ((), ()))


def _params(semantics, pipelined_bytes, resident_bytes):
    limit = 2 * pipelined_bytes + resident_bytes + VMEM_COMPILER_ALLOWANCE
    return pltpu.CompilerParams(dimension_semantics=semantics, vmem_limit_bytes=int(limit))


def _rms(x, gain=None):
    y = x * lax.rsqrt(jnp.mean(x * x, axis=-1, keepdims=True) + EPS)
    return y if gain is None else y * gain


def _silu(g):
    return g * jax.nn.sigmoid(g)


def _in_even_body(x_ref, gain_ref, w_ref, wgb_ref, bg_ref, qkvg_ref, loga_ref, u_ref, h_ref, wq_ref, wu_ref, *, tn):
    nq = qkvg_ref.shape[1]
    nu = u_ref.shape[1]

    @pl.when(pl.program_id(0) == 0)
    def _():
        for r0 in range(0, nq + LANES, LANES):
            wq_ref[:, r0:r0 + LANES] = w_ref[r0:r0 + LANES, :].T.astype(bf16)
        for r0 in range(0, nu, LANES):
            wu_ref[:, r0:r0 + LANES] = w_ref[nq + GATE_RANK + r0:nq + GATE_RANK + r0 + LANES, :].T.astype(bf16)

    h_ref[...] = _rms(x_ref[...], gain_ref[...]).astype(bf16)
    a = jnp.dot(h_ref[...], wq_ref[:, nq:nq + LANES], preferred_element_type=f32)
    a = jnp.where(lax.broadcasted_iota(jnp.int32, a.shape, 1) < GATE_RANK, a, 0.0).astype(bf16)
    for c0 in range(0, nq, tn):
        qkvg_ref[:, c0:c0 + tn] = jnp.dot(h_ref[...], wq_ref[:, c0:c0 + tn], preferred_element_type=f32).astype(bf16)
        if c0 == 0:
            z = jnp.dot(a, wgb_ref[...], preferred_element_type=f32) + bg_ref[...]
            loga_ref[...] = (jnp.minimum(z, 0.0) - jnp.log1p(jnp.exp(-jnp.abs(z)))) * (1.0 / GATE_NORMALIZER)
    for c0 in range(0, nu, tn):
        u_ref[:, c0:c0 + tn] = jnp.dot(h_ref[...], wu_ref[:, c0:c0 + tn], preferred_element_type=f32)


def _cast_rider(cast, steps):
    arrays, in_specs, out_specs, out_shapes, nbytes = [], [], [], [], 0
    for arr, rows, first, count in cast:
        assert count <= steps and rows % 16 == 0 and (first + count) * rows <= arr.shape[0]
        cols = arr.shape[1]
        arrays.append(arr)
        in_specs.append(pl.BlockSpec(
            (rows, cols), lambda i, first=first, count=count: (first + jnp.minimum(i, count - 1), 0)))
        out_specs.append(pl.BlockSpec((rows, cols), lambda i, count=count: (jnp.minimum(i, count - 1), 0)))
        out_shapes.append(jax.ShapeDtypeStruct((count * rows, cols), bf16))
        nbytes += rows * cols * (F32_BYTES + BF16_BYTES)
    return arrays, in_specs, out_specs, out_shapes, nbytes


def _with_cast_rider(body, n_in, n_out, n_cast):
    def wrapped(*refs):
        rest = refs[n_in + n_cast:]
        body(*refs[:n_in], *rest[:n_out], *rest[n_out + n_cast:])
        for src_ref, dst_ref in zip(refs[n_in:n_in + n_cast], rest[n_out:n_out + n_cast]):
            dst_ref[...] = src_ref[...].astype(bf16)
    return wrapped


def _in_even(x, gain, w, wgb, bg, *, tm, cast=()):
    m, d = x.shape
    steps = m // tm
    nq = 2 * GLA_HEADS * GLA_DK + 2 * GLA_HEADS * GLA_DV
    nu = POOL_GW * len(POOL_WINDOWS)
    nk = GLA_HEADS * GLA_DK
    assert w.shape == (nq + GATE_RANK + nu, d)
    const = lambda i: (0, 0)
    tn = 512
    cast_args, cast_in, cast_out, cast_shapes, cast_bytes = _cast_rider(cast, steps)
    pipelined = tm * (d * F32_BYTES + nq * BF16_BYTES + nk * F32_BYTES + nu * F32_BYTES) + cast_bytes
    resident = (w.size * F32_BYTES + (2 * wgb.size + tm * d + d * (nq + LANES) + d * nu) * BF16_BYTES
                + 2 * tm * tn * F32_BYTES)
    return pl.pallas_call(
        _with_cast_rider(functools.partial(_in_even_body, tn=tn), 5, 3, len(cast)),
        grid=(steps,),
        in_specs=[
            pl.BlockSpec((tm, d), lambda i: (i, 0)),
            pl.BlockSpec((1, d), const),
            pl.BlockSpec(w.shape, const, pipeline_mode=pl.Buffered(1)),
            pl.BlockSpec(wgb.shape, const),
            pl.BlockSpec((1, nk), const),
        ] + cast_in,
        out_specs=[
            pl.BlockSpec((tm, nq), lambda i: (i, 0)),
            pl.BlockSpec((tm, nk), lambda i: (i, 0)),
            pl.BlockSpec((tm, nu), lambda i: (i, 0)),
        ] + cast_out,
        out_shape=[
            jax.ShapeDtypeStruct((m, nq), bf16),
            jax.ShapeDtypeStruct((m, nk), f32),
            jax.ShapeDtypeStruct((m, nu), f32),
        ] + cast_shapes,
        scratch_shapes=[pltpu.VMEM((tm, d), bf16), pltpu.VMEM((d, nq + LANES), bf16), pltpu.VMEM((d, nu), bf16)],
        compiler_params=_params(("arbitrary",), pipelined, resident),
        name="in_even",
    )(x, gain, w, wgb, bg, *cast_args)


def _gla_pool_prompt_body(qkvg_ref, loga_ref, u_ref, x_ref, tril_ref, gain_ref, pw_ref, ps_ref, wout_ref,
                          xo_ref, so_ref, st_ref, o_ref, e_ref, p_ref, q_ref, op_ref):
    t = x_ref.shape[0]
    ck = GLA_CHUNK
    kw = GLA_HEADS * GLA_DK
    vw = GLA_HEADS * GLA_DV
    pair_w = 2 * GLA_DK
    i = pl.program_id(1)

    @pl.when(i == 0)
    def _():
        st_ref[...] = jnp.zeros_like(st_ref)
        e_ref[0:POOL_HIST, :] = jnp.zeros((POOL_HIST, e_ref.shape[1]), f32)

    tril = tril_ref[...]
    row = lax.broadcasted_iota(jnp.int32, (2 * ck, pair_w), 0)
    lane = lax.broadcasted_iota(jnp.int32, (2 * ck, pair_w), 1)
    first_lanes = lane < GLA_DK
    first_lanes_ck = lax.broadcasted_iota(jnp.int32, (ck, pair_w), 1) < GLA_DK
    same_head = (row < ck) == first_lanes
    causal = same_head & ((row % ck) >= (lane % GLA_DK))
    pairs = range(GLA_HEADS // 2)
    chunks = range(t // ck)

    hist = POOL_HIST
    n = t + hist
    gw = POOL_GW
    u = u_ref[...]
    e_ref[hist:n, :] = u
    p_ref[8:n, :] = e_ref[8:n, :] + e_ref[7:n - 1, :]
    q_ref[16:n, gw:] = p_ref[16:n, gw:] + p_ref[14:n - 2, gw:]
    p_ref[24:n, 2 * gw:] = q_ref[24:n, 2 * gw:] + q_ref[20:n - 4, 2 * gw:]
    q_ref[32:n, 3 * gw:] = p_ref[32:n, 3 * gw:] + p_ref[24:n - 8, 3 * gw:]

    def rows_of(c):
        return slice(c * ck, (c + 1) * ck)

    def v_pair(c, p):
        va = qkvg_ref[rows_of(c), 2 * kw + (2 * p) * GLA_DV:2 * kw + (2 * p + 1) * GLA_DV]
        vb = qkvg_ref[rows_of(c), 2 * kw + (2 * p + 1) * GLA_DV:2 * kw + (2 * p + 2) * GLA_DV]
        return va, vb

    bcs = []
    for c in chunks:
        la = loga_ref[rows_of(c), :]
        la_hi = la.astype(bf16)
        la_lo = (la - la_hi.astype(f32)).astype(bf16)
        bcs.append(jnp.dot(tril, la_hi, preferred_element_type=f32) + jnp.dot(tril, la_lo, preferred_element_type=f32))
    lhs_q, ke2, kds, elast = [], [], [], []
    for c in chunks:
        bc = bcs[c]
        blast = bc[ck - 1:ck, :]
        q = qkvg_ref[rows_of(c), 0:kw].astype(f32) * (GLA_DK ** -0.5)
        k = qkvg_ref[rows_of(c), kw:2 * kw].astype(f32)
        qe = q * jnp.exp(bc)
        ke = (k * jnp.exp(-bc)).astype(bf16)
        kds.append((k * jnp.exp(blast - bc)).astype(bf16))
        elast.append(jnp.exp(blast))
        for p in pairs:
            pl_ = slice(p * pair_w, (p + 1) * pair_w)
            qe_p = qe[:, pl_]
            lhs_q.append(jnp.concatenate([jnp.where(first_lanes_ck, qe_p, 0.0),
                                          jnp.where(first_lanes_ck, 0.0, qe_p)], axis=0).astype(bf16))
            ke2.append(jnp.concatenate([ke[:, pl_], ke[:, pl_]], axis=0))
    att, upd = [], []
    for c in chunks:
        for p in pairs:
            idx = c * len(pairs) + p
            a = lax.dot_general(lhs_q[idx], ke2[idx], NT_DIMS, preferred_element_type=f32)
            att.append(jnp.where(causal, a, 0.0).astype(bf16))
            va, vb = v_pair(c, p)
            r = lax.dot_general(jnp.concatenate([va, vb], axis=1), kds[c][:, p * pair_w:(p + 1) * pair_w], TN_DIMS,
                                preferred_element_type=f32)
            upd.append(jnp.where(first_lanes, r[:GLA_DV], r[GLA_DV:]))
    st = [st_ref[p] for p in pairs]
    for c in chunks:
        for p in pairs:
            idx = c * len(pairs) + p
            va, vb = v_pair(c, p)
            o = lax.dot_general(lhs_q[idx], st[p].astype(bf16), NT_DIMS, preferred_element_type=f32)
            o = o + jnp.dot(att[idx], jnp.concatenate([va, vb], axis=0), preferred_element_type=f32)
            o_ref[rows_of(c), (2 * p) * GLA_DV:(2 * p + 1) * GLA_DV] = o[:ck]
            o_ref[rows_of(c), (2 * p + 1) * GLA_DV:(2 * p + 2) * GLA_DV] = o[ck:]
            st[p] = st[p] * elast[c][:, p * pair_w:(p + 1) * pair_w] + upd[idx]
    for p in pairs:
        st_ref[p] = st[p]

    sums = (p_ref, q_ref, p_ref, q_ref)
    pos = i * t + lax.broadcasted_iota(jnp.int32, (t, 1), 0)
    for gi, w in enumerate(POOL_WINDOWS):
        ls = slice(gi * gw, (gi + 1) * gw)
        cnt = jnp.minimum(w, pos + 1).astype(f32)
        pooled = (sums[gi][hist:n, ls] / cnt - u[:, ls]).astype(bf16)
        pg = jnp.dot(pooled, pw_ref[gi], preferred_element_type=f32) * ps_ref[:, ls]
        op_ref[:, vw + gi * gw:vw + (gi + 1) * gw] = pg.astype(bf16)
    e_ref[hist - POOL_TAIL:hist, :] = e_ref[n - POOL_TAIL:n, :]

    piece = 2 * GLA_DV
    y = x_ref[...]
    for c0 in (vw, vw + piece):
        y = y + jnp.dot(op_ref[:, c0:c0 + piece], wout_ref[c0:c0 + piece, :], preferred_element_type=f32)
    gain = gain_ref[...]
    for p in pairs:
        for h in (2 * p, 2 * p + 1):
            hs = slice(h * GLA_DV, (h + 1) * GLA_DV)
            g = qkvg_ref[:, 2 * kw + vw + h * GLA_DV:2 * kw + vw + (h + 1) * GLA_DV].astype(f32)
            op_ref[:, hs] = (_rms(o_ref[:, hs], gain) * _silu(g)).astype(bf16)
        c0 = p * piece
        y = y + jnp.dot(op_ref[:, c0:c0 + piece], wout_ref[c0:c0 + piece, :], preferred_element_type=f32)
    xo_ref[...] = y

    @pl.when(i == pl.num_programs(1) - 1)
    def _():
        for p in range(GLA_HEADS // 2):
            s_pair = st_ref[p].T
            so_ref[0, 2 * p] = s_pair[:GLA_DK]
            so_ref[0, 2 * p + 1] = s_pair[GLA_DK:]


def _gla_pool_prompt(qkvg, loga, u, x, tril, gain, pw, ps, wout, *, batch, t):
    m, d = x.shape
    nt = m // batch // t
    row = lambda b, i: (b * nt + i, 0)
    const2 = lambda b, i: (0, 0)
    vw = GLA_HEADS * GLA_DV
    uw = u.shape[1]
    pipelined = t * (qkvg.shape[1] * BF16_BYTES + (loga.shape[1] + uw + 2 * d) * F32_BYTES)
    scratch = (t * vw + 3 * (POOL_HIST + t) * uw) * F32_BYTES + t * (vw + uw) * BF16_BYTES
    waves = (t // GLA_CHUNK) * (GLA_HEADS // 2) * (3 * LANES * LANES * BF16_BYTES + LANES * LANES * F32_BYTES)
    resident = 2 * (wout.size + pw.size) * BF16_BYTES + scratch + waves + t * d * F32_BYTES
    return pl.pallas_call(
        _gla_pool_prompt_body,
        grid=(batch, nt),
        in_specs=[
            pl.BlockSpec((t, qkvg.shape[1]), row),
            pl.BlockSpec((t, loga.shape[1]), row),
            pl.BlockSpec((t, uw), row),
            pl.BlockSpec((t, d), row),
            pl.BlockSpec(tril.shape, const2),
            pl.BlockSpec(gain.shape, const2),
            pl.BlockSpec(pw.shape, lambda b, i: (0, 0, 0)),
            pl.BlockSpec(ps.shape, const2),
            pl.BlockSpec(wout.shape, const2),
        ],
        out_specs=[
            pl.BlockSpec((t, d), row),
            pl.BlockSpec((1, GLA_HEADS, GLA_DK, GLA_DV), lambda b, i: (b, 0, 0, 0)),
        ],
        out_shape=[
            jax.ShapeDtypeStruct((m, d), f32),
            jax.ShapeDtypeStruct((batch, GLA_HEADS, GLA_DK, GLA_DV), f32),
        ],
        scratch_shapes=[
            pltpu.VMEM((GLA_HEADS // 2, GLA_DV, 2 * GLA_DK), f32),
            pltpu.VMEM((t, vw), f32),
            pltpu.VMEM((POOL_HIST + t, uw), f32),
            pltpu.VMEM((POOL_HIST + t, uw), f32),
            pltpu.VMEM((POOL_HIST + t, uw), f32),
            pltpu.VMEM((t, vw + uw), bf16),
        ],
        compiler_params=_params(("arbitrary", "arbitrary"), pipelined, resident),
        name="gla_pool_prompt",
    )(qkvg, loga, u, x, tril, gain, pw, ps, wout)


def _gla_pool_sample_body(qkvg_ref, loga_ref, u_ref, s_ref, buf_ref, gain_ref, pw_ref, ps_ref,
                          op_ref, so_ref):
    bb = u_ref.shape[0]
    kw = GLA_HEADS * GLA_DK
    vw = GLA_HEADS * GLA_DV
    gain = gain_ref[...]
    qkvg = qkvg_ref[...].astype(f32)
    alpha = jnp.exp(loga_ref[...])
    qs = qkvg[:, 0:kw] * (GLA_DK ** -0.5)
    k = qkvg[:, kw:2 * kw]

    def column(row):
        return jnp.broadcast_to(row, (LANES, kw)).T

    o_rows = []
    for b in range(bb):
        acol = column(alpha[b:b + 1, :])
        qcol = column(qs[b:b + 1, :])
        kcol = column(k[b:b + 1, :])
        o_heads = []
        for h in range(GLA_HEADS):
            ks = slice(h * GLA_DK, (h + 1) * GLA_DK)
            v = qkvg[b:b + 1, 2 * kw + h * GLA_DV:2 * kw + (h + 1) * GLA_DV]
            s_new = acol[ks, :] * s_ref[b, h] + kcol[ks, :] * v
            so_ref[b, h] = s_new
            o = jnp.sum(qcol[ks, :] * s_new, axis=0, keepdims=True)
            g = qkvg[b:b + 1, 2 * kw + vw + h * GLA_DV:2 * kw + vw + (h + 1) * GLA_DV]
            o_heads.append(_rms(o, gain) * _silu(g))
        o_rows.append(jnp.concatenate(o_heads, axis=1))
    op_ref[:, 0:vw] = jnp.concatenate(o_rows, axis=0).astype(bf16)

    u = u_ref[...]
    for gi, w in enumerate(POOL_WINDOWS):
        ls = slice(gi * POOL_GW, (gi + 1) * POOL_GW)
        s = u[:, ls] + jnp.sum(buf_ref[:, POOL_BUF - (w - 1):POOL_BUF, ls], axis=1)
        cnt = float(min(w, PAST_LEN + 1))
        pooled = (s / cnt - u[:, ls]).astype(bf16)
        pg = jnp.dot(pooled, pw_ref[gi], preferred_element_type=f32) * ps_ref[:, ls]
        op_ref[:, vw + gi * POOL_GW:vw + (gi + 1) * POOL_GW] = pg.astype(bf16)


def _gla_pool_sample(qkvg, loga, u, s, buf, gain, pw, ps, *, bb):
    n = u.shape[0]
    row = lambda i: (i, 0)
    const2 = lambda i: (0, 0)
    ow = GLA_HEADS * GLA_DV + POOL_GW * len(POOL_WINDOWS)
    state_rows = int(np.prod(s.shape[1:]))
    pipelined = bb * ((qkvg.shape[1] + ow) * BF16_BYTES
                      + (loga.shape[1] + u.shape[1] + 2 * state_rows + POOL_TAIL * buf.shape[2]) * F32_BYTES)
    resident = 2 * pw.size * BF16_BYTES + 3 * LANES * loga.shape[1] * F32_BYTES
    return pl.pallas_call(
        _gla_pool_sample_body,
        grid=(n // bb,),
        in_specs=[
            pl.BlockSpec((bb, qkvg.shape[1]), row),
            pl.BlockSpec((bb, loga.shape[1]), row),
            pl.BlockSpec((bb, u.shape[1]), row),
            pl.BlockSpec((bb,) + s.shape[1:], lambda i: (i, 0, 0, 0)),
            pl.BlockSpec((bb,) + buf.shape[1:], lambda i: (i, 0, 0)),
            pl.BlockSpec(gain.shape, const2),
            pl.BlockSpec(pw.shape, lambda i: (0, 0, 0)),
            pl.BlockSpec(ps.shape, const2),
        ],
        out_specs=[
            pl.BlockSpec((bb, ow), row),
            pl.BlockSpec((bb,) + s.shape[1:], lambda i: (i, 0, 0, 0)),
        ],
        out_shape=[
            jax.ShapeDtypeStruct((n, ow), bf16),
            jax.ShapeDtypeStruct(s.shape, f32),
        ],
        compiler_params=_params(("arbitrary",), pipelined, resident),
        name="gla_pool_sample",
    )(qkvg, loga, u, s, buf, gain, pw, ps)


def _proj_res_body(x_ref, a_ref, w_ref, o_ref):
    o_ref[...] = x_ref[...] + jnp.dot(a_ref[...], w_ref[...], preferred_element_type=f32)


def _proj_res(x, a, w, *, tm):
    m, d = x.shape
    return pl.pallas_call(
        _proj_res_body,
        grid=(m // tm,),
        in_specs=[
            pl.BlockSpec((tm, d), lambda i: (i, 0)),
            pl.BlockSpec((tm, a.shape[1]), lambda i: (i, 0)),
            pl.BlockSpec(w.shape, lambda i: (0, 0)),
        ],
        out_specs=pl.BlockSpec((tm, d), lambda i: (i, 0)),
        out_shape=jax.ShapeDtypeStruct((m, d), f32),
        compiler_params=_params(("arbitrary",), tm * (2 * d * F32_BYTES + a.shape[1] * BF16_BYTES),
                                2 * w.size * BF16_BYTES),
        name="proj_res",
    )(x, a, w)


def _ret_token_pieces(q_ref, k_ref, v_ref, g_ref, s_ref, og_ref, so_ref, gamma):
    def piece(j, h):
        def run():
            ks = slice(h * RET_DK, (h + 1) * RET_DK)
            vs = slice(h * RET_DV, (h + 1) * RET_DV)
            qcol = jnp.broadcast_to(q_ref[j, :, ks].astype(f32), (LANES, RET_DK)).T
            kcol = jnp.broadcast_to(k_ref[j, :, ks].astype(f32), (LANES, RET_DK)).T
            v = v_ref[j, :, vs].astype(f32)
            g = g_ref[j, :, vs].astype(f32)
            o_tiles = []
            for t in range(RET_DV // LANES):
                cs = slice(t * LANES, (t + 1) * LANES)
                s_new = gamma[h] * s_ref[j, h, :, cs] + kcol * v[:, cs]
                so_ref[j, h, :, cs] = s_new
                o_tiles.append(jnp.sum(qcol * s_new, axis=0, keepdims=True))
            o = jnp.concatenate(o_tiles, axis=1)
            og_ref[j, :, vs] = (_rms(o) * _silu(g)).astype(bf16)
            return o
        return run

    return [piece(j, h) for j in range(s_ref.shape[0]) for h in range(RET_HEADS)]


def _ffn_body(*refs, tf, n_sub, final_norm, rider_gamma):
    x_ref, gain_ref, wg_ref, wu_ref, wd_ref, fgain_ref = refs[:6]
    pieces = []
    if rider_gamma is None:
        o_ref, h_ref, acc_ref = refs[6:]
    else:
        rq_ref, rk_ref, rv_ref, rg_ref, rs_ref, o_ref, rog_ref, rso_ref, h_ref, acc_ref = refs[6:]
        pieces = _ret_token_pieces(rq_ref, rk_ref, rv_ref, rg_ref, rs_ref, rog_ref, rso_ref, rider_gamma)
    n_chunks = wg_ref.shape[1] // tf
    bounds = [n_chunks * s // n_sub for s in range(n_sub + 1)]

    def exact_zero(v):
        bits = lax.bitcast_convert_type(v, jnp.uint32)
        return ((bits >> 16) >> 16).astype(f32)

    def run_chunks(chunks):
        pin = None
        for n, c in enumerate(chunks):
            cs = slice(c * tf, (c + 1) * tf)
            g = jnp.dot(h_ref[...], wg_ref[:, cs], preferred_element_type=f32)
            if pin is not None:
                g = g + pin
                pin = None
            u = jnp.dot(h_ref[...], wu_ref[:, cs], preferred_element_type=f32)
            a = (_silu(g) * u).astype(bf16)
            part = jnp.dot(a, wd_ref[cs, :], preferred_element_type=f32)
            if c == 0:
                acc_ref[...] = part
            else:
                acc_ref[...] += part
            for p in range(len(pieces)):
                if p * (len(chunks) - 1) // len(pieces) == n:
                    z = exact_zero(pieces[p]()[:, :tf])
                    pin = z if pin is None else pin + z

    def sub_step(s):
        if s == 0:
            h_ref[...] = _rms(x_ref[...], gain_ref[...]).astype(bf16)
        run_chunks(range(bounds[s], bounds[s + 1]))
        if s == n_sub - 1:
            y = x_ref[...] + acc_ref[...]
            if final_norm:
                y = _rms(y, fgain_ref[...])
            o_ref[...] = y

    if n_sub == 1:
        sub_step(0)
    else:
        for s in range(n_sub):
            pl.when(pl.program_id(1) == s)(functools.partial(sub_step, s))


def _ffn(x, gain, wg, wu, wd, fgain, *, layer, tm, tf, final_norm, rider=None):
    m, d = x.shape
    ff = wg.shape[1]
    steps = m // tm
    n_sub = 1 if rider is None else 2
    single_buffered = dict(pipeline_mode=pl.Buffered(1))
    in_specs = [
        pl.BlockSpec((tm, d), lambda i, s: (i, 0)),
        pl.BlockSpec((None, 1, d), lambda i, s: (layer, 0, 0)),
        pl.BlockSpec((d, ff), lambda i, s: (0, 0), **single_buffered),
        pl.BlockSpec((d, ff), lambda i, s: (0, 0), **single_buffered),
        pl.BlockSpec((ff, d), lambda i, s: (0, 0), **single_buffered),
        pl.BlockSpec((1, d), lambda i, s: (0, 0)),
    ]
    args = [x, gain, wg, wu, wd, fgain]
    out_specs = [pl.BlockSpec((tm, d), lambda i, s: (i, 0))]
    out_shape = [jax.ShapeDtypeStruct((m, d), f32)]
    gamma = None
    pipelined = 2 * tm * d * F32_BYTES
    resident = (3 * d * ff * BF16_BYTES + tm * d * (BF16_BYTES + F32_BYTES)
                + 3 * tm * tf * F32_BYTES + tm * d * F32_BYTES)
    if rider is not None:
        qkvg3, state, rows, gamma = rider
        assert 2 * steps * rows == state.shape[0]
        qw = RET_HEADS * RET_DK
        vw = RET_HEADS * RET_DV
        blk = lambda col: (lambda i, s: (2 * i + s, 0, col))
        state_spec = pl.BlockSpec((rows,) + state.shape[1:], lambda i, s: (2 * i + s, 0, 0, 0))
        in_specs += [
            pl.BlockSpec((rows, 1, qw), blk(0)),
            pl.BlockSpec((rows, 1, qw), blk(1)),
            pl.BlockSpec((rows, 1, vw), blk(1)),
            pl.BlockSpec((rows, 1, vw), blk(2)),
            state_spec,
        ]
        args += [qkvg3, qkvg3, qkvg3, qkvg3, state]
        out_specs += [pl.BlockSpec((rows, 1, vw), blk(0)), state_spec]
        out_shape += [
            jax.ShapeDtypeStruct((state.shape[0], 1, vw), bf16),
            jax.ShapeDtypeStruct(state.shape, f32),
        ]
        pipelined += 2 * rows * int(np.prod(state.shape[1:])) * F32_BYTES
    out = pl.pallas_call(
        functools.partial(_ffn_body, tf=tf, n_sub=n_sub, final_norm=final_norm, rider_gamma=gamma),
        grid=(steps, n_sub),
        in_specs=in_specs,
        out_specs=out_specs,
        out_shape=out_shape,
        scratch_shapes=[pltpu.VMEM((tm, d), bf16), pltpu.VMEM((tm, d), f32)],
        compiler_params=_params(("arbitrary", "arbitrary"), pipelined, resident),
        name="ffn_final" if final_norm else "ffn",
    )(*args)
    return out[0] if rider is None else out


def _in_odd_body(x_ref, gain_ref, w_ref, perm_ref, cos_ref, sin_ref, qsc_ref, ksc_ref, o_ref, h_ref, wqk_ref, *,
                 tn, split_halves):
    qw = RET_HEADS * RET_DK
    half = RET_DK // 2
    if split_halves:
        @pl.when(pl.program_id(0) == 0)
        def _():
            for hh in range(2 * RET_HEADS):
                hs = slice(hh * RET_DK, (hh + 1) * RET_DK)
                wqk_ref[:, hs] = jnp.dot(w_ref[:, hs], perm_ref[...], preferred_element_type=f32).astype(bf16)

    h_ref[...] = _rms(x_ref[...], gain_ref[...]).astype(bf16)
    cos = cos_ref[...]
    sin = sin_ref[...]
    for c in range(2 * qw // tn):
        c0 = c * tn
        w_chunk = wqk_ref[:, c0:c0 + tn] if split_halves else w_ref[:, c0:c0 + tn]
        p = jnp.dot(h_ref[...], w_chunk, preferred_element_type=f32)
        sc_ref = qsc_ref if c0 < qw else ksc_ref
        for hh in range(tn // RET_DK):
            h0 = hh * RET_DK
            head = (c0 % qw + h0) // RET_DK
            sc = sc_ref[:, head * LANES:(head + 1) * LANES]
            if split_halves:
                ev = p[:, h0:h0 + half]
                od = p[:, h0 + half:h0 + RET_DK]
                o_ref[:, c0 + h0:c0 + h0 + half] = ((ev * cos - od * sin) * sc).astype(bf16)
                o_ref[:, c0 + h0 + half:c0 + h0 + RET_DK] = ((od * cos + ev * sin) * sc).astype(bf16)
            else:
                xh = p[:, h0:h0 + RET_DK]
                even = lax.broadcasted_iota(jnp.int32, xh.shape, 1) % 2 == 0
                partner = jnp.where(even, pltpu.roll(xh, RET_DK - 1, 1), pltpu.roll(xh, 1, 1))
                r = xh * cos + partner * sin
                o_ref[:, c0 + h0:c0 + h0 + half] = (r[:, :half] * sc).astype(bf16)
                o_ref[:, c0 + h0 + half:c0 + h0 + RET_DK] = (r[:, half:] * sc).astype(bf16)
    for c0 in range(2 * qw, w_ref.shape[1], tn):
        p = jnp.dot(h_ref[...], w_ref[:, c0:c0 + tn], preferred_element_type=f32)
        o_ref[:, c0:c0 + tn] = p.astype(bf16)


def _in_odd(x, gain, w, perm, tables, *, layer, tm, tn, split_halves, cast=()):
    m, d = x.shape
    n = w.shape[1]
    cos, sin, qsc, ksc = tables
    ntab = cos.shape[0] // tm
    qkw = 2 * RET_HEADS * RET_DK
    single_buffered = dict(pipeline_mode=pl.Buffered(1))
    const = lambda i: (0, 0)
    rope_spec = pl.BlockSpec((tm, cos.shape[1]), lambda i: (i % ntab, 0))
    wqk_shape = (d, qkw) if split_halves else (8, LANES)
    cast_args, cast_in, cast_out, cast_shapes, cast_bytes = _cast_rider(cast, m // tm)
    pipelined = tm * (d * F32_BYTES + n * BF16_BYTES + 2 * cos.shape[1] * F32_BYTES) + cast_bytes
    resident = ((w.size + 2 * perm.size + tm * d + wqk_shape[0] * wqk_shape[1]) * BF16_BYTES
                + 2 * (qsc.size + ksc.size) * F32_BYTES + 2 * tm * tn * F32_BYTES)
    outs = pl.pallas_call(
        _with_cast_rider(functools.partial(_in_odd_body, tn=tn, split_halves=split_halves), 8, 1, len(cast)),
        grid=(m // tm,),
        in_specs=[
            pl.BlockSpec((tm, d), lambda i: (i, 0)),
            pl.BlockSpec((None, 1, d), lambda i: (layer, 0, 0)),
            pl.BlockSpec(w.shape, const, **single_buffered),
            pl.BlockSpec(perm.shape, const),
            rope_spec, rope_spec,
            pl.BlockSpec(qsc.shape, const),
            pl.BlockSpec(ksc.shape, const),
        ] + cast_in,
        out_specs=[pl.BlockSpec((tm, n), lambda i: (i, 0))] + cast_out,
        out_shape=[jax.ShapeDtypeStruct((m, n), bf16)] + cast_shapes,
        scratch_shapes=[pltpu.VMEM((tm, d), bf16), pltpu.VMEM(wqk_shape, bf16)],
        compiler_params=_params(("arbitrary",), pipelined, resident),
        name="in_odd",
    )(x, gain, w, perm, cos, sin, qsc, ksc, *cast_args)
    return tuple(outs) if cast else outs[0]


def _ret_prompt_body(q_ref, k_ref, v_ref, g_ref, x_ref, wout_ref, xo_ref, so_ref, s_ref, sb_ref, slab_ref, *,
                     gamma_c, n):
    c = pl.program_id(1)
    subs = [slice(j * n, (j + 1) * n) for j in range(q_ref.shape[0] // n)]

    @pl.when(c == 0)
    def _():
        s_ref[...] = jnp.zeros_like(s_ref)
        sb_ref[...] = jnp.zeros_like(sb_ref)

    causal = lax.broadcasted_iota(jnp.int32, (n, n), 0) >= lax.broadcasted_iota(jnp.int32, (n, n), 1)
    heads = range(RET_HEADS)
    ks = [slice(h * RET_DK, (h + 1) * RET_DK) for h in heads]
    vs = [slice(h * RET_DV, (h + 1) * RET_DV) for h in heads]
    att = [[jnp.where(causal, lax.dot_general(q_ref[r, ks[h]], k_ref[r, ks[h]], NT_DIMS,
                                              preferred_element_type=f32), 0.0).astype(bf16) for h in heads]
           for r in subs]
    o = []
    for j, r in enumerate(subs):
        o.append([jnp.dot(q_ref[r, ks[h]], sb_ref[h], preferred_element_type=f32)
                  + jnp.dot(att[j][h], v_ref[r, vs[h]], preferred_element_type=f32) for h in heads])
        for h in heads:
            kv = lax.dot_general(k_ref[r, ks[h]], v_ref[r, vs[h]], TN_DIMS, preferred_element_type=f32)
            s_new = gamma_c[h] * (s_ref[h] + kv)
            s_ref[h] = s_new
            sb_ref[h] = s_new.astype(bf16)
    for j, r in enumerate(subs):
        y = x_ref[r, :]
        for h in heads:
            og = (_rms(o[j][h]) * _silu(g_ref[r, vs[h]].astype(f32))).astype(bf16)
            y = y + jnp.dot(og, wout_ref[vs[h], :], preferred_element_type=f32)
        xo_ref[r, :] = y

    @pl.when(c == pl.num_programs(1) - 1)
    def _():
        half = RET_DK // 2
        for h in range(RET_HEADS):
            for t in range(RET_DV // LANES):
                ls = slice(t * LANES, (t + 1) * LANES)
                slab_ref[pl.ds(0, half, stride=2), :] = s_ref[h, 0:half, ls]
                slab_ref[pl.ds(1, half, stride=2), :] = s_ref[h, half:RET_DK, ls]
                so_ref[0, h, :, ls] = slab_ref[...]


def _ret_prompt(qkvg, x, wout, gamma_c, *, batch, c, chunk):
    m, d = x.shape
    nc = m // batch // c
    qw = RET_HEADS * RET_DK
    vw = RET_HEADS * RET_DV
    assert c % chunk == 0
    state = RET_HEADS * RET_DK * RET_DV
    pipelined = c * (2 * (qw + vw) * BF16_BYTES + 2 * d * F32_BYTES) + state * F32_BYTES
    resident = (wout.size * BF16_BYTES + state * (F32_BYTES + BF16_BYTES) + RET_DK * LANES * F32_BYTES
                + c * RET_HEADS * (chunk * BF16_BYTES + RET_DV * F32_BYTES)
                + RET_DK * RET_DV * F32_BYTES + c * d * F32_BYTES)
    return pl.pallas_call(
        functools.partial(_ret_prompt_body, gamma_c=gamma_c, n=chunk),
        grid=(batch, nc),
        in_specs=[
            pl.BlockSpec((c, qw), lambda b, i: (b * nc + i, 0)),
            pl.BlockSpec((c, qw), lambda b, i: (b * nc + i, 1)),
            pl.BlockSpec((c, vw), lambda b, i: (b * nc + i, 1)),
            pl.BlockSpec((c, vw), lambda b, i: (b * nc + i, 2)),
            pl.BlockSpec((c, d), lambda b, i: (b * nc + i, 0)),
            pl.BlockSpec(wout.shape, lambda b, i: (0, 0), pipeline_mode=pl.Buffered(1)),
        ],
        out_specs=[
            pl.BlockSpec((c, d), lambda b, i: (b * nc + i, 0)),
            pl.BlockSpec((1, RET_HEADS, RET_DK, RET_DV), lambda b, i: (b, 0, 0, 0)),
        ],
        out_shape=[
            jax.ShapeDtypeStruct((m, d), f32),
            jax.ShapeDtypeStruct((batch, RET_HEADS, RET_DK, RET_DV), f32),
        ],
        scratch_shapes=[
            pltpu.VMEM((RET_HEADS, RET_DK, RET_DV), f32),
            pltpu.VMEM((RET_HEADS, RET_DK, RET_DV), bf16),
            pltpu.VMEM((RET_DK, LANES), f32),
        ],
        compiler_params=_params(("arbitrary", "arbitrary"), pipelined, resident),
        name="ret_prompt",
    )(qkvg, qkvg, qkvg, qkvg, x, wout)


def _rope_tables(pos, per_pair):
    pair_angle = 1.0 / (ROPE_BASE ** jnp.linspace(0.0, 1.0, RET_DK // 2, dtype=f32))
    if per_pair:
        ang = pos[:, None] * pair_angle[None, :]
        return jnp.cos(ang), jnp.sin(ang)
    ang = pos[:, None] * jnp.repeat(pair_angle, 2)[None, :]
    sign = jnp.where(jnp.arange(RET_DK) % 2 == 0, -1.0, 1.0).astype(f32)
    return jnp.cos(ang), jnp.sin(ang) * sign


def _even_odd_perm():
    half = RET_DK // 2
    src = np.concatenate([2 * np.arange(half), 2 * np.arange(half) + 1])
    perm = np.zeros((RET_DK, RET_DK), np.float32)
    perm[src, np.arange(RET_DK)] = 1.0
    return jnp.asarray(perm, dtype=bf16)


def _lane_replicated(scale):
    return jnp.asarray(np.repeat(scale, LANES, axis=1), dtype=f32)


def _ret_decay(rows, c):
    gam = 1.0 - 2.0 ** (-5.0 - np.arange(RET_HEADS, dtype=np.float64))
    lg = np.log(gam)
    steps = (np.arange(rows) % c + 1.0)[:, None]
    q_scale = _lane_replicated(np.exp(lg[None, :] * steps))
    k_scale = _lane_replicated(np.exp(-lg[None, :] * steps) * RET_DK ** -0.5)
    gamma_c = tuple(float(x) for x in np.exp(lg * c))
    gamma = tuple(float(x) for x in gam)
    return q_scale, k_scale, gamma_c, gamma


def kernel(x_prompt, x_sample, state_gla, state_pool, state_ret, norm_mix, norm_ffn, norm_final, w_in_even,
           w_gate_b, b_gate, gla_gain, pool_w, pool_scale, w_out_even, w_in_odd, w_out_odd, w_ffn_gate,
           w_ffn_up, w_ffn_down):
    batch, seq, d = x_prompt.shape
    n_s = x_sample.shape[0]
    assert norm_mix.shape[0] == 2 and x_sample.shape[1] == 1

    we = w_in_even[0].T
    wgb = jnp.concatenate([w_gate_b[0], jnp.zeros((LANES - GATE_RANK, w_gate_b.shape[2]), f32)], axis=0).astype(bf16)
    bg = b_gate[0][None, :]
    gg = gla_gain[0][None, :]
    pw = pool_w[0].astype(bf16)
    ps = pool_scale[0][None, :]
    nm = norm_mix[:, None, :]
    nf = norm_ffn[:, None, :]
    nfin = norm_final[None, :]
    tril = jnp.asarray(np.tril(np.ones((GLA_CHUNK, GLA_CHUNK), np.float32)), dtype=bf16)
    tf = 256
    tm_p = 512
    q_scale, k_scale, gamma_c, gamma = _ret_decay(tm_p, RET_CHUNK)
    tables_p = _rope_tables(jnp.arange(seq, dtype=f32), True) + (q_scale, k_scale)
    tables_s = _rope_tables(jnp.full((n_s,), float(PAST_LEN), f32), False) + (
        _lane_replicated(np.ones((n_s, RET_HEADS))), _lane_replicated(np.full((n_s, RET_HEADS), RET_DK ** -0.5)))
    perm = _even_odd_perm()

    ff = w_ffn_gate.shape[2]
    xp = x_prompt.reshape(batch * seq, d)
    steps_p = batch * seq // tm_p
    ffn_cast = lambda layer: (
        (w_ffn_gate.reshape(-1, ff), d // steps_p, layer * steps_p, steps_p),
        (w_ffn_up.reshape(-1, ff), d // steps_p, layer * steps_p, steps_p),
        (w_ffn_down.reshape(-1, d), 2 * ff // steps_p, layer * steps_p // 2, steps_p // 2))
    whole = lambda w: (w, w.shape[0] // steps_p, 0, steps_p)
    qkvg, loga, u_p, wg0, wu0, wd0, wio, woe, woo = _in_even(
        xp, nm[0], we, wgb, bg, tm=tm_p,
        cast=ffn_cast(0) + (whole(w_in_odd[0]), whole(w_out_even[0]), whole(w_out_odd[0])))
    xp, gla_p = _gla_pool_prompt(qkvg, loga, u_p, xp, tril, gg, pw, ps, woe, batch=batch, t=512)

    xs = x_sample.reshape(n_s, d)
    qkvg_s, loga_s, u_s = _in_even(xs, nm[0], we, wgb, bg, tm=n_s)
    op_s, gla_s = _gla_pool_sample(qkvg_s, loga_s, u_s, state_gla[0], state_pool[0], gg, pw, ps, bb=8)
    xs = _proj_res(xs, op_s, woe, tm=n_s)
    xs = _ffn(xs, nf, wg0, wu0, wd0, nfin, layer=0, tm=n_s, tf=tf, final_norm=False)
    qkvg2_s = _in_odd(xs, nm, wio, perm, tables_s, layer=1, tm=n_s, tn=512, split_halves=False)
    qkvg2_s = qkvg2_s.reshape(n_s, 1, -1)

    rows = n_s // (2 * steps_p)
    xp, og_s, ret_s = _ffn(xp, nf, wg0, wu0, wd0, nfin, layer=0, tm=tm_p, tf=tf, final_norm=False,
                           rider=(qkvg2_s, state_ret[0], rows, gamma))
    qkvg2, wg1, wu1, wd1 = _in_odd(xp, nm, wio, perm, tables_p, layer=1, tm=tm_p, tn=512, split_halves=True,
                                   cast=ffn_cast(1))
    xp, ret_p = _ret_prompt(qkvg2, xp, woo, gamma_c, batch=batch, c=2 * RET_CHUNK, chunk=RET_CHUNK)
    y_prompt = _ffn(xp, nf, wg1, wu1, wd1, nfin, layer=1, tm=tm_p, tf=tf, final_norm=True)
    pool_p = u_p.reshape(batch, seq, -1)[:, seq - POOL_BUF:, :]

    xs = _proj_res(xs, og_s.reshape(n_s, -1), woo, tm=n_s)
    y_sample = _ffn(xs, nf, wg1, wu1, wd1, nfin, layer=1, tm=n_s, tf=tf, final_norm=True)

    pool_s = jnp.concatenate([state_pool[0][:, 1:, :], u_s[:, None, :]], axis=1)

    return (y_prompt.reshape(batch, seq, d), y_sample.reshape(n_s, 1, d),
            gla_p[None], gla_s[None], pool_p[None], pool_s[None], ret_p[None], ret_s[None])
```

```python
import functools

import numpy as np
import jax
import jax.numpy as jnp
from jax import lax
from jax.experimental import pallas as pl
from jax.experimental.pallas import tpu as pltpu

f32 = jnp.float32
bf16 = jnp.bfloat16

EPS = 1e-6
PAST_LEN = 16384
GLA_HEADS, GLA_DK, GLA_DV = 4, 64, 128
GLA_CHUNK = 64
GATE_RANK = 16
GATE_NORMALIZER = 16.0
POOL_WINDOWS = (2, 4, 8, 16)
POOL_GW = 128
POOL_BUF = max(POOL_WINDOWS) - 1
POOL_HIST = 32
POOL_TAIL = 16
RET_HEADS, RET_DK, RET_DV = 4, 256, 512
RET_CHUNK = 256
ROPE_BASE = 10000.0
LANES = 128
MIB = 1024 * 1024
VMEM_COMPILER_ALLOWANCE = 8 * MIB
F32_BYTES, BF16_BYTES = 4, 2

NT_DIMS = (((1,), (1,)), ((), ()))
TN_DIMS = (((0,), (0,)), ((), ()))


def _params(semantics, pipelined_bytes, resident_bytes):
    limit = 2 * pipelined_bytes + resident_bytes + VMEM_COMPILER_ALLOWANCE
    return pltpu.CompilerParams(dimension_semantics=semantics, vmem_limit_bytes=int(limit))


def _rms(x, gain=None):
    y = x * lax.rsqrt(jnp.mean(x * x, axis=-1, keepdims=True) + EPS)
    return y if gain is None else y * gain


def _silu(g):
    return g * jax.nn.sigmoid(g)


def _in_even_body(x_ref, gain_ref, w_ref, wgb_ref, bg_ref, qkvg_ref, loga_ref, u_ref, h_ref, wq_ref, wu_ref, *, tn):
    nq = qkvg_ref.shape[1]
    nu = u_ref.shape[1]

    @pl.when(pl.program_id(0) == 0)
    def _():
        for r0 in range(0, nq + LANES, LANES):
            wq_ref[:, r0:r0 + LANES] = w_ref[r0:r0 + LANES, :].T.astype(bf16)
        for r0 in range(0, nu, LANES):
            wu_ref[:, r0:r0 + LANES] = w_ref[nq + GATE_RANK + r0:nq + GATE_RANK + r0 + LANES, :].T.astype(bf16)

    h_ref[...] = _rms(x_ref[...], gain_ref[...]).astype(bf16)
    a = jnp.dot(h_ref[...], wq_ref[:, nq:nq + LANES], preferred_element_type=f32)
    a = jnp.where(lax.broadcasted_iota(jnp.int32, a.shape, 1) < GATE_RANK, a, 0.0).astype(bf16)
    for c0 in range(0, nq, tn):
        qkvg_ref[:, c0:c0 + tn] = jnp.dot(h_ref[...], wq_ref[:, c0:c0 + tn], preferred_element_type=f32).astype(bf16)
        if c0 == 0:
            z = jnp.dot(a, wgb_ref[...], preferred_element_type=f32) + bg_ref[...]
            loga_ref[...] = (jnp.minimum(z, 0.0) - jnp.log1p(jnp.exp(-jnp.abs(z)))) * (1.0 / GATE_NORMALIZER)
    for c0 in range(0, nu, tn):
        u_ref[:, c0:c0 + tn] = jnp.dot(h_ref[...], wu_ref[:, c0:c0 + tn], preferred_element_type=f32)


def _cast_rider(cast, steps):
    arrays, in_specs, out_specs, out_shapes, nbytes = [], [], [], [], 0
    for arr, rows, first, count in cast:
        assert count <= steps and rows % 16 == 0 and (first + count) * rows <= arr.shape[0]
        cols = arr.shape[1]
        arrays.append(arr)
        in_specs.append(pl.BlockSpec(
            (rows, cols), lambda i, first=first, count=count: (first + jnp.minimum(i, count - 1), 0)))
        out_specs.append(pl.BlockSpec((rows, cols), lambda i, count=count: (jnp.minimum(i, count - 1), 0)))
        out_shapes.append(jax.ShapeDtypeStruct((count * rows, cols), bf16))
        nbytes += rows * cols * (F32_BYTES + BF16_BYTES)
    return arrays, in_specs, out_specs, out_shapes, nbytes


def _with_cast_rider(body, n_in, n_out, n_cast):
    def wrapped(*refs):
        rest = refs[n_in + n_cast:]
        body(*refs[:n_in], *rest[:n_out], *rest[n_out + n_cast:])
        for src_ref, dst_ref in zip(refs[n_in:n_in + n_cast], rest[n_out:n_out + n_cast]):
            dst_ref[...] = src_ref[...].astype(bf16)
    return wrapped


def _in_even(x, gain, w, wgb, bg, *, tm, cast=()):
    m, d = x.shape
    steps = m // tm
    nq = 2 * GLA_HEADS * GLA_DK + 2 * GLA_HEADS * GLA_DV
    nu = POOL_GW * len(POOL_WINDOWS)
    nk = GLA_HEADS * GLA_DK
    assert w.shape == (nq + GATE_RANK + nu, d)
    const = lambda i: (0, 0)
    tn = 512
    cast_args, cast_in, cast_out, cast_shapes, cast_bytes = _cast_rider(cast, steps)
    pipelined = tm * (d * F32_BYTES + nq * BF16_BYTES + nk * F32_BYTES + nu * F32_BYTES) + cast_bytes
    resident = (w.size * F32_BYTES + (2 * wgb.size + tm * d + d * (nq + LANES) + d * nu) * BF16_BYTES
                + 2 * tm * tn * F32_BYTES)
    return pl.pallas_call(
        _with_cast_rider(functools.partial(_in_even_body, tn=tn), 5, 3, len(cast)),
        grid=(steps,),
        in_specs=[
            pl.BlockSpec((tm, d), lambda i: (i, 0)),
            pl.BlockSpec((1, d), const),
            pl.BlockSpec(w.shape, const, pipeline_mode=pl.Buffered(1)),
            pl.BlockSpec(wgb.shape, const),
            pl.BlockSpec((1, nk), const),
        ] + cast_in,
        out_specs=[
            pl.BlockSpec((tm, nq), lambda i: (i, 0)),
            pl.BlockSpec((tm, nk), lambda i: (i, 0)),
            pl.BlockSpec((tm, nu), lambda i: (i, 0)),
        ] + cast_out,
        out_shape=[
            jax.ShapeDtypeStruct((m, nq), bf16),
            jax.ShapeDtypeStruct((m, nk), f32),
            jax.ShapeDtypeStruct((m, nu), f32),
        ] + cast_shapes,
        scratch_shapes=[pltpu.VMEM((tm, d), bf16), pltpu.VMEM((d, nq + LANES), bf16), pltpu.VMEM((d, nu), bf16)],
        compiler_params=_params(("arbitrary",), pipelined, resident),
        name="in_even",
    )(x, gain, w, wgb, bg, *cast_args)


def _gla_pool_prompt_body(qkvg_ref, loga_ref, u_ref, x_ref, tril_ref, gain_ref, pw_ref, ps_ref, wout_ref,
                          xo_ref, so_ref, st_ref, o_ref, e_ref, p_ref, q_ref, op_ref):
    t = x_ref.shape[0]
    ck = GLA_CHUNK
    kw = GLA_HEADS * GLA_DK
    vw = GLA_HEADS * GLA_DV
    pair_w = 2 * GLA_DK
    i = pl.program_id(1)

    @pl.when(i == 0)
    def _():
        st_ref[...] = jnp.zeros_like(st_ref)
        e_ref[0:POOL_HIST, :] = jnp.zeros((POOL_HIST, e_ref.shape[1]), f32)

    tril = tril_ref[...]
    row = lax.broadcasted_iota(jnp.int32, (2 * ck, pair_w), 0)
    lane = lax.broadcasted_iota(jnp.int32, (2 * ck, pair_w), 1)
    first_lanes = lane < GLA_DK
    first_lanes_ck = lax.broadcasted_iota(jnp.int32, (ck, pair_w), 1) < GLA_DK
    same_head = (row < ck) == first_lanes
    causal = same_head & ((row % ck) >= (lane % GLA_DK))
    pairs = range(GLA_HEADS // 2)
    chunks = range(t // ck)

    hist = POOL_HIST
    n = t + hist
    gw = POOL_GW
    u = u_ref[...]
    e_ref[hist:n, :] = u
    p_ref[8:n, :] = e_ref[8:n, :] + e_ref[7:n - 1, :]
    q_ref[16:n, gw:] = p_ref[16:n, gw:] + p_ref[14:n - 2, gw:]
    p_ref[24:n, 2 * gw:] = q_ref[24:n, 2 * gw:] + q_ref[20:n - 4, 2 * gw:]
    q_ref[32:n, 3 * gw:] = p_ref[32:n, 3 * gw:] + p_ref[24:n - 8, 3 * gw:]

    def rows_of(c):
        return slice(c * ck, (c + 1) * ck)

    def v_pair(c, p):
        va = qkvg_ref[rows_of(c), 2 * kw + (2 * p) * GLA_DV:2 * kw + (2 * p + 1) * GLA_DV]
        vb = qkvg_ref[rows_of(c), 2 * kw + (2 * p + 1) * GLA_DV:2 * kw + (2 * p + 2) * GLA_DV]
        return va, vb

    bcs = []
    for c in chunks:
        la = loga_ref[rows_of(c), :]
        la_hi = la.astype(bf16)
        la_lo = (la - la_hi.astype(f32)).astype(bf16)
        bcs.append(jnp.dot(tril, la_hi, preferred_element_type=f32) + jnp.dot(tril, la_lo, preferred_element_type=f32))
    lhs_q, ke2, kds, elast = [], [], [], []
    for c in chunks:
        bc = bcs[c]
        blast = bc[ck - 1:ck, :]
        q = qkvg_ref[rows_of(c), 0:kw].astype(f32) * (GLA_DK ** -0.5)
        k = qkvg_ref[rows_of(c), kw:2 * kw].astype(f32)
        qe = q * jnp.exp(bc)
        ke = (k * jnp.exp(-bc)).astype(bf16)
        kds.append((k * jnp.exp(blast - bc)).astype(bf16))
        elast.append(jnp.exp(blast))
        for p in pairs:
            pl_ = slice(p * pair_w, (p + 1) * pair_w)
            qe_p = qe[:, pl_]
            lhs_q.append(jnp.concatenate([jnp.where(first_lanes_ck, qe_p, 0.0),
                                          jnp.where(first_lanes_ck, 0.0, qe_p)], axis=0).astype(bf16))
            ke2.append(jnp.concatenate([ke[:, pl_], ke[:, pl_]], axis=0))
    att, upd = [], []
    for c in chunks:
        for p in pairs:
            idx = c * len(pairs) + p
            a = lax.dot_general(lhs_q[idx], ke2[idx], NT_DIMS, preferred_element_type=f32)
            att.append(jnp.where(causal, a, 0.0).astype(bf16))
            va, vb = v_pair(c, p)
            r = lax.dot_general(jnp.concatenate([va, vb], axis=1), kds[c][:, p * pair_w:(p + 1) * pair_w], TN_DIMS,
                                preferred_element_type=f32)
            upd.append(jnp.where(first_lanes, r[:GLA_DV], r[GLA_DV:]))
    st = [st_ref[p] for p in pairs]
    for c in chunks:
        for p in pairs:
            idx = c * len(pairs) + p
            va, vb = v_pair(c, p)
            o = lax.dot_general(lhs_q[idx], st[p].astype(bf16), NT_DIMS, preferred_element_type=f32)
            o = o + jnp.dot(att[idx], jnp.concatenate([va, vb], axis=0), preferred_element_type=f32)
            o_ref[rows_of(c), (2 * p) * GLA_DV:(2 * p + 1) * GLA_DV] = o[:ck]
            o_ref[rows_of(c), (2 * p + 1) * GLA_DV:(2 * p + 2) * GLA_DV] = o[ck:]
            st[p] = st[p] * elast[c][:, p * pair_w:(p + 1) * pair_w] + upd[idx]
    for p in pairs:
        st_ref[p] = st[p]

    sums = (p_ref, q_ref, p_ref, q_ref)
    pos = i * t + lax.broadcasted_iota(jnp.int32, (t, 1), 0)
    for gi, w in enumerate(POOL_WINDOWS):
        ls = slice(gi * gw, (gi + 1) * gw)
        cnt = jnp.minimum(w, pos + 1).astype(f32)
        pooled = (sums[gi][hist:n, ls] / cnt - u[:, ls]).astype(bf16)
        pg = jnp.dot(pooled, pw_ref[gi], preferred_element_type=f32) * ps_ref[:, ls]
        op_ref[:, vw + gi * gw:vw + (gi + 1) * gw] = pg.astype(bf16)
    e_ref[hist - POOL_TAIL:hist, :] = e_ref[n - POOL_TAIL:n, :]

    piece = 2 * GLA_DV
    y = x_ref[...]
    for c0 in (vw, vw + piece):
        y = y + jnp.dot(op_ref[:, c0:c0 + piece], wout_ref[c0:c0 + piece, :], preferred_element_type=f32)
    gain = gain_ref[...]
    for p in pairs:
        for h in (2 * p, 2 * p + 1):
            hs = slice(h * GLA_DV, (h + 1) * GLA_DV)
            g = qkvg_ref[:, 2 * kw + vw + h * GLA_DV:2 * kw + vw + (h + 1) * GLA_DV].astype(f32)
            op_ref[:, hs] = (_rms(o_ref[:, hs], gain) * _silu(g)).astype(bf16)
        c0 = p * piece
        y = y + jnp.dot(op_ref[:, c0:c0 + piece], wout_ref[c0:c0 + piece, :], preferred_element_type=f32)
    xo_ref[...] = y

    @pl.when(i == pl.num_programs(1) - 1)
    def _():
        for p in range(GLA_HEADS // 2):
            s_pair = st_ref[p].T
            so_ref[0, 2 * p] = s_pair[:GLA_DK]
            so_ref[0, 2 * p + 1] = s_pair[GLA_DK:]


def _gla_pool_prompt(qkvg, loga, u, x, tril, gain, pw, ps, wout, *, batch, t):
    m, d = x.shape
    nt = m // batch // t
    row = lambda b, i: (b * nt + i, 0)
    const2 = lambda b, i: (0, 0)
    vw = GLA_HEADS * GLA_DV
    uw = u.shape[1]
    pipelined = t * (qkvg.shape[1] * BF16_BYTES + (loga.shape[1] + uw + 2 * d) * F32_BYTES)
    scratch = (t * vw + 3 * (POOL_HIST + t) * uw) * F32_BYTES + t * (vw + uw) * BF16_BYTES
    waves = (t // GLA_CHUNK) * (GLA_HEADS // 2) * (3 * LANES * LANES * BF16_BYTES + LANES * LANES * F32_BYTES)
    resident = 2 * (wout.size + pw.size) * BF16_BYTES + scratch + waves + t * d * F32_BYTES
    return pl.pallas_call(
        _gla_pool_prompt_body,
        grid=(batch, nt),
        in_specs=[
            pl.BlockSpec((t, qkvg.shape[1]), row),
            pl.BlockSpec((t, loga.shape[1]), row),
            pl.BlockSpec((t, uw), row),
            pl.BlockSpec((t, d), row),
            pl.BlockSpec(tril.shape, const2),
            pl.BlockSpec(gain.shape, const2),
            pl.BlockSpec(pw.shape, lambda b, i: (0, 0, 0)),
            pl.BlockSpec(ps.shape, const2),
            pl.BlockSpec(wout.shape, const2),
        ],
        out_specs=[
            pl.BlockSpec((t, d), row),
            pl.BlockSpec((1, GLA_HEADS, GLA_DK, GLA_DV), lambda b, i: (b, 0, 0, 0)),
        ],
        out_shape=[
            jax.ShapeDtypeStruct((m, d), f32),
            jax.ShapeDtypeStruct((batch, GLA_HEADS, GLA_DK, GLA_DV), f32),
        ],
        scratch_shapes=[
            pltpu.VMEM((GLA_HEADS // 2, GLA_DV, 2 * GLA_DK), f32),
            pltpu.VMEM((t, vw), f32),
            pltpu.VMEM((POOL_HIST + t, uw), f32),
            pltpu.VMEM((POOL_HIST + t, uw), f32),
            pltpu.VMEM((POOL_HIST + t, uw), f32),
            pltpu.VMEM((t, vw + uw), bf16),
        ],
        compiler_params=_params(("arbitrary", "arbitrary"), pipelined, resident),
        name="gla_pool_prompt",
    )(qkvg, loga, u, x, tril, gain, pw, ps, wout)


def _gla_pool_sample_body(qkvg_ref, loga_ref, u_ref, s_ref, buf_ref, gain_ref, pw_ref, ps_ref,
                          op_ref, so_ref, bufo_ref):
    bb = u_ref.shape[0]
    kw = GLA_HEADS * GLA_DK
    vw = GLA_HEADS * GLA_DV
    gain = gain_ref[...]
    qkvg = qkvg_ref[...].astype(f32)
    alpha = jnp.exp(loga_ref[...])
    qs = qkvg[:, 0:kw] * (GLA_DK ** -0.5)
    k = qkvg[:, kw:2 * kw]

    def column(row):
        return jnp.broadcast_to(row, (LANES, kw)).T

    o_rows = []
    for b in range(bb):
        acol = column(alpha[b:b + 1, :])
        qcol = column(qs[b:b + 1, :])
        kcol = column(k[b:b + 1, :])
        o_heads = []
        for h in range(GLA_HEADS):
            ks = slice(h * GLA_DK, (h + 1) * GLA_DK)
            v = qkvg[b:b + 1, 2 * kw + h * GLA_DV:2 * kw + (h + 1) * GLA_DV]
            s_new = acol[ks, :] * s_ref[b, h] + kcol[ks, :] * v
            so_ref[b, h] = s_new
            o = jnp.sum(qcol[ks, :] * s_new, axis=0, keepdims=True)
            g = qkvg[b:b + 1, 2 * kw + vw + h * GLA_DV:2 * kw + vw + (h + 1) * GLA_DV]
            o_heads.append(_rms(o, gain) * _silu(g))
        o_rows.append(jnp.concatenate(o_heads, axis=1))
    op_ref[:, 0:vw] = jnp.concatenate(o_rows, axis=0).astype(bf16)

    u = u_ref[...]
    bufo_ref[0:POOL_BUF - 1] = buf_ref[1:POOL_BUF]
    bufo_ref[POOL_BUF - 1] = u
    for gi, w in enumerate(POOL_WINDOWS):
        ls = slice(gi * POOL_GW, (gi + 1) * POOL_GW)
        s = u[:, ls] + jnp.sum(buf_ref[POOL_BUF - (w - 1):POOL_BUF, :, ls], axis=0)
        cnt = float(min(w, PAST_LEN + 1))
        pooled = (s / cnt - u[:, ls]).astype(bf16)
        pg = jnp.dot(pooled, pw_ref[gi], preferred_element_type=f32) * ps_ref[:, ls]
        op_ref[:, vw + gi * POOL_GW:vw + (gi + 1) * POOL_GW] = pg.astype(bf16)


def _gla_pool_sample(qkvg, loga, u, s, buf, gain, pw, ps, *, bb):
    n = u.shape[0]
    assert buf.shape == (POOL_BUF, n, u.shape[1])
    row = lambda i: (i, 0)
    const2 = lambda i: (0, 0)
    ow = GLA_HEADS * GLA_DV + POOL_GW * len(POOL_WINDOWS)
    state_rows = int(np.prod(s.shape[1:]))
    buf_spec = pl.BlockSpec((POOL_BUF, bb, buf.shape[2]), lambda i: (0, i, 0))
    pipelined = bb * ((qkvg.shape[1] + ow) * BF16_BYTES
                      + (loga.shape[1] + u.shape[1] + 2 * state_rows + 2 * POOL_BUF * buf.shape[2]) * F32_BYTES)
    resident = 2 * pw.size * BF16_BYTES + 3 * LANES * loga.shape[1] * F32_BYTES
    return pl.pallas_call(
        _gla_pool_sample_body,
        grid=(n // bb,),
        in_specs=[
            pl.BlockSpec((bb, qkvg.shape[1]), row),
            pl.BlockSpec((bb, loga.shape[1]), row),
            pl.BlockSpec((bb, u.shape[1]), row),
            pl.BlockSpec((bb,) + s.shape[1:], lambda i: (i, 0, 0, 0)),
            buf_spec,
            pl.BlockSpec(gain.shape, const2),
            pl.BlockSpec(pw.shape, lambda i: (0, 0, 0)),
            pl.BlockSpec(ps.shape, const2),
        ],
        out_specs=[
            pl.BlockSpec((bb, ow), row),
            pl.BlockSpec((bb,) + s.shape[1:], lambda i: (i, 0, 0, 0)),
            buf_spec,
        ],
        out_shape=[
            jax.ShapeDtypeStruct((n, ow), bf16),
            jax.ShapeDtypeStruct(s.shape, f32),
            jax.ShapeDtypeStruct(buf.shape, f32),
        ],
        compiler_params=_params(("arbitrary",), pipelined, resident),
        name="gla_pool_sample",
    )(qkvg, loga, u, s, buf, gain, pw, ps)


def _proj_res_body(x_ref, a_ref, w_ref, o_ref):
    o_ref[...] = x_ref[...] + jnp.dot(a_ref[...], w_ref[...], preferred_element_type=f32)


def _proj_res(x, a, w, *, tm):
    m, d = x.shape
    return pl.pallas_call(
        _proj_res_body,
        grid=(m // tm,),
        in_specs=[
            pl.BlockSpec((tm, d), lambda i: (i, 0)),
            pl.BlockSpec((tm, a.shape[1]), lambda i: (i, 0)),
            pl.BlockSpec(w.shape, lambda i: (0, 0)),
        ],
        out_specs=pl.BlockSpec((tm, d), lambda i: (i, 0)),
        out_shape=jax.ShapeDtypeStruct((m, d), f32),
        compiler_params=_params(("arbitrary",), tm * (2 * d * F32_BYTES + a.shape[1] * BF16_BYTES),
                                2 * w.size * BF16_BYTES),
        name="proj_res",
    )(x, a, w)


def _ret_token_pieces(q_ref, k_ref, v_ref, g_ref, s_ref, og_ref, so_ref, gamma):
    def piece(j, h):
        def run():
            ks = slice(h * RET_DK, (h + 1) * RET_DK)
            vs = slice(h * RET_DV, (h + 1) * RET_DV)
            qcol = jnp.broadcast_to(q_ref[j, :, ks].astype(f32), (LANES, RET_DK)).T
            kcol = jnp.broadcast_to(k_ref[j, :, ks].astype(f32), (LANES, RET_DK)).T
            v = v_ref[j, :, vs].astype(f32)
            g = g_ref[j, :, vs].astype(f32)
            o_tiles = []
            for t in range(RET_DV // LANES):
                cs = slice(t * LANES, (t + 1) * LANES)
                s_new = gamma[h] * s_ref[j, h, :, cs] + kcol * v[:, cs]
                so_ref[j, h, :, cs] = s_new
                o_tiles.append(jnp.sum(qcol * s_new, axis=0, keepdims=True))
            o = jnp.concatenate(o_tiles, axis=1)
            og_ref[j, :, vs] = (_rms(o) * _silu(g)).astype(bf16)
            return o
        return run

    return [piece(j, h) for j in range(s_ref.shape[0]) for h in range(RET_HEADS)]


def _ffn_body(*refs, tf, n_sub, final_norm, rider_gamma):
    x_ref, gain_ref, wg_ref, wu_ref, wd_ref, fgain_ref = refs[:6]
    pieces = []
    if rider_gamma is None:
        o_ref, h_ref, acc_ref = refs[6:]
    else:
        rq_ref, rk_ref, rv_ref, rg_ref, rs_ref, o_ref, rog_ref, rso_ref, h_ref, acc_ref = refs[6:]
        pieces = _ret_token_pieces(rq_ref, rk_ref, rv_ref, rg_ref, rs_ref, rog_ref, rso_ref, rider_gamma)
    n_chunks = wg_ref.shape[1] // tf
    bounds = [n_chunks * s // n_sub for s in range(n_sub + 1)]

    def exact_zero(v):
        bits = lax.bitcast_convert_type(v, jnp.uint32)
        return ((bits >> 16) >> 16).astype(f32)

    def run_chunks(chunks):
        pin = None
        for n, c in enumerate(chunks):
            cs = slice(c * tf, (c + 1) * tf)
            g = jnp.dot(h_ref[...], wg_ref[:, cs], preferred_element_type=f32)
            if pin is not None:
                g = g + pin
                pin = None
            u = jnp.dot(h_ref[...], wu_ref[:, cs], preferred_element_type=f32)
            a = (_silu(g) * u).astype(bf16)
            part = jnp.dot(a, wd_ref[cs, :], preferred_element_type=f32)
            if c == 0:
                acc_ref[...] = part
            else:
                acc_ref[...] += part
            for p in range(len(pieces)):
                if p * (len(chunks) - 1) // len(pieces) == n:
                    z = exact_zero(pieces[p]()[:, :tf])
                    pin = z if pin is None else pin + z

    def sub_step(s):
        if s == 0:
            h_ref[...] = _rms(x_ref[...], gain_ref[...]).astype(bf16)
        run_chunks(range(bounds[s], bounds[s + 1]))
        if s == n_sub - 1:
            y = x_ref[...] + acc_ref[...]
            if final_norm:
                y = _rms(y, fgain_ref[...])
            o_ref[...] = y

    if n_sub == 1:
        sub_step(0)
    else:
        for s in range(n_sub):
            pl.when(pl.program_id(1) == s)(functools.partial(sub_step, s))


def _ffn(x, gain, wg, wu, wd, fgain, *, layer, tm, tf, final_norm, rider=None):
    m, d = x.shape
    ff = wg.shape[1]
    steps = m // tm
    n_sub = 1 if rider is None else 2
    single_buffered = dict(pipeline_mode=pl.Buffered(1))
    in_specs = [
        pl.BlockSpec((tm, d), lambda i, s: (i, 0)),
        pl.BlockSpec((None, 1, d), lambda i, s: (layer, 0, 0)),
        pl.BlockSpec((d, ff), lambda i, s: (0, 0), **single_buffered),
        pl.BlockSpec((d, ff), lambda i, s: (0, 0), **single_buffered),
        pl.BlockSpec((ff, d), lambda i, s: (0, 0), **single_buffered),
        pl.BlockSpec((1, d), lambda i, s: (0, 0)),
    ]
    args = [x, gain, wg, wu, wd, fgain]
    out_specs = [pl.BlockSpec((tm, d), lambda i, s: (i, 0))]
    out_shape = [jax.ShapeDtypeStruct((m, d), f32)]
    gamma = None
    pipelined = 2 * tm * d * F32_BYTES
    resident = (3 * d * ff * BF16_BYTES + tm * d * (BF16_BYTES + F32_BYTES)
                + 3 * tm * tf * F32_BYTES + tm * d * F32_BYTES)
    if rider is not None:
        qkvg3, state, rows, gamma = rider
        assert 2 * steps * rows == state.shape[0]
        qw = RET_HEADS * RET_DK
        vw = RET_HEADS * RET_DV
        blk = lambda col: (lambda i, s: (2 * i + s, 0, col))
        state_spec = pl.BlockSpec((rows,) + state.shape[1:], lambda i, s: (2 * i + s, 0, 0, 0))
        in_specs += [
            pl.BlockSpec((rows, 1, qw), blk(0)),
            pl.BlockSpec((rows, 1, qw), blk(1)),
            pl.BlockSpec((rows, 1, vw), blk(1)),
            pl.BlockSpec((rows, 1, vw), blk(2)),
            state_spec,
        ]
        args += [qkvg3, qkvg3, qkvg3, qkvg3, state]
        out_specs += [pl.BlockSpec((rows, 1, vw), blk(0)), state_spec]
        out_shape += [
            jax.ShapeDtypeStruct((state.shape[0], 1, vw), bf16),
            jax.ShapeDtypeStruct(state.shape, f32),
        ]
        pipelined += 2 * rows * int(np.prod(state.shape[1:])) * F32_BYTES
    out = pl.pallas_call(
        functools.partial(_ffn_body, tf=tf, n_sub=n_sub, final_norm=final_norm, rider_gamma=gamma),
        grid=(steps, n_sub),
        in_specs=in_specs,
        out_specs=out_specs,
        out_shape=out_shape,
        scratch_shapes=[pltpu.VMEM((tm, d), bf16), pltpu.VMEM((tm, d), f32)],
        compiler_params=_params(("arbitrary", "arbitrary"), pipelined, resident),
        name="ffn_final" if final_norm else "ffn",
    )(*args)
    return out[0] if rider is None else out


def _in_odd_body(x_ref, gain_ref, w_ref, perm_ref, cos_ref, sin_ref, qsc_ref, ksc_ref, o_ref, h_ref, wqk_ref, *,
                 tn, split_halves):
    qw = RET_HEADS * RET_DK
    half = RET_DK // 2
    if split_halves:
        @pl.when(pl.program_id(0) == 0)
        def _():
            for hh in range(2 * RET_HEADS):
                hs = slice(hh * RET_DK, (hh + 1) * RET_DK)
                wqk_ref[:, hs] = jnp.dot(w_ref[:, hs], perm_ref[...], preferred_element_type=f32).astype(bf16)

    h_ref[...] = _rms(x_ref[...], gain_ref[...]).astype(bf16)
    cos = cos_ref[...]
    sin = sin_ref[...]
    for c in range(2 * qw // tn):
        c0 = c * tn
        w_chunk = wqk_ref[:, c0:c0 + tn] if split_halves else w_ref[:, c0:c0 + tn]
        p = jnp.dot(h_ref[...], w_chunk, preferred_element_type=f32)
        sc_ref = qsc_ref if c0 < qw else ksc_ref
        for hh in range(tn // RET_DK):
            h0 = hh * RET_DK
            head = (c0 % qw + h0) // RET_DK
            sc = sc_ref[:, head * LANES:(head + 1) * LANES]
            if split_halves:
                ev = p[:, h0:h0 + half]
                od = p[:, h0 + half:h0 + RET_DK]
                o_ref[:, c0 + h0:c0 + h0 + half] = ((ev * cos - od * sin) * sc).astype(bf16)
                o_ref[:, c0 + h0 + half:c0 + h0 + RET_DK] = ((od * cos + ev * sin) * sc).astype(bf16)
            else:
                xh = p[:, h0:h0 + RET_DK]
                even = lax.broadcasted_iota(jnp.int32, xh.shape, 1) % 2 == 0
                partner = jnp.where(even, pltpu.roll(xh, RET_DK - 1, 1), pltpu.roll(xh, 1, 1))
                r = xh * cos + partner * sin
                o_ref[:, c0 + h0:c0 + h0 + half] = (r[:, :half] * sc).astype(bf16)
                o_ref[:, c0 + h0 + half:c0 + h0 + RET_DK] = (r[:, half:] * sc).astype(bf16)
    for c0 in range(2 * qw, w_ref.shape[1], tn):
        p = jnp.dot(h_ref[...], w_ref[:, c0:c0 + tn], preferred_element_type=f32)
        o_ref[:, c0:c0 + tn] = p.astype(bf16)


def _in_odd(x, gain, w, perm, tables, *, layer, tm, tn, split_halves, cast=()):
    m, d = x.shape
    n = w.shape[1]
    cos, sin, qsc, ksc = tables
    ntab = cos.shape[0] // tm
    qkw = 2 * RET_HEADS * RET_DK
    single_buffered = dict(pipeline_mode=pl.Buffered(1))
    const = lambda i: (0, 0)
    rope_spec = pl.BlockSpec((tm, cos.shape[1]), lambda i: (i % ntab, 0))
    wqk_shape = (d, qkw) if split_halves else (8, LANES)
    cast_args, cast_in, cast_out, cast_shapes, cast_bytes = _cast_rider(cast, m // tm)
    pipelined = tm * (d * F32_BYTES + n * BF16_BYTES + 2 * cos.shape[1] * F32_BYTES) + cast_bytes
    resident = ((w.size + 2 * perm.size + tm * d + wqk_shape[0] * wqk_shape[1]) * BF16_BYTES
                + 2 * (qsc.size + ksc.size) * F32_BYTES + 2 * tm * tn * F32_BYTES)
    outs = pl.pallas_call(
        _with_cast_rider(functools.partial(_in_odd_body, tn=tn, split_halves=split_halves), 8, 1, len(cast)),
        grid=(m // tm,),
        in_specs=[
            pl.BlockSpec((tm, d), lambda i: (i, 0)),
            pl.BlockSpec((None, 1, d), lambda i: (layer, 0, 0)),
            pl.BlockSpec(w.shape, const, **single_buffered),
            pl.BlockSpec(perm.shape, const),
            rope_spec, rope_spec,
            pl.BlockSpec(qsc.shape, const),
            pl.BlockSpec(ksc.shape, const),
        ] + cast_in,
        out_specs=[pl.BlockSpec((tm, n), lambda i: (i, 0))] + cast_out,
        out_shape=[jax.ShapeDtypeStruct((m, n), bf16)] + cast_shapes,
        scratch_shapes=[pltpu.VMEM((tm, d), bf16), pltpu.VMEM(wqk_shape, bf16)],
        compiler_params=_params(("arbitrary",), pipelined, resident),
        name="in_odd",
    )(x, gain, w, perm, cos, sin, qsc, ksc, *cast_args)
    return tuple(outs) if cast else outs[0]


def _ret_prompt_body(q_ref, k_ref, v_ref, g_ref, x_ref, wout_ref, xo_ref, so_ref, s_ref, sb_ref, slab_ref, *,
                     gamma_c, n):
    c = pl.program_id(1)
    subs = [slice(j * n, (j + 1) * n) for j in range(q_ref.shape[0] // n)]

    @pl.when(c == 0)
    def _():
        s_ref[...] = jnp.zeros_like(s_ref)
        sb_ref[...] = jnp.zeros_like(sb_ref)

    causal = lax.broadcasted_iota(jnp.int32, (n, n), 0) >= lax.broadcasted_iota(jnp.int32, (n, n), 1)
    heads = range(RET_HEADS)
    ks = [slice(h * RET_DK, (h + 1) * RET_DK) for h in heads]
    vs = [slice(h * RET_DV, (h + 1) * RET_DV) for h in heads]
    att = [[jnp.where(causal, lax.dot_general(q_ref[r, ks[h]], k_ref[r, ks[h]], NT_DIMS,
                                              preferred_element_type=f32), 0.0).astype(bf16) for h in heads]
           for r in subs]
    o = []
    for j, r in enumerate(subs):
        o.append([jnp.dot(q_ref[r, ks[h]], sb_ref[h], preferred_element_type=f32)
                  + jnp.dot(att[j][h], v_ref[r, vs[h]], preferred_element_type=f32) for h in heads])
        for h in heads:
            kv = lax.dot_general(k_ref[r, ks[h]], v_ref[r, vs[h]], TN_DIMS, preferred_element_type=f32)
            s_new = gamma_c[h] * (s_ref[h] + kv)
            s_ref[h] = s_new
            sb_ref[h] = s_new.astype(bf16)
    for j, r in enumerate(subs):
        y = x_ref[r, :]
        for h in heads:
            og = (_rms(o[j][h]) * _silu(g_ref[r, vs[h]].astype(f32))).astype(bf16)
            y = y + jnp.dot(og, wout_ref[vs[h], :], preferred_element_type=f32)
        xo_ref[r, :] = y

    @pl.when(c == pl.num_programs(1) - 1)
    def _():
        half = RET_DK // 2
        for h in range(RET_HEADS):
            for t in range(RET_DV // LANES):
                ls = slice(t * LANES, (t + 1) * LANES)
                slab_ref[pl.ds(0, half, stride=2), :] = s_ref[h, 0:half, ls]
                slab_ref[pl.ds(1, half, stride=2), :] = s_ref[h, half:RET_DK, ls]
                so_ref[0, h, :, ls] = slab_ref[...]


def _ret_prompt(qkvg, x, wout, gamma_c, *, batch, c, chunk):
    m, d = x.shape
    nc = m // batch // c
    qw = RET_HEADS * RET_DK
    vw = RET_HEADS * RET_DV
    assert c % chunk == 0
    state = RET_HEADS * RET_DK * RET_DV
    pipelined = c * (2 * (qw + vw) * BF16_BYTES + 2 * d * F32_BYTES) + state * F32_BYTES
    resident = (wout.size * BF16_BYTES + state * (F32_BYTES + BF16_BYTES) + RET_DK * LANES * F32_BYTES
                + c * RET_HEADS * (chunk * BF16_BYTES + RET_DV * F32_BYTES)
                + RET_DK * RET_DV * F32_BYTES + c * d * F32_BYTES)
    return pl.pallas_call(
        functools.partial(_ret_prompt_body, gamma_c=gamma_c, n=chunk),
        grid=(batch, nc),
        in_specs=[
            pl.BlockSpec((c, qw), lambda b, i: (b * nc + i, 0)),
            pl.BlockSpec((c, qw), lambda b, i: (b * nc + i, 1)),
            pl.BlockSpec((c, vw), lambda b, i: (b * nc + i, 1)),
            pl.BlockSpec((c, vw), lambda b, i: (b * nc + i, 2)),
            pl.BlockSpec((c, d), lambda b, i: (b * nc + i, 0)),
            pl.BlockSpec(wout.shape, lambda b, i: (0, 0), pipeline_mode=pl.Buffered(1)),
        ],
        out_specs=[
            pl.BlockSpec((c, d), lambda b, i: (b * nc + i, 0)),
            pl.BlockSpec((1, RET_HEADS, RET_DK, RET_DV), lambda b, i: (b, 0, 0, 0)),
        ],
        out_shape=[
            jax.ShapeDtypeStruct((m, d), f32),
            jax.ShapeDtypeStruct((batch, RET_HEADS, RET_DK, RET_DV), f32),
        ],
        scratch_shapes=[
            pltpu.VMEM((RET_HEADS, RET_DK, RET_DV), f32),
            pltpu.VMEM((RET_HEADS, RET_DK, RET_DV), bf16),
            pltpu.VMEM((RET_DK, LANES), f32),
        ],
        compiler_params=_params(("arbitrary", "arbitrary"), pipelined, resident),
        name="ret_prompt",
    )(qkvg, qkvg, qkvg, qkvg, x, wout)


def _rope_tables(pos, per_pair):
    pair_angle = 1.0 / (ROPE_BASE ** jnp.linspace(0.0, 1.0, RET_DK // 2, dtype=f32))
    if per_pair:
        ang = pos[:, None] * pair_angle[None, :]
        return jnp.cos(ang), jnp.sin(ang)
    ang = pos[:, None] * jnp.repeat(pair_angle, 2)[None, :]
    sign = jnp.where(jnp.arange(RET_DK) % 2 == 0, -1.0, 1.0).astype(f32)
    return jnp.cos(ang), jnp.sin(ang) * sign


def _even_odd_perm():
    half = RET_DK // 2
    src = np.concatenate([2 * np.arange(half), 2 * np.arange(half) + 1])
    perm = np.zeros((RET_DK, RET_DK), np.float32)
    perm[src, np.arange(RET_DK)] = 1.0
    return jnp.asarray(perm, dtype=bf16)


def _lane_replicated(scale):
    return jnp.asarray(np.repeat(scale, LANES, axis=1), dtype=f32)


def _ret_decay(rows, c):
    gam = 1.0 - 2.0 ** (-5.0 - np.arange(RET_HEADS, dtype=np.float64))
    lg = np.log(gam)
    steps = (np.arange(rows) % c + 1.0)[:, None]
    q_scale = _lane_replicated(np.exp(lg[None, :] * steps))
    k_scale = _lane_replicated(np.exp(-lg[None, :] * steps) * RET_DK ** -0.5)
    gamma_c = tuple(float(x) for x in np.exp(lg * c))
    gamma = tuple(float(x) for x in gam)
    return q_scale, k_scale, gamma_c, gamma


def kernel(x_prompt, x_sample, state_gla, state_pool, state_ret, norm_mix, norm_ffn, norm_final, w_in_even,
           w_gate_b, b_gate, gla_gain, pool_w, pool_scale, w_out_even, w_in_odd, w_out_odd, w_ffn_gate,
           w_ffn_up, w_ffn_down):
    batch, seq, d = x_prompt.shape
    n_s = x_sample.shape[0]
    assert norm_mix.shape[0] == 2 and x_sample.shape[1] == 1

    we = w_in_even[0].T
    wgb = jnp.concatenate([w_gate_b[0], jnp.zeros((LANES - GATE_RANK, w_gate_b.shape[2]), f32)], axis=0).astype(bf16)
    bg = b_gate[0][None, :]
    gg = gla_gain[0][None, :]
    pw = pool_w[0].astype(bf16)
    ps = pool_scale[0][None, :]
    nm = norm_mix[:, None, :]
    nf = norm_ffn[:, None, :]
    nfin = norm_final[None, :]
    tril = jnp.asarray(np.tril(np.ones((GLA_CHUNK, GLA_CHUNK), np.float32)), dtype=bf16)
    tf = 256
    tm_p = 512
    q_scale, k_scale, gamma_c, gamma = _ret_decay(tm_p, RET_CHUNK)
    tables_p = _rope_tables(jnp.arange(seq, dtype=f32), True) + (q_scale, k_scale)
    tables_s = _rope_tables(jnp.full((n_s,), float(PAST_LEN), f32), False) + (
        _lane_replicated(np.ones((n_s, RET_HEADS))), _lane_replicated(np.full((n_s, RET_HEADS), RET_DK ** -0.5)))
    perm = _even_odd_perm()

    ff = w_ffn_gate.shape[2]
    xp = x_prompt.reshape(batch * seq, d)
    steps_p = batch * seq // tm_p
    ffn_cast = lambda layer: (
        (w_ffn_gate.reshape(-1, ff), d // steps_p, layer * steps_p, steps_p),
        (w_ffn_up.reshape(-1, ff), d // steps_p, layer * steps_p, steps_p),
        (w_ffn_down.reshape(-1, d), 2 * ff // steps_p, layer * steps_p // 2, steps_p // 2))
    whole = lambda w: (w, w.shape[0] // steps_p, 0, steps_p)
    qkvg, loga, u_p, wg0, wu0, wd0, wio, woe, woo = _in_even(
        xp, nm[0], we, wgb, bg, tm=tm_p,
        cast=ffn_cast(0) + (whole(w_in_odd[0]), whole(w_out_even[0]), whole(w_out_odd[0])))
    xp, gla_p = _gla_pool_prompt(qkvg, loga, u_p, xp, tril, gg, pw, ps, woe, batch=batch, t=512)

    xs = x_sample.reshape(n_s, d)
    qkvg_s, loga_s, u_s = _in_even(xs, nm[0], we, wgb, bg, tm=n_s)
    op_s, gla_s, pool_s = _gla_pool_sample(qkvg_s, loga_s, u_s, state_gla[0], state_pool[0].transpose(1, 0, 2),
                                           gg, pw, ps, bb=8)
    pool_s = pool_s.transpose(1, 0, 2)
    xs = _proj_res(xs, op_s, woe, tm=n_s)
    xs = _ffn(xs, nf, wg0, wu0, wd0, nfin, layer=0, tm=n_s, tf=tf, final_norm=False)
    qkvg2_s = _in_odd(xs, nm, wio, perm, tables_s, layer=1, tm=n_s, tn=512, split_halves=False)
    qkvg2_s = qkvg2_s.reshape(n_s, 1, -1)

    rows = n_s // (2 * steps_p)
    xp, og_s, ret_s = _ffn(xp, nf, wg0, wu0, wd0, nfin, layer=0, tm=tm_p, tf=tf, final_norm=False,
                           rider=(qkvg2_s, state_ret[0], rows, gamma))
    qkvg2, wg1, wu1, wd1 = _in_odd(xp, nm, wio, perm, tables_p, layer=1, tm=tm_p, tn=512, split_halves=True,
                                   cast=ffn_cast(1))
    xp, ret_p = _ret_prompt(qkvg2, xp, woo, gamma_c, batch=batch, c=2 * RET_CHUNK, chunk=RET_CHUNK)
    y_prompt = _ffn(xp, nf, wg1, wu1, wd1, nfin, layer=1, tm=tm_p, tf=tf, final_norm=True)
    pool_p = u_p.reshape(batch, seq, -1)[:, seq - POOL_BUF:, :]

    xs = _proj_res(xs, og_s.reshape(n_s, -1), woo, tm=n_s)
    y_sample = _ffn(xs, nf, wg1, wu1, wd1, nfin, layer=1, tm=n_s, tf=tf, final_norm=True)

    return (y_prompt.reshape(batch, seq, d), y_sample.reshape(n_s, 1, d),
            gla_p[None], gla_s[None], pool_p[None], pool_s[None], ret_p[None], ret_s[None])
```

```python
import functools

import numpy as np
import jax
import jax.numpy as jnp
from jax import lax
from jax.experimental import pallas as pl
from jax.experimental.pallas import tpu as pltpu

f32 = jnp.float32
bf16 = jnp.bfloat16

EPS = 1e-6
PAST_LEN = 16384
GLA_HEADS, GLA_DK, GLA_DV = 4, 64, 128
GLA_CHUNK = 64
GATE_RANK = 16
GATE_NORMALIZER = 16.0
POOL_WINDOWS = (2, 4, 8, 16)
POOL_GW = 128
POOL_BUF = max(POOL_WINDOWS) - 1
POOL_HIST = 32
POOL_TAIL = 16
RET_HEADS, RET_DK, RET_DV = 4, 256, 512
RET_CHUNK = 256
ROPE_BASE = 10000.0
LANES = 128
MIB = 1024 * 1024
VMEM_COMPILER_ALLOWANCE = 8 * MIB
F32_BYTES, BF16_BYTES = 4, 2

NT_DIMS = (((1,), (1,)), ((), ()))
TN_DIMS = (((0,), (0,)), ((), ()))


def _params(semantics, pipelined_bytes, resident_bytes):
    limit = 2 * pipelined_bytes + resident_bytes + VMEM_COMPILER_ALLOWANCE
    return pltpu.CompilerParams(dimension_semantics=semantics, vmem_limit_bytes=int(limit))


def _rms(x, gain=None):
    y = x * lax.rsqrt(jnp.mean(x * x, axis=-1, keepdims=True) + EPS)
    return y if gain is None else y * gain


def _silu(g):
    return g * jax.nn.sigmoid(g)


def _in_even_body(x_ref, gain_ref, w_ref, wgb_ref, bg_ref, qkvg_ref, loga_ref, u_ref, h_ref, wq_ref, wu_ref, *, tn):
    nq = qkvg_ref.shape[1]
    nu = u_ref.shape[1]

    @pl.when(pl.program_id(0) == 0)
    def _():
        for r0 in range(0, nq + LANES, LANES):
            wq_ref[:, r0:r0 + LANES] = w_ref[r0:r0 + LANES, :].T.astype(bf16)
        for r0 in range(0, nu, LANES):
            wu_ref[:, r0:r0 + LANES] = w_ref[nq + GATE_RANK + r0:nq + GATE_RANK + r0 + LANES, :].T.astype(bf16)

    h_ref[...] = _rms(x_ref[...], gain_ref[...]).astype(bf16)
    a = jnp.dot(h_ref[...], wq_ref[:, nq:nq + LANES], preferred_element_type=f32)
    a = jnp.where(lax.broadcasted_iota(jnp.int32, a.shape, 1) < GATE_RANK, a, 0.0).astype(bf16)
    for c0 in range(0, nq, tn):
        qkvg_ref[:, c0:c0 + tn] = jnp.dot(h_ref[...], wq_ref[:, c0:c0 + tn], preferred_element_type=f32).astype(bf16)
        if c0 == 0:
            z = jnp.dot(a, wgb_ref[...], preferred_element_type=f32) + bg_ref[...]
            loga_ref[...] = (jnp.minimum(z, 0.0) - jnp.log1p(jnp.exp(-jnp.abs(z)))) * (1.0 / GATE_NORMALIZER)
    for c0 in range(0, nu, tn):
        u_ref[:, c0:c0 + tn] = jnp.dot(h_ref[...], wu_ref[:, c0:c0 + tn], preferred_element_type=f32)


def _cast_rider(cast, steps):
    arrays, in_specs, out_specs, out_shapes, nbytes = [], [], [], [], 0
    for arr, rows, first, count in cast:
        assert count <= steps and rows % 16 == 0 and (first + count) * rows <= arr.shape[0]
        cols = arr.shape[1]
        arrays.append(arr)
        in_specs.append(pl.BlockSpec(
            (rows, cols), lambda i, first=first, count=count: (first + jnp.minimum(i, count - 1), 0)))
        out_specs.append(pl.BlockSpec((rows, cols), lambda i, count=count: (jnp.minimum(i, count - 1), 0)))
        out_shapes.append(jax.ShapeDtypeStruct((count * rows, cols), bf16))
        nbytes += rows * cols * (F32_BYTES + BF16_BYTES)
    return arrays, in_specs, out_specs, out_shapes, nbytes


def _with_cast_rider(body, n_in, n_out, n_cast):
    def wrapped(*refs):
        rest = refs[n_in + n_cast:]
        body(*refs[:n_in], *rest[:n_out], *rest[n_out + n_cast:])
        for src_ref, dst_ref in zip(refs[n_in:n_in + n_cast], rest[n_out:n_out + n_cast]):
            dst_ref[...] = src_ref[...].astype(bf16)
    return wrapped


def _in_even(x, gain, w, wgb, bg, *, tm, cast=()):
    m, d = x.shape
    steps = m // tm
    nq = 2 * GLA_HEADS * GLA_DK + 2 * GLA_HEADS * GLA_DV
    nu = POOL_GW * len(POOL_WINDOWS)
    nk = GLA_HEADS * GLA_DK
    assert w.shape == (nq + GATE_RANK + nu, d)
    const = lambda i: (0, 0)
    tn = 512
    cast_args, cast_in, cast_out, cast_shapes, cast_bytes = _cast_rider(cast, steps)
    pipelined = tm * (d * F32_BYTES + nq * BF16_BYTES + nk * F32_BYTES + nu * F32_BYTES) + cast_bytes
    resident = (w.size * F32_BYTES + (2 * wgb.size + tm * d + d * (nq + LANES) + d * nu) * BF16_BYTES
                + 2 * tm * tn * F32_BYTES)
    return pl.pallas_call(
        _with_cast_rider(functools.partial(_in_even_body, tn=tn), 5, 3, len(cast)),
        grid=(steps,),
        in_specs=[
            pl.BlockSpec((tm, d), lambda i: (i, 0)),
            pl.BlockSpec((1, d), const),
            pl.BlockSpec(w.shape, const, pipeline_mode=pl.Buffered(1)),
            pl.BlockSpec(wgb.shape, const),
            pl.BlockSpec((1, nk), const),
        ] + cast_in,
        out_specs=[
            pl.BlockSpec((tm, nq), lambda i: (i, 0)),
            pl.BlockSpec((tm, nk), lambda i: (i, 0)),
            pl.BlockSpec((tm, nu), lambda i: (i, 0)),
        ] + cast_out,
        out_shape=[
            jax.ShapeDtypeStruct((m, nq), bf16),
            jax.ShapeDtypeStruct((m, nk), f32),
            jax.ShapeDtypeStruct((m, nu), f32),
        ] + cast_shapes,
        scratch_shapes=[pltpu.VMEM((tm, d), bf16), pltpu.VMEM((d, nq + LANES), bf16), pltpu.VMEM((d, nu), bf16)],
        compiler_params=_params(("arbitrary",), pipelined, resident),
        name="in_even",
    )(x, gain, w, wgb, bg, *cast_args)


def _gla_pool_prompt_body(qkvg_ref, loga_ref, u_ref, x_ref, tril_ref, gain_ref, pw_ref, ps_ref, wout_ref,
                          xo_ref, so_ref, st_ref, o_ref, e_ref, p_ref, q_ref, op_ref):
    t = x_ref.shape[0]
    ck = GLA_CHUNK
    kw = GLA_HEADS * GLA_DK
    vw = GLA_HEADS * GLA_DV
    pair_w = 2 * GLA_DK
    i = pl.program_id(1)

    @pl.when(i == 0)
    def _():
        st_ref[...] = jnp.zeros_like(st_ref)
        e_ref[0:POOL_HIST, :] = jnp.zeros((POOL_HIST, e_ref.shape[1]), f32)

    tril = tril_ref[...]
    row = lax.broadcasted_iota(jnp.int32, (2 * ck, pair_w), 0)
    lane = lax.broadcasted_iota(jnp.int32, (2 * ck, pair_w), 1)
    first_lanes = lane < GLA_DK
    first_lanes_ck = lax.broadcasted_iota(jnp.int32, (ck, pair_w), 1) < GLA_DK
    same_head = (row < ck) == first_lanes
    causal = same_head & ((row % ck) >= (lane % GLA_DK))
    pairs = range(GLA_HEADS // 2)
    chunks = range(t // ck)

    hist = POOL_HIST
    n = t + hist
    gw = POOL_GW
    u = u_ref[...]
    e_ref[hist:n, :] = u
    p_ref[8:n, :] = e_ref[8:n, :] + e_ref[7:n - 1, :]
    q_ref[16:n, gw:] = p_ref[16:n, gw:] + p_ref[14:n - 2, gw:]
    p_ref[24:n, 2 * gw:] = q_ref[24:n, 2 * gw:] + q_ref[20:n - 4, 2 * gw:]
    q_ref[32:n, 3 * gw:] = p_ref[32:n, 3 * gw:] + p_ref[24:n - 8, 3 * gw:]

    def rows_of(c):
        return slice(c * ck, (c + 1) * ck)

    def v_pair(c, p):
        va = qkvg_ref[rows_of(c), 2 * kw + (2 * p) * GLA_DV:2 * kw + (2 * p + 1) * GLA_DV]
        vb = qkvg_ref[rows_of(c), 2 * kw + (2 * p + 1) * GLA_DV:2 * kw + (2 * p + 2) * GLA_DV]
        return va, vb

    bcs = []
    for c in chunks:
        la = loga_ref[rows_of(c), :]
        la_hi = la.astype(bf16)
        la_lo = (la - la_hi.astype(f32)).astype(bf16)
        bcs.append(jnp.dot(tril, la_hi, preferred_element_type=f32) + jnp.dot(tril, la_lo, preferred_element_type=f32))
    lhs_q, ke2, kds, elast = [], [], [], []
    for c in chunks:
        bc = bcs[c]
        blast = bc[ck - 1:ck, :]
        q = qkvg_ref[rows_of(c), 0:kw].astype(f32) * (GLA_DK ** -0.5)
        k = qkvg_ref[rows_of(c), kw:2 * kw].astype(f32)
        qe = q * jnp.exp(bc)
        ke = (k * jnp.exp(-bc)).astype(bf16)
        kds.append((k * jnp.exp(blast - bc)).astype(bf16))
        elast.append(jnp.exp(blast))
        for p in pairs:
            pl_ = slice(p * pair_w, (p + 1) * pair_w)
            qe_p = qe[:, pl_]
            lhs_q.append(jnp.concatenate([jnp.where(first_lanes_ck, qe_p, 0.0),
                                          jnp.where(first_lanes_ck, 0.0, qe_p)], axis=0).astype(bf16))
            ke2.append(jnp.concatenate([ke[:, pl_], ke[:, pl_]], axis=0))
    att, upd = [], []
    for c in chunks:
        for p in pairs:
            idx = c * len(pairs) + p
            a = lax.dot_general(lhs_q[idx], ke2[idx], NT_DIMS, preferred_element_type=f32)
            att.append(jnp.where(causal, a, 0.0).astype(bf16))
            va, vb = v_pair(c, p)
            r = lax.dot_general(jnp.concatenate([va, vb], axis=1), kds[c][:, p * pair_w:(p + 1) * pair_w], TN_DIMS,
                                preferred_element_type=f32)
            upd.append(jnp.where(first_lanes, r[:GLA_DV], r[GLA_DV:]))
    st = [st_ref[p] for p in pairs]
    for c in chunks:
        for p in pairs:
            idx = c * len(pairs) + p
            va, vb = v_pair(c, p)
            o = lax.dot_general(lhs_q[idx], st[p].astype(bf16), NT_DIMS, preferred_element_type=f32)
            o = o + jnp.dot(att[idx], jnp.concatenate([va, vb], axis=0), preferred_element_type=f32)
            o_ref[rows_of(c), (2 * p) * GLA_DV:(2 * p + 1) * GLA_DV] = o[:ck]
            o_ref[rows_of(c), (2 * p + 1) * GLA_DV:(2 * p + 2) * GLA_DV] = o[ck:]
            st[p] = st[p] * elast[c][:, p * pair_w:(p + 1) * pair_w] + upd[idx]
    for p in pairs:
        st_ref[p] = st[p]

    sums = (p_ref, q_ref, p_ref, q_ref)
    pos = i * t + lax.broadcasted_iota(jnp.int32, (t, 1), 0)
    for gi, w in enumerate(POOL_WINDOWS):
        ls = slice(gi * gw, (gi + 1) * gw)
        cnt = jnp.minimum(w, pos + 1).astype(f32)
        pooled = (sums[gi][hist:n, ls] / cnt - u[:, ls]).astype(bf16)
        pg = jnp.dot(pooled, pw_ref[gi], preferred_element_type=f32) * ps_ref[:, ls]
        op_ref[:, vw + gi * gw:vw + (gi + 1) * gw] = pg.astype(bf16)
    e_ref[hist - POOL_TAIL:hist, :] = e_ref[n - POOL_TAIL:n, :]

    piece = 2 * GLA_DV
    y = x_ref[...]
    for c0 in (vw, vw + piece):
        y = y + jnp.dot(op_ref[:, c0:c0 + piece], wout_ref[c0:c0 + piece, :], preferred_element_type=f32)
    gain = gain_ref[...]
    for p in pairs:
        for h in (2 * p, 2 * p + 1):
            hs = slice(h * GLA_DV, (h + 1) * GLA_DV)
            g = qkvg_ref[:, 2 * kw + vw + h * GLA_DV:2 * kw + vw + (h + 1) * GLA_DV].astype(f32)
            op_ref[:, hs] = (_rms(o_ref[:, hs], gain) * _silu(g)).astype(bf16)
        c0 = p * piece
        y = y + jnp.dot(op_ref[:, c0:c0 + piece], wout_ref[c0:c0 + piece, :], preferred_element_type=f32)
    xo_ref[...] = y

    @pl.when(i == pl.num_programs(1) - 1)
    def _():
        for p in range(GLA_HEADS // 2):
            s_pair = st_ref[p].T
            so_ref[0, 2 * p] = s_pair[:GLA_DK]
            so_ref[0, 2 * p + 1] = s_pair[GLA_DK:]


def _gla_pool_prompt(qkvg, loga, u, x, tril, gain, pw, ps, wout, *, batch, t):
    m, d = x.shape
    nt = m // batch // t
    row = lambda b, i: (b * nt + i, 0)
    const2 = lambda b, i: (0, 0)
    vw = GLA_HEADS * GLA_DV
    uw = u.shape[1]
    pipelined = t * (qkvg.shape[1] * BF16_BYTES + (loga.shape[1] + uw + 2 * d) * F32_BYTES)
    scratch = (t * vw + 3 * (POOL_HIST + t) * uw) * F32_BYTES + t * (vw + uw) * BF16_BYTES
    waves = (t // GLA_CHUNK) * (GLA_HEADS // 2) * (3 * LANES * LANES * BF16_BYTES + LANES * LANES * F32_BYTES)
    resident = 2 * (wout.size + pw.size) * BF16_BYTES + scratch + waves + t * d * F32_BYTES
    return pl.pallas_call(
        _gla_pool_prompt_body,
        grid=(batch, nt),
        in_specs=[
            pl.BlockSpec((t, qkvg.shape[1]), row),
            pl.BlockSpec((t, loga.shape[1]), row),
            pl.BlockSpec((t, uw), row),
            pl.BlockSpec((t, d), row),
            pl.BlockSpec(tril.shape, const2),
            pl.BlockSpec(gain.shape, const2),
            pl.BlockSpec(pw.shape, lambda b, i: (0, 0, 0)),
            pl.BlockSpec(ps.shape, const2),
            pl.BlockSpec(wout.shape, const2),
        ],
        out_specs=[
            pl.BlockSpec((t, d), row),
            pl.BlockSpec((1, GLA_HEADS, GLA_DK, GLA_DV), lambda b, i: (b, 0, 0, 0)),
        ],
        out_shape=[
            jax.ShapeDtypeStruct((m, d), f32),
            jax.ShapeDtypeStruct((batch, GLA_HEADS, GLA_DK, GLA_DV), f32),
        ],
        scratch_shapes=[
            pltpu.VMEM((GLA_HEADS // 2, GLA_DV, 2 * GLA_DK), f32),
            pltpu.VMEM((t, vw), f32),
            pltpu.VMEM((POOL_HIST + t, uw), f32),
            pltpu.VMEM((POOL_HIST + t, uw), f32),
            pltpu.VMEM((POOL_HIST + t, uw), f32),
            pltpu.VMEM((t, vw + uw), bf16),
        ],
        compiler_params=_params(("arbitrary", "arbitrary"), pipelined, resident),
        name="gla_pool_prompt",
    )(qkvg, loga, u, x, tril, gain, pw, ps, wout)


def _gla_pool_sample_body(qkvg_ref, loga_ref, u_ref, s_ref, buf_ref, gain_ref, pw_ref, ps_ref,
                          op_ref, so_ref):
    bb = u_ref.shape[0]
    kw = GLA_HEADS * GLA_DK
    vw = GLA_HEADS * GLA_DV
    gain = gain_ref[...]
    qkvg = qkvg_ref[...].astype(f32)
    alpha = jnp.exp(loga_ref[...])
    qs = qkvg[:, 0:kw] * (GLA_DK ** -0.5)
    k = qkvg[:, kw:2 * kw]

    def column(row):
        return jnp.broadcast_to(row, (LANES, kw)).T

    o_rows = []
    for b in range(bb):
        acol = column(alpha[b:b + 1, :])
        qcol = column(qs[b:b + 1, :])
        kcol = column(k[b:b + 1, :])
        o_heads = []
        for h in range(GLA_HEADS):
            ks = slice(h * GLA_DK, (h + 1) * GLA_DK)
            v = qkvg[b:b + 1, 2 * kw + h * GLA_DV:2 * kw + (h + 1) * GLA_DV]
            s_new = acol[ks, :] * s_ref[b, h] + kcol[ks, :] * v
            so_ref[b, h] = s_new
            o = jnp.sum(qcol[ks, :] * s_new, axis=0, keepdims=True)
            g = qkvg[b:b + 1, 2 * kw + vw + h * GLA_DV:2 * kw + vw + (h + 1) * GLA_DV]
            o_heads.append(_rms(o, gain) * _silu(g))
        o_rows.append(jnp.concatenate(o_heads, axis=1))
    op_ref[:, 0:vw] = jnp.concatenate(o_rows, axis=0).astype(bf16)

    u = u_ref[...]
    for gi, w in enumerate(POOL_WINDOWS):
        ls = slice(gi * POOL_GW, (gi + 1) * POOL_GW)
        s = u[:, ls] + jnp.sum(buf_ref[:, POOL_BUF - (w - 1):POOL_BUF, ls], axis=1)
        cnt = float(min(w, PAST_LEN + 1))
        pooled = (s / cnt - u[:, ls]).astype(bf16)
        pg = jnp.dot(pooled, pw_ref[gi], preferred_element_type=f32) * ps_ref[:, ls]
        op_ref[:, vw + gi * POOL_GW:vw + (gi + 1) * POOL_GW] = pg.astype(bf16)


def _gla_pool_sample(qkvg, loga, u, s, buf, gain, pw, ps, *, bb):
    n = u.shape[0]
    row = lambda i: (i, 0)
    const2 = lambda i: (0, 0)
    ow = GLA_HEADS * GLA_DV + POOL_GW * len(POOL_WINDOWS)
    state_rows = int(np.prod(s.shape[1:]))
    pipelined = bb * ((qkvg.shape[1] + ow) * BF16_BYTES
                      + (loga.shape[1] + u.shape[1] + 2 * state_rows + POOL_TAIL * buf.shape[2]) * F32_BYTES)
    resident = 2 * pw.size * BF16_BYTES + 3 * LANES * loga.shape[1] * F32_BYTES
    return pl.pallas_call(
        _gla_pool_sample_body,
        grid=(n // bb,),
        in_specs=[
            pl.BlockSpec((bb, qkvg.shape[1]), row),
            pl.BlockSpec((bb, loga.shape[1]), row),
            pl.BlockSpec((bb, u.shape[1]), row),
            pl.BlockSpec((bb,) + s.shape[1:], lambda i: (i, 0, 0, 0)),
            pl.BlockSpec((bb,) + buf.shape[1:], lambda i: (i, 0, 0)),
            pl.BlockSpec(gain.shape, const2),
            pl.BlockSpec(pw.shape, lambda i: (0, 0, 0)),
            pl.BlockSpec(ps.shape, const2),
        ],
        out_specs=[
            pl.BlockSpec((bb, ow), row),
            pl.BlockSpec((bb,) + s.shape[1:], lambda i: (i, 0, 0, 0)),
        ],
        out_shape=[
            jax.ShapeDtypeStruct((n, ow), bf16),
            jax.ShapeDtypeStruct(s.shape, f32),
        ],
        compiler_params=_params(("arbitrary",), pipelined, resident),
        name="gla_pool_sample",
    )(qkvg, loga, u, s, buf, gain, pw, ps)


def _proj_res_body(x_ref, a_ref, w_ref, o_ref):
    o_ref[...] = x_ref[...] + jnp.dot(a_ref[...], w_ref[...], preferred_element_type=f32)


def _proj_res(x, a, w, *, tm):
    m, d = x.shape
    return pl.pallas_call(
        _proj_res_body,
        grid=(m // tm,),
        in_specs=[
            pl.BlockSpec((tm, d), lambda i: (i, 0)),
            pl.BlockSpec((tm, a.shape[1]), lambda i: (i, 0)),
            pl.BlockSpec(w.shape, lambda i: (0, 0)),
        ],
        out_specs=pl.BlockSpec((tm, d), lambda i: (i, 0)),
        out_shape=jax.ShapeDtypeStruct((m, d), f32),
        compiler_params=_params(("arbitrary",), tm * (2 * d * F32_BYTES + a.shape[1] * BF16_BYTES),
                                2 * w.size * BF16_BYTES),
        name="proj_res",
    )(x, a, w)


def _ret_token_pieces(q_ref, k_ref, v_ref, g_ref, s_ref, og_ref, so_ref, gamma):
    def piece(j, h):
        def run():
            ks = slice(h * RET_DK, (h + 1) * RET_DK)
            vs = slice(h * RET_DV, (h + 1) * RET_DV)
            qcol = jnp.broadcast_to(q_ref[j, :, ks].astype(f32), (LANES, RET_DK)).T
            kcol = jnp.broadcast_to(k_ref[j, :, ks].astype(f32), (LANES, RET_DK)).T
            v = v_ref[j, :, vs].astype(f32)
            g = g_ref[j, :, vs].astype(f32)
            o_tiles = []
            for t in range(RET_DV // LANES):
                cs = slice(t * LANES, (t + 1) * LANES)
                s_new = gamma[h] * s_ref[j, h, :, cs] + kcol * v[:, cs]
                so_ref[j, h, :, cs] = s_new
                o_tiles.append(jnp.sum(qcol * s_new, axis=0, keepdims=True))
            o = jnp.concatenate(o_tiles, axis=1)
            og_ref[j, :, vs] = (_rms(o) * _silu(g)).astype(bf16)
            return o
        return run

    return [piece(j, h) for j in range(s_ref.shape[0]) for h in range(RET_HEADS)]


def _ffn_body(*refs, tf, n_sub, final_norm, rider_gamma):
    x_ref, gain_ref, wg_ref, wu_ref, wd_ref, fgain_ref = refs[:6]
    pieces = []
    if rider_gamma is None:
        o_ref, h_ref, acc_ref = refs[6:]
    else:
        rq_ref, rk_ref, rv_ref, rg_ref, rs_ref, o_ref, rog_ref, rso_ref, h_ref, acc_ref = refs[6:]
        pieces = _ret_token_pieces(rq_ref, rk_ref, rv_ref, rg_ref, rs_ref, rog_ref, rso_ref, rider_gamma)
    n_chunks = wg_ref.shape[1] // tf
    bounds = [n_chunks * s // n_sub for s in range(n_sub + 1)]

    def exact_zero(v):
        bits = lax.bitcast_convert_type(v, jnp.uint32)
        return ((bits >> 16) >> 16).astype(f32)

    def run_chunks(chunks):
        pin = None
        for n, c in enumerate(chunks):
            cs = slice(c * tf, (c + 1) * tf)
            g = jnp.dot(h_ref[...], wg_ref[:, cs], preferred_element_type=f32)
            if pin is not None:
                g = g + pin
                pin = None
            u = jnp.dot(h_ref[...], wu_ref[:, cs], preferred_element_type=f32)
            a = (_silu(g) * u).astype(bf16)
            part = jnp.dot(a, wd_ref[cs, :], preferred_element_type=f32)
            if c == 0:
                acc_ref[...] = part
            else:
                acc_ref[...] += part
            for p in range(len(pieces)):
                if p * (len(chunks) - 1) // len(pieces) == n:
                    z = exact_zero(pieces[p]()[:, :tf])
                    pin = z if pin is None else pin + z

    def sub_step(s):
        if s == 0:
            h_ref[...] = _rms(x_ref[...], gain_ref[...]).astype(bf16)
        run_chunks(range(bounds[s], bounds[s + 1]))
        if s == n_sub - 1:
            y = x_ref[...] + acc_ref[...]
            if final_norm:
                y = _rms(y, fgain_ref[...])
            o_ref[...] = y

    if n_sub == 1:
        sub_step(0)
    else:
        for s in range(n_sub):
            pl.when(pl.program_id(1) == s)(functools.partial(sub_step, s))


def _ffn(x, gain, wg, wu, wd, fgain, *, layer, tm, tf, final_norm, rider=None):
    m, d = x.shape
    ff = wg.shape[1]
    steps = m // tm
    n_sub = 1 if rider is None else 2
    single_buffered = dict(pipeline_mode=pl.Buffered(1))
    in_specs = [
        pl.BlockSpec((tm, d), lambda i, s: (i, 0)),
        pl.BlockSpec((None, 1, d), lambda i, s: (layer, 0, 0)),
        pl.BlockSpec((d, ff), lambda i, s: (0, 0), **single_buffered),
        pl.BlockSpec((d, ff), lambda i, s: (0, 0), **single_buffered),
        pl.BlockSpec((ff, d), lambda i, s: (0, 0), **single_buffered),
        pl.BlockSpec((1, d), lambda i, s: (0, 0)),
    ]
    args = [x, gain, wg, wu, wd, fgain]
    out_specs = [pl.BlockSpec((tm, d), lambda i, s: (i, 0))]
    out_shape = [jax.ShapeDtypeStruct((m, d), f32)]
    gamma = None
    pipelined = 2 * tm * d * F32_BYTES
    resident = (3 * d * ff * BF16_BYTES + tm * d * (BF16_BYTES + F32_BYTES)
                + 3 * tm * tf * F32_BYTES + tm * d * F32_BYTES)
    if rider is not None:
        qkvg3, state, rows, gamma = rider
        assert 2 * steps * rows == state.shape[0]
        qw = RET_HEADS * RET_DK
        vw = RET_HEADS * RET_DV
        blk = lambda col: (lambda i, s: (2 * i + s, 0, col))
        state_spec = pl.BlockSpec((rows,) + state.shape[1:], lambda i, s: (2 * i + s, 0, 0, 0))
        in_specs += [
            pl.BlockSpec((rows, 1, qw), blk(0)),
            pl.BlockSpec((rows, 1, qw), blk(1)),
            pl.BlockSpec((rows, 1, vw), blk(1)),
            pl.BlockSpec((rows, 1, vw), blk(2)),
            state_spec,
        ]
        args += [qkvg3, qkvg3, qkvg3, qkvg3, state]
        out_specs += [pl.BlockSpec((rows, 1, vw), blk(0)), state_spec]
        out_shape += [
            jax.ShapeDtypeStruct((state.shape[0], 1, vw), bf16),
            jax.ShapeDtypeStruct(state.shape, f32),
        ]
        pipelined += 2 * rows * int(np.prod(state.shape[1:])) * F32_BYTES
    out = pl.pallas_call(
        functools.partial(_ffn_body, tf=tf, n_sub=n_sub, final_norm=final_norm, rider_gamma=gamma),
        grid=(steps, n_sub),
        in_specs=in_specs,
        out_specs=out_specs,
        out_shape=out_shape,
        scratch_shapes=[pltpu.VMEM((tm, d), bf16), pltpu.VMEM((tm, d), f32)],
        compiler_params=_params(("arbitrary", "arbitrary"), pipelined, resident),
        name="ffn_final" if final_norm else "ffn",
    )(*args)
    return out[0] if rider is None else out


def _in_odd_body(x_ref, gain_ref, w_ref, perm_ref, cos_ref, sin_ref, qsc_ref, ksc_ref, o_ref, h_ref, wqk_ref, *,
                 tn, split_halves):
    qw = RET_HEADS * RET_DK
    half = RET_DK // 2
    if split_halves:
        @pl.when(pl.program_id(0) == 0)
        def _():
            for hh in range(2 * RET_HEADS):
                hs = slice(hh * RET_DK, (hh + 1) * RET_DK)
                wqk_ref[:, hs] = jnp.dot(w_ref[:, hs], perm_ref[...], preferred_element_type=f32).astype(bf16)

    h_ref[...] = _rms(x_ref[...], gain_ref[...]).astype(bf16)
    cos = cos_ref[...]
    sin = sin_ref[...]
    for c in range(2 * qw // tn):
        c0 = c * tn
        w_chunk = wqk_ref[:, c0:c0 + tn] if split_halves else w_ref[:, c0:c0 + tn]
        p = jnp.dot(h_ref[...], w_chunk, preferred_element_type=f32)
        sc_ref = qsc_ref if c0 < qw else ksc_ref
        for hh in range(tn // RET_DK):
            h0 = hh * RET_DK
            head = (c0 % qw + h0) // RET_DK
            sc = sc_ref[:, head * LANES:(head + 1) * LANES]
            if split_halves:
                ev = p[:, h0:h0 + half]
                od = p[:, h0 + half:h0 + RET_DK]
                o_ref[:, c0 + h0:c0 + h0 + half] = ((ev * cos - od * sin) * sc).astype(bf16)
                o_ref[:, c0 + h0 + half:c0 + h0 + RET_DK] = ((od * cos + ev * sin) * sc).astype(bf16)
            else:
                xh = p[:, h0:h0 + RET_DK]
                even = lax.broadcasted_iota(jnp.int32, xh.shape, 1) % 2 == 0
                partner = jnp.where(even, pltpu.roll(xh, RET_DK - 1, 1), pltpu.roll(xh, 1, 1))
                r = xh * cos + partner * sin
                o_ref[:, c0 + h0:c0 + h0 + half] = (r[:, :half] * sc).astype(bf16)
                o_ref[:, c0 + h0 + half:c0 + h0 + RET_DK] = (r[:, half:] * sc).astype(bf16)
    for c0 in range(2 * qw, w_ref.shape[1], tn):
        p = jnp.dot(h_ref[...], w_ref[:, c0:c0 + tn], preferred_element_type=f32)
        o_ref[:, c0:c0 + tn] = p.astype(bf16)


def _in_odd(x, gain, w, perm, tables, *, layer, tm, tn, split_halves, cast=()):
    m, d = x.shape
    n = w.shape[1]
    cos, sin, qsc, ksc = tables
    ntab = cos.shape[0] // tm
    qkw = 2 * RET_HEADS * RET_DK
    single_buffered = dict(pipeline_mode=pl.Buffered(1))
    const = lambda i: (0, 0)
    rope_spec = pl.BlockSpec((tm, cos.shape[1]), lambda i: (i % ntab, 0))
    wqk_shape = (d, qkw) if split_halves else (8, LANES)
    cast_args, cast_in, cast_out, cast_shapes, cast_bytes = _cast_rider(cast, m // tm)
    pipelined = tm * (d * F32_BYTES + n * BF16_BYTES + 2 * cos.shape[1] * F32_BYTES) + cast_bytes
    resident = ((w.size + 2 * perm.size + tm * d + wqk_shape[0] * wqk_shape[1]) * BF16_BYTES
                + 2 * (qsc.size + ksc.size) * F32_BYTES + 2 * tm * tn * F32_BYTES)
    outs = pl.pallas_call(
        _with_cast_rider(functools.partial(_in_odd_body, tn=tn, split_halves=split_halves), 8, 1, len(cast)),
        grid=(m // tm,),
        in_specs=[
            pl.BlockSpec((tm, d), lambda i: (i, 0)),
            pl.BlockSpec((None, 1, d), lambda i: (layer, 0, 0)),
            pl.BlockSpec(w.shape, const, **single_buffered),
            pl.BlockSpec(perm.shape, const),
            rope_spec, rope_spec,
            pl.BlockSpec(qsc.shape, const),
            pl.BlockSpec(ksc.shape, const),
        ] + cast_in,
        out_specs=[pl.BlockSpec((tm, n), lambda i: (i, 0))] + cast_out,
        out_shape=[jax.ShapeDtypeStruct((m, n), bf16)] + cast_shapes,
        scratch_shapes=[pltpu.VMEM((tm, d), bf16), pltpu.VMEM(wqk_shape, bf16)],
        compiler_params=_params(("arbitrary",), pipelined, resident),
        name="in_odd",
    )(x, gain, w, perm, cos, sin, qsc, ksc, *cast_args)
    return tuple(outs) if cast else outs[0]


def _ret_prompt_body(q_ref, k_ref, v_ref, g_ref, x_ref, wout_ref, xo_ref, so_ref, s_ref, sb_ref, slab_ref, *,
                     gamma_c, n):
    c = pl.program_id(1)
    subs = [slice(j * n, (j + 1) * n) for j in range(q_ref.shape[0] // n)]

    @pl.when(c == 0)
    def _():
        s_ref[...] = jnp.zeros_like(s_ref)
        sb_ref[...] = jnp.zeros_like(sb_ref)

    causal = lax.broadcasted_iota(jnp.int32, (n, n), 0) >= lax.broadcasted_iota(jnp.int32, (n, n), 1)
    heads = range(RET_HEADS)
    ks = [slice(h * RET_DK, (h + 1) * RET_DK) for h in heads]
    vs = [slice(h * RET_DV, (h + 1) * RET_DV) for h in heads]
    att = [[jnp.where(causal, lax.dot_general(q_ref[r, ks[h]], k_ref[r, ks[h]], NT_DIMS,
                                              preferred_element_type=f32), 0.0).astype(bf16) for h in heads]
           for r in subs]
    o = []
    for j, r in enumerate(subs):
        o.append([jnp.dot(q_ref[r, ks[h]], sb_ref[h], preferred_element_type=f32)
                  + jnp.dot(att[j][h], v_ref[r, vs[h]], preferred_element_type=f32) for h in heads])
        for h in heads:
            kv = lax.dot_general(k_ref[r, ks[h]], v_ref[r, vs[h]], TN_DIMS, preferred_element_type=f32)
            s_new = gamma_c[h] * (s_ref[h] + kv)
            s_ref[h] = s_new
            sb_ref[h] = s_new.astype(bf16)
    for j, r in enumerate(subs):
        y = x_ref[r, :]
        for h in heads:
            og = (_rms(o[j][h]) * _silu(g_ref[r, vs[h]].astype(f32))).astype(bf16)
            y = y + jnp.dot(og, wout_ref[vs[h], :], preferred_element_type=f32)
        xo_ref[r, :] = y

    @pl.when(c == pl.num_programs(1) - 1)
    def _():
        half = RET_DK // 2
        for h in range(RET_HEADS):
            for t in range(RET_DV // LANES):
                ls = slice(t * LANES, (t + 1) * LANES)
                slab_ref[pl.ds(0, half, stride=2), :] = s_ref[h, 0:half, ls]
                slab_ref[pl.ds(1, half, stride=2), :] = s_ref[h, half:RET_DK, ls]
                so_ref[0, h, :, ls] = slab_ref[...]


def _ret_prompt(qkvg, x, wout, gamma_c, *, batch, c, chunk):
    m, d = x.shape
    nc = m // batch // c
    qw = RET_HEADS * RET_DK
    vw = RET_HEADS * RET_DV
    assert c % chunk == 0
    state = RET_HEADS * RET_DK * RET_DV
    pipelined = c * (2 * (qw + vw) * BF16_BYTES + 2 * d * F32_BYTES) + state * F32_BYTES
    resident = (wout.size * BF16_BYTES + state * (F32_BYTES + BF16_BYTES) + RET_DK * LANES * F32_BYTES
                + c * RET_HEADS * (chunk * BF16_BYTES + RET_DV * F32_BYTES)
                + RET_DK * RET_DV * F32_BYTES + c * d * F32_BYTES)
    return pl.pallas_call(
        functools.partial(_ret_prompt_body, gamma_c=gamma_c, n=chunk),
        grid=(batch, nc),
        in_specs=[
            pl.BlockSpec((c, qw), lambda b, i: (b * nc + i, 0)),
            pl.BlockSpec((c, qw), lambda b, i: (b * nc + i, 1)),
            pl.BlockSpec((c, vw), lambda b, i: (b * nc + i, 1)),
            pl.BlockSpec((c, vw), lambda b, i: (b * nc + i, 2)),
            pl.BlockSpec((c, d), lambda b, i: (b * nc + i, 0)),
            pl.BlockSpec(wout.shape, lambda b, i: (0, 0), pipeline_mode=pl.Buffered(1)),
        ],
        out_specs=[
            pl.BlockSpec((c, d), lambda b, i: (b * nc + i, 0)),
            pl.BlockSpec((1, RET_HEADS, RET_DK, RET_DV), lambda b, i: (b, 0, 0, 0)),
        ],
        out_shape=[
            jax.ShapeDtypeStruct((m, d), f32),
            jax.ShapeDtypeStruct((batch, RET_HEADS, RET_DK, RET_DV), f32),
        ],
        scratch_shapes=[
            pltpu.VMEM((RET_HEADS, RET_DK, RET_DV), f32),
            pltpu.VMEM((RET_HEADS, RET_DK, RET_DV), bf16),
            pltpu.VMEM((RET_DK, LANES), f32),
        ],
        compiler_params=_params(("arbitrary", "arbitrary"), pipelined, resident),
        name="ret_prompt",
    )(qkvg, qkvg, qkvg, qkvg, x, wout)


def _rope_tables(pos, per_pair):
    pair_angle = 1.0 / (ROPE_BASE ** jnp.linspace(0.0, 1.0, RET_DK // 2, dtype=f32))
    if per_pair:
        ang = pos[:, None] * pair_angle[None, :]
        return jnp.cos(ang), jnp.sin(ang)
    ang = pos[:, None] * jnp.repeat(pair_angle, 2)[None, :]
    sign = jnp.where(jnp.arange(RET_DK) % 2 == 0, -1.0, 1.0).astype(f32)
    return jnp.cos(ang), jnp.sin(ang) * sign


def _even_odd_perm():
    half = RET_DK // 2
    src = np.concatenate([2 * np.arange(half), 2 * np.arange(half) + 1])
    perm = np.zeros((RET_DK, RET_DK), np.float32)
    perm[src, np.arange(RET_DK)] = 1.0
    return jnp.asarray(perm, dtype=bf16)


def _lane_replicated(scale):
    return jnp.asarray(np.repeat(scale, LANES, axis=1), dtype=f32)


def _ret_decay(rows, c):
    gam = 1.0 - 2.0 ** (-5.0 - np.arange(RET_HEADS, dtype=np.float64))
    lg = np.log(gam)
    steps = (np.arange(rows) % c + 1.0)[:, None]
    q_scale = _lane_replicated(np.exp(lg[None, :] * steps))
    k_scale = _lane_replicated(np.exp(-lg[None, :] * steps) * RET_DK ** -0.5)
    gamma_c = tuple(float(x) for x in np.exp(lg * c))
    gamma = tuple(float(x) for x in gam)
    return q_scale, k_scale, gamma_c, gamma


def kernel(x_prompt, x_sample, state_gla, state_pool, state_ret, norm_mix, norm_ffn, norm_final, w_in_even,
           w_gate_b, b_gate, gla_gain, pool_w, pool_scale, w_out_even, w_in_odd, w_out_odd, w_ffn_gate,
           w_ffn_up, w_ffn_down):
    batch, seq, d = x_prompt.shape
    n_s = x_sample.shape[0]
    assert norm_mix.shape[0] == 2 and x_sample.shape[1] == 1

    we = w_in_even[0].T
    wgb = jnp.concatenate([w_gate_b[0], jnp.zeros((LANES - GATE_RANK, w_gate_b.shape[2]), f32)], axis=0).astype(bf16)
    bg = b_gate[0][None, :]
    gg = gla_gain[0][None, :]
    pw = pool_w[0].astype(bf16)
    ps = pool_scale[0][None, :]
    nm = norm_mix[:, None, :]
    nf = norm_ffn[:, None, :]
    nfin = norm_final[None, :]
    tril = jnp.asarray(np.tril(np.ones((GLA_CHUNK, GLA_CHUNK), np.float32)), dtype=bf16)
    tf = 256
    tm_p = 512
    q_scale, k_scale, gamma_c, gamma = _ret_decay(tm_p, RET_CHUNK)
    tables_p = _rope_tables(jnp.arange(seq, dtype=f32), True) + (q_scale, k_scale)
    tables_s = _rope_tables(jnp.full((n_s,), float(PAST_LEN), f32), False) + (
        _lane_replicated(np.ones((n_s, RET_HEADS))), _lane_replicated(np.full((n_s, RET_HEADS), RET_DK ** -0.5)))
    perm = _even_odd_perm()

    ff = w_ffn_gate.shape[2]
    xp = x_prompt.reshape(batch * seq, d)
    steps_p = batch * seq // tm_p
    ffn_cast = lambda layer: (
        (w_ffn_gate.reshape(-1, ff), d // steps_p, layer * steps_p, steps_p),
        (w_ffn_up.reshape(-1, ff), d // steps_p, layer * steps_p, steps_p),
        (w_ffn_down.reshape(-1, d), 2 * ff // steps_p, layer * steps_p // 2, steps_p // 2))
    whole = lambda w: (w, w.shape[0] // steps_p, 0, steps_p)
    qkvg, loga, u_p, wg0, wu0, wd0, wio, woe, woo = _in_even(
        xp, nm[0], we, wgb, bg, tm=tm_p,
        cast=ffn_cast(0) + (whole(w_in_odd[0]), whole(w_out_even[0]), whole(w_out_odd[0])))
    xp, gla_p = _gla_pool_prompt(qkvg, loga, u_p, xp, tril, gg, pw, ps, woe, batch=batch, t=512)

    xs = x_sample.reshape(n_s, d)
    qkvg_s, loga_s, u_s = _in_even(xs, nm[0], we, wgb, bg, tm=n_s)
    op_s, gla_s = _gla_pool_sample(qkvg_s, loga_s, u_s, state_gla[0], state_pool[0], gg, pw, ps, bb=16)
    xs = _proj_res(xs, op_s, woe, tm=n_s)
    xs = _ffn(xs, nf, wg0, wu0, wd0, nfin, layer=0, tm=n_s, tf=tf, final_norm=False)
    qkvg2_s = _in_odd(xs, nm, wio, perm, tables_s, layer=1, tm=n_s, tn=512, split_halves=False)
    qkvg2_s = qkvg2_s.reshape(n_s, 1, -1)

    rows = n_s // (2 * steps_p)
    xp, og_s, ret_s = _ffn(xp, nf, wg0, wu0, wd0, nfin, layer=0, tm=tm_p, tf=tf, final_norm=False,
                           rider=(qkvg2_s, state_ret[0], rows, gamma))
    qkvg2, wg1, wu1, wd1 = _in_odd(xp, nm, wio, perm, tables_p, layer=1, tm=tm_p, tn=512, split_halves=True,
                                   cast=ffn_cast(1))
    xp, ret_p = _ret_prompt(qkvg2, xp, woo, gamma_c, batch=batch, c=2 * RET_CHUNK, chunk=RET_CHUNK)
    y_prompt = _ffn(xp, nf, wg1, wu1, wd1, nfin, layer=1, tm=tm_p, tf=tf, final_norm=True)
    pool_p = u_p.reshape(batch, seq, -1)[:, seq - POOL_BUF:, :]

    xs = _proj_res(xs, og_s.reshape(n_s, -1), woo, tm=n_s)
    y_sample = _ffn(xs, nf, wg1, wu1, wd1, nfin, layer=1, tm=n_s, tf=tf, final_norm=True)

    pool_s = jnp.concatenate([state_pool[0][:, 1:, :], u_s[:, None, :]], axis=1)

    return (y_prompt.reshape(batch, seq, d), y_sample.reshape(n_s, 1, d),
            gla_p[None], gla_s[None], pool_p[None], pool_s[None], ret_p[None], ret_s[None])
```

```python
import functools

import numpy as np
import jax
import jax.numpy as jnp
from jax import lax
from jax.experimental import pallas as pl
from jax.experimental.pallas import tpu as pltpu

f32 = jnp.float32
bf16 = jnp.bfloat16

EPS = 1e-6
PAST_LEN = 16384
GLA_HEADS, GLA_DK, GLA_DV = 4, 64, 128
GLA_CHUNK = 64
GATE_RANK = 16
GATE_NORMALIZER = 16.0
POOL_WINDOWS = (2, 4, 8, 16)
POOL_GW = 128
POOL_BUF = max(POOL_WINDOWS) - 1
POOL_HIST = 32
POOL_TAIL = 16
RET_HEADS, RET_DK, RET_DV = 4, 256, 512
RET_CHUNK = 256
ROPE_BASE = 10000.0
LANES = 128
MIB = 1024 * 1024
VMEM_COMPILER_ALLOWANCE = 8 * MIB
F32_BYTES, BF16_BYTES = 4, 2

NT_DIMS = (((1,), (1,)), ((), ()))
TN_DIMS = (((0,), (0,)), ((), ()))


def _params(semantics, pipelined_bytes, resident_bytes):
    limit = 2 * pipelined_bytes + resident_bytes + VMEM_COMPILER_ALLOWANCE
    return pltpu.CompilerParams(dimension_semantics=semantics, vmem_limit_bytes=int(limit))


def _rms(x, gain=None):
    y = x * lax.rsqrt(jnp.mean(x * x, axis=-1, keepdims=True) + EPS)
    return y if gain is None else y * gain


def _silu(g):
    return g * jax.nn.sigmoid(g)


def _in_even_body(x_ref, gain_ref, w_ref, wgb_ref, bg_ref, qkvg_ref, loga_ref, u_ref, h_ref, wq_ref, wu_ref, *, tn):
    nq = qkvg_ref.shape[1]
    nu = u_ref.shape[1]

    @pl.when(pl.program_id(0) == 0)
    def _():
        for r0 in range(0, nq + LANES, LANES):
            wq_ref[:, r0:r0 + LANES] = w_ref[r0:r0 + LANES, :].T.astype(bf16)
        for r0 in range(0, nu, LANES):
            wu_ref[:, r0:r0 + LANES] = w_ref[nq + GATE_RANK + r0:nq + GATE_RANK + r0 + LANES, :].T.astype(bf16)

    h_ref[...] = _rms(x_ref[...], gain_ref[...]).astype(bf16)
    a = jnp.dot(h_ref[...], wq_ref[:, nq:nq + LANES], preferred_element_type=f32)
    a = jnp.where(lax.broadcasted_iota(jnp.int32, a.shape, 1) < GATE_RANK, a, 0.0).astype(bf16)
    for c0 in range(0, nq, tn):
        qkvg_ref[:, c0:c0 + tn] = jnp.dot(h_ref[...], wq_ref[:, c0:c0 + tn], preferred_element_type=f32).astype(bf16)
        if c0 == 0:
            z = jnp.dot(a, wgb_ref[...], preferred_element_type=f32) + bg_ref[...]
            loga_ref[...] = (jnp.minimum(z, 0.0) - jnp.log1p(jnp.exp(-jnp.abs(z)))) * (1.0 / GATE_NORMALIZER)
    for c0 in range(0, nu, tn):
        u_ref[:, c0:c0 + tn] = jnp.dot(h_ref[...], wu_ref[:, c0:c0 + tn], preferred_element_type=f32)


def _cast_rider(cast, steps):
    arrays, in_specs, out_specs, out_shapes, nbytes = [], [], [], [], 0
    for arr, rows, first, count in cast:
        assert count <= steps and rows % 16 == 0 and (first + count) * rows <= arr.shape[0]
        cols = arr.shape[1]
        arrays.append(arr)
        in_specs.append(pl.BlockSpec(
            (rows, cols), lambda i, first=first, count=count: (first + jnp.minimum(i, count - 1), 0)))
        out_specs.append(pl.BlockSpec((rows, cols), lambda i, count=count: (jnp.minimum(i, count - 1), 0)))
        out_shapes.append(jax.ShapeDtypeStruct((count * rows, cols), bf16))
        nbytes += rows * cols * (F32_BYTES + BF16_BYTES)
    return arrays, in_specs, out_specs, out_shapes, nbytes


def _with_cast_rider(body, n_in, n_out, n_cast):
    def wrapped(*refs):
        rest = refs[n_in + n_cast:]
        body(*refs[:n_in], *rest[:n_out], *rest[n_out + n_cast:])
        for src_ref, dst_ref in zip(refs[n_in:n_in + n_cast], rest[n_out:n_out + n_cast]):
            dst_ref[...] = src_ref[...].astype(bf16)
    return wrapped


def _in_even(x, gain, w, wgb, bg, *, tm, cast=()):
    m, d = x.shape
    steps = m // tm
    nq = 2 * GLA_HEADS * GLA_DK + 2 * GLA_HEADS * GLA_DV
    nu = POOL_GW * len(POOL_WINDOWS)
    nk = GLA_HEADS * GLA_DK
    assert w.shape == (nq + GATE_RANK + nu, d)
    const = lambda i: (0, 0)
    tn = 512
    cast_args, cast_in, cast_out, cast_shapes, cast_bytes = _cast_rider(cast, steps)
    pipelined = tm * (d * F32_BYTES + nq * BF16_BYTES + nk * F32_BYTES + nu * F32_BYTES) + cast_bytes
    resident = (w.size * F32_BYTES + (2 * wgb.size + tm * d + d * (nq + LANES) + d * nu) * BF16_BYTES
                + 2 * tm * tn * F32_BYTES)
    return pl.pallas_call(
        _with_cast_rider(functools.partial(_in_even_body, tn=tn), 5, 3, len(cast)),
        grid=(steps,),
        in_specs=[
            pl.BlockSpec((tm, d), lambda i: (i, 0)),
            pl.BlockSpec((1, d), const),
            pl.BlockSpec(w.shape, const, pipeline_mode=pl.Buffered(1)),
            pl.BlockSpec(wgb.shape, const),
            pl.BlockSpec((1, nk), const),
        ] + cast_in,
        out_specs=[
            pl.BlockSpec((tm, nq), lambda i: (i, 0)),
            pl.BlockSpec((tm, nk), lambda i: (i, 0)),
            pl.BlockSpec((tm, nu), lambda i: (i, 0)),
        ] + cast_out,
        out_shape=[
            jax.ShapeDtypeStruct((m, nq), bf16),
            jax.ShapeDtypeStruct((m, nk), f32),
            jax.ShapeDtypeStruct((m, nu), f32),
        ] + cast_shapes,
        scratch_shapes=[pltpu.VMEM((tm, d), bf16), pltpu.VMEM((d, nq + LANES), bf16), pltpu.VMEM((d, nu), bf16)],
        compiler_params=_params(("arbitrary",), pipelined, resident),
        name="in_even",
    )(x, gain, w, wgb, bg, *cast_args)


def _gla_pool_prompt_body(qkvg_ref, loga_ref, u_ref, x_ref, tril_ref, gain_ref, pw_ref, ps_ref, wout_ref,
                          xo_ref, so_ref, st_ref, o_ref, e_ref, p_ref, q_ref, op_ref):
    t = x_ref.shape[0]
    ck = GLA_CHUNK
    kw = GLA_HEADS * GLA_DK
    vw = GLA_HEADS * GLA_DV
    pair_w = 2 * GLA_DK
    i = pl.program_id(1)

    @pl.when(i == 0)
    def _():
        st_ref[...] = jnp.zeros_like(st_ref)
        e_ref[0:POOL_HIST, :] = jnp.zeros((POOL_HIST, e_ref.shape[1]), f32)

    tril = tril_ref[...]
    row = lax.broadcasted_iota(jnp.int32, (2 * ck, pair_w), 0)
    lane = lax.broadcasted_iota(jnp.int32, (2 * ck, pair_w), 1)
    first_lanes = lane < GLA_DK
    first_lanes_ck = lax.broadcasted_iota(jnp.int32, (ck, pair_w), 1) < GLA_DK
    same_head = (row < ck) == first_lanes
    causal = same_head & ((row % ck) >= (lane % GLA_DK))
    pairs = range(GLA_HEADS // 2)
    chunks = range(t // ck)

    hist = POOL_HIST
    n = t + hist
    gw = POOL_GW
    u = u_ref[...]
    e_ref[hist:n, :] = u
    p_ref[8:n, :] = e_ref[8:n, :] + e_ref[7:n - 1, :]
    q_ref[16:n, gw:] = p_ref[16:n, gw:] + p_ref[14:n - 2, gw:]
    p_ref[24:n, 2 * gw:] = q_ref[24:n, 2 * gw:] + q_ref[20:n - 4, 2 * gw:]
    q_ref[32:n, 3 * gw:] = p_ref[32:n, 3 * gw:] + p_ref[24:n - 8, 3 * gw:]

    def rows_of(c):
        return slice(c * ck, (c + 1) * ck)

    def v_pair(c, p):
        va = qkvg_ref[rows_of(c), 2 * kw + (2 * p) * GLA_DV:2 * kw + (2 * p + 1) * GLA_DV]
        vb = qkvg_ref[rows_of(c), 2 * kw + (2 * p + 1) * GLA_DV:2 * kw + (2 * p + 2) * GLA_DV]
        return va, vb

    bcs = []
    for c in chunks:
        la = loga_ref[rows_of(c), :]
        la_hi = la.astype(bf16)
        la_lo = (la - la_hi.astype(f32)).astype(bf16)
        bcs.append(jnp.dot(tril, la_hi, preferred_element_type=f32) + jnp.dot(tril, la_lo, preferred_element_type=f32))
    lhs_q, ke2, kds, elast = [], [], [], []
    for c in chunks:
        bc = bcs[c]
        blast = bc[ck - 1:ck, :]
        q = qkvg_ref[rows_of(c), 0:kw].astype(f32) * (GLA_DK ** -0.5)
        k = qkvg_ref[rows_of(c), kw:2 * kw].astype(f32)
        qe = q * jnp.exp(bc)
        ke = (k * jnp.exp(-bc)).astype(bf16)
        kds.append((k * jnp.exp(blast - bc)).astype(bf16))
        elast.append(jnp.exp(blast))
        for p in pairs:
            pl_ = slice(p * pair_w, (p + 1) * pair_w)
            qe_p = qe[:, pl_]
            lhs_q.append(jnp.concatenate([jnp.where(first_lanes_ck, qe_p, 0.0),
                                          jnp.where(first_lanes_ck, 0.0, qe_p)], axis=0).astype(bf16))
            ke2.append(jnp.concatenate([ke[:, pl_], ke[:, pl_]], axis=0))
    att, upd = [], []
    for c in chunks:
        for p in pairs:
            idx = c * len(pairs) + p
            a = lax.dot_general(lhs_q[idx], ke2[idx], NT_DIMS, preferred_element_type=f32)
            att.append(jnp.where(causal, a, 0.0).astype(bf16))
            va, vb = v_pair(c, p)
            r = lax.dot_general(jnp.concatenate([va, vb], axis=1), kds[c][:, p * pair_w:(p + 1) * pair_w], TN_DIMS,
                                preferred_element_type=f32)
            upd.append(jnp.where(first_lanes, r[:GLA_DV], r[GLA_DV:]))
    st = [st_ref[p] for p in pairs]
    for c in chunks:
        for p in pairs:
            idx = c * len(pairs) + p
            va, vb = v_pair(c, p)
            o = lax.dot_general(lhs_q[idx], st[p].astype(bf16), NT_DIMS, preferred_element_type=f32)
            o = o + jnp.dot(att[idx], jnp.concatenate([va, vb], axis=0), preferred_element_type=f32)
            o_ref[rows_of(c), (2 * p) * GLA_DV:(2 * p + 1) * GLA_DV] = o[:ck]
            o_ref[rows_of(c), (2 * p + 1) * GLA_DV:(2 * p + 2) * GLA_DV] = o[ck:]
            st[p] = st[p] * elast[c][:, p * pair_w:(p + 1) * pair_w] + upd[idx]
    for p in pairs:
        st_ref[p] = st[p]

    sums = (p_ref, q_ref, p_ref, q_ref)
    pos = i * t + lax.broadcasted_iota(jnp.int32, (t, 1), 0)
    for gi, w in enumerate(POOL_WINDOWS):
        ls = slice(gi * gw, (gi + 1) * gw)
        cnt = jnp.minimum(w, pos + 1).astype(f32)
        pooled = (sums[gi][hist:n, ls] / cnt - u[:, ls]).astype(bf16)
        pg = jnp.dot(pooled, pw_ref[gi], preferred_element_type=f32) * ps_ref[:, ls]
        op_ref[:, vw + gi * gw:vw + (gi + 1) * gw] = pg.astype(bf16)
    e_ref[hist - POOL_TAIL:hist, :] = e_ref[n - POOL_TAIL:n, :]

    piece = 2 * GLA_DV
    y = x_ref[...]
    for c0 in (vw, vw + piece):
        y = y + jnp.dot(op_ref[:, c0:c0 + piece], wout_ref[c0:c0 + piece, :], preferred_element_type=f32)
    gain = gain_ref[...]
    for p in pairs:
        for h in (2 * p, 2 * p + 1):
            hs = slice(h * GLA_DV, (h + 1) * GLA_DV)
            g = qkvg_ref[:, 2 * kw + vw + h * GLA_DV:2 * kw + vw + (h + 1) * GLA_DV].astype(f32)
            op_ref[:, hs] = (_rms(o_ref[:, hs], gain) * _silu(g)).astype(bf16)
        c0 = p * piece
        y = y + jnp.dot(op_ref[:, c0:c0 + piece], wout_ref[c0:c0 + piece, :], preferred_element_type=f32)
    xo_ref[...] = y

    @pl.when(i == pl.num_programs(1) - 1)
    def _():
        for p in range(GLA_HEADS // 2):
            s_pair = st_ref[p].T
            so_ref[0, 2 * p] = s_pair[:GLA_DK]
            so_ref[0, 2 * p + 1] = s_pair[GLA_DK:]


def _gla_pool_prompt(qkvg, loga, u, x, tril, gain, pw, ps, wout, *, batch, t):
    m, d = x.shape
    nt = m // batch // t
    row = lambda b, i: (b * nt + i, 0)
    const2 = lambda b, i: (0, 0)
    vw = GLA_HEADS * GLA_DV
    uw = u.shape[1]
    pipelined = t * (qkvg.shape[1] * BF16_BYTES + (loga.shape[1] + uw + 2 * d) * F32_BYTES)
    scratch = (t * vw + 3 * (POOL_HIST + t) * uw) * F32_BYTES + t * (vw + uw) * BF16_BYTES
    waves = (t // GLA_CHUNK) * (GLA_HEADS // 2) * (3 * LANES * LANES * BF16_BYTES + LANES * LANES * F32_BYTES)
    resident = 2 * (wout.size + pw.size) * BF16_BYTES + scratch + waves + t * d * F32_BYTES
    return pl.pallas_call(
        _gla_pool_prompt_body,
        grid=(batch, nt),
        in_specs=[
            pl.BlockSpec((t, qkvg.shape[1]), row),
            pl.BlockSpec((t, loga.shape[1]), row),
            pl.BlockSpec((t, uw), row),
            pl.BlockSpec((t, d), row),
            pl.BlockSpec(tril.shape, const2),
            pl.BlockSpec(gain.shape, const2),
            pl.BlockSpec(pw.shape, lambda b, i: (0, 0, 0)),
            pl.BlockSpec(ps.shape, const2),
            pl.BlockSpec(wout.shape, const2),
        ],
        out_specs=[
            pl.BlockSpec((t, d), row),
            pl.BlockSpec((1, GLA_HEADS, GLA_DK, GLA_DV), lambda b, i: (b, 0, 0, 0)),
        ],
        out_shape=[
            jax.ShapeDtypeStruct((m, d), f32),
            jax.ShapeDtypeStruct((batch, GLA_HEADS, GLA_DK, GLA_DV), f32),
        ],
        scratch_shapes=[
            pltpu.VMEM((GLA_HEADS // 2, GLA_DV, 2 * GLA_DK), f32),
            pltpu.VMEM((t, vw), f32),
            pltpu.VMEM((POOL_HIST + t, uw), f32),
            pltpu.VMEM((POOL_HIST + t, uw), f32),
            pltpu.VMEM((POOL_HIST + t, uw), f32),
            pltpu.VMEM((t, vw + uw), bf16),
        ],
        compiler_params=_params(("arbitrary", "arbitrary"), pipelined, resident),
        name="gla_pool_prompt",
    )(qkvg, loga, u, x, tril, gain, pw, ps, wout)


def _gla_pool_sample_body(qkvg_ref, loga_ref, u_ref, s_ref, buf_ref, gain_ref, pw_ref, ps_ref,
                          op_ref, so_ref):
    bb = u_ref.shape[0]
    kw = GLA_HEADS * GLA_DK
    vw = GLA_HEADS * GLA_DV
    gain = gain_ref[...]
    qkvg = qkvg_ref[...].astype(f32)
    alpha = jnp.exp(loga_ref[...])
    qs = qkvg[:, 0:kw] * (GLA_DK ** -0.5)
    k = qkvg[:, kw:2 * kw]

    def column(row):
        return jnp.broadcast_to(row, (LANES, kw)).T

    o_rows = []
    for b in range(bb):
        acol = column(alpha[b:b + 1, :])
        qcol = column(qs[b:b + 1, :])
        kcol = column(k[b:b + 1, :])
        o_heads = []
        for h in range(GLA_HEADS):
            ks = slice(h * GLA_DK, (h + 1) * GLA_DK)
            v = qkvg[b:b + 1, 2 * kw + h * GLA_DV:2 * kw + (h + 1) * GLA_DV]
            s_new = acol[ks, :] * s_ref[b, h] + kcol[ks, :] * v
            so_ref[b, h] = s_new
            o = jnp.sum(qcol[ks, :] * s_new, axis=0, keepdims=True)
            g = qkvg[b:b + 1, 2 * kw + vw + h * GLA_DV:2 * kw + vw + (h + 1) * GLA_DV]
            o_heads.append(_rms(o, gain) * _silu(g))
        o_rows.append(jnp.concatenate(o_heads, axis=1))
    op_ref[:, 0:vw] = jnp.concatenate(o_rows, axis=0).astype(bf16)

    u = u_ref[...]
    for gi, w in enumerate(POOL_WINDOWS):
        ls = slice(gi * POOL_GW, (gi + 1) * POOL_GW)
        s = u[:, ls] + jnp.sum(buf_ref[:, POOL_BUF - (w - 1):POOL_BUF, ls], axis=1)
        cnt = float(min(w, PAST_LEN + 1))
        pooled = (s / cnt - u[:, ls]).astype(bf16)
        pg = jnp.dot(pooled, pw_ref[gi], preferred_element_type=f32) * ps_ref[:, ls]
        op_ref[:, vw + gi * POOL_GW:vw + (gi + 1) * POOL_GW] = pg.astype(bf16)


def _gla_pool_sample(qkvg, loga, u, s, buf, gain, pw, ps, *, bb):
    n = u.shape[0]
    row = lambda i: (i, 0)
    const2 = lambda i: (0, 0)
    ow = GLA_HEADS * GLA_DV + POOL_GW * len(POOL_WINDOWS)
    state_rows = int(np.prod(s.shape[1:]))
    pipelined = bb * ((qkvg.shape[1] + ow) * BF16_BYTES
                      + (loga.shape[1] + u.shape[1] + 2 * state_rows + POOL_TAIL * buf.shape[2]) * F32_BYTES)
    resident = 2 * pw.size * BF16_BYTES + 3 * LANES * loga.shape[1] * F32_BYTES
    return pl.pallas_call(
        _gla_pool_sample_body,
        grid=(n // bb,),
        in_specs=[
            pl.BlockSpec((bb, qkvg.shape[1]), row),
            pl.BlockSpec((bb, loga.shape[1]), row),
            pl.BlockSpec((bb, u.shape[1]), row),
            pl.BlockSpec((bb,) + s.shape[1:], lambda i: (i, 0, 0, 0)),
            pl.BlockSpec((bb,) + buf.shape[1:], lambda i: (i, 0, 0)),
            pl.BlockSpec(gain.shape, const2),
            pl.BlockSpec(pw.shape, lambda i: (0, 0, 0)),
            pl.BlockSpec(ps.shape, const2),
        ],
        out_specs=[
            pl.BlockSpec((bb, ow), row),
            pl.BlockSpec((bb,) + s.shape[1:], lambda i: (i, 0, 0, 0)),
        ],
        out_shape=[
            jax.ShapeDtypeStruct((n, ow), bf16),
            jax.ShapeDtypeStruct(s.shape, f32),
        ],
        compiler_params=_params(("arbitrary",), pipelined, resident),
        name="gla_pool_sample",
    )(qkvg, loga, u, s, buf, gain, pw, ps)


def _proj_res_body(x_ref, a_ref, w_ref, o_ref):
    o_ref[...] = x_ref[...] + jnp.dot(a_ref[...], w_ref[...], preferred_element_type=f32)


def _proj_res(x, a, w, *, tm):
    m, d = x.shape
    return pl.pallas_call(
        _proj_res_body,
        grid=(m // tm,),
        in_specs=[
            pl.BlockSpec((tm, d), lambda i: (i, 0)),
            pl.BlockSpec((tm, a.shape[1]), lambda i: (i, 0)),
            pl.BlockSpec(w.shape, lambda i: (0, 0)),
        ],
        out_specs=pl.BlockSpec((tm, d), lambda i: (i, 0)),
        out_shape=jax.ShapeDtypeStruct((m, d), f32),
        compiler_params=_params(("arbitrary",), tm * (2 * d * F32_BYTES + a.shape[1] * BF16_BYTES),
                                2 * w.size * BF16_BYTES),
        name="proj_res",
    )(x, a, w)


def _ret_token_pieces(q_ref, k_ref, v_ref, g_ref, s_ref, og_ref, so_ref, gamma):
    def piece(j, h):
        def run():
            ks = slice(h * RET_DK, (h + 1) * RET_DK)
            vs = slice(h * RET_DV, (h + 1) * RET_DV)
            qcol = jnp.broadcast_to(q_ref[j, :, ks].astype(f32), (LANES, RET_DK)).T
            kcol = jnp.broadcast_to(k_ref[j, :, ks].astype(f32), (LANES, RET_DK)).T
            v = v_ref[j, :, vs].astype(f32)
            g = g_ref[j, :, vs].astype(f32)
            o_tiles = []
            for t in range(RET_DV // LANES):
                cs = slice(t * LANES, (t + 1) * LANES)
                s_new = gamma[h] * s_ref[j, h, :, cs] + kcol * v[:, cs]
                so_ref[j, h, :, cs] = s_new
                o_tiles.append(jnp.sum(qcol * s_new, axis=0, keepdims=True))
            o = jnp.concatenate(o_tiles, axis=1)
            og_ref[j, :, vs] = (_rms(o) * _silu(g)).astype(bf16)
            return o
        return run

    return [piece(j, h) for j in range(s_ref.shape[0]) for h in range(RET_HEADS)]


def _ffn_body(*refs, tf, n_sub, final_norm, rider_gamma):
    x_ref, gain_ref, wg_ref, wu_ref, wd_ref, fgain_ref = refs[:6]
    pieces = []
    if rider_gamma is None:
        o_ref, h_ref, acc_ref = refs[6:]
    else:
        rq_ref, rk_ref, rv_ref, rg_ref, rs_ref, o_ref, rog_ref, rso_ref, h_ref, acc_ref = refs[6:]
        pieces = _ret_token_pieces(rq_ref, rk_ref, rv_ref, rg_ref, rs_ref, rog_ref, rso_ref, rider_gamma)
    n_chunks = wg_ref.shape[1] // tf
    bounds = [n_chunks * s // n_sub for s in range(n_sub + 1)]

    def exact_zero(v):
        bits = lax.bitcast_convert_type(v, jnp.uint32)
        return ((bits >> 16) >> 16).astype(f32)

    def run_chunks(chunks):
        pin = None
        for n, c in enumerate(chunks):
            cs = slice(c * tf, (c + 1) * tf)
            g = jnp.dot(h_ref[...], wg_ref[:, cs], preferred_element_type=f32)
            if pin is not None:
                g = g + pin
                pin = None
            u = jnp.dot(h_ref[...], wu_ref[:, cs], preferred_element_type=f32)
            a = (_silu(g) * u).astype(bf16)
            part = jnp.dot(a, wd_ref[cs, :], preferred_element_type=f32)
            if c == 0:
                acc_ref[...] = part
            else:
                acc_ref[...] += part
            for p in range(len(pieces)):
                if p * (len(chunks) - 1) // len(pieces) == n:
                    z = exact_zero(pieces[p]()[:, :tf])
                    pin = z if pin is None else pin + z

    def sub_step(s):
        if s == 0:
            h_ref[...] = _rms(x_ref[...], gain_ref[...]).astype(bf16)
        run_chunks(range(bounds[s], bounds[s + 1]))
        if s == n_sub - 1:
            y = x_ref[...] + acc_ref[...]
            if final_norm:
                y = _rms(y, fgain_ref[...])
            o_ref[...] = y

    if n_sub == 1:
        sub_step(0)
    else:
        for s in range(n_sub):
            pl.when(pl.program_id(1) == s)(functools.partial(sub_step, s))


def _ffn(x, gain, wg, wu, wd, fgain, *, layer, tm, tf, final_norm, rider=None):
    m, d = x.shape
    ff = wg.shape[1]
    steps = m // tm
    n_sub = 1 if rider is None else 2
    single_buffered = dict(pipeline_mode=pl.Buffered(1))
    in_specs = [
        pl.BlockSpec((tm, d), lambda i, s: (i, 0)),
        pl.BlockSpec((None, 1, d), lambda i, s: (layer, 0, 0)),
        pl.BlockSpec((d, ff), lambda i, s: (0, 0), **single_buffered),
        pl.BlockSpec((d, ff), lambda i, s: (0, 0), **single_buffered),
        pl.BlockSpec((ff, d), lambda i, s: (0, 0), **single_buffered),
        pl.BlockSpec((1, d), lambda i, s: (0, 0)),
    ]
    args = [x, gain, wg, wu, wd, fgain]
    out_specs = [pl.BlockSpec((tm, d), lambda i, s: (i, 0))]
    out_shape = [jax.ShapeDtypeStruct((m, d), f32)]
    gamma = None
    pipelined = 2 * tm * d * F32_BYTES
    resident = (3 * d * ff * BF16_BYTES + tm * d * (BF16_BYTES + F32_BYTES)
                + 3 * tm * tf * F32_BYTES + tm * d * F32_BYTES)
    if rider is not None:
        qkvg3, state, rows, gamma = rider
        assert 2 * steps * rows == state.shape[0]
        qw = RET_HEADS * RET_DK
        vw = RET_HEADS * RET_DV
        blk = lambda col: (lambda i, s: (2 * i + s, 0, col))
        state_spec = pl.BlockSpec((rows,) + state.shape[1:], lambda i, s: (2 * i + s, 0, 0, 0))
        in_specs += [
            pl.BlockSpec((rows, 1, qw), blk(0)),
            pl.BlockSpec((rows, 1, qw), blk(1)),
            pl.BlockSpec((rows, 1, vw), blk(1)),
            pl.BlockSpec((rows, 1, vw), blk(2)),
            state_spec,
        ]
        args += [qkvg3, qkvg3, qkvg3, qkvg3, state]
        out_specs += [pl.BlockSpec((rows, 1, vw), blk(0)), state_spec]
        out_shape += [
            jax.ShapeDtypeStruct((state.shape[0], 1, vw), bf16),
            jax.ShapeDtypeStruct(state.shape, f32),
        ]
        pipelined += 2 * rows * int(np.prod(state.shape[1:])) * F32_BYTES
    out = pl.pallas_call(
        functools.partial(_ffn_body, tf=tf, n_sub=n_sub, final_norm=final_norm, rider_gamma=gamma),
        grid=(steps, n_sub),
        in_specs=in_specs,
        out_specs=out_specs,
        out_shape=out_shape,
        scratch_shapes=[pltpu.VMEM((tm, d), bf16), pltpu.VMEM((tm, d), f32)],
        compiler_params=_params(("arbitrary", "arbitrary"), pipelined, resident),
        name="ffn_final" if final_norm else "ffn",
    )(*args)
    return out[0] if rider is None else out


def _in_odd_body(x_ref, gain_ref, w_ref, perm_ref, cos_ref, sin_ref, qsc_ref, ksc_ref, o_ref, h_ref, wqk_ref, *,
                 tn, split_halves):
    qw = RET_HEADS * RET_DK
    half = RET_DK // 2
    if split_halves:
        @pl.when(pl.program_id(0) == 0)
        def _():
            for hh in range(2 * RET_HEADS):
                hs = slice(hh * RET_DK, (hh + 1) * RET_DK)
                wqk_ref[:, hs] = jnp.dot(w_ref[:, hs], perm_ref[...], preferred_element_type=f32).astype(bf16)

    h_ref[...] = _rms(x_ref[...], gain_ref[...]).astype(bf16)
    cos = cos_ref[...]
    sin = sin_ref[...]
    for c in range(2 * qw // tn):
        c0 = c * tn
        w_chunk = wqk_ref[:, c0:c0 + tn] if split_halves else w_ref[:, c0:c0 + tn]
        p = jnp.dot(h_ref[...], w_chunk, preferred_element_type=f32)
        sc_ref = qsc_ref if c0 < qw else ksc_ref
        for hh in range(tn // RET_DK):
            h0 = hh * RET_DK
            head = (c0 % qw + h0) // RET_DK
            sc = sc_ref[:, head * LANES:(head + 1) * LANES]
            if split_halves:
                ev = p[:, h0:h0 + half]
                od = p[:, h0 + half:h0 + RET_DK]
                o_ref[:, c0 + h0:c0 + h0 + half] = ((ev * cos - od * sin) * sc).astype(bf16)
                o_ref[:, c0 + h0 + half:c0 + h0 + RET_DK] = ((od * cos + ev * sin) * sc).astype(bf16)
            else:
                xh = p[:, h0:h0 + RET_DK]
                even = lax.broadcasted_iota(jnp.int32, xh.shape, 1) % 2 == 0
                partner = jnp.where(even, pltpu.roll(xh, RET_DK - 1, 1), pltpu.roll(xh, 1, 1))
                r = xh * cos + partner * sin
                o_ref[:, c0 + h0:c0 + h0 + half] = (r[:, :half] * sc).astype(bf16)
                o_ref[:, c0 + h0 + half:c0 + h0 + RET_DK] = (r[:, half:] * sc).astype(bf16)
    for c0 in range(2 * qw, w_ref.shape[1], tn):
        p = jnp.dot(h_ref[...], w_ref[:, c0:c0 + tn], preferred_element_type=f32)
        o_ref[:, c0:c0 + tn] = p.astype(bf16)


def _in_odd(x, gain, w, perm, tables, *, layer, tm, tn, split_halves, cast=()):
    m, d = x.shape
    n = w.shape[1]
    cos, sin, qsc, ksc = tables
    ntab = cos.shape[0] // tm
    qkw = 2 * RET_HEADS * RET_DK
    single_buffered = dict(pipeline_mode=pl.Buffered(1))
    const = lambda i: (0, 0)
    rope_spec = pl.BlockSpec((tm, cos.shape[1]), lambda i: (i % ntab, 0))
    wqk_shape = (d, qkw) if split_halves else (8, LANES)
    cast_args, cast_in, cast_out, cast_shapes, cast_bytes = _cast_rider(cast, m // tm)
    pipelined = tm * (d * F32_BYTES + n * BF16_BYTES + 2 * cos.shape[1] * F32_BYTES) + cast_bytes
    resident = ((w.size + 2 * perm.size + tm * d + wqk_shape[0] * wqk_shape[1]) * BF16_BYTES
                + 2 * (qsc.size + ksc.size) * F32_BYTES + 2 * tm * tn * F32_BYTES)
    outs = pl.pallas_call(
        _with_cast_rider(functools.partial(_in_odd_body, tn=tn, split_halves=split_halves), 8, 1, len(cast)),
        grid=(m // tm,),
        in_specs=[
            pl.BlockSpec((tm, d), lambda i: (i, 0)),
            pl.BlockSpec((None, 1, d), lambda i: (layer, 0, 0)),
            pl.BlockSpec(w.shape, const, **single_buffered),
            pl.BlockSpec(perm.shape, const),
            rope_spec, rope_spec,
            pl.BlockSpec(qsc.shape, const),
            pl.BlockSpec(ksc.shape, const),
        ] + cast_in,
        out_specs=[pl.BlockSpec((tm, n), lambda i: (i, 0))] + cast_out,
        out_shape=[jax.ShapeDtypeStruct((m, n), bf16)] + cast_shapes,
        scratch_shapes=[pltpu.VMEM((tm, d), bf16), pltpu.VMEM(wqk_shape, bf16)],
        compiler_params=_params(("arbitrary",), pipelined, resident),
        name="in_odd",
    )(x, gain, w, perm, cos, sin, qsc, ksc, *cast_args)
    return tuple(outs) if cast else outs[0]


def _ret_prompt_body(q_ref, k_ref, v_ref, g_ref, x_ref, wout_ref, xo_ref, so_ref, s_ref, sb_ref, slab_ref, *,
                     gamma_c, n):
    c = pl.program_id(1)
    subs = [slice(j * n, (j + 1) * n) for j in range(q_ref.shape[0] // n)]

    @pl.when(c == 0)
    def _():
        s_ref[...] = jnp.zeros_like(s_ref)
        sb_ref[...] = jnp.zeros_like(sb_ref)

    causal = lax.broadcasted_iota(jnp.int32, (n, n), 0) >= lax.broadcasted_iota(jnp.int32, (n, n), 1)
    heads = range(RET_HEADS)
    ks = [slice(h * RET_DK, (h + 1) * RET_DK) for h in heads]
    vs = [slice(h * RET_DV, (h + 1) * RET_DV) for h in heads]
    att = [[jnp.where(causal, lax.dot_general(q_ref[r, ks[h]], k_ref[r, ks[h]], NT_DIMS,
                                              preferred_element_type=f32), 0.0).astype(bf16) for h in heads]
           for r in subs]
    o = []
    for j, r in enumerate(subs):
        o.append([jnp.dot(q_ref[r, ks[h]], sb_ref[h], preferred_element_type=f32)
                  + jnp.dot(att[j][h], v_ref[r, vs[h]], preferred_element_type=f32) for h in heads])
        for h in heads:
            kv = lax.dot_general(k_ref[r, ks[h]], v_ref[r, vs[h]], TN_DIMS, preferred_element_type=f32)
            s_new = gamma_c[h] * (s_ref[h] + kv)
            s_ref[h] = s_new
            sb_ref[h] = s_new.astype(bf16)
    for j, r in enumerate(subs):
        y = x_ref[r, :]
        for h in heads:
            og = (_rms(o[j][h]) * _silu(g_ref[r, vs[h]].astype(f32))).astype(bf16)
            y = y + jnp.dot(og, wout_ref[vs[h], :], preferred_element_type=f32)
        xo_ref[r, :] = y

    @pl.when(c == pl.num_programs(1) - 1)
    def _():
        half = RET_DK // 2
        for h in range(RET_HEADS):
            for t in range(RET_DV // LANES):
                ls = slice(t * LANES, (t + 1) * LANES)
                slab_ref[pl.ds(0, half, stride=2), :] = s_ref[h, 0:half, ls]
                slab_ref[pl.ds(1, half, stride=2), :] = s_ref[h, half:RET_DK, ls]
                so_ref[0, h, :, ls] = slab_ref[...]


def _ret_prompt(qkvg, x, wout, gamma_c, *, batch, c, chunk):
    m, d = x.shape
    nc = m // batch // c
    qw = RET_HEADS * RET_DK
    vw = RET_HEADS * RET_DV
    assert c % chunk == 0
    state = RET_HEADS * RET_DK * RET_DV
    pipelined = c * (2 * (qw + vw) * BF16_BYTES + 2 * d * F32_BYTES) + state * F32_BYTES
    resident = (wout.size * BF16_BYTES + state * (F32_BYTES + BF16_BYTES) + RET_DK * LANES * F32_BYTES
                + c * RET_HEADS * (chunk * BF16_BYTES + RET_DV * F32_BYTES)
                + RET_DK * RET_DV * F32_BYTES + c * d * F32_BYTES)
    return pl.pallas_call(
        functools.partial(_ret_prompt_body, gamma_c=gamma_c, n=chunk),
        grid=(batch, nc),
        in_specs=[
            pl.BlockSpec((c, qw), lambda b, i: (b * nc + i, 0)),
            pl.BlockSpec((c, qw), lambda b, i: (b * nc + i, 1)),
            pl.BlockSpec((c, vw), lambda b, i: (b * nc + i, 1)),
            pl.BlockSpec((c, vw), lambda b, i: (b * nc + i, 2)),
            pl.BlockSpec((c, d), lambda b, i: (b * nc + i, 0)),
            pl.BlockSpec(wout.shape, lambda b, i: (0, 0), pipeline_mode=pl.Buffered(1)),
        ],
        out_specs=[
            pl.BlockSpec((c, d), lambda b, i: (b * nc + i, 0)),
            pl.BlockSpec((1, RET_HEADS, RET_DK, RET_DV), lambda b, i: (b, 0, 0, 0)),
        ],
        out_shape=[
            jax.ShapeDtypeStruct((m, d), f32),
            jax.ShapeDtypeStruct((batch, RET_HEADS, RET_DK, RET_DV), f32),
        ],
        scratch_shapes=[
            pltpu.VMEM((RET_HEADS, RET_DK, RET_DV), f32),
            pltpu.VMEM((RET_HEADS, RET_DK, RET_DV), bf16),
            pltpu.VMEM((RET_DK, LANES), f32),
        ],
        compiler_params=_params(("arbitrary", "arbitrary"), pipelined, resident),
        name="ret_prompt",
    )(qkvg, qkvg, qkvg, qkvg, x, wout)


def _rope_tables(pos, per_pair):
    pair_angle = 1.0 / (ROPE_BASE ** jnp.linspace(0.0, 1.0, RET_DK // 2, dtype=f32))
    if per_pair:
        ang = pos[:, None] * pair_angle[None, :]
        return jnp.cos(ang), jnp.sin(ang)
    ang = pos[:, None] * jnp.repeat(pair_angle, 2)[None, :]
    sign = jnp.where(jnp.arange(RET_DK) % 2 == 0, -1.0, 1.0).astype(f32)
    return jnp.cos(ang), jnp.sin(ang) * sign


def _even_odd_perm():
    half = RET_DK // 2
    src = np.concatenate([2 * np.arange(half), 2 * np.arange(half) + 1])
    perm = np.zeros((RET_DK, RET_DK), np.float32)
    perm[src, np.arange(RET_DK)] = 1.0
    return jnp.asarray(perm, dtype=bf16)


def _lane_replicated(scale):
    return jnp.asarray(np.repeat(scale, LANES, axis=1), dtype=f32)


def _ret_decay(rows, c):
    gam = 1.0 - 2.0 ** (-5.0 - np.arange(RET_HEADS, dtype=np.float64))
    lg = np.log(gam)
    steps = (np.arange(rows) % c + 1.0)[:, None]
    q_scale = _lane_replicated(np.exp(lg[None, :] * steps))
    k_scale = _lane_replicated(np.exp(-lg[None, :] * steps) * RET_DK ** -0.5)
    gamma_c = tuple(float(x) for x in np.exp(lg * c))
    gamma = tuple(float(x) for x in gam)
    return q_scale, k_scale, gamma_c, gamma


def kernel(x_prompt, x_sample, state_gla, state_pool, state_ret, norm_mix, norm_ffn, norm_final, w_in_even,
           w_gate_b, b_gate, gla_gain, pool_w, pool_scale, w_out_even, w_in_odd, w_out_odd, w_ffn_gate,
           w_ffn_up, w_ffn_down):
    batch, seq, d = x_prompt.shape
    n_s = x_sample.shape[0]
    assert norm_mix.shape[0] == 2 and x_sample.shape[1] == 1

    we = w_in_even[0].T
    wgb = jnp.concatenate([w_gate_b[0], jnp.zeros((LANES - GATE_RANK, w_gate_b.shape[2]), f32)], axis=0).astype(bf16)
    bg = b_gate[0][None, :]
    gg = gla_gain[0][None, :]
    pw = pool_w[0].astype(bf16)
    ps = pool_scale[0][None, :]
    nm = norm_mix[:, None, :]
    nf = norm_ffn[:, None, :]
    nfin = norm_final[None, :]
    tril = jnp.asarray(np.tril(np.ones((GLA_CHUNK, GLA_CHUNK), np.float32)), dtype=bf16)
    tf = 256
    tm_p = 512
    q_scale, k_scale, gamma_c, gamma = _ret_decay(tm_p, RET_CHUNK)
    tables_p = _rope_tables(jnp.arange(seq, dtype=f32), True) + (q_scale, k_scale)
    tables_s = _rope_tables(jnp.full((n_s,), float(PAST_LEN), f32), False) + (
        _lane_replicated(np.ones((n_s, RET_HEADS))), _lane_replicated(np.full((n_s, RET_HEADS), RET_DK ** -0.5)))
    perm = _even_odd_perm()

    ff = w_ffn_gate.shape[2]
    xp = x_prompt.reshape(batch * seq, d)
    steps_p = batch * seq // tm_p
    ffn_cast = lambda layer: (
        (w_ffn_gate.reshape(-1, ff), d // steps_p, layer * steps_p, steps_p),
        (w_ffn_up.reshape(-1, ff), d // steps_p, layer * steps_p, steps_p),
        (w_ffn_down.reshape(-1, d), 2 * ff // steps_p, layer * steps_p // 2, steps_p // 2))
    whole = lambda w: (w, w.shape[0] // steps_p, 0, steps_p)
    qkvg, loga, u_p, wg0, wu0, wd0, wio, woe, woo = _in_even(
        xp, nm[0], we, wgb, bg, tm=tm_p,
        cast=ffn_cast(0) + (whole(w_in_odd[0]), whole(w_out_even[0]), whole(w_out_odd[0])))
    xp, gla_p = _gla_pool_prompt(qkvg, loga, u_p, xp, tril, gg, pw, ps, woe, batch=batch, t=512)

    xs = x_sample.reshape(n_s, d)
    qkvg_s, loga_s, u_s = _in_even(xs, nm[0], we, wgb, bg, tm=n_s)
    op_s, gla_s = _gla_pool_sample(qkvg_s, loga_s, u_s, state_gla[0], state_pool[0], gg, pw, ps, bb=32)
    xs = _proj_res(xs, op_s, woe, tm=n_s)
    xs = _ffn(xs, nf, wg0, wu0, wd0, nfin, layer=0, tm=n_s, tf=tf, final_norm=False)
    qkvg2_s = _in_odd(xs, nm, wio, perm, tables_s, layer=1, tm=n_s, tn=512, split_halves=False)
    qkvg2_s = qkvg2_s.reshape(n_s, 1, -1)

    rows = n_s // (2 * steps_p)
    xp, og_s, ret_s = _ffn(xp, nf, wg0, wu0, wd0, nfin, layer=0, tm=tm_p, tf=tf, final_norm=False,
                           rider=(qkvg2_s, state_ret[0], rows, gamma))
    qkvg2, wg1, wu1, wd1 = _in_odd(xp, nm, wio, perm, tables_p, layer=1, tm=tm_p, tn=512, split_halves=True,
                                   cast=ffn_cast(1))
    xp, ret_p = _ret_prompt(qkvg2, xp, woo, gamma_c, batch=batch, c=2 * RET_CHUNK, chunk=RET_CHUNK)
    y_prompt = _ffn(xp, nf, wg1, wu1, wd1, nfin, layer=1, tm=tm_p, tf=tf, final_norm=True)
    pool_p = u_p.reshape(batch, seq, -1)[:, seq - POOL_BUF:, :]

    xs = _proj_res(xs, og_s.reshape(n_s, -1), woo, tm=n_s)
    y_sample = _ffn(xs, nf, wg1, wu1, wd1, nfin, layer=1, tm=n_s, tf=tf, final_norm=True)

    pool_s = jnp.concatenate([state_pool[0][:, 1:, :], u_s[:, None, :]], axis=1)

    return (y_prompt.reshape(batch, seq, d), y_sample.reshape(n_s, 1, d),
            gla_p[None], gla_s[None], pool_p[None], pool_s[None], ret_p[None], ret_s[None])
```

```python
import functools

import numpy as np
import jax
import jax.numpy as jnp
from jax import lax
from jax.experimental import pallas as pl
from jax.experimental.pallas import tpu as pltpu

f32 = jnp.float32
bf16 = jnp.bfloat16

EPS = 1e-6
PAST_LEN = 16384
GLA_HEADS, GLA_DK, GLA_DV = 4, 64, 128
GLA_CHUNK = 64
GATE_RANK = 16
GATE_NORMALIZER = 16.0
POOL_WINDOWS = (2, 4, 8, 16)
POOL_GW = 128
POOL_BUF = max(POOL_WINDOWS) - 1
POOL_HIST = 32
POOL_TAIL = 16
RET_HEADS, RET_DK, RET_DV = 4, 256, 512
RET_CHUNK = 256
ROPE_BASE = 10000.0
LANES = 128
MIB = 1024 * 1024
VMEM_COMPILER_ALLOWANCE = 8 * MIB
F32_BYTES, BF16_BYTES = 4, 2

NT_DIMS = (((1,), (1,)), ((), ()))
TN_DIMS = (((0,), (0,)), ((), ()))


def _params(semantics, pipelined_bytes, resident_bytes):
    limit = 2 * pipelined_bytes + resident_bytes + VMEM_COMPILER_ALLOWANCE
    return pltpu.CompilerParams(dimension_semantics=semantics, vmem_limit_bytes=int(limit))


def _rms(x, gain=None):
    y = x * lax.rsqrt(jnp.mean(x * x, axis=-1, keepdims=True) + EPS)
    return y if gain is None else y * gain


def _silu(g):
    return g * jax.nn.sigmoid(g)


def _in_even_prepare(w_ref, wq_ref, wu_ref, nq, nu):
    for r0 in range(0, nq + LANES, LANES):
        wq_ref[:, r0:r0 + LANES] = w_ref[r0:r0 + LANES, :].T.astype(bf16)
    for r0 in range(0, nu, LANES):
        wu_ref[:, r0:r0 + LANES] = w_ref[nq + GATE_RANK + r0:nq + GATE_RANK + r0 + LANES, :].T.astype(bf16)


def _in_even_body(x_ref, gain_ref, w_ref, wgb_ref, bg_ref, qkvg_ref, loga_ref, u_ref, h_ref, wq_ref, wu_ref, *, tn):
    nq = qkvg_ref.shape[1]
    nu = u_ref.shape[1]

    if w_ref is not None:
        pl.when(pl.program_id(0) == 0)(lambda: _in_even_prepare(w_ref, wq_ref, wu_ref, nq, nu))

    h_ref[...] = _rms(x_ref[...], gain_ref[...]).astype(bf16)
    a = jnp.dot(h_ref[...], wq_ref[:, nq:nq + LANES], preferred_element_type=f32)
    a = jnp.where(lax.broadcasted_iota(jnp.int32, a.shape, 1) < GATE_RANK, a, 0.0).astype(bf16)
    for c0 in range(0, nq, tn):
        qkvg_ref[:, c0:c0 + tn] = jnp.dot(h_ref[...], wq_ref[:, c0:c0 + tn], preferred_element_type=f32).astype(bf16)
        if c0 == 0:
            z = jnp.dot(a, wgb_ref[...], preferred_element_type=f32) + bg_ref[...]
            loga_ref[...] = (jnp.minimum(z, 0.0) - jnp.log1p(jnp.exp(-jnp.abs(z)))) * (1.0 / GATE_NORMALIZER)
    for c0 in range(0, nu, tn):
        u_ref[:, c0:c0 + tn] = jnp.dot(h_ref[...], wu_ref[:, c0:c0 + tn], preferred_element_type=f32)


def _cast_rider(cast, steps, in_buffers=2):
    arrays, in_specs, out_specs, out_shapes, nbytes = [], [], [], [], 0
    for arr, rows, first, count in cast:
        assert count <= steps and rows % 16 == 0 and (first + count) * rows <= arr.shape[0]
        cols = arr.shape[1]
        arrays.append(arr)
        in_specs.append(pl.BlockSpec(
            (rows, cols), lambda i, first=first, count=count: (first + jnp.minimum(i, count - 1), 0),
            pipeline_mode=pl.Buffered(in_buffers)))
        out_specs.append(pl.BlockSpec((rows, cols), lambda i, count=count: (jnp.minimum(i, count - 1), 0)))
        out_shapes.append(jax.ShapeDtypeStruct((count * rows, cols), bf16))
        nbytes += rows * cols * (F32_BYTES * in_buffers // 2 + BF16_BYTES)
    return arrays, in_specs, out_specs, out_shapes, nbytes


def _with_cast_rider(body, n_in, n_out, n_cast):
    def wrapped(*refs):
        rest = refs[n_in + n_cast:]
        body(*refs[:n_in], *rest[:n_out], *rest[n_out + n_cast:])
        for src_ref, dst_ref in zip(refs[n_in:n_in + n_cast], rest[n_out:n_out + n_cast]):
            dst_ref[...] = src_ref[...].astype(bf16)
    return wrapped


def _in_even(x, gain, w, wgb, bg, *, tm, cast=()):
    m, d = x.shape
    steps = m // tm
    nq = 2 * GLA_HEADS * GLA_DK + 2 * GLA_HEADS * GLA_DV
    nu = POOL_GW * len(POOL_WINDOWS)
    nk = GLA_HEADS * GLA_DK
    assert w.shape == (nq + GATE_RANK + nu, d)
    const = lambda i: (0, 0)
    tn = 512
    in_buffers = 3 if cast else 2
    cast_args, cast_in, cast_out, cast_shapes, cast_bytes = _cast_rider(cast, steps, in_buffers)
    pipelined = (tm * (d * F32_BYTES * in_buffers // 2 + nq * BF16_BYTES + nk * F32_BYTES + nu * F32_BYTES)
                 + cast_bytes)
    resident = (w.size * F32_BYTES + (2 * wgb.size + tm * d + d * (nq + LANES) + d * nu) * BF16_BYTES
                + 2 * tm * tn * F32_BYTES)
    x_spec = pl.BlockSpec((tm, d), lambda i: (i, 0), pipeline_mode=pl.Buffered(in_buffers))
    out_specs = [
        pl.BlockSpec((tm, nq), lambda i: (i, 0)),
        pl.BlockSpec((tm, nk), lambda i: (i, 0)),
        pl.BlockSpec((tm, nu), lambda i: (i, 0)),
    ] + cast_out
    out_shape = [
        jax.ShapeDtypeStruct((m, nq), bf16),
        jax.ShapeDtypeStruct((m, nk), f32),
        jax.ShapeDtypeStruct((m, nu), f32),
    ] + cast_shapes
    scratch_shapes = [pltpu.VMEM((tm, d), bf16), pltpu.VMEM((d, nq + LANES), bf16), pltpu.VMEM((d, nu), bf16)]
    if cast:
        n = len(cast)

        def streamed(x_hbm, gain_ref, w_ref, wgb_ref, bg_ref, *rest):
            h_ref, wq_ref, wu_ref = rest[2 * n + 3:]
            _in_even_prepare(w_ref, wq_ref, wu_ref, nq, nu)

            def step(x_ref, *blocks):
                qkvg_ref, loga_ref, u_ref = blocks[n:n + 3]
                _in_even_body(x_ref, gain_ref, None, wgb_ref, bg_ref, qkvg_ref, loga_ref, u_ref, h_ref, wq_ref,
                              wu_ref, tn=tn)
                for src_ref, dst_ref in zip(blocks[:n], blocks[n + 3:]):
                    dst_ref[...] = src_ref[...].astype(bf16)

            pltpu.emit_pipeline(step, grid=(steps,), in_specs=[x_spec] + cast_in, out_specs=out_specs)(
                x_hbm, *rest[:2 * n + 3])

        any_spec = pl.BlockSpec(memory_space=pl.ANY)
        vmem_spec = pl.BlockSpec(memory_space=pltpu.VMEM)
        return pl.pallas_call(
            streamed,
            in_specs=[any_spec] + [vmem_spec] * 4 + [any_spec] * n,
            out_specs=[any_spec] * (3 + n),
            out_shape=out_shape,
            scratch_shapes=scratch_shapes,
            compiler_params=pltpu.CompilerParams(
                vmem_limit_bytes=_params((), pipelined, resident).vmem_limit_bytes),
            name="in_even",
        )(x, gain, w, wgb, bg, *cast_args)
    return pl.pallas_call(
        _with_cast_rider(functools.partial(_in_even_body, tn=tn), 5, 3, len(cast)),
        grid=(steps,),
        in_specs=[
            x_spec,
            pl.BlockSpec((1, d), const),
            pl.BlockSpec(w.shape, const, pipeline_mode=pl.Buffered(1)),
            pl.BlockSpec(wgb.shape, const),
            pl.BlockSpec((1, nk), const),
        ] + cast_in,
        out_specs=[
            pl.BlockSpec((tm, nq), lambda i: (i, 0)),
            pl.BlockSpec((tm, nk), lambda i: (i, 0)),
            pl.BlockSpec((tm, nu), lambda i: (i, 0)),
        ] + cast_out,
        out_shape=[
            jax.ShapeDtypeStruct((m, nq), bf16),
            jax.ShapeDtypeStruct((m, nk), f32),
            jax.ShapeDtypeStruct((m, nu), f32),
        ] + cast_shapes,
        scratch_shapes=[pltpu.VMEM((tm, d), bf16), pltpu.VMEM((d, nq + LANES), bf16), pltpu.VMEM((d, nu), bf16)],
        compiler_params=_params(("arbitrary",), pipelined, resident),
        name="in_even",
    )(x, gain, w, wgb, bg, *cast_args)


def _gla_pool_prompt_body(qkvg_ref, loga_ref, u_ref, x_ref, tril_ref, gain_ref, pw_ref, ps_ref, wout_ref,
                          xo_ref, so_ref, st_ref, o_ref, e_ref, p_ref, q_ref, op_ref):
    t = x_ref.shape[0]
    ck = GLA_CHUNK
    kw = GLA_HEADS * GLA_DK
    vw = GLA_HEADS * GLA_DV
    pair_w = 2 * GLA_DK
    i = pl.program_id(1)

    @pl.when(i == 0)
    def _():
        st_ref[...] = jnp.zeros_like(st_ref)
        e_ref[0:POOL_HIST, :] = jnp.zeros((POOL_HIST, e_ref.shape[1]), f32)

    tril = tril_ref[...]
    row = lax.broadcasted_iota(jnp.int32, (2 * ck, pair_w), 0)
    lane = lax.broadcasted_iota(jnp.int32, (2 * ck, pair_w), 1)
    first_lanes = lane < GLA_DK
    first_lanes_ck = lax.broadcasted_iota(jnp.int32, (ck, pair_w), 1) < GLA_DK
    same_head = (row < ck) == first_lanes
    causal = same_head & ((row % ck) >= (lane % GLA_DK))
    pairs = range(GLA_HEADS // 2)
    chunks = range(t // ck)

    hist = POOL_HIST
    n = t + hist
    gw = POOL_GW
    u = u_ref[...]
    e_ref[hist:n, :] = u
    p_ref[8:n, :] = e_ref[8:n, :] + e_ref[7:n - 1, :]
    q_ref[16:n, gw:] = p_ref[16:n, gw:] + p_ref[14:n - 2, gw:]
    p_ref[24:n, 2 * gw:] = q_ref[24:n, 2 * gw:] + q_ref[20:n - 4, 2 * gw:]
    q_ref[32:n, 3 * gw:] = p_ref[32:n, 3 * gw:] + p_ref[24:n - 8, 3 * gw:]

    def rows_of(c):
        return slice(c * ck, (c + 1) * ck)

    def v_pair(c, p):
        va = qkvg_ref[rows_of(c), 2 * kw + (2 * p) * GLA_DV:2 * kw + (2 * p + 1) * GLA_DV]
        vb = qkvg_ref[rows_of(c), 2 * kw + (2 * p + 1) * GLA_DV:2 * kw + (2 * p + 2) * GLA_DV]
        return va, vb

    bcs = []
    for c in chunks:
        la = loga_ref[rows_of(c), :]
        la_hi = la.astype(bf16)
        la_lo = (la - la_hi.astype(f32)).astype(bf16)
        bcs.append(jnp.dot(tril, la_hi, preferred_element_type=f32) + jnp.dot(tril, la_lo, preferred_element_type=f32))
    lhs_q, ke2, kds, elast = [], [], [], []
    for c in chunks:
        bc = bcs[c]
        blast = bc[ck - 1:ck, :]
        q = qkvg_ref[rows_of(c), 0:kw].astype(f32) * (GLA_DK ** -0.5)
        k = qkvg_ref[rows_of(c), kw:2 * kw].astype(f32)
        qe = q * jnp.exp(bc)
        ke = (k * jnp.exp(-bc)).astype(bf16)
        kds.append((k * jnp.exp(blast - bc)).astype(bf16))
        elast.append(jnp.exp(blast))
        for p in pairs:
            pl_ = slice(p * pair_w, (p + 1) * pair_w)
            qe_p = qe[:, pl_]
            lhs_q.append(jnp.concatenate([jnp.where(first_lanes_ck, qe_p, 0.0),
                                          jnp.where(first_lanes_ck, 0.0, qe_p)], axis=0).astype(bf16))
            ke2.append(jnp.concatenate([ke[:, pl_], ke[:, pl_]], axis=0))
    att, upd = [], []
    for c in chunks:
        for p in pairs:
            idx = c * len(pairs) + p
            a = lax.dot_general(lhs_q[idx], ke2[idx], NT_DIMS, preferred_element_type=f32)
            att.append(jnp.where(causal, a, 0.0).astype(bf16))
            va, vb = v_pair(c, p)
            r = lax.dot_general(jnp.concatenate([va, vb], axis=1), kds[c][:, p * pair_w:(p + 1) * pair_w], TN_DIMS,
                                preferred_element_type=f32)
            upd.append(jnp.where(first_lanes, r[:GLA_DV], r[GLA_DV:]))
    st = [st_ref[p] for p in pairs]
    for c in chunks:
        for p in pairs:
            idx = c * len(pairs) + p
            va, vb = v_pair(c, p)
            o = lax.dot_general(lhs_q[idx], st[p].astype(bf16), NT_DIMS, preferred_element_type=f32)
            o = o + jnp.dot(att[idx], jnp.concatenate([va, vb], axis=0), preferred_element_type=f32)
            o_ref[rows_of(c), (2 * p) * GLA_DV:(2 * p + 1) * GLA_DV] = o[:ck]
            o_ref[rows_of(c), (2 * p + 1) * GLA_DV:(2 * p + 2) * GLA_DV] = o[ck:]
            st[p] = st[p] * elast[c][:, p * pair_w:(p + 1) * pair_w] + upd[idx]
    for p in pairs:
        st_ref[p] = st[p]

    sums = (p_ref, q_ref, p_ref, q_ref)
    pos = i * t + lax.broadcasted_iota(jnp.int32, (t, 1), 0)
    for gi, w in enumerate(POOL_WINDOWS):
        ls = slice(gi * gw, (gi + 1) * gw)
        cnt = jnp.minimum(w, pos + 1).astype(f32)
        pooled = (sums[gi][hist:n, ls] / cnt - u[:, ls]).astype(bf16)
        pg = jnp.dot(pooled, pw_ref[gi], preferred_element_type=f32) * ps_ref[:, ls]
        op_ref[:, vw + gi * gw:vw + (gi + 1) * gw] = pg.astype(bf16)
    e_ref[hist - POOL_TAIL:hist, :] = e_ref[n - POOL_TAIL:n, :]

    piece = 2 * GLA_DV
    y = x_ref[...]
    for c0 in (vw, vw + piece):
        y = y + jnp.dot(op_ref[:, c0:c0 + piece], wout_ref[c0:c0 + piece, :], preferred_element_type=f32)
    gain = gain_ref[...]
    for p in pairs:
        for h in (2 * p, 2 * p + 1):
            hs = slice(h * GLA_DV, (h + 1) * GLA_DV)
            g = qkvg_ref[:, 2 * kw + vw + h * GLA_DV:2 * kw + vw + (h + 1) * GLA_DV].astype(f32)
            op_ref[:, hs] = (_rms(o_ref[:, hs], gain) * _silu(g)).astype(bf16)
        c0 = p * piece
        y = y + jnp.dot(op_ref[:, c0:c0 + piece], wout_ref[c0:c0 + piece, :], preferred_element_type=f32)
    xo_ref[...] = y

    @pl.when(i == pl.num_programs(1) - 1)
    def _():
        for p in range(GLA_HEADS // 2):
            s_pair = st_ref[p].T
            so_ref[0, 2 * p] = s_pair[:GLA_DK]
            so_ref[0, 2 * p + 1] = s_pair[GLA_DK:]


def _gla_pool_prompt(qkvg, loga, u, x, tril, gain, pw, ps, wout, *, batch, t):
    m, d = x.shape
    nt = m // batch // t
    row = lambda b, i: (b * nt + i, 0)
    const2 = lambda b, i: (0, 0)
    vw = GLA_HEADS * GLA_DV
    uw = u.shape[1]
    pipelined = t * (qkvg.shape[1] * BF16_BYTES + (loga.shape[1] + uw + 2 * d) * F32_BYTES)
    scratch = (t * vw + 3 * (POOL_HIST + t) * uw) * F32_BYTES + t * (vw + uw) * BF16_BYTES
    waves = (t // GLA_CHUNK) * (GLA_HEADS // 2) * (3 * LANES * LANES * BF16_BYTES + LANES * LANES * F32_BYTES)
    resident = 2 * (wout.size + pw.size) * BF16_BYTES + scratch + waves + t * d * F32_BYTES
    return pl.pallas_call(
        _gla_pool_prompt_body,
        grid=(batch, nt),
        in_specs=[
            pl.BlockSpec((t, qkvg.shape[1]), row),
            pl.BlockSpec((t, loga.shape[1]), row),
            pl.BlockSpec((t, uw), row),
            pl.BlockSpec((t, d), row),
            pl.BlockSpec(tril.shape, const2),
            pl.BlockSpec(gain.shape, const2),
            pl.BlockSpec(pw.shape, lambda b, i: (0, 0, 0)),
            pl.BlockSpec(ps.shape, const2),
            pl.BlockSpec(wout.shape, const2),
        ],
        out_specs=[
            pl.BlockSpec((t, d), row),
            pl.BlockSpec((1, GLA_HEADS, GLA_DK, GLA_DV), lambda b, i: (b, 0, 0, 0)),
        ],
        out_shape=[
            jax.ShapeDtypeStruct((m, d), f32),
            jax.ShapeDtypeStruct((batch, GLA_HEADS, GLA_DK, GLA_DV), f32),
        ],
        scratch_shapes=[
            pltpu.VMEM((GLA_HEADS // 2, GLA_DV, 2 * GLA_DK), f32),
            pltpu.VMEM((t, vw), f32),
            pltpu.VMEM((POOL_HIST + t, uw), f32),
            pltpu.VMEM((POOL_HIST + t, uw), f32),
            pltpu.VMEM((POOL_HIST + t, uw), f32),
            pltpu.VMEM((t, vw + uw), bf16),
        ],
        compiler_params=_params(("arbitrary", "arbitrary"), pipelined, resident),
        name="gla_pool_prompt",
    )(qkvg, loga, u, x, tril, gain, pw, ps, wout)


def _gla_pool_sample_body(qkvg_ref, loga_ref, u_ref, s_ref, buf_ref, gain_ref, pw_ref, ps_ref,
                          op_ref, so_ref):
    bb = u_ref.shape[0]
    kw = GLA_HEADS * GLA_DK
    vw = GLA_HEADS * GLA_DV
    gain = gain_ref[...]
    qkvg = qkvg_ref[...].astype(f32)
    alpha = jnp.exp(loga_ref[...])
    qs = qkvg[:, 0:kw] * (GLA_DK ** -0.5)
    k = qkvg[:, kw:2 * kw]

    def column(row):
        return jnp.broadcast_to(row, (LANES, kw)).T

    o_rows = []
    for b in range(bb):
        acol = column(alpha[b:b + 1, :])
        qcol = column(qs[b:b + 1, :])
        kcol = column(k[b:b + 1, :])
        o_heads = []
        for h in range(GLA_HEADS):
            ks = slice(h * GLA_DK, (h + 1) * GLA_DK)
            v = qkvg[b:b + 1, 2 * kw + h * GLA_DV:2 * kw + (h + 1) * GLA_DV]
            s_new = acol[ks, :] * s_ref[b, h] + kcol[ks, :] * v
            so_ref[b, h] = s_new
            o = jnp.sum(qcol[ks, :] * s_new, axis=0, keepdims=True)
            g = qkvg[b:b + 1, 2 * kw + vw + h * GLA_DV:2 * kw + vw + (h + 1) * GLA_DV]
            o_heads.append(_rms(o, gain) * _silu(g))
        o_rows.append(jnp.concatenate(o_heads, axis=1))
    op_ref[:, 0:vw] = jnp.concatenate(o_rows, axis=0).astype(bf16)

    u = u_ref[...]
    for gi, w in enumerate(POOL_WINDOWS):
        ls = slice(gi * POOL_GW, (gi + 1) * POOL_GW)
        s = u[:, ls] + jnp.sum(buf_ref[:, POOL_BUF - (w - 1):POOL_BUF, ls], axis=1)
        cnt = float(min(w, PAST_LEN + 1))
        pooled = (s / cnt - u[:, ls]).astype(bf16)
        pg = jnp.dot(pooled, pw_ref[gi], preferred_element_type=f32) * ps_ref[:, ls]
        op_ref[:, vw + gi * POOL_GW:vw + (gi + 1) * POOL_GW] = pg.astype(bf16)


def _gla_pool_sample(qkvg, loga, u, s, buf, gain, pw, ps, *, bb):
    n = u.shape[0]
    row = lambda i: (i, 0)
    const2 = lambda i: (0, 0)
    ow = GLA_HEADS * GLA_DV + POOL_GW * len(POOL_WINDOWS)
    state_rows = int(np.prod(s.shape[1:]))
    pipelined = bb * ((qkvg.shape[1] + ow) * BF16_BYTES
                      + (loga.shape[1] + u.shape[1] + 2 * state_rows + POOL_TAIL * buf.shape[2]) * F32_BYTES)
    resident = 2 * pw.size * BF16_BYTES + 3 * LANES * loga.shape[1] * F32_BYTES
    return pl.pallas_call(
        _gla_pool_sample_body,
        grid=(n // bb,),
        in_specs=[
            pl.BlockSpec((bb, qkvg.shape[1]), row),
            pl.BlockSpec((bb, loga.shape[1]), row),
            pl.BlockSpec((bb, u.shape[1]), row),
            pl.BlockSpec((bb,) + s.shape[1:], lambda i: (i, 0, 0, 0)),
            pl.BlockSpec((bb,) + buf.shape[1:], lambda i: (i, 0, 0)),
            pl.BlockSpec(gain.shape, const2),
            pl.BlockSpec(pw.shape, lambda i: (0, 0, 0)),
            pl.BlockSpec(ps.shape, const2),
        ],
        out_specs=[
            pl.BlockSpec((bb, ow), row),
            pl.BlockSpec((bb,) + s.shape[1:], lambda i: (i, 0, 0, 0)),
        ],
        out_shape=[
            jax.ShapeDtypeStruct((n, ow), bf16),
            jax.ShapeDtypeStruct(s.shape, f32),
        ],
        compiler_params=_params(("arbitrary",), pipelined, resident),
        name="gla_pool_sample",
    )(qkvg, loga, u, s, buf, gain, pw, ps)


def _proj_res_body(x_ref, a_ref, w_ref, o_ref):
    o_ref[...] = x_ref[...] + jnp.dot(a_ref[...], w_ref[...], preferred_element_type=f32)


def _proj_res(x, a, w, *, tm):
    m, d = x.shape
    return pl.pallas_call(
        _proj_res_body,
        grid=(m // tm,),
        in_specs=[
            pl.BlockSpec((tm, d), lambda i: (i, 0)),
            pl.BlockSpec((tm, a.shape[1]), lambda i: (i, 0)),
            pl.BlockSpec(w.shape, lambda i: (0, 0)),
        ],
        out_specs=pl.BlockSpec((tm, d), lambda i: (i, 0)),
        out_shape=jax.ShapeDtypeStruct((m, d), f32),
        compiler_params=_params(("arbitrary",), tm * (2 * d * F32_BYTES + a.shape[1] * BF16_BYTES),
                                2 * w.size * BF16_BYTES),
        name="proj_res",
    )(x, a, w)


def _ret_token_pieces(q_ref, k_ref, v_ref, g_ref, s_ref, og_ref, so_ref, gamma):
    def piece(j, h):
        def run():
            ks = slice(h * RET_DK, (h + 1) * RET_DK)
            vs = slice(h * RET_DV, (h + 1) * RET_DV)
            qcol = jnp.broadcast_to(q_ref[j, :, ks].astype(f32), (LANES, RET_DK)).T
            kcol = jnp.broadcast_to(k_ref[j, :, ks].astype(f32), (LANES, RET_DK)).T
            v = v_ref[j, :, vs].astype(f32)
            g = g_ref[j, :, vs].astype(f32)
            o_tiles = []
            for t in range(RET_DV // LANES):
                cs = slice(t * LANES, (t + 1) * LANES)
                s_new = gamma[h] * s_ref[j, h, :, cs] + kcol * v[:, cs]
                so_ref[j, h, :, cs] = s_new
                o_tiles.append(jnp.sum(qcol * s_new, axis=0, keepdims=True))
            o = jnp.concatenate(o_tiles, axis=1)
            og_ref[j, :, vs] = (_rms(o) * _silu(g)).astype(bf16)
            return o
        return run

    return [piece(j, h) for j in range(s_ref.shape[0]) for h in range(RET_HEADS)]


def _ffn_body(*refs, tf, n_sub, final_norm, rider_gamma):
    x_ref, gain_ref, wg_ref, wu_ref, wd_ref, fgain_ref = refs[:6]
    pieces = []
    if rider_gamma is None:
        o_ref, h_ref, acc_ref = refs[6:]
    else:
        rq_ref, rk_ref, rv_ref, rg_ref, rs_ref, o_ref, rog_ref, rso_ref, h_ref, acc_ref = refs[6:]
        pieces = _ret_token_pieces(rq_ref, rk_ref, rv_ref, rg_ref, rs_ref, rog_ref, rso_ref, rider_gamma)
    n_chunks = wg_ref.shape[1] // tf
    bounds = [n_chunks * s // n_sub for s in range(n_sub + 1)]

    def exact_zero(v):
        bits = lax.bitcast_convert_type(v, jnp.uint32)
        return ((bits >> 16) >> 16).astype(f32)

    def run_chunks(chunks):
        pin = None
        for n, c in enumerate(chunks):
            cs = slice(c * tf, (c + 1) * tf)
            g = jnp.dot(h_ref[...], wg_ref[:, cs], preferred_element_type=f32)
            if pin is not None:
                g = g + pin
                pin = None
            u = jnp.dot(h_ref[...], wu_ref[:, cs], preferred_element_type=f32)
            a = (_silu(g) * u).astype(bf16)
            part = jnp.dot(a, wd_ref[cs, :], preferred_element_type=f32)
            if c == 0:
                acc_ref[...] = part
            else:
                acc_ref[...] += part
            for p in range(len(pieces)):
                if p * (len(chunks) - 1) // len(pieces) == n:
                    z = exact_zero(pieces[p]()[:, :tf])
                    pin = z if pin is None else pin + z

    def sub_step(s):
        if s == 0:
            h_ref[...] = _rms(x_ref[...], gain_ref[...]).astype(bf16)
        run_chunks(range(bounds[s], bounds[s + 1]))
        if s == n_sub - 1:
            y = x_ref[...] + acc_ref[...]
            if final_norm:
                y = _rms(y, fgain_ref[...])
            o_ref[...] = y

    if n_sub == 1:
        sub_step(0)
    else:
        for s in range(n_sub):
            pl.when(pl.program_id(1) == s)(functools.partial(sub_step, s))


def _ffn(x, gain, wg, wu, wd, fgain, *, layer, tm, tf, final_norm, rider=None):
    m, d = x.shape
    ff = wg.shape[1]
    steps = m // tm
    n_sub = 1 if rider is None else 2
    single_buffered = dict(pipeline_mode=pl.Buffered(1))
    in_specs = [
        pl.BlockSpec((tm, d), lambda i, s: (i, 0)),
        pl.BlockSpec((None, 1, d), lambda i, s: (layer, 0, 0)),
        pl.BlockSpec((d, ff), lambda i, s: (0, 0), **single_buffered),
        pl.BlockSpec((d, ff), lambda i, s: (0, 0), **single_buffered),
        pl.BlockSpec((ff, d), lambda i, s: (0, 0), **single_buffered),
        pl.BlockSpec((1, d), lambda i, s: (0, 0)),
    ]
    args = [x, gain, wg, wu, wd, fgain]
    out_specs = [pl.BlockSpec((tm, d), lambda i, s: (i, 0))]
    out_shape = [jax.ShapeDtypeStruct((m, d), f32)]
    gamma = None
    pipelined = 2 * tm * d * F32_BYTES
    resident = (3 * d * ff * BF16_BYTES + tm * d * (BF16_BYTES + F32_BYTES)
                + 3 * tm * tf * F32_BYTES + tm * d * F32_BYTES)
    if rider is not None:
        qkvg3, state, rows, gamma = rider
        assert 2 * steps * rows == state.shape[0]
        qw = RET_HEADS * RET_DK
        vw = RET_HEADS * RET_DV
        blk = lambda col: (lambda i, s: (2 * i + s, 0, col))
        state_spec = pl.BlockSpec((rows,) + state.shape[1:], lambda i, s: (2 * i + s, 0, 0, 0))
        in_specs += [
            pl.BlockSpec((rows, 1, qw), blk(0)),
            pl.BlockSpec((rows, 1, qw), blk(1)),
            pl.BlockSpec((rows, 1, vw), blk(1)),
            pl.BlockSpec((rows, 1, vw), blk(2)),
            state_spec,
        ]
        args += [qkvg3, qkvg3, qkvg3, qkvg3, state]
        out_specs += [pl.BlockSpec((rows, 1, vw), blk(0)), state_spec]
        out_shape += [
            jax.ShapeDtypeStruct((state.shape[0], 1, vw), bf16),
            jax.ShapeDtypeStruct(state.shape, f32),
        ]
        pipelined += 2 * rows * int(np.prod(state.shape[1:])) * F32_BYTES
    out = pl.pallas_call(
        functools.partial(_ffn_body, tf=tf, n_sub=n_sub, final_norm=final_norm, rider_gamma=gamma),
        grid=(steps, n_sub),
        in_specs=in_specs,
        out_specs=out_specs,
        out_shape=out_shape,
        scratch_shapes=[pltpu.VMEM((tm, d), bf16), pltpu.VMEM((tm, d), f32)],
        compiler_params=_params(("arbitrary", "arbitrary"), pipelined, resident),
        name="ffn_final" if final_norm else "ffn",
    )(*args)
    return out[0] if rider is None else out


def _in_odd_body(x_ref, gain_ref, w_ref, perm_ref, cos_ref, sin_ref, qsc_ref, ksc_ref, o_ref, h_ref, wqk_ref, *,
                 tn, split_halves):
    qw = RET_HEADS * RET_DK
    half = RET_DK // 2
    if split_halves:
        @pl.when(pl.program_id(0) == 0)
        def _():
            for hh in range(2 * RET_HEADS):
                hs = slice(hh * RET_DK, (hh + 1) * RET_DK)
                wqk_ref[:, hs] = jnp.dot(w_ref[:, hs], perm_ref[...], preferred_element_type=f32).astype(bf16)

    h_ref[...] = _rms(x_ref[...], gain_ref[...]).astype(bf16)
    cos = cos_ref[...]
    sin = sin_ref[...]
    for c in range(2 * qw // tn):
        c0 = c * tn
        w_chunk = wqk_ref[:, c0:c0 + tn] if split_halves else w_ref[:, c0:c0 + tn]
        p = jnp.dot(h_ref[...], w_chunk, preferred_element_type=f32)
        sc_ref = qsc_ref if c0 < qw else ksc_ref
        for hh in range(tn // RET_DK):
            h0 = hh * RET_DK
            head = (c0 % qw + h0) // RET_DK
            sc = sc_ref[:, head * LANES:(head + 1) * LANES]
            if split_halves:
                ev = p[:, h0:h0 + half]
                od = p[:, h0 + half:h0 + RET_DK]
                o_ref[:, c0 + h0:c0 + h0 + half] = ((ev * cos - od * sin) * sc).astype(bf16)
                o_ref[:, c0 + h0 + half:c0 + h0 + RET_DK] = ((od * cos + ev * sin) * sc).astype(bf16)
            else:
                xh = p[:, h0:h0 + RET_DK]
                even = lax.broadcasted_iota(jnp.int32, xh.shape, 1) % 2 == 0
                partner = jnp.where(even, pltpu.roll(xh, RET_DK - 1, 1), pltpu.roll(xh, 1, 1))
                r = xh * cos + partner * sin
                o_ref[:, c0 + h0:c0 + h0 + half] = (r[:, :half] * sc).astype(bf16)
                o_ref[:, c0 + h0 + half:c0 + h0 + RET_DK] = (r[:, half:] * sc).astype(bf16)
    for c0 in range(2 * qw, w_ref.shape[1], tn):
        p = jnp.dot(h_ref[...], w_ref[:, c0:c0 + tn], preferred_element_type=f32)
        o_ref[:, c0:c0 + tn] = p.astype(bf16)


def _in_odd(x, gain, w, perm, tables, *, layer, tm, tn, split_halves, cast=()):
    m, d = x.shape
    n = w.shape[1]
    cos, sin, qsc, ksc = tables
    ntab = cos.shape[0] // tm
    qkw = 2 * RET_HEADS * RET_DK
    single_buffered = dict(pipeline_mode=pl.Buffered(1))
    const = lambda i: (0, 0)
    rope_spec = pl.BlockSpec((tm, cos.shape[1]), lambda i: (i % ntab, 0))
    wqk_shape = (d, qkw) if split_halves else (8, LANES)
    cast_args, cast_in, cast_out, cast_shapes, cast_bytes = _cast_rider(cast, m // tm)
    pipelined = tm * (d * F32_BYTES + n * BF16_BYTES + 2 * cos.shape[1] * F32_BYTES) + cast_bytes
    resident = ((w.size + 2 * perm.size + tm * d + wqk_shape[0] * wqk_shape[1]) * BF16_BYTES
                + 2 * (qsc.size + ksc.size) * F32_BYTES + 2 * tm * tn * F32_BYTES)
    outs = pl.pallas_call(
        _with_cast_rider(functools.partial(_in_odd_body, tn=tn, split_halves=split_halves), 8, 1, len(cast)),
        grid=(m // tm,),
        in_specs=[
            pl.BlockSpec((tm, d), lambda i: (i, 0)),
            pl.BlockSpec((None, 1, d), lambda i: (layer, 0, 0)),
            pl.BlockSpec(w.shape, const, **single_buffered),
            pl.BlockSpec(perm.shape, const),
            rope_spec, rope_spec,
            pl.BlockSpec(qsc.shape, const),
            pl.BlockSpec(ksc.shape, const),
        ] + cast_in,
        out_specs=[pl.BlockSpec((tm, n), lambda i: (i, 0))] + cast_out,
        out_shape=[jax.ShapeDtypeStruct((m, n), bf16)] + cast_shapes,
        scratch_shapes=[pltpu.VMEM((tm, d), bf16), pltpu.VMEM(wqk_shape, bf16)],
        compiler_params=_params(("arbitrary",), pipelined, resident),
        name="in_odd",
    )(x, gain, w, perm, cos, sin, qsc, ksc, *cast_args)
    return tuple(outs) if cast else outs[0]


def _ret_prompt_body(q_ref, k_ref, v_ref, g_ref, x_ref, wout_ref, xo_ref, so_ref, s_ref, sb_ref, slab_ref, *,
                     gamma_c, n):
    c = pl.program_id(1)
    subs = [slice(j * n, (j + 1) * n) for j in range(q_ref.shape[0] // n)]

    @pl.when(c == 0)
    def _():
        s_ref[...] = jnp.zeros_like(s_ref)
        sb_ref[...] = jnp.zeros_like(sb_ref)

    causal = lax.broadcasted_iota(jnp.int32, (n, n), 0) >= lax.broadcasted_iota(jnp.int32, (n, n), 1)
    heads = range(RET_HEADS)
    ks = [slice(h * RET_DK, (h + 1) * RET_DK) for h in heads]
    vs = [slice(h * RET_DV, (h + 1) * RET_DV) for h in heads]
    att = [[jnp.where(causal, lax.dot_general(q_ref[r, ks[h]], k_ref[r, ks[h]], NT_DIMS,
                                              preferred_element_type=f32), 0.0).astype(bf16) for h in heads]
           for r in subs]
    o = []
    for j, r in enumerate(subs):
        o.append([jnp.dot(q_ref[r, ks[h]], sb_ref[h], preferred_element_type=f32)
                  + jnp.dot(att[j][h], v_ref[r, vs[h]], preferred_element_type=f32) for h in heads])
        for h in heads:
            kv = lax.dot_general(k_ref[r, ks[h]], v_ref[r, vs[h]], TN_DIMS, preferred_element_type=f32)
            s_new = gamma_c[h] * (s_ref[h] + kv)
            s_ref[h] = s_new
            sb_ref[h] = s_new.astype(bf16)
    for j, r in enumerate(subs):
        y = x_ref[r, :]
        for h in heads:
            og = (_rms(o[j][h]) * _silu(g_ref[r, vs[h]].astype(f32))).astype(bf16)
            y = y + jnp.dot(og, wout_ref[vs[h], :], preferred_element_type=f32)
        xo_ref[r, :] = y

    @pl.when(c == pl.num_programs(1) - 1)
    def _():
        half = RET_DK // 2
        for h in range(RET_HEADS):
            for t in range(RET_DV // LANES):
                ls = slice(t * LANES, (t + 1) * LANES)
                slab_ref[pl.ds(0, half, stride=2), :] = s_ref[h, 0:half, ls]
                slab_ref[pl.ds(1, half, stride=2), :] = s_ref[h, half:RET_DK, ls]
                so_ref[0, h, :, ls] = slab_ref[...]


def _ret_prompt(qkvg, x, wout, gamma_c, *, batch, c, chunk):
    m, d = x.shape
    nc = m // batch // c
    qw = RET_HEADS * RET_DK
    vw = RET_HEADS * RET_DV
    assert c % chunk == 0
    state = RET_HEADS * RET_DK * RET_DV
    pipelined = c * (2 * (qw + vw) * BF16_BYTES + 2 * d * F32_BYTES) + state * F32_BYTES
    resident = (wout.size * BF16_BYTES + state * (F32_BYTES + BF16_BYTES) + RET_DK * LANES * F32_BYTES
                + c * RET_HEADS * (chunk * BF16_BYTES + RET_DV * F32_BYTES)
                + RET_DK * RET_DV * F32_BYTES + c * d * F32_BYTES)
    return pl.pallas_call(
        functools.partial(_ret_prompt_body, gamma_c=gamma_c, n=chunk),
        grid=(batch, nc),
        in_specs=[
            pl.BlockSpec((c, qw), lambda b, i: (b * nc + i, 0)),
            pl.BlockSpec((c, qw), lambda b, i: (b * nc + i, 1)),
            pl.BlockSpec((c, vw), lambda b, i: (b * nc + i, 1)),
            pl.BlockSpec((c, vw), lambda b, i: (b * nc + i, 2)),
            pl.BlockSpec((c, d), lambda b, i: (b * nc + i, 0)),
            pl.BlockSpec(wout.shape, lambda b, i: (0, 0), pipeline_mode=pl.Buffered(1)),
        ],
        out_specs=[
            pl.BlockSpec((c, d), lambda b, i: (b * nc + i, 0)),
            pl.BlockSpec((1, RET_HEADS, RET_DK, RET_DV), lambda b, i: (b, 0, 0, 0)),
        ],
        out_shape=[
            jax.ShapeDtypeStruct((m, d), f32),
            jax.ShapeDtypeStruct((batch, RET_HEADS, RET_DK, RET_DV), f32),
        ],
        scratch_shapes=[
            pltpu.VMEM((RET_HEADS, RET_DK, RET_DV), f32),
            pltpu.VMEM((RET_HEADS, RET_DK, RET_DV), bf16),
            pltpu.VMEM((RET_DK, LANES), f32),
        ],
        compiler_params=_params(("arbitrary", "arbitrary"), pipelined, resident),
        name="ret_prompt",
    )(qkvg, qkvg, qkvg, qkvg, x, wout)


def _rope_tables(pos, per_pair):
    pair_angle = 1.0 / (ROPE_BASE ** jnp.linspace(0.0, 1.0, RET_DK // 2, dtype=f32))
    if per_pair:
        ang = pos[:, None] * pair_angle[None, :]
        return jnp.cos(ang), jnp.sin(ang)
    ang = pos[:, None] * jnp.repeat(pair_angle, 2)[None, :]
    sign = jnp.where(jnp.arange(RET_DK) % 2 == 0, -1.0, 1.0).astype(f32)
    return jnp.cos(ang), jnp.sin(ang) * sign


def _even_odd_perm():
    half = RET_DK // 2
    src = np.concatenate([2 * np.arange(half), 2 * np.arange(half) + 1])
    perm = np.zeros((RET_DK, RET_DK), np.float32)
    perm[src, np.arange(RET_DK)] = 1.0
    return jnp.asarray(perm, dtype=bf16)


def _lane_replicated(scale):
    return jnp.asarray(np.repeat(scale, LANES, axis=1), dtype=f32)


def _ret_decay(rows, c):
    gam = 1.0 - 2.0 ** (-5.0 - np.arange(RET_HEADS, dtype=np.float64))
    lg = np.log(gam)
    steps = (np.arange(rows) % c + 1.0)[:, None]
    q_scale = _lane_replicated(np.exp(lg[None, :] * steps))
    k_scale = _lane_replicated(np.exp(-lg[None, :] * steps) * RET_DK ** -0.5)
    gamma_c = tuple(float(x) for x in np.exp(lg * c))
    gamma = tuple(float(x) for x in gam)
    return q_scale, k_scale, gamma_c, gamma


def kernel(x_prompt, x_sample, state_gla, state_pool, state_ret, norm_mix, norm_ffn, norm_final, w_in_even,
           w_gate_b, b_gate, gla_gain, pool_w, pool_scale, w_out_even, w_in_odd, w_out_odd, w_ffn_gate,
           w_ffn_up, w_ffn_down):
    batch, seq, d = x_prompt.shape
    n_s = x_sample.shape[0]
    assert norm_mix.shape[0] == 2 and x_sample.shape[1] == 1

    we = w_in_even[0].T
    wgb = jnp.concatenate([w_gate_b[0], jnp.zeros((LANES - GATE_RANK, w_gate_b.shape[2]), f32)], axis=0).astype(bf16)
    bg = b_gate[0][None, :]
    gg = gla_gain[0][None, :]
    pw = pool_w[0].astype(bf16)
    ps = pool_scale[0][None, :]
    nm = norm_mix[:, None, :]
    nf = norm_ffn[:, None, :]
    nfin = norm_final[None, :]
    tril = jnp.asarray(np.tril(np.ones((GLA_CHUNK, GLA_CHUNK), np.float32)), dtype=bf16)
    tf = 256
    tm_p = 512
    q_scale, k_scale, gamma_c, gamma = _ret_decay(tm_p, RET_CHUNK)
    tables_p = _rope_tables(jnp.arange(seq, dtype=f32), True) + (q_scale, k_scale)
    tables_s = _rope_tables(jnp.full((n_s,), float(PAST_LEN), f32), False) + (
        _lane_replicated(np.ones((n_s, RET_HEADS))), _lane_replicated(np.full((n_s, RET_HEADS), RET_DK ** -0.5)))
    perm = _even_odd_perm()

    ff = w_ffn_gate.shape[2]
    xp = x_prompt.reshape(batch * seq, d)
    steps_p = batch * seq // tm_p
    ffn_cast = lambda layer: (
        (w_ffn_gate.reshape(-1, ff), d // steps_p, layer * steps_p, steps_p),
        (w_ffn_up.reshape(-1, ff), d // steps_p, layer * steps_p, steps_p),
        (w_ffn_down.reshape(-1, d), 2 * ff // steps_p, layer * steps_p // 2, steps_p // 2))
    whole = lambda w: (w, w.shape[0] // steps_p, 0, steps_p)
    qkvg, loga, u_p, wg0, wu0, wd0, wio, woe, woo = _in_even(
        xp, nm[0], we, wgb, bg, tm=tm_p,
        cast=ffn_cast(0) + (whole(w_in_odd[0]), whole(w_out_even[0]), whole(w_out_odd[0])))
    xp, gla_p = _gla_pool_prompt(qkvg, loga, u_p, xp, tril, gg, pw, ps, woe, batch=batch, t=512)

    xs = x_sample.reshape(n_s, d)
    qkvg_s, loga_s, u_s = _in_even(xs, nm[0], we, wgb, bg, tm=n_s)
    op_s, gla_s = _gla_pool_sample(qkvg_s, loga_s, u_s, state_gla[0], state_pool[0], gg, pw, ps, bb=16)
    xs = _proj_res(xs, op_s, woe, tm=n_s)
    xs = _ffn(xs, nf, wg0, wu0, wd0, nfin, layer=0, tm=n_s, tf=tf, final_norm=False)
    qkvg2_s = _in_odd(xs, nm, wio, perm, tables_s, layer=1, tm=n_s, tn=512, split_halves=False)
    qkvg2_s = qkvg2_s.reshape(n_s, 1, -1)

    rows = n_s // (2 * steps_p)
    xp, og_s, ret_s = _ffn(xp, nf, wg0, wu0, wd0, nfin, layer=0, tm=tm_p, tf=tf, final_norm=False,
                           rider=(qkvg2_s, state_ret[0], rows, gamma))
    qkvg2, wg1, wu1, wd1 = _in_odd(xp, nm, wio, perm, tables_p, layer=1, tm=tm_p, tn=512, split_halves=True,
                                   cast=ffn_cast(1))
    xp, ret_p = _ret_prompt(qkvg2, xp, woo, gamma_c, batch=batch, c=2 * RET_CHUNK, chunk=RET_CHUNK)
    y_prompt = _ffn(xp, nf, wg1, wu1, wd1, nfin, layer=1, tm=tm_p, tf=tf, final_norm=True)
    pool_p = u_p.reshape(batch, seq, -1)[:, seq - POOL_BUF:, :]

    xs = _proj_res(xs, og_s.reshape(n_s, -1), woo, tm=n_s)
    y_sample = _ffn(xs, nf, wg1, wu1, wd1, nfin, layer=1, tm=n_s, tf=tf, final_norm=True)

    pool_s = jnp.concatenate([state_pool[0][:, 1:, :], u_s[:, None, :]], axis=1)

    return (y_prompt.reshape(batch, seq, d), y_sample.reshape(n_s, 1, d),
            gla_p[None], gla_s[None], pool_p[None], pool_s[None], ret_p[None], ret_s[None])
```
